```python
import jax, jax.numpy as jnp
from jax import lax
import numpy as np

D_MODEL = 2048
BATCH = 8
SEQ = 2048
DEPTH = 1

MEM_LEN = 256
HEAD_DIM = 64
RWKV_WIDTH = D_MODEL // 2
RWKV_HEADS = RWKV_WIDTH // HEAD_DIM
DECAY_LORA = max(32, int(round(1.8 * RWKV_WIDTH ** 0.5 / 32)) * 32)
AAA_LORA = max(32, int(round(1.8 * RWKV_WIDTH ** 0.5 / 32)) * 32)
GATE_LORA = max(32, int(round(0.6 * RWKV_WIDTH ** 0.8 / 32)) * 32)
GN_EPS = 64e-5
SWA_WIDTH = D_MODEL - RWKV_WIDTH
SWA_Q_HEADS = SWA_WIDTH // HEAD_DIM
SWA_KV_HEADS = max(1, SWA_Q_HEADS // 8)
SWA_GROUP = SWA_Q_HEADS // SWA_KV_HEADS
WINDOW = 128
BLOCK = 128
ROPE_THETA = 10000.0
SHIFT_COLS = 3 * RWKV_WIDTH + DECAY_LORA + AAA_LORA + GATE_LORA
SWA_COLS = SWA_WIDTH + 2 * SWA_KV_HEADS * HEAD_DIM
IN_COLS = SHIFT_COLS + SWA_COLS
XATTN_HEADS = 4
XATTN_HEAD_DIM = D_MODEL // XATTN_HEADS
D_FF = ((8 * D_MODEL // 3 + 255) // 256) * 256
RMS_EPS = 1e-6
NEG_INF = -1e30

kernel_name = 'hymba_rwkv7_swa_sink_macaron_layer'


def _rmsnorm(x, g):
    xf = x.astype(jnp.float32)
    y = xf * lax.rsqrt(jnp.mean(xf * xf, axis=-1, keepdims=True) + RMS_EPS)
    return (y * g.astype(jnp.float32)).astype(x.dtype)


def _swiglu(h, w_gate, w_up, w_down):
    return (jax.nn.silu(h @ w_gate) * (h @ w_up)) @ w_down


def _rope(t, pos):
    hd = t.shape[-1]
    inv_freq = ROPE_THETA ** (-jnp.arange(0, hd, 2, dtype=jnp.float32) / hd)
    ang = pos[:, None] * inv_freq[None, :]
    cos = jnp.cos(ang)[None, :, None, :]
    sin = jnp.sin(ang)[None, :, None, :]
    tf = t.astype(jnp.float32)
    t1, t2 = tf[..., : hd // 2], tf[..., hd // 2:]
    return jnp.concatenate([t1 * cos - t2 * sin, t2 * cos + t1 * sin], axis=-1).astype(t.dtype)


def _rwkv7_time_mix(z, w0, decay_up, a0, aaa_up, gate_up, k_k, k_a, r_k, lnx_w, lnx_b):
    out_dtype = z.dtype
    f32 = jnp.float32
    z = z.astype(f32)
    B, T, _ = z.shape
    C, H, N = RWKV_WIDTH, RWKV_HEADS, HEAD_DIM
    cuts = [C, 2 * C, 3 * C, 3 * C + DECAY_LORA, 3 * C + DECAY_LORA + AAA_LORA]
    r, k, v, wd, ad, gd = jnp.split(z, cuts, axis=-1)
    w = -jax.nn.softplus(-(w0.astype(f32) + jnp.tanh(wd) @ decay_up.astype(f32))) - 0.5
    a = jax.nn.sigmoid(a0.astype(f32) + ad @ aaa_up.astype(f32))
    g = jax.nn.sigmoid(gd) @ gate_up.astype(f32)
    kk = (k * k_k.astype(f32)).reshape(B, T, H, N)
    kk = kk / jnp.maximum(jnp.sqrt(jnp.sum(kk * kk, axis=-1, keepdims=True)), 1e-12)
    k = k * (1.0 + (a - 1.0) * k_a.astype(f32))
    heads = lambda t: t.reshape(B, T, H, N)
    r_h, k_h, v_h, a_h = heads(r), heads(k), heads(v), heads(a)
    decay = jnp.exp(-jnp.exp(heads(w)))
    tm = lambda t: jnp.swapaxes(t, 0, 1)
    seq_in = (tm(r_h), tm(decay), tm(k_h), tm(v_h), tm(-kk), tm(kk * a_h))

    def step(S, inp):
        r_t, w_t, k_t, v_t, a_t, b_t = inp
        sa = jnp.einsum('bhij,bhj->bhi', S, a_t)
        S = S * w_t[:, :, None, :] + sa[..., None] * b_t[:, :, None, :] + v_t[..., None] * k_t[:, :, None, :]
        return S, jnp.einsum('bhij,bhj->bhi', S, r_t)

    S0 = jnp.zeros((B, H, N, N), f32)
    _, y = lax.scan(step, S0, seq_in)
    y = tm(y)
    mu = jnp.mean(y, axis=-1, keepdims=True)
    var = jnp.mean(jnp.square(y - mu), axis=-1, keepdims=True)
    y = (y - mu) * lax.rsqrt(var + GN_EPS) * lnx_w.astype(f32).reshape(H, N) + lnx_b.astype(f32).reshape(H, N)
    y = y + jnp.sum(r_h * k_h * r_k.astype(f32), axis=-1, keepdims=True) * v_h
    return (y.reshape(B, T, C) * g).astype(out_dtype)


def _swa_gqa_sinks(q, k, v, sinks):
    B, T, _, hd = q.shape
    nb = T // BLOCK
    qb = q.reshape(B, nb, BLOCK, SWA_KV_HEADS, SWA_GROUP, hd)

    def band(t):
        tb = t.reshape(B, nb, BLOCK, SWA_KV_HEADS, hd)
        prev = jnp.pad(tb[:, :-1], ((0, 0), (1, 0), (0, 0), (0, 0), (0, 0)))
        return jnp.concatenate([prev, tb], axis=2)

    kb, vb = band(k), band(v)
    s = jnp.einsum('bnqhgd,bnkhd->bnhgqk', qb, kb).astype(jnp.float32) * (hd ** -0.5)
    blk = jnp.arange(nb)[:, None]
    qpos = blk * BLOCK + jnp.arange(BLOCK)[None, :]
    kpos = (blk - 1) * BLOCK + jnp.arange(2 * BLOCK)[None, :]
    diff = qpos[:, :, None] - kpos[:, None, :]
    valid = (diff >= 0) & (diff < WINDOW) & (kpos[:, None, :] >= 0)
    s = jnp.where(valid[None, :, None, None], s, NEG_INF)
    sink = jnp.broadcast_to(sinks.astype(jnp.float32).reshape(SWA_KV_HEADS, SWA_GROUP)[None, None, :, :, None, None],
                            s.shape[:-1] + (1,))
    p = jax.nn.softmax(jnp.concatenate([s, sink], axis=-1), axis=-1)[..., :-1]
    o = jnp.einsum('bnhgqk,bnkhd->bnqhgd', p.astype(v.dtype), vb)
    return o.reshape(B, T, SWA_Q_HEADS * hd)


def _memory_cross_attn(h, mem_n, w_xq, w_xkv, w_xo):
    B, T, _ = h.shape
    M = mem_n.shape[1]
    q = (h @ w_xq).reshape(B, T, XATTN_HEADS, XATTN_HEAD_DIM)
    k, v = jnp.split(mem_n @ w_xkv, 2, axis=-1)
    k = k.reshape(B, M, XATTN_HEADS, XATTN_HEAD_DIM)
    v = v.reshape(B, M, XATTN_HEADS, XATTN_HEAD_DIM)
    s = jnp.einsum('bthd,bmhd->bhtm', q, k).astype(jnp.float32) * (XATTN_HEAD_DIM ** -0.5)
    p = jax.nn.softmax(s, axis=-1).astype(v.dtype)
    o = jnp.einsum('bhtm,bmhd->bthd', p, v).reshape(B, T, D_MODEL)
    return o @ w_xo


def setup_inputs(seed: int = 0) -> dict:
    key = jax.random.key(seed)
    ks = iter(jax.random.split(key, 40))
    f32 = jnp.float32
    L, D, C = DEPTH, D_MODEL, RWKV_WIDTH
    nrm = lambda shape, scale: jax.random.normal(next(ks), shape, f32) * scale
    uni = lambda shape, lo, hi: jax.random.uniform(next(ks), shape, f32, lo, hi)
    gain = lambda shape: 1.0 + nrm(shape, 0.02)
    return {
        'x': nrm((BATCH, SEQ, D), 1.0),
        'mem': nrm((BATCH, MEM_LEN, D), 1.0),
        'f1_norm': gain((L, D)),
        'f1_gate': nrm((L, D, D_FF), D ** -0.5),
        'f1_up': nrm((L, D, D_FF), D ** -0.5),
        'f1_down': nrm((L, D_FF, D), D_FF ** -0.5),
        'mix_norm': gain((L, D)),
        'w_in': nrm((L, D, IN_COLS), D ** -0.5),
        'b_in_attn': nrm((L, SWA_COLS), 0.02),
        'rw_mu': uni((L, SHIFT_COLS), 0.0, 1.0),
        'rw_w0': uni((L, C), -6.0, -1.0),
        'rw_decay_up': nrm((L, DECAY_LORA, C), 0.1),
        'rw_a0': nrm((L, C), 0.1),
        'rw_aaa_up': nrm((L, AAA_LORA, C), 0.5 * AAA_LORA ** -0.5),
        'rw_gate_up': nrm((L, GATE_LORA, C), GATE_LORA ** -0.5),
        'rw_k_k': 0.85 + nrm((L, C), 0.02),
        'rw_k_a': 1.0 + nrm((L, C), 0.02),
        'rw_r_k': nrm((L, RWKV_HEADS, HEAD_DIM), 0.1),
        'rw_lnx_w': gain((L, C)),
        'rw_lnx_b': nrm((L, C), 0.02),
        'attn_sinks': nrm((L, SWA_Q_HEADS), 0.5),
        'w_out': nrm((L, D, D), D ** -0.5),
        'b_out': nrm((L, D), 0.02),
        'xa_norm': gain((L, D)),
        'mem_norm': gain((L, D)),
        'w_xq': nrm((L, D, D), D ** -0.5),
        'w_xkv': nrm((L, D, 2 * D), D ** -0.5),
        'w_xo': nrm((L, D, D), D ** -0.5),
        'f2_norm': gain((L, D)),
        'f2_gate': nrm((L, D, D_FF), D ** -0.5),
        'f2_up': nrm((L, D, D_FF), D ** -0.5),
        'f2_down': nrm((L, D_FF, D), D_FF ** -0.5),
        'final_norm': gain((D,)),
    }


def reference(x, mem, f1_norm, f1_gate, f1_up, f1_down, mix_norm, w_in, b_in_attn, rw_mu, rw_w0,
              rw_decay_up, rw_a0, rw_aaa_up, rw_gate_up, rw_k_k, rw_k_a, rw_r_k, rw_lnx_w, rw_lnx_b,
              attn_sinks, w_out, b_out, xa_norm, mem_norm, w_xq, w_xkv, w_xo, f2_norm, f2_gate, f2_up,
              f2_down, final_norm):
    B, T, _ = x.shape
    pos = jnp.arange(T, dtype=jnp.float32)
    kv_w = SWA_KV_HEADS * HEAD_DIM
    for l in range(DEPTH):
        x = x + 0.5 * _swiglu(_rmsnorm(x, f1_norm[l]), f1_gate[l], f1_up[l], f1_down[l])
        h = _rmsnorm(x, mix_norm[l])
        proj = h @ w_in[l]
        zr = proj[..., :SHIFT_COLS]
        zr_prev = jnp.pad(zr[:, :-1], ((0, 0), (1, 0), (0, 0)))
        zr = zr + (zr_prev - zr) * rw_mu[l]
        za = proj[..., SHIFT_COLS:] + b_in_attn[l]
        q = za[..., :SWA_WIDTH].reshape(B, T, SWA_Q_HEADS, HEAD_DIM)
        k = za[..., SWA_WIDTH:SWA_WIDTH + kv_w].reshape(B, T, SWA_KV_HEADS, HEAD_DIM)
        v = za[..., SWA_WIDTH + kv_w:].reshape(B, T, SWA_KV_HEADS, HEAD_DIM)
        y_rwkv = _rwkv7_time_mix(zr, rw_w0[l], rw_decay_up[l], rw_a0[l], rw_aaa_up[l], rw_gate_up[l],
                                 rw_k_k[l], rw_k_a[l], rw_r_k[l], rw_lnx_w[l], rw_lnx_b[l])
        y_swa = _swa_gqa_sinks(_rope(q, pos), _rope(k, pos), v, attn_sinks[l])
        x = x + jnp.concatenate([y_rwkv, y_swa], axis=-1) @ w_out[l] + b_out[l]
        x = x + _memory_cross_attn(_rmsnorm(x, xa_norm[l]), _rmsnorm(mem, mem_norm[l]), w_xq[l], w_xkv[l], w_xo[l])
        x = x + 0.5 * _swiglu(_rmsnorm(x, f2_norm[l]), f2_gate[l], f2_up[l], f2_down[l])
    return _rmsnorm(x, final_norm)
```

```python
import functools

import jax
import jax.numpy as jnp
from jax import lax
from jax.experimental import pallas as pl
from jax.experimental.pallas import tpu as pltpu

F32, BF16 = jnp.float32, jnp.bfloat16

LANES = 128
SUBLANES = 8
VMEM_LIMIT_BYTES = 56 * 1024 * 1024

HEAD_DIM = 64
HEADS_PER_TILE = LANES // HEAD_DIM
CHUNK = 128
BLOCK = 128
XATTN_HEADS = 4
RMS_EPS = 1e-6
GN_EPS = 64e-5
NEG_INF = -1e30
ROPE_THETA = 10000.0


def _round_up(n, m):
    return (n + m - 1) // m * m


def _pick(n, prefs):
    for p in prefs:
        if n % p == 0:
            return p
    raise ValueError(f"no tile in {prefs} divides {n}")


def _cparams(n_axes):
    return pltpu.CompilerParams(dimension_semantics=("arbitrary",) * n_axes,
                                vmem_limit_bytes=VMEM_LIMIT_BYTES)


def _dot(a, b):
    return jnp.dot(a, b, preferred_element_type=F32)


def _dot_nt(a, b):
    return lax.dot_general(a, b, (((1,), (1,)), ((), ())), preferred_element_type=F32)


def _split(x):
    hi = x.astype(BF16)
    lo = (x - hi.astype(F32)).astype(BF16)
    return hi, lo


def _dot3(a, b):
    ah, al = _split(a)
    bh, bl = _split(b)
    return _dot(ah, bh) + _dot(ah, bl) + _dot(al, bh)


def _dot3_nt(a, b):
    ah, al = _split(a)
    bh, bl = _split(b)
    return _dot_nt(ah, bh) + _dot_nt(ah, bl) + _dot_nt(al, bh)


def _dot3_w(a, wh, wl):
    ah, al = _split(a)
    return _dot(ah, wh) + _dot(ah, wl) + _dot(al, wh)


def _rms(x, g):
    ms = jnp.mean(x * x, axis=-1, keepdims=True)
    return x * lax.rsqrt(ms + RMS_EPS) * g


def _softplus(x):
    return jnp.maximum(x, 0.0) + jnp.log(1.0 + jnp.exp(-jnp.abs(x)))


def _ffn_kernel(x_ref, g_ref, wg_ref, wu_ref, wd_ref, fg_ref, o_ref, h_ref, *, final_norm):
    j = pl.program_id(1)

    @pl.when(j == 0)
    def _():
        h_ref[...] = _rms(x_ref[...], g_ref[...]).astype(BF16)

    h = h_ref[...]
    gate = _dot(h, wg_ref[...])
    up = _dot(h, wu_ref[...])
    act = (gate * jax.nn.sigmoid(gate) * up).astype(BF16)
    part = _dot(act, wd_ref[...])

    @pl.when(j == 0)
    def _():
        o_ref[...] = part

    @pl.when(j > 0)
    def _():
        o_ref[...] += part

    @pl.when(j == pl.num_programs(1) - 1)
    def _():
        y = x_ref[...] + 0.5 * o_ref[...]
        if final_norm:
            y = _rms(y, fg_ref[...])
        o_ref[...] = y


def _ffn(x, g, wg, wu, wd, fg):
    n, d = x.shape
    f = wg.shape[1]
    tm = _pick(n, (512, 256, 128))
    tf = _pick(f, (512, 256, 128))
    final_norm = fg is not None
    fg = g if fg is None else fg
    return pl.pallas_call(
        functools.partial(_ffn_kernel, final_norm=final_norm),
        grid=(n // tm, f // tf),
        in_specs=[
            pl.BlockSpec((tm, d), lambda i, j: (i, 0)),
            pl.BlockSpec((1, d), lambda i, j: (0, 0)),
            pl.BlockSpec((d, tf), lambda i, j: (0, j)),
            pl.BlockSpec((d, tf), lambda i, j: (0, j)),
            pl.BlockSpec((tf, d), lambda i, j: (j, 0)),
            pl.BlockSpec((1, d), lambda i, j: (0, 0)),
        ],
        out_specs=pl.BlockSpec((tm, d), lambda i, j: (i, 0)),
        out_shape=jax.ShapeDtypeStruct((n, d), F32),
        scratch_shapes=[pltpu.VMEM((tm, d), BF16)],
        compiler_params=_cparams(2),
        name="ffn",
    )(x, g.reshape(1, d), wg, wu, wd, fg.reshape(1, d))


def _proj_kernel(x_ref, g_ref, w_ref, mu_ref, b_ref, o_ref, h_ref, carry_ref, *, tiles_per_seq):
    i = pl.program_id(0)
    j = pl.program_id(1)
    tm = x_ref.shape[0]

    @pl.when(j == 0)
    def _():
        h_ref[...] = _rms(x_ref[...], g_ref[...]).astype(BF16)

    @pl.when(i % tiles_per_seq == 0)
    def _():
        carry_ref[j] = jnp.zeros(carry_ref.shape[1:], F32)

    z = _dot(h_ref[...], w_ref[...])
    prev_tail = carry_ref[j]
    carry_ref[j] = z[tm - SUBLANES:, :]
    zs = pltpu.roll(z, 1, 0)
    row = lax.broadcasted_iota(jnp.int32, prev_tail.shape, 0)
    head = jnp.where(row == 0, pltpu.roll(prev_tail, 1, 0), zs[:SUBLANES])
    mu = mu_ref[...]
    b = b_ref[...]
    z0 = z[:SUBLANES]
    o_ref[:SUBLANES, :] = z0 + (head - z0) * mu + b
    z1 = z[SUBLANES:]
    o_ref[SUBLANES:, :] = z1 + (zs[SUBLANES:] - z1) * mu + b


def _proj(x, g, w_all, mu_all, b_all, seq):
    n, d = x.shape
    ncols = w_all.shape[1]
    tm = _pick(seq, (512, 256, 128))
    tn = _pick(ncols, (1024, 512, 256, 128))
    return pl.pallas_call(
        functools.partial(_proj_kernel, tiles_per_seq=seq // tm),
        grid=(n // tm, ncols // tn),
        in_specs=[
            pl.BlockSpec((tm, d), lambda i, j: (i, 0)),
            pl.BlockSpec((1, d), lambda i, j: (0, 0)),
            pl.BlockSpec((d, tn), lambda i, j: (0, j)),
            pl.BlockSpec((1, tn), lambda i, j: (0, j)),
            pl.BlockSpec((1, tn), lambda i, j: (0, j)),
        ],
        out_specs=pl.BlockSpec((tm, tn), lambda i, j: (i, j)),
        out_shape=jax.ShapeDtypeStruct((n, ncols), F32),
        scratch_shapes=[pltpu.VMEM((tm, d), BF16), pltpu.VMEM((ncols // tn, SUBLANES, tn), F32)],
        compiler_params=_cparams(2),
        name="proj",
    )(x, g.reshape(1, d), w_all, mu_all, b_all)


def _inv_unit_lower(nmat, eye):
    t = eye + nmat
    pw = nmat
    steps = CHUNK.bit_length() - 2
    for _ in range(steps):
        pw = _dot3(pw, pw)
        t = t + _dot3(t, pw)
    return t


def _rwkv_kernel(r_ref, k_ref, v_ref, lora_ref, w0_ref, a0_ref, kk_ref, ka_ref, rk_ref, gw_ref, gb_ref,
                 duh_ref, dul_ref, auh_ref, aul_ref, guh_ref, gul_ref,
                 o_ref,
                 s_ref, pr_ref, pk_ref, pv_ref, pa_ref, pg_ref, plp_ref, plw_ref, po_ref,
                 *, n_tiles, dlp, alp):
    L = CHUNK

    @pl.when(pl.program_id(1) == 0)
    def _():
        s_ref[...] = jnp.zeros(s_ref.shape, F32)

    lora = lora_ref[...]
    wd = jnp.tanh(lora[:, :dlp])
    ad = lora[:, dlp:dlp + alp]
    gd = jax.nn.sigmoid(lora[:, dlp + alp:])
    w = -_softplus(-(w0_ref[...] + _dot3_w(wd, duh_ref[...], dul_ref[...]))) - 0.5
    lw = -jnp.exp(w)
    asig = jax.nn.sigmoid(a0_ref[...] + _dot3_w(ad, auh_ref[...], aul_ref[...]))
    gate = _dot3_w(gd, guh_ref[...], gul_ref[...])
    row = lax.broadcasted_iota(jnp.int32, (L, L), 0)
    col = lax.broadcasted_iota(jnp.int32, (L, L), 1)
    incl = row >= col
    strict = row > col
    tril = jnp.where(incl, 1.0, 0.0).astype(BF16)
    h1 = lw.astype(BF16)
    r1 = lw - h1.astype(F32)
    h2 = r1.astype(BF16)
    h3 = (r1 - h2.astype(F32)).astype(BF16)
    logp = _dot(tril, h1) + _dot(tril, h2) + _dot(tril, h3)
    r_all = r_ref[...]
    k_all = k_ref[...]
    v_all = v_ref[...]
    for p in range(n_tiles):
        sl = slice(p * LANES, (p + 1) * LANES)
        pr_ref[p] = r_all[:, sl]
        pk_ref[p] = k_all[:, sl]
        pv_ref[p] = v_all[:, sl]
        pa_ref[p] = asig[:, sl]
        pg_ref[p] = gate[:, sl]
        plp_ref[p] = logp[:, sl]
        plw_ref[p] = lw[:, sl]

    lane = lax.broadcasted_iota(jnp.int32, (1, LANES), 1)
    lo = lane < HEAD_DIM
    eye = jnp.where(row == col, 1.0, 0.0)
    same_head = jnp.where(row < HEAD_DIM, 0, 1) == jnp.where(col < HEAD_DIM, 0, 1)
    ones_bd = jnp.where(same_head, 1.0, 0.0).astype(BF16)
    inv_hd = 1.0 / HEAD_DIM

    def segsum(x):
        xh, xl = _split(x)
        return _dot(xh, ones_bd) + _dot(xl, ones_bd)

    def first(x):
        return jnp.where(lo, x, 0.0)

    def second(x):
        return jnp.where(lo, 0.0, x)

    def tile_step(p, carry):
        r = pr_ref[p]
        k = pk_ref[p]
        v = pv_ref[p]
        a_s = pa_ref[p]
        lp = plp_ref[p]
        lwp = plw_ref[p]
        kk = k * kk_ref[p]
        ss = segsum(kk * kk)
        kk = kk / jnp.maximum(jnp.sqrt(ss), 1e-12)
        a = -kk
        b = kk * a_s
        km = k * (1.0 + (a_s - 1.0) * ka_ref[p])
        cmid = lp[L // 2 - 1:L // 2, :]
        clast = lp[L - 1:L, :]
        lpe = lp - lwp
        e_inv = jnp.exp(cmid - lp)
        at = a * jnp.exp(lpe - cmid)
        a_abs = a * jnp.exp(lpe)
        rt = r * jnp.exp(lp - cmid)
        r_abs = r * jnp.exp(lp)
        bt = b * e_inv
        kt = km * e_inv
        e_l = jnp.exp(clast - cmid)
        e_p = jnp.exp(clast)
        lhs4 = jnp.concatenate([first(at), second(at), first(rt), second(rt)], axis=0)
        gmat = _dot3_nt(lhs4, jnp.concatenate([bt, kt], axis=0))
        n_lo = jnp.where(strict, gmat[0:L, 0:L], 0.0)
        ak_lo = jnp.where(strict, gmat[0:L, L:], 0.0)
        n_hi = jnp.where(strict, gmat[L:2 * L, 0:L], 0.0)
        ak_hi = jnp.where(strict, gmat[L:2 * L, L:], 0.0)
        rb_lo = jnp.where(incl, gmat[2 * L:3 * L, 0:L], 0.0)
        rk_lo = jnp.where(incl, gmat[2 * L:3 * L, L:], 0.0)
        rb_hi = jnp.where(incl, gmat[3 * L:, 0:L], 0.0)
        rk_hi = jnp.where(incl, gmat[3 * L:, L:], 0.0)
        t_lo = _inv_unit_lower(n_lo, eye)
        t_hi = _inv_unit_lower(n_hi, eye)
        s = s_ref[p]
        v_lo, v_hi = first(v), second(v)
        rhs = _dot3_nt(a_abs, s) + _dot3(jnp.concatenate([ak_lo, ak_hi], axis=1),
                                         jnp.concatenate([v_lo, v_hi], axis=0))
        u = _dot3(jnp.concatenate([t_lo, t_hi], axis=1),
                  jnp.concatenate([first(rhs), second(rhs)], axis=0))
        y = _dot3_nt(r_abs, s) + _dot3(jnp.concatenate([rb_lo, rb_hi, rk_lo, rk_hi], axis=1),
                                       jnp.concatenate([first(u), second(u), v_lo, v_hi], axis=0))
        upd = _dot3(jnp.concatenate([u.T, v.T], axis=1), jnp.concatenate([bt * e_l, kt * e_l], axis=0))
        s_ref[p] = s * e_p + jnp.where(same_head, upd, 0.0)
        mean = segsum(y) * inv_hd
        dev = y - mean
        var = segsum(dev * dev) * inv_hd
        yn = dev * lax.rsqrt(var + GN_EPS) * gw_ref[p] + gb_ref[p]
        bonus = segsum(r * km * rk_ref[p]) * v
        po_ref[p] = ((yn + bonus) * pg_ref[p]).astype(BF16)
        return carry

    lax.fori_loop(0, n_tiles, tile_step, 0)
    for p in range(n_tiles):
        o_ref[:, p * LANES:(p + 1) * LANES] = po_ref[p]


def _rwkv(z, batch, seq, c, lora_off, lora_w, dlp, alp, vecs, loras):
    n = z.shape[0]
    nc = seq // CHUNK
    n_tiles = c // LANES
    assert lora_off % lora_w == 0
    lora_blk = lora_off // lora_w
    vec_spec = pl.BlockSpec((n_tiles, 1, LANES), lambda b, t: (0, 0, 0))
    in_specs = [
        pl.BlockSpec((CHUNK, c), lambda b, t: (b * nc + t, 0)),
        pl.BlockSpec((CHUNK, c), lambda b, t: (b * nc + t, 1)),
        pl.BlockSpec((CHUNK, c), lambda b, t: (b * nc + t, 2)),
        pl.BlockSpec((CHUNK, lora_w), lambda b, t: (b * nc + t, lora_blk)),
        pl.BlockSpec((1, c), lambda b, t: (0, 0)),
        pl.BlockSpec((1, c), lambda b, t: (0, 0)),
    ] + [vec_spec] * 5 + [pl.BlockSpec(w.shape, lambda b, t: (0, 0)) for w in loras]
    tile_f32 = pltpu.VMEM((n_tiles, CHUNK, LANES), F32)
    return pl.pallas_call(
        functools.partial(_rwkv_kernel, n_tiles=n_tiles, dlp=dlp, alp=alp),
        grid=(batch, nc),
        in_specs=in_specs,
        out_specs=pl.BlockSpec((CHUNK, c), lambda b, t: (b * nc + t, 0)),
        out_shape=jax.ShapeDtypeStruct((n, c), BF16),
        scratch_shapes=[pltpu.VMEM((n_tiles, LANES, LANES), F32)] + [tile_f32] * 7
        + [pltpu.VMEM((n_tiles, CHUNK, LANES), BF16)],
        compiler_params=_cparams(2),
        name="rwkv",
    )(z, z, z, z, *vecs, *loras)


def _swa_kernel(sink_ref, q_ref, kvc_ref, kvp_ref, cosc_ref, sinc_ref, cosp_ref, sinp_ref, o_ref,
                *, n_tiles, group):
    nblk = pl.program_id(1)
    lane = lax.broadcasted_iota(jnp.int32, (1, LANES), 1)
    lo = lane < HEAD_DIM
    rot_lo = jnp.bitwise_and(lane, HEAD_DIM - 1) < HEAD_DIM // 2

    def rope(x, cos, sin_signed):
        partner = jnp.where(rot_lo, pltpu.roll(x, LANES - HEAD_DIM // 2, 1), pltpu.roll(x, HEAD_DIM // 2, 1))
        return x * cos + partner * sin_signed

    cosc = cosc_ref[...]
    sinc = sinc_ref[...]
    kvc = kvc_ref[...]
    kvp = kvp_ref[...]
    keys = jnp.concatenate([rope(kvp[:, :LANES], cosp_ref[...], sinp_ref[...]),
                            rope(kvc[:, :LANES], cosc, sinc)], axis=0)
    vals = jnp.concatenate([kvp[:, LANES:], kvc[:, LANES:]], axis=0)
    keys_sw = pltpu.roll(keys, HEAD_DIM, 1)
    vals_sw = pltpu.roll(vals, HEAD_DIM, 1)
    k_first = [jnp.where(lo, keys, 0.0).astype(BF16), jnp.where(lo, keys_sw, 0.0).astype(BF16)]
    k_second = [jnp.where(lo, 0.0, keys_sw).astype(BF16), jnp.where(lo, 0.0, keys).astype(BF16)]
    v_first = [jnp.where(lo, vals, 0.0).astype(BF16), jnp.where(lo, vals_sw, 0.0).astype(BF16)]
    v_second = [jnp.where(lo, 0.0, vals_sw).astype(BF16), jnp.where(lo, 0.0, vals).astype(BF16)]
    qi = lax.broadcasted_iota(jnp.int32, (BLOCK, 2 * BLOCK), 0)
    ki = lax.broadcasted_iota(jnp.int32, (BLOCK, 2 * BLOCK), 1)
    valid = (ki > qi) & (ki <= qi + BLOCK) & ((nblk > 0) | (ki >= BLOCK))
    scale = HEAD_DIM ** -0.5
    for p in range(n_tiles):
        g = (p * HEADS_PER_TILE) // group
        qp = (rope(q_ref[:, p * LANES:(p + 1) * LANES], cosc, sinc) * scale).astype(BF16)
        acc = None
        for half, (kmat, vmat) in enumerate(((k_first[g], v_first[g]), (k_second[g], v_second[g]))):
            s = jnp.where(valid, _dot_nt(qp, kmat), NEG_INF)
            sink = sink_ref[p * HEADS_PER_TILE + half]
            m = jnp.maximum(jnp.max(s, axis=-1, keepdims=True), sink)
            e = jnp.exp(s - m)
            den = jnp.sum(e, axis=-1, keepdims=True) + jnp.exp(sink - m)
            o = _dot(e.astype(BF16), vmat) * (1.0 / den)
            acc = o if acc is None else acc + o
        o_ref[:, p * LANES:(p + 1) * LANES] = acc.astype(BF16)


def _swa(z, sinks, cos_t, sin_t, batch, seq, q_off, qw, kv_off, kvw, group):
    n = z.shape[0]
    nb = seq // BLOCK
    assert q_off % qw == 0 and kv_off % kvw == 0 and kvw == 2 * LANES
    q_blk, kv_blk = q_off // qw, kv_off // kvw
    prev = lambda t: jnp.maximum(t - 1, 0)
    return pl.pallas_call(
        functools.partial(_swa_kernel, n_tiles=qw // LANES, group=group),
        grid=(batch, nb),
        in_specs=[
            pl.BlockSpec(memory_space=pltpu.SMEM),
            pl.BlockSpec((BLOCK, qw), lambda b, t: (b * nb + t, q_blk)),
            pl.BlockSpec((BLOCK, kvw), lambda b, t: (b * nb + t, kv_blk)),
            pl.BlockSpec((BLOCK, kvw), lambda b, t: (b * nb + prev(t), kv_blk)),
            pl.BlockSpec((BLOCK, LANES), lambda b, t: (t, 0)),
            pl.BlockSpec((BLOCK, LANES), lambda b, t: (t, 0)),
            pl.BlockSpec((BLOCK, LANES), lambda b, t: (prev(t), 0)),
            pl.BlockSpec((BLOCK, LANES), lambda b, t: (prev(t), 0)),
        ],
        out_specs=pl.BlockSpec((BLOCK, qw), lambda b, t: (b * nb + t, 0)),
        out_shape=jax.ShapeDtypeStruct((n, qw), BF16),
        compiler_params=_cparams(2),
        name="swa",
    )(sinks, z, z, z, cos_t, sin_t, cos_t, sin_t)


def _memkv_kernel(m_ref, g_ref, w_ref, o_ref):
    h = _rms(m_ref[...], g_ref[...]).astype(BF16)
    o_ref[...] = _dot(h, w_ref[...]).astype(BF16)


def _memkv(mem, g, w):
    n, d = mem.shape
    ncols = w.shape[1]
    tm = _pick(n, (512, 256, 128))
    tn = _pick(ncols, (1024, 512, 256, 128))
    return pl.pallas_call(
        _memkv_kernel,
        grid=(ncols // tn, n // tm),
        in_specs=[
            pl.BlockSpec((tm, d), lambda j, i: (i, 0)),
            pl.BlockSpec((1, d), lambda j, i: (0, 0)),
            pl.BlockSpec((d, tn), lambda j, i: (0, j)),
        ],
        out_specs=pl.BlockSpec((tm, tn), lambda j, i: (i, j)),
        out_shape=jax.ShapeDtypeStruct((n, ncols), BF16),
        compiler_params=_cparams(2),
        name="memkv",
    )(mem, g.reshape(1, d), w)


def _mix_kernel(x_ref, yr_ref, ys_ref, wo_ref, bo_ref, g_ref, wq_ref, k_ref, v_ref, wxo_ref, o_ref):
    c = yr_ref.shape[1]
    d = x_ref.shape[1]
    hd = d // XATTN_HEADS
    x2 = x_ref[...] + _dot(yr_ref[...], wo_ref[:c, :]) + _dot(ys_ref[...], wo_ref[c:, :]) + bo_ref[...]
    q = _dot(_rms(x2, g_ref[...]).astype(BF16), wq_ref[...]).astype(BF16)
    scale = hd ** -0.5
    outs = []
    for h in range(XATTN_HEADS):
        sl = slice(h * hd, (h + 1) * hd)
        s = _dot_nt(q[:, sl], k_ref[:, sl]) * scale
        m = jnp.max(s, axis=-1, keepdims=True)
        e = jnp.exp(s - m)
        den = jnp.sum(e, axis=-1, keepdims=True)
        outs.append((_dot(e.astype(BF16), v_ref[:, sl]) * (1.0 / den)).astype(BF16))
    o_ref[...] = x2 + _dot(jnp.concatenate(outs, axis=1), wxo_ref[...])


def _mix(x, yr, ys, wo, bo, g, wq, kv, wxo, seq, mlen):
    n, d = x.shape
    c = yr.shape[1]
    tm = _pick(seq, (256, 128))
    per_seq = seq // tm
    once = pl.Buffered(1)
    return pl.pallas_call(
        _mix_kernel,
        grid=(n // tm,),
        in_specs=[
            pl.BlockSpec((tm, d), lambda i: (i, 0)),
            pl.BlockSpec((tm, c), lambda i: (i, 0)),
            pl.BlockSpec((tm, d - c), lambda i: (i, 0)),
            pl.BlockSpec((d, d), lambda i: (0, 0), pipeline_mode=once),
            pl.BlockSpec((1, d), lambda i: (0, 0)),
            pl.BlockSpec((1, d), lambda i: (0, 0)),
            pl.BlockSpec((d, d), lambda i: (0, 0), pipeline_mode=once),
            pl.BlockSpec((mlen, d), lambda i: (i // per_seq, 0)),
            pl.BlockSpec((mlen, d), lambda i: (i // per_seq, 1)),
            pl.BlockSpec((d, d), lambda i: (0, 0), pipeline_mode=once),
        ],
        out_specs=pl.BlockSpec((tm, d), lambda i: (i, 0)),
        out_shape=jax.ShapeDtypeStruct((n, d), F32),
        compiler_params=_cparams(1),
        name="mix",
    )(x, yr, ys, wo, bo.reshape(1, d), g.reshape(1, d), wq, kv, kv, wxo)


def _pad_cols(w, width):
    return jnp.pad(w, ((0, 0), (0, width - w.shape[1])))


def _pad_rows(w, height):
    return jnp.pad(w, ((0, height - w.shape[0]), (0, 0)))


def _hi_lo(w):
    hi = w.astype(BF16)
    return hi, (w - hi.astype(F32)).astype(BF16)


def _rope_tables(seq):
    half = HEAD_DIM // 2
    lane = jnp.arange(LANES)
    inv_freq = ROPE_THETA ** (-jnp.arange(0, HEAD_DIM, 2, dtype=F32) / HEAD_DIM)
    ang = jnp.arange(seq, dtype=F32)[:, None] * inv_freq[lane % half][None, :]
    sign = jnp.where((lane % HEAD_DIM) < half, -1.0, 1.0)
    return jnp.cos(ang), jnp.sin(ang) * sign[None, :]


def kernel(x, mem, f1_norm, f1_gate, f1_up, f1_down, mix_norm, w_in, b_in_attn, rw_mu, rw_w0, rw_decay_up, rw_a0, rw_aaa_up, rw_gate_up, rw_k_k, rw_k_a, rw_r_k, rw_lnx_w, rw_lnx_b, attn_sinks, w_out, b_out, xa_norm, mem_norm, w_xq, w_xkv, w_xo, f2_norm, f2_gate, f2_up, f2_down, final_norm):
    batch, seq, d = x.shape
    mlen = mem.shape[1]
    depth = f1_norm.shape[0]
    c = rw_w0.shape[1]
    sw = d - c
    dl, al, gl = rw_decay_up.shape[1], rw_aaa_up.shape[1], rw_gate_up.shape[1]
    dlp, alp, glp = (_round_up(v, LANES) for v in (dl, al, gl))
    kvw = b_in_attn.shape[1] - sw
    q_heads = sw // HEAD_DIM
    group = q_heads // (kvw // (2 * HEAD_DIM))
    n_tiles = c // LANES
    n = batch * seq
    q_off = 3 * c
    lora_off = q_off + sw
    lora_w = dlp + alp + glp
    kv_off = lora_off + lora_w
    ncols = _round_up(kv_off + kvw, 1024)
    cos_t, sin_t = _rope_tables(seq)

    xf = x.reshape(n, d)
    memf = mem.reshape(batch * mlen, d)
    for l in range(depth):
        xf = _ffn(xf, f1_norm[l], f1_gate[l].astype(BF16), f1_up[l].astype(BF16), f1_down[l].astype(BF16), None)

        wl = w_in[l]
        o1, o2, o3 = 3 * c, 3 * c + dl, 3 * c + dl + al
        shift = o3 + gl
        w_all = jnp.concatenate([
            wl[:, :o1], wl[:, shift:shift + sw],
            _pad_cols(wl[:, o1:o2], dlp), _pad_cols(wl[:, o2:o3], alp), _pad_cols(wl[:, o3:shift], glp),
            wl[:, shift + sw:]], axis=1)
        w_all = _pad_cols(w_all, ncols).astype(BF16)
        mu = rw_mu[l][None, :]
        mu_all = _pad_cols(jnp.concatenate([
            mu[:, :o1], jnp.zeros((1, sw), F32),
            _pad_cols(mu[:, o1:o2], dlp), _pad_cols(mu[:, o2:o3], alp), _pad_cols(mu[:, o3:shift], glp)],
            axis=1), ncols)
        bia = b_in_attn[l][None, :]
        b_all = _pad_cols(jnp.concatenate([
            jnp.zeros((1, q_off), F32), bia[:, :sw], jnp.zeros((1, lora_w), F32), bia[:, sw:]], axis=1), ncols)
        z = _proj(xf, mix_norm[l], w_all, mu_all, b_all, seq)

        tiles = lambda v: v.reshape(n_tiles, 1, LANES)
        vecs = [rw_w0[l].reshape(1, c), rw_a0[l].reshape(1, c), tiles(rw_k_k[l]), tiles(rw_k_a[l]),
                tiles(rw_r_k[l]), tiles(rw_lnx_w[l]), tiles(rw_lnx_b[l])]
        loras = [*_hi_lo(_pad_rows(rw_decay_up[l], dlp)), *_hi_lo(_pad_rows(rw_aaa_up[l], alp)),
                 *_hi_lo(_pad_rows(rw_gate_up[l], glp))]
        y_rwkv = _rwkv(z, batch, seq, c, lora_off, lora_w, dlp, alp, vecs, loras)
        y_swa = _swa(z, attn_sinks[l], cos_t, sin_t, batch, seq, q_off, sw, kv_off, kvw, group)

        kv_mem = _memkv(memf, mem_norm[l], w_xkv[l].astype(BF16))
        xf = _mix(xf, y_rwkv, y_swa, w_out[l].astype(BF16), b_out[l], xa_norm[l], w_xq[l].astype(BF16),
                  kv_mem, w_xo[l].astype(BF16), seq, mlen)

        last = l == depth - 1
        xf = _ffn(xf, f2_norm[l], f2_gate[l].astype(BF16), f2_up[l].astype(BF16), f2_down[l].astype(BF16),
                  final_norm if last else None)
    return xf.reshape(batch, seq, d)
```

```python
import functools

import jax
import jax.numpy as jnp
from jax import lax
from jax.experimental import pallas as pl
from jax.experimental.pallas import tpu as pltpu

F32, BF16 = jnp.float32, jnp.bfloat16

LANES = 128
SUBLANES = 8
VMEM_LIMIT_BYTES = 56 * 1024 * 1024

HEAD_DIM = 64
HEADS_PER_TILE = LANES // HEAD_DIM
CHUNK = 128
BLOCK = 128
XATTN_HEADS = 4
RMS_EPS = 1e-6
GN_EPS = 64e-5
NEG_INF = -1e30
ROPE_THETA = 10000.0


def _round_up(n, m):
    return (n + m - 1) // m * m


def _pick(n, prefs):
    for p in prefs:
        if n % p == 0:
            return p
    raise ValueError(f"no tile in {prefs} divides {n}")


def _cparams(n_axes):
    return pltpu.CompilerParams(dimension_semantics=("arbitrary",) * n_axes,
                                vmem_limit_bytes=VMEM_LIMIT_BYTES)


def _dot(a, b):
    return jnp.dot(a, b, preferred_element_type=F32)


def _dot_nt(a, b):
    return lax.dot_general(a, b, (((1,), (1,)), ((), ())), preferred_element_type=F32)


def _split(x):
    hi = x.astype(BF16)
    lo = (x - hi.astype(F32)).astype(BF16)
    return hi, lo


def _dot3(a, b):
    ah, al = _split(a)
    bh, bl = _split(b)
    return _dot(ah, bh) + _dot(ah, bl) + _dot(al, bh)


def _dot3_nt(a, b):
    ah, al = _split(a)
    bh, bl = _split(b)
    return _dot_nt(ah, bh) + _dot_nt(ah, bl) + _dot_nt(al, bh)


def _dot3_w(a, wh, wl):
    ah, al = _split(a)
    return _dot(ah, wh) + _dot(ah, wl) + _dot(al, wh)


def _rms(x, g):
    ms = jnp.mean(x * x, axis=-1, keepdims=True)
    return x * lax.rsqrt(ms + RMS_EPS) * g


def _softplus(x):
    return jnp.maximum(x, 0.0) + jnp.log(1.0 + jnp.exp(-jnp.abs(x)))


def _ffn_kernel(x_ref, g_ref, wg_ref, wu_ref, wd_ref, fg_ref, o_ref, h_ref, acc_ref, *, final_norm):
    i = pl.program_id(0)
    j = pl.program_id(1)

    @pl.when(j == 0)
    def _():
        h_ref[...] = _rms(x_ref[...], g_ref[...]).astype(BF16)

    @pl.when((i == 0) & (j == 0))
    def _():
        acc_ref[...] = jnp.zeros(acc_ref.shape, F32)

    h = h_ref[...]
    gate = _dot(h, wg_ref[...])
    up = _dot(h, wu_ref[...])
    act = (gate * jax.nn.sigmoid(gate) * up).astype(BF16)
    part = _dot(act, wd_ref[...])
    acc_ref[...] = jnp.where(j == 0, x_ref[...], acc_ref[...]) + 0.5 * part

    @pl.when(j == pl.num_programs(1) - 1)
    def _():
        y = acc_ref[...]
        if final_norm:
            y = _rms(y, fg_ref[...])
        o_ref[...] = y


def _ffn(x, g, wg, wu, wd, fg):
    n, d = x.shape
    f = wg.shape[1]
    tm = _pick(n, (512, 256, 128))
    tf = _pick(f, (512, 256, 128))
    final_norm = fg is not None
    fg = g if fg is None else fg
    return pl.pallas_call(
        functools.partial(_ffn_kernel, final_norm=final_norm),
        grid=(n // tm, f // tf),
        in_specs=[
            pl.BlockSpec((tm, d), lambda i, j: (i, 0)),
            pl.BlockSpec((1, d), lambda i, j: (0, 0)),
            pl.BlockSpec((d, tf), lambda i, j: (0, j)),
            pl.BlockSpec((d, tf), lambda i, j: (0, j)),
            pl.BlockSpec((tf, d), lambda i, j: (j, 0)),
            pl.BlockSpec((1, d), lambda i, j: (0, 0)),
        ],
        out_specs=pl.BlockSpec((tm, d), lambda i, j: (i, 0)),
        out_shape=jax.ShapeDtypeStruct((n, d), F32),
        scratch_shapes=[pltpu.VMEM((tm, d), BF16), pltpu.VMEM((tm, d), F32)],
        compiler_params=_cparams(2),
        name="ffn",
    )(x, g.reshape(1, d), wg, wu, wd, fg.reshape(1, d))


def _proj_kernel(x_ref, g_ref, w_ref, mu_ref, b_ref, o_ref, h_ref, carry_ref, *, tiles_per_seq):
    i = pl.program_id(0)
    j = pl.program_id(1)
    tm = x_ref.shape[0]

    @pl.when(j == 0)
    def _():
        h_ref[...] = _rms(x_ref[...], g_ref[...]).astype(BF16)

    @pl.when(i % tiles_per_seq == 0)
    def _():
        carry_ref[j] = jnp.zeros(carry_ref.shape[1:], F32)

    z = _dot(h_ref[...], w_ref[...])
    prev_tail = carry_ref[j]
    carry_ref[j] = z[tm - SUBLANES:, :]
    zs = pltpu.roll(z, 1, 0)
    row = lax.broadcasted_iota(jnp.int32, prev_tail.shape, 0)
    head = jnp.where(row == 0, pltpu.roll(prev_tail, 1, 0), zs[:SUBLANES])
    mu = mu_ref[...]
    b = b_ref[...]
    z0 = z[:SUBLANES]
    o_ref[:SUBLANES, :] = z0 + (head - z0) * mu + b
    z1 = z[SUBLANES:]
    o_ref[SUBLANES:, :] = z1 + (zs[SUBLANES:] - z1) * mu + b


def _proj(x, g, w_all, mu_all, b_all, seq):
    n, d = x.shape
    ncols = w_all.shape[1]
    tm = _pick(seq, (512, 256, 128))
    tn = _pick(ncols, (1024, 512, 256, 128))
    return pl.pallas_call(
        functools.partial(_proj_kernel, tiles_per_seq=seq // tm),
        grid=(n // tm, ncols // tn),
        in_specs=[
            pl.BlockSpec((tm, d), lambda i, j: (i, 0)),
            pl.BlockSpec((1, d), lambda i, j: (0, 0)),
            pl.BlockSpec((d, tn), lambda i, j: (0, j)),
            pl.BlockSpec((1, tn), lambda i, j: (0, j)),
            pl.BlockSpec((1, tn), lambda i, j: (0, j)),
        ],
        out_specs=pl.BlockSpec((tm, tn), lambda i, j: (i, j)),
        out_shape=jax.ShapeDtypeStruct((n, ncols), F32),
        scratch_shapes=[pltpu.VMEM((tm, d), BF16), pltpu.VMEM((ncols // tn, SUBLANES, tn), F32)],
        compiler_params=_cparams(2),
        name="proj",
    )(x, g.reshape(1, d), w_all, mu_all, b_all)


def _bdot(a, b):
    return lax.dot_general(a, b, (((2,), (1,)), ((0,), (0,))), preferred_element_type=F32)


def _bdot_nt(a, b):
    return lax.dot_general(a, b, (((2,), (2,)), ((0,), (0,))), preferred_element_type=F32)


def _bdot3(a, b):
    ah, al = _split(a)
    bh, bl = _split(b)
    return _bdot(ah, bh) + _bdot(ah, bl) + _bdot(al, bh)


def _bdot3_nt(a, b):
    ah, al = _split(a)
    bh, bl = _split(b)
    return _bdot_nt(ah, bh) + _bdot_nt(ah, bl) + _bdot_nt(al, bh)


def _inv_unit_lower(nmat, eye):
    t = eye + nmat
    pw = nmat
    steps = CHUNK.bit_length() - 2
    for _ in range(steps):
        pw = _bdot3(pw, pw)
        t = t + _bdot3(t, pw)
    return t


def _rwkv_kernel(r_ref, k_ref, v_ref, lora_ref, w0_ref, a0_ref, kk_ref, ka_ref, rk_ref, gw_ref, gb_ref,
                 duh_ref, dul_ref, auh_ref, aul_ref, guh_ref, gul_ref,
                 o_ref,
                 s_ref, pr_ref, pk_ref, pv_ref, pa_ref, pg_ref, plp_ref, plw_ref,
                 *, n_tiles, dlp, alp):
    L = CHUNK

    @pl.when(pl.program_id(1) == 0)
    def _():
        s_ref[...] = jnp.zeros(s_ref.shape, F32)

    lora = lora_ref[...]
    wd = jnp.tanh(lora[:, :dlp])
    ad = lora[:, dlp:dlp + alp]
    gd = jax.nn.sigmoid(lora[:, dlp + alp:])
    w = -_softplus(-(w0_ref[...] + _dot3_w(wd, duh_ref[...], dul_ref[...]))) - 0.5
    lw = -jnp.exp(w)
    asig = jax.nn.sigmoid(a0_ref[...] + _dot3_w(ad, auh_ref[...], aul_ref[...]))
    gate = _dot3_w(gd, guh_ref[...], gul_ref[...])
    row = lax.broadcasted_iota(jnp.int32, (L, L), 0)
    col = lax.broadcasted_iota(jnp.int32, (L, L), 1)
    incl = row >= col
    strict = row > col
    tril = jnp.where(incl, 1.0, 0.0).astype(BF16)
    h1 = lw.astype(BF16)
    r1 = lw - h1.astype(F32)
    h2 = r1.astype(BF16)
    h3 = (r1 - h2.astype(F32)).astype(BF16)
    logp = _dot(tril, h1) + _dot(tril, h2) + _dot(tril, h3)
    r_all = r_ref[...]
    k_all = k_ref[...]
    v_all = v_ref[...]
    for p in range(n_tiles):
        sl = slice(p * LANES, (p + 1) * LANES)
        pr_ref[p] = r_all[:, sl]
        pk_ref[p] = k_all[:, sl]
        pv_ref[p] = v_all[:, sl]
        pa_ref[p] = asig[:, sl]
        pg_ref[p] = gate[:, sl]
        plp_ref[p] = logp[:, sl]
        plw_ref[p] = lw[:, sl]

    lane = lax.broadcasted_iota(jnp.int32, (1, LANES), 1)
    lo = lane < HEAD_DIM
    eye = jnp.where(row == col, 1.0, 0.0)
    same_head = jnp.where(row < HEAD_DIM, 0, 1) == jnp.where(col < HEAD_DIM, 0, 1)
    ones_bd = jnp.where(same_head, 1.0, 0.0).astype(BF16)
    inv_hd = 1.0 / HEAD_DIM

    def segsum(x):
        xh, xl = _split(x.reshape(n_tiles * L, LANES))
        return (_dot(xh, ones_bd) + _dot(xl, ones_bd)).reshape(n_tiles, L, LANES)

    def first(x):
        return jnp.where(lo, x, 0.0)

    def second(x):
        return jnp.where(lo, 0.0, x)

    r = pr_ref[...]
    k = pk_ref[...]
    v = pv_ref[...]
    a_s = pa_ref[...]
    lp = plp_ref[...]
    kk = k * kk_ref[...]
    ss = segsum(kk * kk)
    kk = kk / jnp.maximum(jnp.sqrt(ss), 1e-12)
    a = -kk
    b = kk * a_s
    km = k * (1.0 + (a_s - 1.0) * ka_ref[...])
    cmid = lp[:, L // 2 - 1:L // 2, :]
    clast = lp[:, L - 1:L, :]
    lpe = lp - plw_ref[...]
    e_inv = jnp.exp(cmid - lp)
    at = a * jnp.exp(lpe - cmid)
    a_abs = a * jnp.exp(lpe)
    rt = r * jnp.exp(lp - cmid)
    r_abs = r * jnp.exp(lp)
    bt = b * e_inv
    kt = km * e_inv
    e_l = jnp.exp(clast - cmid)
    e_p = jnp.exp(clast)
    lhs4 = jnp.concatenate([first(at), second(at), first(rt), second(rt)], axis=1)
    gmat = _bdot3_nt(lhs4, jnp.concatenate([bt, kt], axis=1))
    n_lo = jnp.where(strict, gmat[:, 0:L, 0:L], 0.0)
    ak_lo = jnp.where(strict, gmat[:, 0:L, L:], 0.0)
    n_hi = jnp.where(strict, gmat[:, L:2 * L, 0:L], 0.0)
    ak_hi = jnp.where(strict, gmat[:, L:2 * L, L:], 0.0)
    rb_lo = jnp.where(incl, gmat[:, 2 * L:3 * L, 0:L], 0.0)
    rk_lo = jnp.where(incl, gmat[:, 2 * L:3 * L, L:], 0.0)
    rb_hi = jnp.where(incl, gmat[:, 3 * L:, 0:L], 0.0)
    rk_hi = jnp.where(incl, gmat[:, 3 * L:, L:], 0.0)
    t_all = _inv_unit_lower(jnp.concatenate([n_lo, n_hi], axis=0), eye)
    t_lo, t_hi = t_all[:n_tiles], t_all[n_tiles:]
    s = s_ref[...]
    v_lo, v_hi = first(v), second(v)
    rhs = _bdot3_nt(a_abs, s) + _bdot3(jnp.concatenate([ak_lo, ak_hi], axis=2),
                                       jnp.concatenate([v_lo, v_hi], axis=1))
    u = _bdot3(jnp.concatenate([t_lo, t_hi], axis=2),
               jnp.concatenate([first(rhs), second(rhs)], axis=1))
    y = _bdot3_nt(r_abs, s) + _bdot3(jnp.concatenate([rb_lo, rb_hi, rk_lo, rk_hi], axis=2),
                                     jnp.concatenate([first(u), second(u), v_lo, v_hi], axis=1))
    upd = _bdot3(jnp.concatenate([jnp.swapaxes(u, 1, 2), jnp.swapaxes(v, 1, 2)], axis=2),
                 jnp.concatenate([bt * e_l, kt * e_l], axis=1))
    s_ref[...] = s * e_p + jnp.where(same_head, upd, 0.0)
    mean = segsum(y) * inv_hd
    dev = y - mean
    var = segsum(dev * dev) * inv_hd
    yn = dev * lax.rsqrt(var + GN_EPS) * gw_ref[...] + gb_ref[...]
    bonus = segsum(r * km * rk_ref[...]) * v
    out = ((yn + bonus) * pg_ref[...]).astype(BF16)
    for p in range(n_tiles):
        o_ref[:, p * LANES:(p + 1) * LANES] = out[p]


def _rwkv(z, batch, seq, c, lora_off, lora_w, dlp, alp, vecs, loras):
    n = z.shape[0]
    nc = seq // CHUNK
    n_tiles = c // LANES
    assert lora_off % lora_w == 0
    lora_blk = lora_off // lora_w
    vec_spec = pl.BlockSpec((n_tiles, 1, LANES), lambda b, t: (0, 0, 0))
    in_specs = [
        pl.BlockSpec((CHUNK, c), lambda b, t: (b * nc + t, 0)),
        pl.BlockSpec((CHUNK, c), lambda b, t: (b * nc + t, 1)),
        pl.BlockSpec((CHUNK, c), lambda b, t: (b * nc + t, 2)),
        pl.BlockSpec((CHUNK, lora_w), lambda b, t: (b * nc + t, lora_blk)),
        pl.BlockSpec((1, c), lambda b, t: (0, 0)),
        pl.BlockSpec((1, c), lambda b, t: (0, 0)),
    ] + [vec_spec] * 5 + [pl.BlockSpec(w.shape, lambda b, t: (0, 0)) for w in loras]
    tile_f32 = pltpu.VMEM((n_tiles, CHUNK, LANES), F32)
    return pl.pallas_call(
        functools.partial(_rwkv_kernel, n_tiles=n_tiles, dlp=dlp, alp=alp),
        grid=(batch, nc),
        in_specs=in_specs,
        out_specs=pl.BlockSpec((CHUNK, c), lambda b, t: (b * nc + t, 0)),
        out_shape=jax.ShapeDtypeStruct((n, c), BF16),
        scratch_shapes=[pltpu.VMEM((n_tiles, LANES, LANES), F32)] + [tile_f32] * 7,
        compiler_params=_cparams(2),
        name="rwkv",
    )(z, z, z, z, *vecs, *loras)


def _swa_kernel(sink_ref, q_ref, kvc_ref, kvp_ref, cosc_ref, sinc_ref, cosp_ref, sinp_ref, o_ref,
                *, n_tiles, group):
    nblk = pl.program_id(1)
    lane = lax.broadcasted_iota(jnp.int32, (1, LANES), 1)
    lo = lane < HEAD_DIM
    rot_lo = jnp.bitwise_and(lane, HEAD_DIM - 1) < HEAD_DIM // 2

    def rope(x, cos, sin_signed):
        partner = jnp.where(rot_lo, pltpu.roll(x, LANES - HEAD_DIM // 2, 1), pltpu.roll(x, HEAD_DIM // 2, 1))
        return x * cos + partner * sin_signed

    cosc = cosc_ref[...]
    sinc = sinc_ref[...]
    kvc = kvc_ref[...]
    kvp = kvp_ref[...]
    keys = jnp.concatenate([rope(kvp[:, :LANES], cosp_ref[...], sinp_ref[...]),
                            rope(kvc[:, :LANES], cosc, sinc)], axis=0)
    vals = jnp.concatenate([kvp[:, LANES:], kvc[:, LANES:]], axis=0)
    keys_sw = pltpu.roll(keys, HEAD_DIM, 1)
    vals_sw = pltpu.roll(vals, HEAD_DIM, 1)
    k_first = [jnp.where(lo, keys, 0.0).astype(BF16), jnp.where(lo, keys_sw, 0.0).astype(BF16)]
    k_second = [jnp.where(lo, 0.0, keys_sw).astype(BF16), jnp.where(lo, 0.0, keys).astype(BF16)]
    v_first = [jnp.where(lo, vals, 0.0).astype(BF16), jnp.where(lo, vals_sw, 0.0).astype(BF16)]
    v_second = [jnp.where(lo, 0.0, vals_sw).astype(BF16), jnp.where(lo, 0.0, vals).astype(BF16)]
    qi = lax.broadcasted_iota(jnp.int32, (BLOCK, 2 * BLOCK), 0)
    ki = lax.broadcasted_iota(jnp.int32, (BLOCK, 2 * BLOCK), 1)
    valid = (ki > qi) & (ki <= qi + BLOCK) & ((nblk > 0) | (ki >= BLOCK))
    scale = HEAD_DIM ** -0.5
    for p in range(n_tiles):
        g = (p * HEADS_PER_TILE) // group
        qp = (rope(q_ref[:, p * LANES:(p + 1) * LANES], cosc, sinc) * scale).astype(BF16)
        acc = None
        for half, (kmat, vmat) in enumerate(((k_first[g], v_first[g]), (k_second[g], v_second[g]))):
            s = jnp.where(valid, _dot_nt(qp, kmat), NEG_INF)
            sink = sink_ref[p * HEADS_PER_TILE + half]
            m = jnp.maximum(jnp.max(s, axis=-1, keepdims=True), sink)
            e = jnp.exp(s - m)
            den = jnp.sum(e, axis=-1, keepdims=True) + jnp.exp(sink - m)
            o = _dot(e.astype(BF16), vmat) * (1.0 / den)
            acc = o if acc is None else acc + o
        o_ref[:, p * LANES:(p + 1) * LANES] = acc.astype(BF16)


def _swa(z, sinks, cos_t, sin_t, batch, seq, q_off, qw, kv_off, kvw, group):
    n = z.shape[0]
    nb = seq // BLOCK
    assert q_off % qw == 0 and kv_off % kvw == 0 and kvw == 2 * LANES
    q_blk, kv_blk = q_off // qw, kv_off // kvw
    prev = lambda t: jnp.maximum(t - 1, 0)
    return pl.pallas_call(
        functools.partial(_swa_kernel, n_tiles=qw // LANES, group=group),
        grid=(batch, nb),
        in_specs=[
            pl.BlockSpec(memory_space=pltpu.SMEM),
            pl.BlockSpec((BLOCK, qw), lambda b, t: (b * nb + t, q_blk)),
            pl.BlockSpec((BLOCK, kvw), lambda b, t: (b * nb + t, kv_blk)),
            pl.BlockSpec((BLOCK, kvw), lambda b, t: (b * nb + prev(t), kv_blk)),
            pl.BlockSpec((BLOCK, LANES), lambda b, t: (t, 0)),
            pl.BlockSpec((BLOCK, LANES), lambda b, t: (t, 0)),
            pl.BlockSpec((BLOCK, LANES), lambda b, t: (prev(t), 0)),
            pl.BlockSpec((BLOCK, LANES), lambda b, t: (prev(t), 0)),
        ],
        out_specs=pl.BlockSpec((BLOCK, qw), lambda b, t: (b * nb + t, 0)),
        out_shape=jax.ShapeDtypeStruct((n, qw), BF16),
        compiler_params=_cparams(2),
        name="swa",
    )(sinks, z, z, z, cos_t, sin_t, cos_t, sin_t)


def _memkv_kernel(m_ref, g_ref, w_ref, o_ref):
    h = _rms(m_ref[...], g_ref[...]).astype(BF16)
    o_ref[...] = _dot(h, w_ref[...]).astype(BF16)


def _memkv(mem, g, w):
    n, d = mem.shape
    ncols = w.shape[1]
    tm = _pick(n, (512, 256, 128))
    tn = _pick(ncols, (1024, 512, 256, 128))
    return pl.pallas_call(
        _memkv_kernel,
        grid=(ncols // tn, n // tm),
        in_specs=[
            pl.BlockSpec((tm, d), lambda j, i: (i, 0)),
            pl.BlockSpec((1, d), lambda j, i: (0, 0)),
            pl.BlockSpec((d, tn), lambda j, i: (0, j)),
        ],
        out_specs=pl.BlockSpec((tm, tn), lambda j, i: (i, j)),
        out_shape=jax.ShapeDtypeStruct((n, ncols), BF16),
        compiler_params=_cparams(2),
        name="memkv",
    )(mem, g.reshape(1, d), w)


def _mix_kernel(x_ref, yr_ref, ys_ref, wo_ref, bo_ref, g_ref, wq_ref, k_ref, v_ref, wxo_ref, o_ref):
    c = yr_ref.shape[1]
    d = x_ref.shape[1]
    hd = d // XATTN_HEADS
    x2 = x_ref[...] + _dot(yr_ref[...], wo_ref[:c, :]) + _dot(ys_ref[...], wo_ref[c:, :]) + bo_ref[...]
    q = _dot(_rms(x2, g_ref[...]).astype(BF16), wq_ref[...]).astype(BF16)
    scale = hd ** -0.5
    outs = []
    for h in range(XATTN_HEADS):
        sl = slice(h * hd, (h + 1) * hd)
        s = _dot_nt(q[:, sl], k_ref[:, sl]) * scale
        m = jnp.max(s, axis=-1, keepdims=True)
        e = jnp.exp(s - m)
        den = jnp.sum(e, axis=-1, keepdims=True)
        outs.append((_dot(e.astype(BF16), v_ref[:, sl]) * (1.0 / den)).astype(BF16))
    o_ref[...] = x2 + _dot(jnp.concatenate(outs, axis=1), wxo_ref[...])


def _mix(x, yr, ys, wo, bo, g, wq, kv, wxo, seq, mlen):
    n, d = x.shape
    c = yr.shape[1]
    tm = _pick(seq, (256, 128))
    per_seq = seq // tm
    once = pl.Buffered(1)
    return pl.pallas_call(
        _mix_kernel,
        grid=(n // tm,),
        in_specs=[
            pl.BlockSpec((tm, d), lambda i: (i, 0)),
            pl.BlockSpec((tm, c), lambda i: (i, 0)),
            pl.BlockSpec((tm, d - c), lambda i: (i, 0)),
            pl.BlockSpec((d, d), lambda i: (0, 0), pipeline_mode=once),
            pl.BlockSpec((1, d), lambda i: (0, 0)),
            pl.BlockSpec((1, d), lambda i: (0, 0)),
            pl.BlockSpec((d, d), lambda i: (0, 0), pipeline_mode=once),
            pl.BlockSpec((mlen, d), lambda i: (i // per_seq, 0)),
            pl.BlockSpec((mlen, d), lambda i: (i // per_seq, 1)),
            pl.BlockSpec((d, d), lambda i: (0, 0), pipeline_mode=once),
        ],
        out_specs=pl.BlockSpec((tm, d), lambda i: (i, 0)),
        out_shape=jax.ShapeDtypeStruct((n, d), F32),
        compiler_params=_cparams(1),
        name="mix",
    )(x, yr, ys, wo, bo.reshape(1, d), g.reshape(1, d), wq, kv, kv, wxo)


def _pad_cols(w, width):
    return jnp.pad(w, ((0, 0), (0, width - w.shape[1])))


def _pad_rows(w, height):
    return jnp.pad(w, ((0, height - w.shape[0]), (0, 0)))


def _hi_lo(w):
    hi = w.astype(BF16)
    return hi, (w - hi.astype(F32)).astype(BF16)


def _rope_tables(seq):
    half = HEAD_DIM // 2
    lane = jnp.arange(LANES)
    inv_freq = ROPE_THETA ** (-jnp.arange(0, HEAD_DIM, 2, dtype=F32) / HEAD_DIM)
    ang = jnp.arange(seq, dtype=F32)[:, None] * inv_freq[lane % half][None, :]
    sign = jnp.where((lane % HEAD_DIM) < half, -1.0, 1.0)
    return jnp.cos(ang), jnp.sin(ang) * sign[None, :]


def kernel(x, mem, f1_norm, f1_gate, f1_up, f1_down, mix_norm, w_in, b_in_attn, rw_mu, rw_w0, rw_decay_up, rw_a0, rw_aaa_up, rw_gate_up, rw_k_k, rw_k_a, rw_r_k, rw_lnx_w, rw_lnx_b, attn_sinks, w_out, b_out, xa_norm, mem_norm, w_xq, w_xkv, w_xo, f2_norm, f2_gate, f2_up, f2_down, final_norm):
    batch, seq, d = x.shape
    mlen = mem.shape[1]
    depth = f1_norm.shape[0]
    c = rw_w0.shape[1]
    sw = d - c
    dl, al, gl = rw_decay_up.shape[1], rw_aaa_up.shape[1], rw_gate_up.shape[1]
    dlp, alp, glp = (_round_up(v, LANES) for v in (dl, al, gl))
    kvw = b_in_attn.shape[1] - sw
    q_heads = sw // HEAD_DIM
    group = q_heads // (kvw // (2 * HEAD_DIM))
    n_tiles = c // LANES
    n = batch * seq
    q_off = 3 * c
    lora_off = q_off + sw
    lora_w = dlp + alp + glp
    kv_off = lora_off + lora_w
    ncols = _round_up(kv_off + kvw, 1024)
    cos_t, sin_t = _rope_tables(seq)

    xf = x.reshape(n, d)
    memf = mem.reshape(batch * mlen, d)
    for l in range(depth):
        xf = _ffn(xf, f1_norm[l], f1_gate[l].astype(BF16), f1_up[l].astype(BF16), f1_down[l].astype(BF16), None)

        wl = w_in[l]
        o1, o2, o3 = 3 * c, 3 * c + dl, 3 * c + dl + al
        shift = o3 + gl
        w_all = jnp.concatenate([
            wl[:, :o1], wl[:, shift:shift + sw],
            _pad_cols(wl[:, o1:o2], dlp), _pad_cols(wl[:, o2:o3], alp), _pad_cols(wl[:, o3:shift], glp),
            wl[:, shift + sw:]], axis=1)
        w_all = _pad_cols(w_all, ncols).astype(BF16)
        mu = rw_mu[l][None, :]
        mu_all = _pad_cols(jnp.concatenate([
            mu[:, :o1], jnp.zeros((1, sw), F32),
            _pad_cols(mu[:, o1:o2], dlp), _pad_cols(mu[:, o2:o3], alp), _pad_cols(mu[:, o3:shift], glp)],
            axis=1), ncols)
        bia = b_in_attn[l][None, :]
        b_all = _pad_cols(jnp.concatenate([
            jnp.zeros((1, q_off), F32), bia[:, :sw], jnp.zeros((1, lora_w), F32), bia[:, sw:]], axis=1), ncols)
        z = _proj(xf, mix_norm[l], w_all, mu_all, b_all, seq)

        tiles = lambda v: v.reshape(n_tiles, 1, LANES)
        vecs = [rw_w0[l].reshape(1, c), rw_a0[l].reshape(1, c), tiles(rw_k_k[l]), tiles(rw_k_a[l]),
                tiles(rw_r_k[l]), tiles(rw_lnx_w[l]), tiles(rw_lnx_b[l])]
        loras = [*_hi_lo(_pad_rows(rw_decay_up[l], dlp)), *_hi_lo(_pad_rows(rw_aaa_up[l], alp)),
                 *_hi_lo(_pad_rows(rw_gate_up[l], glp))]
        y_rwkv = _rwkv(z, batch, seq, c, lora_off, lora_w, dlp, alp, vecs, loras)
        y_swa = _swa(z, attn_sinks[l], cos_t, sin_t, batch, seq, q_off, sw, kv_off, kvw, group)

        kv_mem = _memkv(memf, mem_norm[l], w_xkv[l].astype(BF16))
        xf = _mix(xf, y_rwkv, y_swa, w_out[l].astype(BF16), b_out[l], xa_norm[l], w_xq[l].astype(BF16),
                  kv_mem, w_xo[l].astype(BF16), seq, mlen)

        last = l == depth - 1
        xf = _ffn(xf, f2_norm[l], f2_gate[l].astype(BF16), f2_up[l].astype(BF16), f2_down[l].astype(BF16),
                  final_norm if last else None)
    return xf.reshape(batch, seq, d)
```

```python
import functools

import jax
import jax.numpy as jnp
from jax import lax
from jax.experimental import pallas as pl
from jax.experimental.pallas import tpu as pltpu

F32, BF16 = jnp.float32, jnp.bfloat16

LANES = 128
SUBLANES = 8
VMEM_LIMIT_BYTES = 56 * 1024 * 1024

HEAD_DIM = 64
HEADS_PER_TILE = LANES // HEAD_DIM
CHUNK = 128
BLOCK = 128
XATTN_HEADS = 4
RMS_EPS = 1e-6
GN_EPS = 64e-5
NEG_INF = -1e30
ROPE_THETA = 10000.0


def _round_up(n, m):
    return (n + m - 1) // m * m


def _pick(n, prefs):
    for p in prefs:
        if n % p == 0:
            return p
    raise ValueError(f"no tile in {prefs} divides {n}")


def _cparams(n_axes):
    return pltpu.CompilerParams(dimension_semantics=("arbitrary",) * n_axes,
                                vmem_limit_bytes=VMEM_LIMIT_BYTES)


def _dot(a, b):
    return jnp.dot(a, b, preferred_element_type=F32)


def _dot_nt(a, b):
    return lax.dot_general(a, b, (((1,), (1,)), ((), ())), preferred_element_type=F32)


def _split(x):
    hi = x.astype(BF16)
    lo = (x - hi.astype(F32)).astype(BF16)
    return hi, lo


def _dot3(a, b):
    ah, al = _split(a)
    bh, bl = _split(b)
    return _dot(ah, bh) + _dot(ah, bl) + _dot(al, bh)


def _dot3_nt(a, b):
    ah, al = _split(a)
    bh, bl = _split(b)
    return _dot_nt(ah, bh) + _dot_nt(ah, bl) + _dot_nt(al, bh)


def _dot3_w(a, wh, wl):
    ah, al = _split(a)
    return _dot(ah, wh) + _dot(ah, wl) + _dot(al, wh)


def _rms(x, g):
    ms = jnp.mean(x * x, axis=-1, keepdims=True)
    return x * lax.rsqrt(ms + RMS_EPS) * g


def _softplus(x):
    return jnp.maximum(x, 0.0) + jnp.log(1.0 + jnp.exp(-jnp.abs(x)))


def _ffn_kernel(x_ref, g_ref, wg_ref, wu_ref, wd_ref, fg_ref, o_ref, h_ref, *, final_norm, rows):
    j = pl.program_id(1)
    tm = x_ref.shape[0]

    @pl.when(j == 0)
    def _():
        for r0 in range(0, tm, rows):
            x = x_ref[r0:r0 + rows, :]
            h_ref[r0:r0 + rows, :] = _rms(x, g_ref[...]).astype(BF16)
            o_ref[r0:r0 + rows, :] = x

    for r0 in range(0, tm, rows):
        h = h_ref[r0:r0 + rows, :]
        gate = _dot(h, wg_ref[...])
        up = _dot(h, wu_ref[...])
        act = (gate * jax.nn.sigmoid(gate) * up).astype(BF16)
        o_ref[r0:r0 + rows, :] += 0.5 * _dot(act, wd_ref[...])

    if final_norm:
        @pl.when(j == pl.num_programs(1) - 1)
        def _():
            for r0 in range(0, tm, rows):
                o_ref[r0:r0 + rows, :] = _rms(o_ref[r0:r0 + rows, :], fg_ref[...])


def _ffn(x, g, wg, wu, wd, fg):
    n, d = x.shape
    f = wg.shape[1]
    tm = _pick(n, (1024, 512, 256, 128))
    tf = _pick(f, (512, 256, 128))
    rows = min(tm, 512)
    final_norm = fg is not None
    fg = g if fg is None else fg
    return pl.pallas_call(
        functools.partial(_ffn_kernel, final_norm=final_norm, rows=rows),
        grid=(n // tm, f // tf),
        in_specs=[
            pl.BlockSpec((tm, d), lambda i, j: (i, 0), pipeline_mode=pl.Buffered(1)),
            pl.BlockSpec((1, d), lambda i, j: (0, 0)),
            pl.BlockSpec((d, tf), lambda i, j: (0, j)),
            pl.BlockSpec((d, tf), lambda i, j: (0, j)),
            pl.BlockSpec((tf, d), lambda i, j: (j, 0)),
            pl.BlockSpec((1, d), lambda i, j: (0, 0)),
        ],
        out_specs=pl.BlockSpec((tm, d), lambda i, j: (i, 0)),
        out_shape=jax.ShapeDtypeStruct((n, d), F32),
        scratch_shapes=[pltpu.VMEM((tm, d), BF16)],
        compiler_params=_cparams(2),
        name="ffn",
    )(x, g.reshape(1, d), wg, wu, wd, fg.reshape(1, d))


def _proj_kernel(x_ref, g_ref, w_ref, mu_ref, b_ref, o_ref, carry_ref, *, tiles_per_seq, tn):
    i = pl.program_id(0)
    tm = x_ref.shape[0]
    ncols = w_ref.shape[1]

    @pl.when(i % tiles_per_seq == 0)
    def _():
        carry_ref[...] = jnp.zeros(carry_ref.shape, F32)

    h = _rms(x_ref[...], g_ref[...]).astype(BF16)
    row = lax.broadcasted_iota(jnp.int32, (SUBLANES, tn), 0)
    for c0 in range(0, ncols, tn):
        cs = slice(c0, c0 + tn)
        z = _dot(h, w_ref[:, cs])
        prev_tail = carry_ref[:, cs]
        carry_ref[:, cs] = z[tm - SUBLANES:, :]
        zs = pltpu.roll(z, 1, 0)
        head = jnp.where(row == 0, pltpu.roll(prev_tail, 1, 0), zs[:SUBLANES])
        mu = mu_ref[:, cs]
        b = b_ref[:, cs]
        z0 = z[:SUBLANES]
        o_ref[:SUBLANES, cs] = z0 + (head - z0) * mu + b
        z1 = z[SUBLANES:]
        o_ref[SUBLANES:, cs] = z1 + (zs[SUBLANES:] - z1) * mu + b


def _proj(x, g, w_all, mu_all, b_all, seq):
    n, d = x.shape
    ncols = w_all.shape[1]
    tm = _pick(seq, (256, 128))
    tn = _pick(ncols, (1024, 512, 256, 128))
    return pl.pallas_call(
        functools.partial(_proj_kernel, tiles_per_seq=seq // tm, tn=tn),
        grid=(n // tm,),
        in_specs=[
            pl.BlockSpec((tm, d), lambda i: (i, 0)),
            pl.BlockSpec((1, d), lambda i: (0, 0)),
            pl.BlockSpec((d, ncols), lambda i: (0, 0), pipeline_mode=pl.Buffered(1)),
            pl.BlockSpec((1, ncols), lambda i: (0, 0)),
            pl.BlockSpec((1, ncols), lambda i: (0, 0)),
        ],
        out_specs=pl.BlockSpec((tm, ncols), lambda i: (i, 0)),
        out_shape=jax.ShapeDtypeStruct((n, ncols), F32),
        scratch_shapes=[pltpu.VMEM((SUBLANES, ncols), F32)],
        compiler_params=_cparams(1),
        name="proj",
    )(x, g.reshape(1, d), w_all, mu_all, b_all)


def _bdot(a, b):
    return lax.dot_general(a, b, (((2,), (1,)), ((0,), (0,))), preferred_element_type=F32)


def _bdot_nt(a, b):
    return lax.dot_general(a, b, (((2,), (2,)), ((0,), (0,))), preferred_element_type=F32)


def _bdot3(a, b):
    ah, al = _split(a)
    bh, bl = _split(b)
    return _bdot(ah, bh) + _bdot(ah, bl) + _bdot(al, bh)


def _bdot3_nt(a, b):
    ah, al = _split(a)
    bh, bl = _split(b)
    return _bdot_nt(ah, bh) + _bdot_nt(ah, bl) + _bdot_nt(al, bh)


def _inv_unit_lower(nmat, eye):
    t = eye + nmat
    pw = nmat.astype(BF16)
    steps = CHUNK.bit_length() - 2
    for _ in range(steps):
        pw = _bdot(pw, pw).astype(BF16)
        t = t + _bdot(t.astype(BF16), pw)
    resid = eye - t + _bdot3(nmat, t)
    return t + _bdot(t.astype(BF16), resid.astype(BF16))


def _rwkv_kernel(r_ref, k_ref, v_ref, lora_ref, w0_ref, a0_ref, kk_ref, ka_ref, rk_ref, gw_ref, gb_ref,
                 duh_ref, dul_ref, auh_ref, aul_ref, guh_ref, gul_ref,
                 o_ref,
                 s_ref, pr_ref, pk_ref, pv_ref, pa_ref, pg_ref, plp_ref, plw_ref,
                 *, n_tiles, dlp, alp):
    L = CHUNK

    @pl.when(pl.program_id(1) == 0)
    def _():
        s_ref[...] = jnp.zeros(s_ref.shape, F32)

    lora = lora_ref[...]
    wd = jnp.tanh(lora[:, :dlp])
    ad = lora[:, dlp:dlp + alp]
    gd = jax.nn.sigmoid(lora[:, dlp + alp:])
    w = -_softplus(-(w0_ref[...] + _dot3_w(wd, duh_ref[...], dul_ref[...]))) - 0.5
    lw = -jnp.exp(w)
    asig = jax.nn.sigmoid(a0_ref[...] + _dot3_w(ad, auh_ref[...], aul_ref[...]))
    gate = _dot3_w(gd, guh_ref[...], gul_ref[...])
    row = lax.broadcasted_iota(jnp.int32, (L, L), 0)
    col = lax.broadcasted_iota(jnp.int32, (L, L), 1)
    incl = row >= col
    strict = row > col
    tril = jnp.where(incl, 1.0, 0.0).astype(BF16)
    h1 = lw.astype(BF16)
    r1 = lw - h1.astype(F32)
    h2 = r1.astype(BF16)
    h3 = (r1 - h2.astype(F32)).astype(BF16)
    logp = _dot(tril, h1) + _dot(tril, h2) + _dot(tril, h3)
    r_all = r_ref[...]
    k_all = k_ref[...]
    v_all = v_ref[...]
    for p in range(n_tiles):
        sl = slice(p * LANES, (p + 1) * LANES)
        pr_ref[p] = r_all[:, sl]
        pk_ref[p] = k_all[:, sl]
        pv_ref[p] = v_all[:, sl]
        pa_ref[p] = asig[:, sl]
        pg_ref[p] = gate[:, sl]
        plp_ref[p] = logp[:, sl]
        plw_ref[p] = lw[:, sl]

    lane = lax.broadcasted_iota(jnp.int32, (1, LANES), 1)
    lo = lane < HEAD_DIM
    eye = jnp.where(row == col, 1.0, 0.0)
    same_head = jnp.where(row < HEAD_DIM, 0, 1) == jnp.where(col < HEAD_DIM, 0, 1)
    ones_bd = jnp.where(same_head, 1.0, 0.0).astype(BF16)
    inv_hd = 1.0 / HEAD_DIM

    def segsum(x):
        xh, xl = _split(x.reshape(n_tiles * L, LANES))
        return (_dot(xh, ones_bd) + _dot(xl, ones_bd)).reshape(n_tiles, L, LANES)

    def first(x):
        return jnp.where(lo, x, 0.0)

    def second(x):
        return jnp.where(lo, 0.0, x)

    r = pr_ref[...]
    k = pk_ref[...]
    v = pv_ref[...]
    a_s = pa_ref[...]
    lp = plp_ref[...]
    kk = k * kk_ref[...]
    ss = segsum(kk * kk)
    kk = kk / jnp.maximum(jnp.sqrt(ss), 1e-12)
    a = -kk
    b = kk * a_s
    km = k * (1.0 + (a_s - 1.0) * ka_ref[...])
    cmid = lp[:, L // 2 - 1:L // 2, :]
    clast = lp[:, L - 1:L, :]
    lpe = lp - plw_ref[...]
    e_inv = jnp.exp(cmid - lp)
    at = a * jnp.exp(lpe - cmid)
    a_abs = a * jnp.exp(lpe)
    rt = r * jnp.exp(lp - cmid)
    r_abs = r * jnp.exp(lp)
    bt = b * e_inv
    kt = km * e_inv
    e_l = jnp.exp(clast - cmid)
    e_p = jnp.exp(clast)
    bk = jnp.concatenate([bt, kt], axis=1)
    g_a = _bdot3_nt(jnp.concatenate([first(at), second(at)], axis=1), bk)
    g_r = _bdot_nt(jnp.concatenate([first(rt), second(rt)], axis=1).astype(BF16), bk.astype(BF16))
    n_lo = jnp.where(strict, g_a[:, 0:L, 0:L], 0.0)
    ak_lo = jnp.where(strict, g_a[:, 0:L, L:], 0.0)
    n_hi = jnp.where(strict, g_a[:, L:, 0:L], 0.0)
    ak_hi = jnp.where(strict, g_a[:, L:, L:], 0.0)
    rb_lo = jnp.where(incl, g_r[:, 0:L, 0:L], 0.0)
    rk_lo = jnp.where(incl, g_r[:, 0:L, L:], 0.0)
    rb_hi = jnp.where(incl, g_r[:, L:, 0:L], 0.0)
    rk_hi = jnp.where(incl, g_r[:, L:, L:], 0.0)
    t_all = _inv_unit_lower(jnp.concatenate([n_lo, n_hi], axis=0), eye)
    t_lo, t_hi = t_all[:n_tiles], t_all[n_tiles:]
    s = s_ref[...]
    v_lo, v_hi = first(v), second(v)
    rhs = _bdot3_nt(a_abs, s) + _bdot3(jnp.concatenate([ak_lo, ak_hi], axis=2),
                                       jnp.concatenate([v_lo, v_hi], axis=1))
    u = _bdot3(jnp.concatenate([t_lo, t_hi], axis=2),
               jnp.concatenate([first(rhs), second(rhs)], axis=1))
    y = _bdot_nt(r_abs.astype(BF16), s.astype(BF16)) + _bdot(
        jnp.concatenate([rb_lo, rb_hi, rk_lo, rk_hi], axis=2).astype(BF16),
        jnp.concatenate([first(u), second(u), v_lo, v_hi], axis=1).astype(BF16))
    upd = _bdot3(jnp.concatenate([jnp.swapaxes(u, 1, 2), jnp.swapaxes(v, 1, 2)], axis=2),
                 jnp.concatenate([bt * e_l, kt * e_l], axis=1))
    s_ref[...] = s * e_p + jnp.where(same_head, upd, 0.0)
    mean = segsum(y) * inv_hd
    dev = y - mean
    var = segsum(dev * dev) * inv_hd
    yn = dev * lax.rsqrt(var + GN_EPS) * gw_ref[...] + gb_ref[...]
    bonus = segsum(r * km * rk_ref[...]) * v
    out = ((yn + bonus) * pg_ref[...]).astype(BF16)
    for p in range(n_tiles):
        o_ref[:, p * LANES:(p + 1) * LANES] = out[p]


def _rwkv(z, batch, seq, c, lora_off, lora_w, dlp, alp, vecs, loras):
    n = z.shape[0]
    nc = seq // CHUNK
    n_tiles = c // LANES
    assert lora_off % lora_w == 0
    lora_blk = lora_off // lora_w
    vec_spec = pl.BlockSpec((n_tiles, 1, LANES), lambda b, t: (0, 0, 0))
    in_specs = [
        pl.BlockSpec((CHUNK, c), lambda b, t: (b * nc + t, 0)),
        pl.BlockSpec((CHUNK, c), lambda b, t: (b * nc + t, 1)),
        pl.BlockSpec((CHUNK, c), lambda b, t: (b * nc + t, 2)),
        pl.BlockSpec((CHUNK, lora_w), lambda b, t: (b * nc + t, lora_blk)),
        pl.BlockSpec((1, c), lambda b, t: (0, 0)),
        pl.BlockSpec((1, c), lambda b, t: (0, 0)),
    ] + [vec_spec] * 5 + [pl.BlockSpec(w.shape, lambda b, t: (0, 0)) for w in loras]
    tile_f32 = pltpu.VMEM((n_tiles, CHUNK, LANES), F32)
    return pl.pallas_call(
        functools.partial(_rwkv_kernel, n_tiles=n_tiles, dlp=dlp, alp=alp),
        grid=(batch, nc),
        in_specs=in_specs,
        out_specs=pl.BlockSpec((CHUNK, c), lambda b, t: (b * nc + t, 0)),
        out_shape=jax.ShapeDtypeStruct((n, c), BF16),
        scratch_shapes=[pltpu.VMEM((n_tiles, LANES, LANES), F32)] + [tile_f32] * 7,
        compiler_params=_cparams(2),
        name="rwkv",
    )(z, z, z, z, *vecs, *loras)


def _swa_kernel(sink_ref, q_ref, kvc_ref, kvp_ref, cosc_ref, sinc_ref, cosp_ref, sinp_ref, o_ref,
                *, n_tiles, group):
    nblk = pl.program_id(1)
    lane = lax.broadcasted_iota(jnp.int32, (1, LANES), 1)
    lo = lane < HEAD_DIM
    rot_lo = jnp.bitwise_and(lane, HEAD_DIM - 1) < HEAD_DIM // 2

    def rope(x, cos, sin_signed):
        partner = jnp.where(rot_lo, pltpu.roll(x, LANES - HEAD_DIM // 2, 1), pltpu.roll(x, HEAD_DIM // 2, 1))
        return x * cos + partner * sin_signed

    cosc = cosc_ref[...]
    sinc = sinc_ref[...]
    kvc = kvc_ref[...]
    kvp = kvp_ref[...]
    keys = jnp.concatenate([rope(kvp[:, :LANES], cosp_ref[...], sinp_ref[...]),
                            rope(kvc[:, :LANES], cosc, sinc)], axis=0)
    vals = jnp.concatenate([kvp[:, LANES:], kvc[:, LANES:]], axis=0)
    keys_sw = pltpu.roll(keys, HEAD_DIM, 1)
    vals_sw = pltpu.roll(vals, HEAD_DIM, 1)
    k_first = [jnp.where(lo, keys, 0.0).astype(BF16), jnp.where(lo, keys_sw, 0.0).astype(BF16)]
    k_second = [jnp.where(lo, 0.0, keys_sw).astype(BF16), jnp.where(lo, 0.0, keys).astype(BF16)]
    v_first = [jnp.where(lo, vals, 0.0).astype(BF16), jnp.where(lo, vals_sw, 0.0).astype(BF16)]
    v_second = [jnp.where(lo, 0.0, vals_sw).astype(BF16), jnp.where(lo, 0.0, vals).astype(BF16)]
    qi = lax.broadcasted_iota(jnp.int32, (BLOCK, 2 * BLOCK), 0)
    ki = lax.broadcasted_iota(jnp.int32, (BLOCK, 2 * BLOCK), 1)
    valid = (ki > qi) & (ki <= qi + BLOCK) & ((nblk > 0) | (ki >= BLOCK))
    scale = HEAD_DIM ** -0.5
    ones_kv = jnp.ones((2 * BLOCK, LANES), BF16)
    heads = [(p, half) for p in range(n_tiles) for half in range(HEADS_PER_TILE)]
    scores = []
    for p in range(n_tiles):
        g = (p * HEADS_PER_TILE) // group
        qp = (rope(q_ref[:, p * LANES:(p + 1) * LANES], cosc, sinc) * scale).astype(BF16)
        scores += [_dot_nt(qp, k_first[g]), _dot_nt(qp, k_second[g])]
    probs, sink_terms = [], []
    for (p, half), s in zip(heads, scores):
        s = jnp.where(valid, s, NEG_INF)
        sink = sink_ref[p * HEADS_PER_TILE + half]
        m = jnp.maximum(jnp.max(s, axis=-1, keepdims=True), sink)
        probs.append(jnp.exp(s - m).astype(BF16))
        sink_terms.append(jnp.exp(sink - m))
    outs = []
    for (p, half), e, st in zip(heads, probs, sink_terms):
        g = (p * HEADS_PER_TILE) // group
        den = _dot(e, ones_kv) + st
        outs.append(_dot(e, v_second[g] if half else v_first[g]) * (1.0 / den))
    for p in range(n_tiles):
        o_ref[:, p * LANES:(p + 1) * LANES] = (outs[2 * p] + outs[2 * p + 1]).astype(BF16)


def _swa(z, sinks, cos_t, sin_t, batch, seq, q_off, qw, kv_off, kvw, group):
    n = z.shape[0]
    nb = seq // BLOCK
    assert q_off % qw == 0 and kv_off % kvw == 0 and kvw == 2 * LANES
    q_blk, kv_blk = q_off // qw, kv_off // kvw
    prev = lambda t: jnp.maximum(t - 1, 0)
    return pl.pallas_call(
        functools.partial(_swa_kernel, n_tiles=qw // LANES, group=group),
        grid=(batch, nb),
        in_specs=[
            pl.BlockSpec(memory_space=pltpu.SMEM),
            pl.BlockSpec((BLOCK, qw), lambda b, t: (b * nb + t, q_blk)),
            pl.BlockSpec((BLOCK, kvw), lambda b, t: (b * nb + t, kv_blk)),
            pl.BlockSpec((BLOCK, kvw), lambda b, t: (b * nb + prev(t), kv_blk)),
            pl.BlockSpec((BLOCK, LANES), lambda b, t: (t, 0)),
            pl.BlockSpec((BLOCK, LANES), lambda b, t: (t, 0)),
            pl.BlockSpec((BLOCK, LANES), lambda b, t: (prev(t), 0)),
            pl.BlockSpec((BLOCK, LANES), lambda b, t: (prev(t), 0)),
        ],
        out_specs=pl.BlockSpec((BLOCK, qw), lambda b, t: (b * nb + t, 0)),
        out_shape=jax.ShapeDtypeStruct((n, qw), BF16),
        compiler_params=_cparams(2),
        name="swa",
    )(sinks, z, z, z, cos_t, sin_t, cos_t, sin_t)


def _memkv_kernel(m_ref, g_ref, w_ref, o_ref):
    h = _rms(m_ref[...], g_ref[...]).astype(BF16)
    o_ref[...] = _dot(h, w_ref[...]).astype(BF16)


def _memkv(mem, g, w):
    n, d = mem.shape
    ncols = w.shape[1]
    tm = _pick(n, (512, 256, 128))
    tn = _pick(ncols, (1024, 512, 256, 128))
    return pl.pallas_call(
        _memkv_kernel,
        grid=(ncols // tn, n // tm),
        in_specs=[
            pl.BlockSpec((tm, d), lambda j, i: (i, 0)),
            pl.BlockSpec((1, d), lambda j, i: (0, 0)),
            pl.BlockSpec((d, tn), lambda j, i: (0, j)),
        ],
        out_specs=pl.BlockSpec((tm, tn), lambda j, i: (i, j)),
        out_shape=jax.ShapeDtypeStruct((n, ncols), BF16),
        compiler_params=_cparams(2),
        name="memkv",
    )(mem, g.reshape(1, d), w)


def _mix_kernel(x_ref, yr_ref, ys_ref, wo_ref, bo_ref, g_ref, wq_ref, k_ref, v_ref, wxo_ref, o_ref):
    c = yr_ref.shape[1]
    d = x_ref.shape[1]
    hd = d // XATTN_HEADS
    x2 = x_ref[...] + _dot(yr_ref[...], wo_ref[:c, :]) + _dot(ys_ref[...], wo_ref[c:, :]) + bo_ref[...]
    q = _dot(_rms(x2, g_ref[...]).astype(BF16), wq_ref[...]).astype(BF16)
    scale = hd ** -0.5
    outs = []
    for h in range(XATTN_HEADS):
        sl = slice(h * hd, (h + 1) * hd)
        s = _dot_nt(q[:, sl], k_ref[:, sl]) * scale
        m = jnp.max(s, axis=-1, keepdims=True)
        e = jnp.exp(s - m)
        den = jnp.sum(e, axis=-1, keepdims=True)
        outs.append((_dot(e.astype(BF16), v_ref[:, sl]) * (1.0 / den)).astype(BF16))
    o_ref[...] = x2 + _dot(jnp.concatenate(outs, axis=1), wxo_ref[...])


def _mix(x, yr, ys, wo, bo, g, wq, kv, wxo, seq, mlen):
    n, d = x.shape
    c = yr.shape[1]
    tm = _pick(seq, (256, 128))
    per_seq = seq // tm
    once = pl.Buffered(1)
    return pl.pallas_call(
        _mix_kernel,
        grid=(n // tm,),
        in_specs=[
            pl.BlockSpec((tm, d), lambda i: (i, 0)),
            pl.BlockSpec((tm, c), lambda i: (i, 0)),
            pl.BlockSpec((tm, d - c), lambda i: (i, 0)),
            pl.BlockSpec((d, d), lambda i: (0, 0), pipeline_mode=once),
            pl.BlockSpec((1, d), lambda i: (0, 0)),
            pl.BlockSpec((1, d), lambda i: (0, 0)),
            pl.BlockSpec((d, d), lambda i: (0, 0), pipeline_mode=once),
            pl.BlockSpec((mlen, d), lambda i: (i // per_seq, 0)),
            pl.BlockSpec((mlen, d), lambda i: (i // per_seq, 1)),
            pl.BlockSpec((d, d), lambda i: (0, 0), pipeline_mode=once),
        ],
        out_specs=pl.BlockSpec((tm, d), lambda i: (i, 0)),
        out_shape=jax.ShapeDtypeStruct((n, d), F32),
        compiler_params=_cparams(1),
        name="mix",
    )(x, yr, ys, wo, bo.reshape(1, d), g.reshape(1, d), wq, kv, kv, wxo)


def _pad_cols(w, width):
    return jnp.pad(w, ((0, 0), (0, width - w.shape[1])))


def _pad_rows(w, height):
    return jnp.pad(w, ((0, height - w.shape[0]), (0, 0)))


def _hi_lo(w):
    hi = w.astype(BF16)
    return hi, (w - hi.astype(F32)).astype(BF16)


def _rope_tables(seq):
    half = HEAD_DIM // 2
    lane = jnp.arange(LANES)
    inv_freq = ROPE_THETA ** (-jnp.arange(0, HEAD_DIM, 2, dtype=F32) / HEAD_DIM)
    ang = jnp.arange(seq, dtype=F32)[:, None] * inv_freq[lane % half][None, :]
    sign = jnp.where((lane % HEAD_DIM) < half, -1.0, 1.0)
    return jnp.cos(ang), jnp.sin(ang) * sign[None, :]


def kernel(x, mem, f1_norm, f1_gate, f1_up, f1_down, mix_norm, w_in, b_in_attn, rw_mu, rw_w0, rw_decay_up, rw_a0, rw_aaa_up, rw_gate_up, rw_k_k, rw_k_a, rw_r_k, rw_lnx_w, rw_lnx_b, attn_sinks, w_out, b_out, xa_norm, mem_norm, w_xq, w_xkv, w_xo, f2_norm, f2_gate, f2_up, f2_down, final_norm):
    batch, seq, d = x.shape
    mlen = mem.shape[1]
    depth = f1_norm.shape[0]
    c = rw_w0.shape[1]
    sw = d - c
    dl, al, gl = rw_decay_up.shape[1], rw_aaa_up.shape[1], rw_gate_up.shape[1]
    dlp, alp, glp = (_round_up(v, LANES) for v in (dl, al, gl))
    kvw = b_in_attn.shape[1] - sw
    q_heads = sw // HEAD_DIM
    group = q_heads // (kvw // (2 * HEAD_DIM))
    n_tiles = c // LANES
    n = batch * seq
    q_off = 3 * c
    lora_off = q_off + sw
    lora_w = dlp + alp + glp
    kv_off = lora_off + lora_w
    ncols = _round_up(kv_off + kvw, 1024)
    cos_t, sin_t = _rope_tables(seq)

    xf = x.reshape(n, d)
    memf = mem.reshape(batch * mlen, d)
    for l in range(depth):
        xf = _ffn(xf, f1_norm[l], f1_gate[l].astype(BF16), f1_up[l].astype(BF16), f1_down[l].astype(BF16), None)

        wl = w_in[l]
        o1, o2, o3 = 3 * c, 3 * c + dl, 3 * c + dl + al
        shift = o3 + gl
        w_all = jnp.concatenate([
            wl[:, :o1], wl[:, shift:shift + sw],
            _pad_cols(wl[:, o1:o2], dlp), _pad_cols(wl[:, o2:o3], alp), _pad_cols(wl[:, o3:shift], glp),
            wl[:, shift + sw:]], axis=1)
        w_all = _pad_cols(w_all, ncols).astype(BF16)
        mu = rw_mu[l][None, :]
        mu_all = _pad_cols(jnp.concatenate([
            mu[:, :o1], jnp.zeros((1, sw), F32),
            _pad_cols(mu[:, o1:o2], dlp), _pad_cols(mu[:, o2:o3], alp), _pad_cols(mu[:, o3:shift], glp)],
            axis=1), ncols)
        bia = b_in_attn[l][None, :]
        b_all = _pad_cols(jnp.concatenate([
            jnp.zeros((1, q_off), F32), bia[:, :sw], jnp.zeros((1, lora_w), F32), bia[:, sw:]], axis=1), ncols)
        z = _proj(xf, mix_norm[l], w_all, mu_all, b_all, seq)

        tiles = lambda v: v.reshape(n_tiles, 1, LANES)
        vecs = [rw_w0[l].reshape(1, c), rw_a0[l].reshape(1, c), tiles(rw_k_k[l]), tiles(rw_k_a[l]),
                tiles(rw_r_k[l]), tiles(rw_lnx_w[l]), tiles(rw_lnx_b[l])]
        loras = [*_hi_lo(_pad_rows(rw_decay_up[l], dlp)), *_hi_lo(_pad_rows(rw_aaa_up[l], alp)),
                 *_hi_lo(_pad_rows(rw_gate_up[l], glp))]
        y_rwkv = _rwkv(z, batch, seq, c, lora_off, lora_w, dlp, alp, vecs, loras)
        y_swa = _swa(z, attn_sinks[l], cos_t, sin_t, batch, seq, q_off, sw, kv_off, kvw, group)

        kv_mem = _memkv(memf, mem_norm[l], w_xkv[l].astype(BF16))
        xf = _mix(xf, y_rwkv, y_swa, w_out[l].astype(BF16), b_out[l], xa_norm[l], w_xq[l].astype(BF16),
                  kv_mem, w_xo[l].astype(BF16), seq, mlen)

        last = l == depth - 1
        xf = _ffn(xf, f2_norm[l], f2_gate[l].astype(BF16), f2_up[l].astype(BF16), f2_down[l].astype(BF16),
                  final_norm if last else None)
    return xf.reshape(batch, seq, d)
```

```python
import functools

import jax
import jax.numpy as jnp
from jax import lax
from jax.experimental import pallas as pl
from jax.experimental.pallas import tpu as pltpu

F32, BF16 = jnp.float32, jnp.bfloat16

LANES = 128
SUBLANES = 8
VMEM_LIMIT_BYTES = 56 * 1024 * 1024

HEAD_DIM = 64
HEADS_PER_TILE = LANES // HEAD_DIM
CHUNK = 128
BLOCK = 128
XATTN_HEADS = 4
RMS_EPS = 1e-6
GN_EPS = 64e-5
NEG_INF = -1e30
ROPE_THETA = 10000.0


def _round_up(n, m):
    return (n + m - 1) // m * m


def _pick(n, prefs):
    for p in prefs:
        if n % p == 0:
            return p
    raise ValueError(f"no tile in {prefs} divides {n}")


def _cparams(n_axes):
    return pltpu.CompilerParams(dimension_semantics=("arbitrary",) * n_axes,
                                vmem_limit_bytes=VMEM_LIMIT_BYTES)


def _dot(a, b):
    return jnp.dot(a, b, preferred_element_type=F32)


def _dot_nt(a, b):
    return lax.dot_general(a, b, (((1,), (1,)), ((), ())), preferred_element_type=F32)


def _split(x):
    hi = x.astype(BF16)
    lo = (x - hi.astype(F32)).astype(BF16)
    return hi, lo


def _rms(x, g):
    ms = jnp.mean(x * x, axis=-1, keepdims=True)
    return x * lax.rsqrt(ms + RMS_EPS) * g


def _softplus(x):
    return jnp.maximum(x, 0.0) + jnp.log(1.0 + jnp.exp(-jnp.abs(x)))


def _ffn_kernel(x_ref, g_ref, wg_ref, wu_ref, wd_ref, fg_ref, o_ref, h_ref, *, final_norm, rows):
    j = pl.program_id(1)
    tm = x_ref.shape[0]

    @pl.when(j == 0)
    def _():
        for r0 in range(0, tm, rows):
            x = x_ref[r0:r0 + rows, :]
            h_ref[r0:r0 + rows, :] = _rms(x, g_ref[...]).astype(BF16)
            o_ref[r0:r0 + rows, :] = x

    for r0 in range(0, tm, rows):
        h = h_ref[r0:r0 + rows, :]
        gate = _dot(h, wg_ref[...])
        up = _dot(h, wu_ref[...])
        act = (gate * jax.nn.sigmoid(gate) * up).astype(BF16)
        o_ref[r0:r0 + rows, :] += 0.5 * _dot(act, wd_ref[...])

    if final_norm:
        @pl.when(j == pl.num_programs(1) - 1)
        def _():
            for r0 in range(0, tm, rows):
                o_ref[r0:r0 + rows, :] = _rms(o_ref[r0:r0 + rows, :], fg_ref[...])


def _ffn(x, g, wg, wu, wd, fg):
    n, d = x.shape
    f = wg.shape[1]
    tm = _pick(n, (1024, 512, 256, 128))
    tf = _pick(f, (512, 256, 128))
    rows = min(tm, 512)
    final_norm = fg is not None
    fg = g if fg is None else fg
    return pl.pallas_call(
        functools.partial(_ffn_kernel, final_norm=final_norm, rows=rows),
        grid=(n // tm, f // tf),
        in_specs=[
            pl.BlockSpec((tm, d), lambda i, j: (i, 0), pipeline_mode=pl.Buffered(1)),
            pl.BlockSpec((1, d), lambda i, j: (0, 0)),
            pl.BlockSpec((d, tf), lambda i, j: (0, j)),
            pl.BlockSpec((d, tf), lambda i, j: (0, j)),
            pl.BlockSpec((tf, d), lambda i, j: (j, 0)),
            pl.BlockSpec((1, d), lambda i, j: (0, 0)),
        ],
        out_specs=pl.BlockSpec((tm, d), lambda i, j: (i, 0)),
        out_shape=jax.ShapeDtypeStruct((n, d), F32),
        scratch_shapes=[pltpu.VMEM((tm, d), BF16)],
        compiler_params=_cparams(2),
        name="ffn",
    )(x, g.reshape(1, d), wg, wu, wd, fg.reshape(1, d))


def _proj_kernel(x_ref, g_ref, w_ref, mu_ref, b_ref, o_ref, carry_ref, *, tiles_per_seq, tn):
    i = pl.program_id(0)
    tm = x_ref.shape[0]
    ncols = w_ref.shape[1]

    @pl.when(i % tiles_per_seq == 0)
    def _():
        carry_ref[...] = jnp.zeros(carry_ref.shape, F32)

    h = _rms(x_ref[...], g_ref[...]).astype(BF16)
    row = lax.broadcasted_iota(jnp.int32, (SUBLANES, tn), 0)
    for c0 in range(0, ncols, tn):
        cs = slice(c0, c0 + tn)
        z = _dot(h, w_ref[:, cs])
        prev_tail = carry_ref[:, cs]
        carry_ref[:, cs] = z[tm - SUBLANES:, :]
        zs = pltpu.roll(z, 1, 0)
        head = jnp.where(row == 0, pltpu.roll(prev_tail, 1, 0), zs[:SUBLANES])
        mu = mu_ref[:, cs]
        b = b_ref[:, cs]
        z0 = z[:SUBLANES]
        o_ref[:SUBLANES, cs] = z0 + (head - z0) * mu + b
        z1 = z[SUBLANES:]
        o_ref[SUBLANES:, cs] = z1 + (zs[SUBLANES:] - z1) * mu + b


def _proj(x, g, w_all, mu_all, b_all, seq):
    n, d = x.shape
    ncols = w_all.shape[1]
    tm = _pick(seq, (256, 128))
    tn = _pick(ncols, (1024, 512, 256, 128))
    return pl.pallas_call(
        functools.partial(_proj_kernel, tiles_per_seq=seq // tm, tn=tn),
        grid=(n // tm,),
        in_specs=[
            pl.BlockSpec((tm, d), lambda i: (i, 0)),
            pl.BlockSpec((1, d), lambda i: (0, 0)),
            pl.BlockSpec((d, ncols), lambda i: (0, 0), pipeline_mode=pl.Buffered(1)),
            pl.BlockSpec((1, ncols), lambda i: (0, 0)),
            pl.BlockSpec((1, ncols), lambda i: (0, 0)),
        ],
        out_specs=pl.BlockSpec((tm, ncols), lambda i: (i, 0)),
        out_shape=jax.ShapeDtypeStruct((n, ncols), F32),
        scratch_shapes=[pltpu.VMEM((SUBLANES, ncols), F32)],
        compiler_params=_cparams(1),
        name="proj",
    )(x, g.reshape(1, d), w_all, mu_all, b_all)


def _bdot(a, b):
    return lax.dot_general(a, b, (((2,), (1,)), ((0,), (0,))), preferred_element_type=F32)


def _bdot_nt(a, b):
    return lax.dot_general(a, b, (((2,), (2,)), ((0,), (0,))), preferred_element_type=F32)


def _bdot3(a, b):
    ah, al = _split(a)
    bh, bl = _split(b)
    return _bdot(ah, bh) + _bdot(ah, bl) + _bdot(al, bh)


def _bdot3_nt(a, b):
    ah, al = _split(a)
    bh, bl = _split(b)
    return _bdot_nt(ah, bh) + _bdot_nt(ah, bl) + _bdot_nt(al, bh)


def _inv_unit_lower(nmat):
    L = nmat.shape[-1]
    r = lax.broadcasted_iota(jnp.int32, (L, L), 0)
    c = lax.broadcasted_iota(jnp.int32, (L, L), 1)
    eye = jnp.where(r == c, 1.0, 0.0)
    t = eye + nmat
    pw = nmat.astype(BF16)
    pw = _bdot(pw, pw).astype(BF16)
    steps = L.bit_length() - 2
    for i in range(steps):
        if i + 1 < steps:
            both = _bdot(jnp.concatenate([t.astype(BF16), pw], axis=1), pw)
            t = t + both[:, :L]
            pw = both[:, L:].astype(BF16)
        else:
            t = t + _bdot(t.astype(BF16), pw)
    resid = eye - t + _bdot3(nmat, t)
    return t + _bdot(t.astype(BF16), resid.astype(BF16))


def _rwkv_kernel(r_ref, k_ref, v_ref, lora_ref, w0_ref, a0_ref, kk_ref, ka_ref, rk_ref, gw_ref, gb_ref,
                 du_ref, au_ref, gu_ref,
                 o_ref,
                 s_ref, pr_ref, pk_ref, pv_ref, pa_ref, pg_ref, plp_ref, plw_ref,
                 *, n_tiles, seqs, dlp, alp):
    L = CHUNK
    nt = seqs * n_tiles

    @pl.when(pl.program_id(1) == 0)
    def _():
        s_ref[...] = jnp.zeros(s_ref.shape, F32)

    lora = lora_ref[...].reshape(seqs * L, lora_ref.shape[2])
    wd = jnp.tanh(lora[:, :dlp])
    ad = lora[:, dlp:dlp + alp]
    gd = jax.nn.sigmoid(lora[:, dlp + alp:])
    w = -_softplus(-(w0_ref[...] + _dot(wd.astype(BF16), du_ref[...]))) - 0.5
    lw = -jnp.exp(w)
    asig = jax.nn.sigmoid(a0_ref[...] + _dot(ad.astype(BF16), au_ref[...]))
    gate = _dot(gd.astype(BF16), gu_ref[...])
    row = lax.broadcasted_iota(jnp.int32, (L, L), 0)
    col = lax.broadcasted_iota(jnp.int32, (L, L), 1)
    incl = row >= col
    strict = row > col
    tril = jnp.where(incl, 1.0, 0.0).astype(BF16)
    h1 = lw.astype(BF16)
    r1 = lw - h1.astype(F32)
    h2 = r1.astype(BF16)
    h3 = (r1 - h2.astype(F32)).astype(BF16)
    for q in range(seqs):
        rows = slice(q * L, (q + 1) * L)
        logp = _dot(tril, h1[rows]) + _dot(tril, h2[rows]) + _dot(tril, h3[rows])
        for p in range(n_tiles):
            sl = slice(p * LANES, (p + 1) * LANES)
            i = q * n_tiles + p
            pr_ref[i] = r_ref[q, :, sl]
            pk_ref[i] = k_ref[q, :, sl]
            pv_ref[i] = v_ref[q, :, sl]
            pa_ref[i] = asig[rows, sl]
            pg_ref[i] = gate[rows, sl]
            plp_ref[i] = logp[:, sl]
            plw_ref[i] = lw[rows, sl]

    lane = lax.broadcasted_iota(jnp.int32, (1, LANES), 1)
    lo = lane < HEAD_DIM
    same_head = jnp.where(row < HEAD_DIM, 0, 1) == jnp.where(col < HEAD_DIM, 0, 1)
    inv_hd = 1.0 / HEAD_DIM

    def first(x):
        return jnp.where(lo, x, 0.0)

    def second(x):
        return jnp.where(lo, 0.0, x)

    def segsum(x):
        return jnp.where(lo, jnp.sum(first(x), axis=-1, keepdims=True),
                         jnp.sum(second(x), axis=-1, keepdims=True))

    r = pr_ref[...]
    k = pk_ref[...]
    v = pv_ref[...]
    a_s = pa_ref[...]
    lp = plp_ref[...]
    kk = k * kk_ref[...]
    ss = segsum(kk * kk)
    kk = kk / jnp.maximum(jnp.sqrt(ss), 1e-12)
    a = -kk
    b = kk * a_s
    km = k * (1.0 + (a_s - 1.0) * ka_ref[...])
    cmid = lp[:, L // 2 - 1:L // 2, :]
    clast = lp[:, L - 1:L, :]
    lpe = lp - plw_ref[...]
    e_inv = jnp.exp(cmid - lp)
    at = a * jnp.exp(lpe - cmid)
    a_abs = a * jnp.exp(lpe)
    rt = r * jnp.exp(lp - cmid)
    r_abs = r * jnp.exp(lp)
    bt = b * e_inv
    kt = km * e_inv
    e_l = jnp.exp(clast - cmid)
    e_p = jnp.exp(clast)
    bk = jnp.concatenate([bt, kt], axis=1)
    g_a = _bdot3_nt(jnp.concatenate([first(at), second(at)], axis=1), bk)
    g_r = _bdot_nt(jnp.concatenate([first(rt), second(rt)], axis=1).astype(BF16), bk.astype(BF16))
    n_lo = jnp.where(strict, g_a[:, 0:L, 0:L], 0.0)
    ak_lo = jnp.where(strict, g_a[:, 0:L, L:], 0.0)
    n_hi = jnp.where(strict, g_a[:, L:, 0:L], 0.0)
    ak_hi = jnp.where(strict, g_a[:, L:, L:], 0.0)
    rb_lo = jnp.where(incl, g_r[:, 0:L, 0:L], 0.0)
    rk_lo = jnp.where(incl, g_r[:, 0:L, L:], 0.0)
    rb_hi = jnp.where(incl, g_r[:, L:, 0:L], 0.0)
    rk_hi = jnp.where(incl, g_r[:, L:, L:], 0.0)
    t_all = _inv_unit_lower(jnp.concatenate([n_lo, n_hi], axis=0))
    t_lo, t_hi = t_all[:nt], t_all[nt:]
    s = s_ref[...]
    v_lo, v_hi = first(v), second(v)
    rhs = _bdot3_nt(a_abs, s) + _bdot3(jnp.concatenate([ak_lo, ak_hi], axis=2),
                                       jnp.concatenate([v_lo, v_hi], axis=1))
    u = _bdot3(jnp.concatenate([t_lo, t_hi], axis=2),
               jnp.concatenate([first(rhs), second(rhs)], axis=1))
    y = _bdot_nt(r_abs.astype(BF16), s.astype(BF16)) + _bdot(
        jnp.concatenate([rb_lo, rb_hi, rk_lo, rk_hi], axis=2).astype(BF16),
        jnp.concatenate([first(u), second(u), v_lo, v_hi], axis=1).astype(BF16))
    upd = _bdot3(jnp.concatenate([jnp.swapaxes(u, 1, 2), jnp.swapaxes(v, 1, 2)], axis=2),
                 jnp.concatenate([bt * e_l, kt * e_l], axis=1))
    s_ref[...] = s * e_p + jnp.where(same_head, upd, 0.0)
    mean = segsum(y) * inv_hd
    dev = y - mean
    var = segsum(dev * dev) * inv_hd
    yn = dev * lax.rsqrt(var + GN_EPS) * gw_ref[...] + gb_ref[...]
    bonus = segsum(r * km * rk_ref[...]) * v
    out = ((yn + bonus) * pg_ref[...]).astype(BF16)
    for q in range(seqs):
        for p in range(n_tiles):
            o_ref[q, :, p * LANES:(p + 1) * LANES] = out[q * n_tiles + p]


def _rwkv(z, batch, seq, c, lora_off, lora_w, dlp, alp, vecs, loras):
    n = z.shape[0]
    nc = seq // CHUNK
    n_tiles = c // LANES
    seqs = 1
    nt = seqs * n_tiles
    assert lora_off % lora_w == 0
    lora_blk = lora_off // lora_w
    z3 = z.reshape(batch, seq, z.shape[1])
    vecs = vecs[:2] + [jnp.tile(v, (seqs, 1, 1)) for v in vecs[2:]]
    vec_spec = pl.BlockSpec((nt, 1, LANES), lambda b, t: (0, 0, 0))
    in_specs = [
        pl.BlockSpec((seqs, CHUNK, c), lambda b, t: (b, t, 0)),
        pl.BlockSpec((seqs, CHUNK, c), lambda b, t: (b, t, 1)),
        pl.BlockSpec((seqs, CHUNK, c), lambda b, t: (b, t, 2)),
        pl.BlockSpec((seqs, CHUNK, lora_w), lambda b, t: (b, t, lora_blk)),
        pl.BlockSpec((1, c), lambda b, t: (0, 0)),
        pl.BlockSpec((1, c), lambda b, t: (0, 0)),
    ] + [vec_spec] * 5 + [pl.BlockSpec(w.shape, lambda b, t: (0, 0)) for w in loras]
    tile_f32 = pltpu.VMEM((nt, CHUNK, LANES), F32)
    return pl.pallas_call(
        functools.partial(_rwkv_kernel, n_tiles=n_tiles, seqs=seqs, dlp=dlp, alp=alp),
        grid=(batch // seqs, nc),
        in_specs=in_specs,
        out_specs=pl.BlockSpec((seqs, CHUNK, c), lambda b, t: (b, t, 0)),
        out_shape=jax.ShapeDtypeStruct((batch, seq, c), BF16),
        scratch_shapes=[pltpu.VMEM((nt, LANES, LANES), F32)] + [tile_f32] * 7,
        compiler_params=_cparams(2),
        name="rwkv",
    )(z3, z3, z3, z3, *vecs, *loras).reshape(n, c)


def _swa_kernel(sink_ref, q_ref, kvc_ref, kvp_ref, cosc_ref, sinc_ref, cosp_ref, sinp_ref, o_ref,
                *, n_tiles, group, qb):
    nblk = pl.program_id(1)
    lane = lax.broadcasted_iota(jnp.int32, (1, LANES), 1)
    lo = lane < HEAD_DIM
    rot_lo = jnp.bitwise_and(lane, HEAD_DIM - 1) < HEAD_DIM // 2

    def rope(x, cos, sin_signed):
        partner = jnp.where(rot_lo, pltpu.roll(x, LANES - HEAD_DIM // 2, 1), pltpu.roll(x, HEAD_DIM // 2, 1))
        return x * cos + partner * sin_signed

    cosc = cosc_ref[...]
    sinc = sinc_ref[...]
    kvc = kvc_ref[...]
    kvp = kvp_ref[...]
    keys = jnp.concatenate([rope(kvp[:, :LANES], cosp_ref[...], sinp_ref[...]),
                            rope(kvc[:, :LANES], cosc, sinc)], axis=0)
    vals = jnp.concatenate([kvp[:, LANES:], kvc[:, LANES:]], axis=0)
    keys_sw = pltpu.roll(keys, HEAD_DIM, 1)
    vals_sw = pltpu.roll(vals, HEAD_DIM, 1)
    k_first = [jnp.where(lo, keys, 0.0).astype(BF16), jnp.where(lo, keys_sw, 0.0).astype(BF16)]
    k_second = [jnp.where(lo, 0.0, keys_sw).astype(BF16), jnp.where(lo, 0.0, keys).astype(BF16)]
    v_first = [jnp.where(lo, vals, 0.0).astype(BF16), jnp.where(lo, vals_sw, 0.0).astype(BF16)]
    v_second = [jnp.where(lo, 0.0, vals_sw).astype(BF16), jnp.where(lo, 0.0, vals).astype(BF16)]
    qi = lax.broadcasted_iota(jnp.int32, (BLOCK, 2 * BLOCK), 0)
    ki = lax.broadcasted_iota(jnp.int32, (BLOCK, 2 * BLOCK), 1)
    window = (ki > qi) & (ki <= qi + BLOCK)
    valid = [window & ((nblk > 0) | (ki >= BLOCK))] + [window] * (qb - 1)
    scale = HEAD_DIM ** -0.5
    ones_kv = jnp.ones((2 * BLOCK, LANES), BF16)
    heads = [(j, p, half) for j in range(qb) for p in range(n_tiles) for half in range(HEADS_PER_TILE)]
    scores = []
    for j in range(qb):
        qrows = slice(j * BLOCK, (j + 1) * BLOCK)
        krows = slice(j * BLOCK, (j + 2) * BLOCK)
        for p in range(n_tiles):
            g = (p * HEADS_PER_TILE) // group
            qp = (rope(q_ref[qrows, p * LANES:(p + 1) * LANES], cosc[qrows], sinc[qrows]) * scale).astype(BF16)
            scores += [_dot_nt(qp, k_first[g][krows]), _dot_nt(qp, k_second[g][krows])]
    probs, sink_terms = [], []
    for (j, p, half), s in zip(heads, scores):
        s = jnp.where(valid[j], s, NEG_INF)
        sink = sink_ref[p * HEADS_PER_TILE + half]
        m = jnp.maximum(jnp.max(s, axis=-1, keepdims=True), sink)
        probs.append(jnp.exp(s - m).astype(BF16))
        sink_terms.append(jnp.exp(sink - m))
    outs = []
    for (j, p, half), e, st in zip(heads, probs, sink_terms):
        g = (p * HEADS_PER_TILE) // group
        krows = slice(j * BLOCK, (j + 2) * BLOCK)
        den = _dot(e, ones_kv) + st
        outs.append(_dot(e, (v_second[g] if half else v_first[g])[krows]) * (1.0 / den))
    for j in range(qb):
        for p in range(n_tiles):
            i = (j * n_tiles + p) * HEADS_PER_TILE
            o_ref[j * BLOCK:(j + 1) * BLOCK, p * LANES:(p + 1) * LANES] = (outs[i] + outs[i + 1]).astype(BF16)


def _swa(z, sinks, cos_t, sin_t, batch, seq, q_off, qw, kv_off, kvw, group):
    n = z.shape[0]
    nb = seq // BLOCK
    assert q_off % qw == 0 and kv_off % kvw == 0 and kvw == 2 * LANES
    q_blk, kv_blk = q_off // qw, kv_off // kvw
    qb = 2 if nb % 2 == 0 else 1
    steps = nb // qb
    prev = lambda t: jnp.maximum(t * qb - 1, 0)
    return pl.pallas_call(
        functools.partial(_swa_kernel, n_tiles=qw // LANES, group=group, qb=qb),
        grid=(batch, steps),
        in_specs=[
            pl.BlockSpec(memory_space=pltpu.SMEM),
            pl.BlockSpec((qb * BLOCK, qw), lambda b, t: (b * steps + t, q_blk)),
            pl.BlockSpec((qb * BLOCK, kvw), lambda b, t: (b * steps + t, kv_blk)),
            pl.BlockSpec((BLOCK, kvw), lambda b, t: (b * nb + prev(t), kv_blk)),
            pl.BlockSpec((qb * BLOCK, LANES), lambda b, t: (t, 0)),
            pl.BlockSpec((qb * BLOCK, LANES), lambda b, t: (t, 0)),
            pl.BlockSpec((BLOCK, LANES), lambda b, t: (prev(t), 0)),
            pl.BlockSpec((BLOCK, LANES), lambda b, t: (prev(t), 0)),
        ],
        out_specs=pl.BlockSpec((qb * BLOCK, qw), lambda b, t: (b * steps + t, 0)),
        out_shape=jax.ShapeDtypeStruct((n, qw), BF16),
        compiler_params=_cparams(2),
        name="swa",
    )(sinks, z, z, z, cos_t, sin_t, cos_t, sin_t)


def _memkv_kernel(m_ref, g_ref, w_ref, o_ref):
    h = _rms(m_ref[...], g_ref[...]).astype(BF16)
    o_ref[...] = _dot(h, w_ref[...].astype(BF16)).astype(BF16)


def _memkv(mem, g, w):
    n, d = mem.shape
    ncols = w.shape[1]
    tm = _pick(n, (512, 256, 128))
    tn = _pick(ncols, (1024, 512, 256, 128))
    return pl.pallas_call(
        _memkv_kernel,
        grid=(ncols // tn, n // tm),
        in_specs=[
            pl.BlockSpec((tm, d), lambda j, i: (i, 0)),
            pl.BlockSpec((1, d), lambda j, i: (0, 0)),
            pl.BlockSpec((d, tn), lambda j, i: (0, j)),
        ],
        out_specs=pl.BlockSpec((tm, tn), lambda j, i: (i, j)),
        out_shape=jax.ShapeDtypeStruct((n, ncols), BF16),
        compiler_params=_cparams(2),
        name="memkv",
    )(mem, g.reshape(1, d), w)


def _mix_kernel(x_ref, yr_ref, ys_ref, wo_ref, bo_ref, g_ref, wq_ref, k_ref, v_ref, wxo_ref, o_ref):
    c = yr_ref.shape[1]
    d = x_ref.shape[1]
    hd = d // XATTN_HEADS
    x2 = x_ref[...] + _dot(yr_ref[...], wo_ref[:c, :]) + _dot(ys_ref[...], wo_ref[c:, :]) + bo_ref[...]
    q = _dot(_rms(x2, g_ref[...]).astype(BF16), wq_ref[...]).astype(BF16)
    scale = hd ** -0.5
    outs = []
    for h in range(XATTN_HEADS):
        sl = slice(h * hd, (h + 1) * hd)
        s = _dot_nt(q[:, sl], k_ref[:, sl]) * scale
        m = jnp.max(s, axis=-1, keepdims=True)
        e = jnp.exp(s - m)
        den = jnp.sum(e, axis=-1, keepdims=True)
        outs.append((_dot(e.astype(BF16), v_ref[:, sl]) * (1.0 / den)).astype(BF16))
    o_ref[...] = x2 + _dot(jnp.concatenate(outs, axis=1), wxo_ref[...])


def _mix(x, yr, ys, wo, bo, g, wq, kv, wxo, seq, mlen):
    n, d = x.shape
    c = yr.shape[1]
    tm = _pick(seq, (256, 128))
    per_seq = seq // tm
    once = pl.Buffered(1)
    return pl.pallas_call(
        _mix_kernel,
        grid=(n // tm,),
        in_specs=[
            pl.BlockSpec((tm, d), lambda i: (i, 0)),
            pl.BlockSpec((tm, c), lambda i: (i, 0)),
            pl.BlockSpec((tm, d - c), lambda i: (i, 0)),
            pl.BlockSpec((d, d), lambda i: (0, 0), pipeline_mode=once),
            pl.BlockSpec((1, d), lambda i: (0, 0)),
            pl.BlockSpec((1, d), lambda i: (0, 0)),
            pl.BlockSpec((d, d), lambda i: (0, 0), pipeline_mode=once),
            pl.BlockSpec((mlen, d), lambda i: (i // per_seq, 0)),
            pl.BlockSpec((mlen, d), lambda i: (i // per_seq, 1)),
            pl.BlockSpec((d, d), lambda i: (0, 0), pipeline_mode=once),
        ],
        out_specs=pl.BlockSpec((tm, d), lambda i: (i, 0)),
        out_shape=jax.ShapeDtypeStruct((n, d), F32),
        compiler_params=_cparams(1),
        name="mix",
    )(x, yr, ys, wo, bo.reshape(1, d), g.reshape(1, d), wq, kv, kv, wxo)


def _pad_cols(w, width):
    return jnp.pad(w, ((0, 0), (0, width - w.shape[1])))


def _pad_rows(w, height):
    return jnp.pad(w, ((0, height - w.shape[0]), (0, 0)))


def _rope_tables(seq):
    half = HEAD_DIM // 2
    lane = jnp.arange(LANES)
    inv_freq = ROPE_THETA ** (-jnp.arange(0, HEAD_DIM, 2, dtype=F32) / HEAD_DIM)
    ang = jnp.arange(seq, dtype=F32)[:, None] * inv_freq[lane % half][None, :]
    sign = jnp.where((lane % HEAD_DIM) < half, -1.0, 1.0)
    return jnp.cos(ang), jnp.sin(ang) * sign[None, :]


def kernel(x, mem, f1_norm, f1_gate, f1_up, f1_down, mix_norm, w_in, b_in_attn, rw_mu, rw_w0, rw_decay_up, rw_a0, rw_aaa_up, rw_gate_up, rw_k_k, rw_k_a, rw_r_k, rw_lnx_w, rw_lnx_b, attn_sinks, w_out, b_out, xa_norm, mem_norm, w_xq, w_xkv, w_xo, f2_norm, f2_gate, f2_up, f2_down, final_norm):
    batch, seq, d = x.shape
    mlen = mem.shape[1]
    depth = f1_norm.shape[0]
    c = rw_w0.shape[1]
    sw = d - c
    dl, al, gl = rw_decay_up.shape[1], rw_aaa_up.shape[1], rw_gate_up.shape[1]
    dlp, alp, glp = (_round_up(v, LANES) for v in (dl, al, gl))
    kvw = b_in_attn.shape[1] - sw
    q_heads = sw // HEAD_DIM
    group = q_heads // (kvw // (2 * HEAD_DIM))
    n_tiles = c // LANES
    n = batch * seq
    q_off = 3 * c
    lora_off = q_off + sw
    lora_w = dlp + alp + glp
    kv_off = lora_off + lora_w
    ncols = _round_up(kv_off + kvw, 1024)
    cos_t, sin_t = _rope_tables(seq)

    xf = x.reshape(n, d)
    memf = mem.reshape(batch * mlen, d)
    for l in range(depth):
        xf = _ffn(xf, f1_norm[l], f1_gate[l].astype(BF16), f1_up[l].astype(BF16), f1_down[l].astype(BF16), None)

        wl = w_in[l]
        o1, o2, o3 = 3 * c, 3 * c + dl, 3 * c + dl + al
        shift = o3 + gl
        w_all = jnp.concatenate([
            wl[:, :o1], wl[:, shift:shift + sw],
            _pad_cols(wl[:, o1:o2], dlp), _pad_cols(wl[:, o2:o3], alp), _pad_cols(wl[:, o3:shift], glp),
            wl[:, shift + sw:]], axis=1)
        w_all = _pad_cols(w_all, ncols).astype(BF16)
        mu = rw_mu[l][None, :]
        mu_all = _pad_cols(jnp.concatenate([
            mu[:, :o1], jnp.zeros((1, sw), F32),
            _pad_cols(mu[:, o1:o2], dlp), _pad_cols(mu[:, o2:o3], alp), _pad_cols(mu[:, o3:shift], glp)],
            axis=1), ncols)
        bia = b_in_attn[l][None, :]
        b_all = _pad_cols(jnp.concatenate([
            jnp.zeros((1, q_off), F32), bia[:, :sw], jnp.zeros((1, lora_w), F32), bia[:, sw:]], axis=1), ncols)
        z = _proj(xf, mix_norm[l], w_all, mu_all, b_all, seq)

        tiles = lambda v: v.reshape(n_tiles, 1, LANES)
        vecs = [rw_w0[l].reshape(1, c), rw_a0[l].reshape(1, c), tiles(rw_k_k[l]), tiles(rw_k_a[l]),
                tiles(rw_r_k[l]), tiles(rw_lnx_w[l]), tiles(rw_lnx_b[l])]
        loras = [_pad_rows(rw_decay_up[l], dlp).astype(BF16), _pad_rows(rw_aaa_up[l], alp).astype(BF16),
                 _pad_rows(rw_gate_up[l], glp).astype(BF16)]
        y_rwkv = _rwkv(z, batch, seq, c, lora_off, lora_w, dlp, alp, vecs, loras)
        y_swa = _swa(z, attn_sinks[l], cos_t, sin_t, batch, seq, q_off, sw, kv_off, kvw, group)

        kv_mem = _memkv(memf, mem_norm[l], w_xkv[l])
        xf = _mix(xf, y_rwkv, y_swa, w_out[l].astype(BF16), b_out[l], xa_norm[l], w_xq[l].astype(BF16),
                  kv_mem, w_xo[l].astype(BF16), seq, mlen)

        last = l == depth - 1
        xf = _ffn(xf, f2_norm[l], f2_gate[l].astype(BF16), f2_up[l].astype(BF16), f2_down[l].astype(BF16),
                  final_norm if last else None)
    return xf.reshape(batch, seq, d)
```

```python
import functools

import jax
import jax.numpy as jnp
from jax import lax
from jax.experimental import pallas as pl
from jax.experimental.pallas import tpu as pltpu

F32, BF16 = jnp.float32, jnp.bfloat16

LANES = 128
SUBLANES = 8
VMEM_LIMIT_BYTES = 56 * 1024 * 1024

HEAD_DIM = 64
HEADS_PER_TILE = LANES // HEAD_DIM
CHUNK = 128
BLOCK = 128
XATTN_HEADS = 4
RMS_EPS = 1e-6
GN_EPS = 64e-5
NEG_INF = -1e30
ROPE_THETA = 10000.0


def _round_up(n, m):
    return (n + m - 1) // m * m


def _pick(n, prefs):
    for p in prefs:
        if n % p == 0:
            return p
    raise ValueError(f"no tile in {prefs} divides {n}")


def _cparams(n_axes):
    return pltpu.CompilerParams(dimension_semantics=("arbitrary",) * n_axes,
                                vmem_limit_bytes=VMEM_LIMIT_BYTES)


def _dot(a, b):
    return jnp.dot(a, b, preferred_element_type=F32)


def _dot_nt(a, b):
    return lax.dot_general(a, b, (((1,), (1,)), ((), ())), preferred_element_type=F32)


def _split(x):
    hi = x.astype(BF16)
    lo = (x - hi.astype(F32)).astype(BF16)
    return hi, lo


def _rms(x, g):
    ms = jnp.mean(x * x, axis=-1, keepdims=True)
    return x * lax.rsqrt(ms + RMS_EPS) * g


def _softplus(x):
    return jnp.maximum(x, 0.0) + jnp.log(1.0 + jnp.exp(-jnp.abs(x)))


def _ffn_kernel(x_ref, g_ref, wg_ref, wu_ref, wd_ref, fg_ref, o_ref, h_ref, *, final_norm, rows):
    j = pl.program_id(1)
    tm = x_ref.shape[0]

    @pl.when(j == 0)
    def _():
        for r0 in range(0, tm, rows):
            x = x_ref[r0:r0 + rows, :]
            h_ref[r0:r0 + rows, :] = _rms(x, g_ref[...]).astype(BF16)
            o_ref[r0:r0 + rows, :] = x

    for r0 in range(0, tm, rows):
        h = h_ref[r0:r0 + rows, :]
        gate = _dot(h, wg_ref[...])
        up = _dot(h, wu_ref[...])
        act = (gate * jax.nn.sigmoid(gate) * up).astype(BF16)
        o_ref[r0:r0 + rows, :] += 0.5 * _dot(act, wd_ref[...])

    if final_norm:
        @pl.when(j == pl.num_programs(1) - 1)
        def _():
            for r0 in range(0, tm, rows):
                o_ref[r0:r0 + rows, :] = _rms(o_ref[r0:r0 + rows, :], fg_ref[...])


def _ffn(x, g, wg, wu, wd, fg):
    n, d = x.shape
    f = wg.shape[1]
    tm = _pick(n, (1024, 512, 256, 128))
    tf = _pick(f, (512, 256, 128))
    rows = min(tm, 512)
    final_norm = fg is not None
    fg = g if fg is None else fg
    return pl.pallas_call(
        functools.partial(_ffn_kernel, final_norm=final_norm, rows=rows),
        grid=(n // tm, f // tf),
        in_specs=[
            pl.BlockSpec((tm, d), lambda i, j: (i, 0)),
            pl.BlockSpec((1, d), lambda i, j: (0, 0)),
            pl.BlockSpec((d, tf), lambda i, j: (0, j)),
            pl.BlockSpec((d, tf), lambda i, j: (0, j)),
            pl.BlockSpec((tf, d), lambda i, j: (j, 0)),
            pl.BlockSpec((1, d), lambda i, j: (0, 0)),
        ],
        out_specs=pl.BlockSpec((tm, d), lambda i, j: (i, 0)),
        out_shape=jax.ShapeDtypeStruct((n, d), F32),
        scratch_shapes=[pltpu.VMEM((tm, d), BF16)],
        compiler_params=_cparams(2),
        name="ffn",
    )(x, g.reshape(1, d), wg, wu, wd, fg.reshape(1, d))


def _proj_kernel(x_ref, g_ref, w_ref, mu_ref, b_ref, o_ref, carry_ref, *, tiles_per_seq, tn):
    i = pl.program_id(0)
    tm = x_ref.shape[0]
    ncols = w_ref.shape[1]

    @pl.when(i % tiles_per_seq == 0)
    def _():
        carry_ref[...] = jnp.zeros(carry_ref.shape, F32)

    h = _rms(x_ref[...], g_ref[...]).astype(BF16)
    row = lax.broadcasted_iota(jnp.int32, (SUBLANES, tn), 0)
    for c0 in range(0, ncols, tn):
        cs = slice(c0, c0 + tn)
        z = _dot(h, w_ref[:, cs])
        prev_tail = carry_ref[:, cs]
        carry_ref[:, cs] = z[tm - SUBLANES:, :]
        zs = pltpu.roll(z, 1, 0)
        head = jnp.where(row == 0, pltpu.roll(prev_tail, 1, 0), zs[:SUBLANES])
        mu = mu_ref[:, cs]
        b = b_ref[:, cs]
        z0 = z[:SUBLANES]
        o_ref[:SUBLANES, cs] = z0 + (head - z0) * mu + b
        z1 = z[SUBLANES:]
        o_ref[SUBLANES:, cs] = z1 + (zs[SUBLANES:] - z1) * mu + b


def _proj(x, g, w_all, mu_all, b_all, seq):
    n, d = x.shape
    ncols = w_all.shape[1]
    tm = _pick(seq, (256, 128))
    tn = _pick(ncols, (1024, 512, 256, 128))
    return pl.pallas_call(
        functools.partial(_proj_kernel, tiles_per_seq=seq // tm, tn=tn),
        grid=(n // tm,),
        in_specs=[
            pl.BlockSpec((tm, d), lambda i: (i, 0)),
            pl.BlockSpec((1, d), lambda i: (0, 0)),
            pl.BlockSpec((d, ncols), lambda i: (0, 0), pipeline_mode=pl.Buffered(1)),
            pl.BlockSpec((1, ncols), lambda i: (0, 0)),
            pl.BlockSpec((1, ncols), lambda i: (0, 0)),
        ],
        out_specs=pl.BlockSpec((tm, ncols), lambda i: (i, 0)),
        out_shape=jax.ShapeDtypeStruct((n, ncols), F32),
        scratch_shapes=[pltpu.VMEM((SUBLANES, ncols), F32)],
        compiler_params=_cparams(1),
        name="proj",
    )(x, g.reshape(1, d), w_all, mu_all, b_all)


def _bdot(a, b):
    return lax.dot_general(a, b, (((2,), (1,)), ((0,), (0,))), preferred_element_type=F32)


def _bdot_nt(a, b):
    return lax.dot_general(a, b, (((2,), (2,)), ((0,), (0,))), preferred_element_type=F32)


def _bdot3(a, b):
    ah, al = _split(a)
    bh, bl = _split(b)
    return _bdot(ah, bh) + _bdot(ah, bl) + _bdot(al, bh)


def _bdot3_nt(a, b):
    ah, al = _split(a)
    bh, bl = _split(b)
    return _bdot_nt(ah, bh) + _bdot_nt(ah, bl) + _bdot_nt(al, bh)


def _inv_unit_lower(nmat):
    L = nmat.shape[-1]
    r = lax.broadcasted_iota(jnp.int32, (L, L), 0)
    c = lax.broadcasted_iota(jnp.int32, (L, L), 1)
    eye = jnp.where(r == c, 1.0, 0.0)
    t = eye + nmat
    pw = nmat.astype(BF16)
    pw = _bdot(pw, pw).astype(BF16)
    steps = L.bit_length() - 2
    for i in range(steps):
        if i + 1 < steps:
            both = _bdot(jnp.concatenate([t.astype(BF16), pw], axis=1), pw)
            t = t + both[:, :L]
            pw = both[:, L:].astype(BF16)
        else:
            t = t + _bdot(t.astype(BF16), pw)
    resid = eye - t + _bdot3(nmat, t)
    return t + _bdot(t.astype(BF16), resid.astype(BF16))


def _rwkv_kernel(r_ref, k_ref, v_ref, lora_ref, w0_ref, a0_ref, kk_ref, ka_ref, rk_ref, gw_ref, gb_ref,
                 du_ref, au_ref, gu_ref,
                 o_ref,
                 s_ref, pr_ref, pk_ref, pv_ref, pa_ref, pg_ref, plp_ref, plw_ref,
                 *, n_tiles, seqs, dlp, alp):
    L = CHUNK
    nt = seqs * n_tiles

    @pl.when(pl.program_id(1) == 0)
    def _():
        s_ref[...] = jnp.zeros(s_ref.shape, F32)

    lora = lora_ref[...].reshape(seqs * L, lora_ref.shape[2])
    wd = jnp.tanh(lora[:, :dlp])
    ad = lora[:, dlp:dlp + alp]
    gd = jax.nn.sigmoid(lora[:, dlp + alp:])
    w = -_softplus(-(w0_ref[...] + _dot(wd.astype(BF16), du_ref[...]))) - 0.5
    lw = -jnp.exp(w)
    asig = jax.nn.sigmoid(a0_ref[...] + _dot(ad.astype(BF16), au_ref[...]))
    gate = _dot(gd.astype(BF16), gu_ref[...])
    row = lax.broadcasted_iota(jnp.int32, (L, L), 0)
    col = lax.broadcasted_iota(jnp.int32, (L, L), 1)
    incl = row >= col
    strict = row > col
    tril = jnp.where(incl, 1.0, 0.0).astype(BF16)
    h1 = lw.astype(BF16)
    r1 = lw - h1.astype(F32)
    h2 = r1.astype(BF16)
    h3 = (r1 - h2.astype(F32)).astype(BF16)
    for q in range(seqs):
        rows = slice(q * L, (q + 1) * L)
        logp = _dot(tril, h1[rows]) + _dot(tril, h2[rows]) + _dot(tril, h3[rows])
        for p in range(n_tiles):
            sl = slice(p * LANES, (p + 1) * LANES)
            i = q * n_tiles + p
            pr_ref[i] = r_ref[q, :, sl]
            pk_ref[i] = k_ref[q, :, sl]
            pv_ref[i] = v_ref[q, :, sl]
            pa_ref[i] = asig[rows, sl]
            pg_ref[i] = gate[rows, sl]
            plp_ref[i] = logp[:, sl]
            plw_ref[i] = lw[rows, sl]

    lane = lax.broadcasted_iota(jnp.int32, (1, LANES), 1)
    lo = lane < HEAD_DIM
    same_head = jnp.where(row < HEAD_DIM, 0, 1) == jnp.where(col < HEAD_DIM, 0, 1)
    inv_hd = 1.0 / HEAD_DIM

    def first(x):
        return jnp.where(lo, x, 0.0)

    def second(x):
        return jnp.where(lo, 0.0, x)

    def segsum(x):
        return jnp.where(lo, jnp.sum(first(x), axis=-1, keepdims=True),
                         jnp.sum(second(x), axis=-1, keepdims=True))

    r = pr_ref[...]
    k = pk_ref[...]
    v = pv_ref[...]
    a_s = pa_ref[...]
    lp = plp_ref[...]
    kk = k * kk_ref[...]
    ss = segsum(kk * kk)
    kk = kk / jnp.maximum(jnp.sqrt(ss), 1e-12)
    a = -kk
    b = kk * a_s
    km = k * (1.0 + (a_s - 1.0) * ka_ref[...])
    cmid = lp[:, L // 2 - 1:L // 2, :]
    clast = lp[:, L - 1:L, :]
    lpe = lp - plw_ref[...]
    e_inv = jnp.exp(cmid - lp)
    at = a * jnp.exp(lpe - cmid)
    a_abs = a * jnp.exp(lpe)
    rt = r * jnp.exp(lp - cmid)
    r_abs = r * jnp.exp(lp)
    bt = b * e_inv
    kt = km * e_inv
    e_l = jnp.exp(clast - cmid)
    e_p = jnp.exp(clast)
    bk = jnp.concatenate([bt, kt], axis=1)
    g_a = _bdot3_nt(jnp.concatenate([first(at), second(at)], axis=1), bk)
    g_r = _bdot_nt(jnp.concatenate([first(rt), second(rt)], axis=1).astype(BF16), bk.astype(BF16))
    n_lo = jnp.where(strict, g_a[:, 0:L, 0:L], 0.0)
    ak_lo = jnp.where(strict, g_a[:, 0:L, L:], 0.0)
    n_hi = jnp.where(strict, g_a[:, L:, 0:L], 0.0)
    ak_hi = jnp.where(strict, g_a[:, L:, L:], 0.0)
    rb_lo = jnp.where(incl, g_r[:, 0:L, 0:L], 0.0)
    rk_lo = jnp.where(incl, g_r[:, 0:L, L:], 0.0)
    rb_hi = jnp.where(incl, g_r[:, L:, 0:L], 0.0)
    rk_hi = jnp.where(incl, g_r[:, L:, L:], 0.0)
    t_all = _inv_unit_lower(jnp.concatenate([n_lo, n_hi], axis=0))
    t_lo, t_hi = t_all[:nt], t_all[nt:]
    s = s_ref[...]
    v_lo, v_hi = first(v), second(v)
    rhs = _bdot3_nt(a_abs, s) + _bdot3(jnp.concatenate([ak_lo, ak_hi], axis=2),
                                       jnp.concatenate([v_lo, v_hi], axis=1))
    u = _bdot3(jnp.concatenate([t_lo, t_hi], axis=2),
               jnp.concatenate([first(rhs), second(rhs)], axis=1))
    y = _bdot_nt(r_abs.astype(BF16), s.astype(BF16)) + _bdot(
        jnp.concatenate([rb_lo, rb_hi, rk_lo, rk_hi], axis=2).astype(BF16),
        jnp.concatenate([first(u), second(u), v_lo, v_hi], axis=1).astype(BF16))
    upd = _bdot3(jnp.concatenate([jnp.swapaxes(u, 1, 2), jnp.swapaxes(v, 1, 2)], axis=2),
                 jnp.concatenate([bt * e_l, kt * e_l], axis=1))
    s_ref[...] = s * e_p + jnp.where(same_head, upd, 0.0)
    mean = segsum(y) * inv_hd
    dev = y - mean
    var = segsum(dev * dev) * inv_hd
    yn = dev * lax.rsqrt(var + GN_EPS) * gw_ref[...] + gb_ref[...]
    bonus = segsum(r * km * rk_ref[...]) * v
    out = ((yn + bonus) * pg_ref[...]).astype(BF16)
    for q in range(seqs):
        for p in range(n_tiles):
            o_ref[q, :, p * LANES:(p + 1) * LANES] = out[q * n_tiles + p]


def _rwkv(z, batch, seq, c, lora_off, lora_w, dlp, alp, vecs, loras):
    n = z.shape[0]
    nc = seq // CHUNK
    n_tiles = c // LANES
    seqs = 1
    nt = seqs * n_tiles
    assert lora_off % lora_w == 0
    lora_blk = lora_off // lora_w
    z3 = z.reshape(batch, seq, z.shape[1])
    vecs = vecs[:2] + [jnp.tile(v, (seqs, 1, 1)) for v in vecs[2:]]
    vec_spec = pl.BlockSpec((nt, 1, LANES), lambda b, t: (0, 0, 0))
    in_specs = [
        pl.BlockSpec((seqs, CHUNK, c), lambda b, t: (b, t, 0)),
        pl.BlockSpec((seqs, CHUNK, c), lambda b, t: (b, t, 1)),
        pl.BlockSpec((seqs, CHUNK, c), lambda b, t: (b, t, 2)),
        pl.BlockSpec((seqs, CHUNK, lora_w), lambda b, t: (b, t, lora_blk)),
        pl.BlockSpec((1, c), lambda b, t: (0, 0)),
        pl.BlockSpec((1, c), lambda b, t: (0, 0)),
    ] + [vec_spec] * 5 + [pl.BlockSpec(w.shape, lambda b, t: (0, 0)) for w in loras]
    tile_f32 = pltpu.VMEM((nt, CHUNK, LANES), F32)
    return pl.pallas_call(
        functools.partial(_rwkv_kernel, n_tiles=n_tiles, seqs=seqs, dlp=dlp, alp=alp),
        grid=(batch // seqs, nc),
        in_specs=in_specs,
        out_specs=pl.BlockSpec((seqs, CHUNK, c), lambda b, t: (b, t, 0)),
        out_shape=jax.ShapeDtypeStruct((batch, seq, c), BF16),
        scratch_shapes=[pltpu.VMEM((nt, LANES, LANES), F32)] + [tile_f32] * 7,
        compiler_params=_cparams(2),
        name="rwkv",
    )(z3, z3, z3, z3, *vecs, *loras).reshape(n, c)


def _swa_kernel(sink_ref, q_ref, kvc_ref, kvp_ref, cosc_ref, sinc_ref, cosp_ref, sinp_ref, o_ref,
                *, n_tiles, group, qb):
    nblk = pl.program_id(1)
    lane = lax.broadcasted_iota(jnp.int32, (1, LANES), 1)
    lo = lane < HEAD_DIM
    rot_lo = jnp.bitwise_and(lane, HEAD_DIM - 1) < HEAD_DIM // 2

    def rope(x, cos, sin_signed):
        partner = jnp.where(rot_lo, pltpu.roll(x, LANES - HEAD_DIM // 2, 1), pltpu.roll(x, HEAD_DIM // 2, 1))
        return x * cos + partner * sin_signed

    cosc = cosc_ref[...]
    sinc = sinc_ref[...]
    kvc = kvc_ref[...]
    kvp = kvp_ref[...]
    keys = jnp.concatenate([rope(kvp[:, :LANES], cosp_ref[...], sinp_ref[...]),
                            rope(kvc[:, :LANES], cosc, sinc)], axis=0)
    vals = jnp.concatenate([kvp[:, LANES:], kvc[:, LANES:]], axis=0)
    keys_sw = pltpu.roll(keys, HEAD_DIM, 1)
    vals_sw = pltpu.roll(vals, HEAD_DIM, 1)
    k_first = [jnp.where(lo, keys, 0.0).astype(BF16), jnp.where(lo, keys_sw, 0.0).astype(BF16)]
    k_second = [jnp.where(lo, 0.0, keys_sw).astype(BF16), jnp.where(lo, 0.0, keys).astype(BF16)]
    v_first = [jnp.where(lo, vals, 0.0).astype(BF16), jnp.where(lo, vals_sw, 0.0).astype(BF16)]
    v_second = [jnp.where(lo, 0.0, vals_sw).astype(BF16), jnp.where(lo, 0.0, vals).astype(BF16)]
    qi = lax.broadcasted_iota(jnp.int32, (BLOCK, 2 * BLOCK), 0)
    ki = lax.broadcasted_iota(jnp.int32, (BLOCK, 2 * BLOCK), 1)
    window = (ki > qi) & (ki <= qi + BLOCK)
    valid = [window & ((nblk > 0) | (ki >= BLOCK))] + [window] * (qb - 1)
    scale = HEAD_DIM ** -0.5
    ones_kv = jnp.ones((2 * BLOCK, LANES), BF16)
    heads = [(j, p, half) for j in range(qb) for p in range(n_tiles) for half in range(HEADS_PER_TILE)]
    scores = []
    for j in range(qb):
        qrows = slice(j * BLOCK, (j + 1) * BLOCK)
        krows = slice(j * BLOCK, (j + 2) * BLOCK)
        for p in range(n_tiles):
            g = (p * HEADS_PER_TILE) // group
            qp = (rope(q_ref[qrows, p * LANES:(p + 1) * LANES], cosc[qrows], sinc[qrows]) * scale).astype(BF16)
            scores += [_dot_nt(qp, k_first[g][krows]), _dot_nt(qp, k_second[g][krows])]
    probs, sink_terms = [], []
    for (j, p, half), s in zip(heads, scores):
        s = jnp.where(valid[j], s, NEG_INF)
        sink = sink_ref[p * HEADS_PER_TILE + half]
        m = jnp.maximum(jnp.max(s, axis=-1, keepdims=True), sink)
        probs.append(jnp.exp(s - m).astype(BF16))
        sink_terms.append(jnp.exp(sink - m))
    outs = []
    for (j, p, half), e, st in zip(heads, probs, sink_terms):
        g = (p * HEADS_PER_TILE) // group
        krows = slice(j * BLOCK, (j + 2) * BLOCK)
        den = _dot(e, ones_kv) + st
        outs.append(_dot(e, (v_second[g] if half else v_first[g])[krows]) * (1.0 / den))
    for j in range(qb):
        for p in range(n_tiles):
            i = (j * n_tiles + p) * HEADS_PER_TILE
            o_ref[j * BLOCK:(j + 1) * BLOCK, p * LANES:(p + 1) * LANES] = (outs[i] + outs[i + 1]).astype(BF16)


def _swa(z, sinks, cos_t, sin_t, batch, seq, q_off, qw, kv_off, kvw, group):
    n = z.shape[0]
    nb = seq // BLOCK
    assert q_off % qw == 0 and kv_off % kvw == 0 and kvw == 2 * LANES
    q_blk, kv_blk = q_off // qw, kv_off // kvw
    qb = 2 if nb % 2 == 0 else 1
    steps = nb // qb
    prev = lambda t: jnp.maximum(t * qb - 1, 0)
    return pl.pallas_call(
        functools.partial(_swa_kernel, n_tiles=qw // LANES, group=group, qb=qb),
        grid=(batch, steps),
        in_specs=[
            pl.BlockSpec(memory_space=pltpu.SMEM),
            pl.BlockSpec((qb * BLOCK, qw), lambda b, t: (b * steps + t, q_blk)),
            pl.BlockSpec((qb * BLOCK, kvw), lambda b, t: (b * steps + t, kv_blk)),
            pl.BlockSpec((BLOCK, kvw), lambda b, t: (b * nb + prev(t), kv_blk)),
            pl.BlockSpec((qb * BLOCK, LANES), lambda b, t: (t, 0)),
            pl.BlockSpec((qb * BLOCK, LANES), lambda b, t: (t, 0)),
            pl.BlockSpec((BLOCK, LANES), lambda b, t: (prev(t), 0)),
            pl.BlockSpec((BLOCK, LANES), lambda b, t: (prev(t), 0)),
        ],
        out_specs=pl.BlockSpec((qb * BLOCK, qw), lambda b, t: (b * steps + t, 0)),
        out_shape=jax.ShapeDtypeStruct((n, qw), BF16),
        compiler_params=_cparams(2),
        name="swa",
    )(sinks, z, z, z, cos_t, sin_t, cos_t, sin_t)


def _memkv_kernel(m_ref, g_ref, w_ref, o_ref):
    h = _rms(m_ref[...], g_ref[...]).astype(BF16)
    o_ref[...] = _dot(h, w_ref[...].astype(BF16)).astype(BF16)


def _memkv(mem, g, w):
    n, d = mem.shape
    ncols = w.shape[1]
    tm = _pick(n, (512, 256, 128))
    tn = _pick(ncols, (1024, 512, 256, 128))
    return pl.pallas_call(
        _memkv_kernel,
        grid=(ncols // tn, n // tm),
        in_specs=[
            pl.BlockSpec((tm, d), lambda j, i: (i, 0)),
            pl.BlockSpec((1, d), lambda j, i: (0, 0)),
            pl.BlockSpec((d, tn), lambda j, i: (0, j)),
        ],
        out_specs=pl.BlockSpec((tm, tn), lambda j, i: (i, j)),
        out_shape=jax.ShapeDtypeStruct((n, ncols), BF16),
        compiler_params=_cparams(2),
        name="memkv",
    )(mem, g.reshape(1, d), w)


def _mix_kernel(x_ref, yr_ref, ys_ref, wo_ref, bo_ref, g_ref, wq_ref, k_ref, v_ref, wxo_ref, o_ref):
    c = yr_ref.shape[1]
    d = x_ref.shape[1]
    hd = d // XATTN_HEADS
    x2 = x_ref[...] + _dot(yr_ref[...], wo_ref[:c, :]) + _dot(ys_ref[...], wo_ref[c:, :]) + bo_ref[...]
    q = _dot(_rms(x2, g_ref[...]).astype(BF16), wq_ref[...]).astype(BF16)
    scale = hd ** -0.5
    outs = []
    for h in range(XATTN_HEADS):
        sl = slice(h * hd, (h + 1) * hd)
        s = _dot_nt(q[:, sl], k_ref[:, sl]) * scale
        m = jnp.max(s, axis=-1, keepdims=True)
        e = jnp.exp(s - m)
        den = jnp.sum(e, axis=-1, keepdims=True)
        outs.append((_dot(e.astype(BF16), v_ref[:, sl]) * (1.0 / den)).astype(BF16))
    o_ref[...] = x2 + _dot(jnp.concatenate(outs, axis=1), wxo_ref[...])


def _mix(x, yr, ys, wo, bo, g, wq, kv, wxo, seq, mlen):
    n, d = x.shape
    c = yr.shape[1]
    tm = _pick(seq, (256, 128))
    per_seq = seq // tm
    once = pl.Buffered(1)
    return pl.pallas_call(
        _mix_kernel,
        grid=(n // tm,),
        in_specs=[
            pl.BlockSpec((tm, d), lambda i: (i, 0)),
            pl.BlockSpec((tm, c), lambda i: (i, 0)),
            pl.BlockSpec((tm, d - c), lambda i: (i, 0)),
            pl.BlockSpec((d, d), lambda i: (0, 0), pipeline_mode=once),
            pl.BlockSpec((1, d), lambda i: (0, 0)),
            pl.BlockSpec((1, d), lambda i: (0, 0)),
            pl.BlockSpec((d, d), lambda i: (0, 0), pipeline_mode=once),
            pl.BlockSpec((mlen, d), lambda i: (i // per_seq, 0)),
            pl.BlockSpec((mlen, d), lambda i: (i // per_seq, 1)),
            pl.BlockSpec((d, d), lambda i: (0, 0), pipeline_mode=once),
        ],
        out_specs=pl.BlockSpec((tm, d), lambda i: (i, 0)),
        out_shape=jax.ShapeDtypeStruct((n, d), F32),
        compiler_params=_cparams(1),
        name="mix",
    )(x, yr, ys, wo, bo.reshape(1, d), g.reshape(1, d), wq, kv, kv, wxo)


def _pad_cols(w, width):
    return jnp.pad(w, ((0, 0), (0, width - w.shape[1])))


def _pad_rows(w, height):
    return jnp.pad(w, ((0, height - w.shape[0]), (0, 0)))


def _rope_tables(seq):
    half = HEAD_DIM // 2
    lane = jnp.arange(LANES)
    inv_freq = ROPE_THETA ** (-jnp.arange(0, HEAD_DIM, 2, dtype=F32) / HEAD_DIM)
    ang = jnp.arange(seq, dtype=F32)[:, None] * inv_freq[lane % half][None, :]
    sign = jnp.where((lane % HEAD_DIM) < half, -1.0, 1.0)
    return jnp.cos(ang), jnp.sin(ang) * sign[None, :]


def kernel(x, mem, f1_norm, f1_gate, f1_up, f1_down, mix_norm, w_in, b_in_attn, rw_mu, rw_w0, rw_decay_up, rw_a0, rw_aaa_up, rw_gate_up, rw_k_k, rw_k_a, rw_r_k, rw_lnx_w, rw_lnx_b, attn_sinks, w_out, b_out, xa_norm, mem_norm, w_xq, w_xkv, w_xo, f2_norm, f2_gate, f2_up, f2_down, final_norm):
    batch, seq, d = x.shape
    mlen = mem.shape[1]
    depth = f1_norm.shape[0]
    c = rw_w0.shape[1]
    sw = d - c
    dl, al, gl = rw_decay_up.shape[1], rw_aaa_up.shape[1], rw_gate_up.shape[1]
    dlp, alp, glp = (_round_up(v, LANES) for v in (dl, al, gl))
    kvw = b_in_attn.shape[1] - sw
    q_heads = sw // HEAD_DIM
    group = q_heads // (kvw // (2 * HEAD_DIM))
    n_tiles = c // LANES
    n = batch * seq
    q_off = 3 * c
    lora_off = q_off + sw
    lora_w = dlp + alp + glp
    kv_off = lora_off + lora_w
    ncols = _round_up(kv_off + kvw, 2 * LANES)
    cos_t, sin_t = _rope_tables(seq)

    xf = x.reshape(n, d)
    memf = mem.reshape(batch * mlen, d)
    for l in range(depth):
        xf = _ffn(xf, f1_norm[l], f1_gate[l].astype(BF16), f1_up[l].astype(BF16), f1_down[l].astype(BF16), None)

        wl = w_in[l]
        o1, o2, o3 = 3 * c, 3 * c + dl, 3 * c + dl + al
        shift = o3 + gl
        w_all = jnp.concatenate([
            wl[:, :o1], wl[:, shift:shift + sw],
            _pad_cols(wl[:, o1:o2], dlp), _pad_cols(wl[:, o2:o3], alp), _pad_cols(wl[:, o3:shift], glp),
            wl[:, shift + sw:]], axis=1)
        w_all = _pad_cols(w_all, ncols).astype(BF16)
        mu = rw_mu[l][None, :]
        mu_all = _pad_cols(jnp.concatenate([
            mu[:, :o1], jnp.zeros((1, sw), F32),
            _pad_cols(mu[:, o1:o2], dlp), _pad_cols(mu[:, o2:o3], alp), _pad_cols(mu[:, o3:shift], glp)],
            axis=1), ncols)
        bia = b_in_attn[l][None, :]
        b_all = _pad_cols(jnp.concatenate([
            jnp.zeros((1, q_off), F32), bia[:, :sw], jnp.zeros((1, lora_w), F32), bia[:, sw:]], axis=1), ncols)
        z = _proj(xf, mix_norm[l], w_all, mu_all, b_all, seq)

        tiles = lambda v: v.reshape(n_tiles, 1, LANES)
        vecs = [rw_w0[l].reshape(1, c), rw_a0[l].reshape(1, c), tiles(rw_k_k[l]), tiles(rw_k_a[l]),
                tiles(rw_r_k[l]), tiles(rw_lnx_w[l]), tiles(rw_lnx_b[l])]
        loras = [_pad_rows(rw_decay_up[l], dlp).astype(BF16), _pad_rows(rw_aaa_up[l], alp).astype(BF16),
                 _pad_rows(rw_gate_up[l], glp).astype(BF16)]
        y_rwkv = _rwkv(z, batch, seq, c, lora_off, lora_w, dlp, alp, vecs, loras)
        y_swa = _swa(z, attn_sinks[l], cos_t, sin_t, batch, seq, q_off, sw, kv_off, kvw, group)

        kv_mem = _memkv(memf, mem_norm[l], w_xkv[l])
        xf = _mix(xf, y_rwkv, y_swa, w_out[l].astype(BF16), b_out[l], xa_norm[l], w_xq[l].astype(BF16),
                  kv_mem, w_xo[l].astype(BF16), seq, mlen)

        last = l == depth - 1
        xf = _ffn(xf, f2_norm[l], f2_gate[l].astype(BF16), f2_up[l].astype(BF16), f2_down[l].astype(BF16),
                  final_norm if last else None)
    return xf.reshape(batch, seq, d)
```

```python
import functools

import jax
import jax.numpy as jnp
from jax import lax
from jax.experimental import pallas as pl
from jax.experimental.pallas import tpu as pltpu

F32, BF16 = jnp.float32, jnp.bfloat16

LANES = 128
SUBLANES = 8
VMEM_LIMIT_BYTES = 56 * 1024 * 1024

HEAD_DIM = 64
HEADS_PER_TILE = LANES // HEAD_DIM
CHUNK = 128
BLOCK = 128
XATTN_HEADS = 4
RMS_EPS = 1e-6
GN_EPS = 64e-5
NEG_INF = -1e30
ROPE_THETA = 10000.0


def _round_up(n, m):
    return (n + m - 1) // m * m


def _pick(n, prefs):
    for p in prefs:
        if n % p == 0:
            return p
    raise ValueError(f"no tile in {prefs} divides {n}")


def _cparams(n_axes):
    return pltpu.CompilerParams(dimension_semantics=("arbitrary",) * n_axes,
                                vmem_limit_bytes=VMEM_LIMIT_BYTES)


def _dot(a, b):
    return jnp.dot(a, b, preferred_element_type=F32)


def _dot_nt(a, b):
    return lax.dot_general(a, b, (((1,), (1,)), ((), ())), preferred_element_type=F32)


def _rms(x, g):
    ms = jnp.mean(x * x, axis=-1, keepdims=True)
    return x * lax.rsqrt(ms + RMS_EPS) * g


def _softplus(x):
    return jnp.maximum(x, 0.0) + jnp.log(1.0 + jnp.exp(-jnp.abs(x)))


def _ffn_kernel(x_ref, g_ref, wg_ref, wu_ref, wd_ref, fg_ref, o_ref, h_ref, *, final_norm, rows):
    j = pl.program_id(1)
    tm = x_ref.shape[0]

    @pl.when(j == 0)
    def _():
        for r0 in range(0, tm, rows):
            x = x_ref[r0:r0 + rows, :]
            h_ref[r0:r0 + rows, :] = _rms(x, g_ref[...]).astype(BF16)
            o_ref[r0:r0 + rows, :] = x

    for r0 in range(0, tm, rows):
        h = h_ref[r0:r0 + rows, :]
        gate = _dot(h, wg_ref[...])
        up = _dot(h, wu_ref[...])
        act = (gate * jax.nn.sigmoid(gate) * up).astype(BF16)
        o_ref[r0:r0 + rows, :] += 0.5 * _dot(act, wd_ref[...])

    if final_norm:
        @pl.when(j == pl.num_programs(1) - 1)
        def _():
            for r0 in range(0, tm, rows):
                o_ref[r0:r0 + rows, :] = _rms(o_ref[r0:r0 + rows, :], fg_ref[...])


def _ffn(x, g, wg, wu, wd, fg):
    n, d = x.shape
    f = wg.shape[1]
    tm = _pick(n, (1024, 512, 256, 128))
    tf = _pick(f, (512, 256, 128))
    rows = min(tm, 512)
    final_norm = fg is not None
    fg = g if fg is None else fg
    return pl.pallas_call(
        functools.partial(_ffn_kernel, final_norm=final_norm, rows=rows),
        grid=(n // tm, f // tf),
        in_specs=[
            pl.BlockSpec((tm, d), lambda i, j: (i, 0)),
            pl.BlockSpec((1, d), lambda i, j: (0, 0)),
            pl.BlockSpec((d, tf), lambda i, j: (0, j)),
            pl.BlockSpec((d, tf), lambda i, j: (0, j)),
            pl.BlockSpec((tf, d), lambda i, j: (j, 0)),
            pl.BlockSpec((1, d), lambda i, j: (0, 0)),
        ],
        out_specs=pl.BlockSpec((tm, d), lambda i, j: (i, 0)),
        out_shape=jax.ShapeDtypeStruct((n, d), F32),
        scratch_shapes=[pltpu.VMEM((tm, d), BF16)],
        compiler_params=_cparams(2),
        name="ffn",
    )(x, g.reshape(1, d), wg, wu, wd, fg.reshape(1, d))


def _proj_kernel(x_ref, g_ref, w_ref, mu_ref, b_ref, o_ref, carry_ref, *, tiles_per_seq, tn):
    i = pl.program_id(0)
    tm = x_ref.shape[0]
    ncols = w_ref.shape[1]

    @pl.when(i % tiles_per_seq == 0)
    def _():
        carry_ref[...] = jnp.zeros(carry_ref.shape, F32)

    h = _rms(x_ref[...], g_ref[...]).astype(BF16)
    row = lax.broadcasted_iota(jnp.int32, (SUBLANES, tn), 0)
    for c0 in range(0, ncols, tn):
        cs = slice(c0, c0 + tn)
        z = _dot(h, w_ref[:, cs])
        prev_tail = carry_ref[:, cs]
        carry_ref[:, cs] = z[tm - SUBLANES:, :]
        zs = pltpu.roll(z, 1, 0)
        head = jnp.where(row == 0, pltpu.roll(prev_tail, 1, 0), zs[:SUBLANES])
        mu = mu_ref[:, cs]
        b = b_ref[:, cs]
        z0 = z[:SUBLANES]
        o_ref[:SUBLANES, cs] = z0 + (head - z0) * mu + b
        z1 = z[SUBLANES:]
        o_ref[SUBLANES:, cs] = z1 + (zs[SUBLANES:] - z1) * mu + b


def _proj(x, g, w_all, mu_all, b_all, seq):
    n, d = x.shape
    ncols = w_all.shape[1]
    tm = _pick(seq, (256, 128))
    tn = _pick(ncols, (1024, 512, 256, 128))
    return pl.pallas_call(
        functools.partial(_proj_kernel, tiles_per_seq=seq // tm, tn=tn),
        grid=(n // tm,),
        in_specs=[
            pl.BlockSpec((tm, d), lambda i: (i, 0)),
            pl.BlockSpec((1, d), lambda i: (0, 0)),
            pl.BlockSpec((d, ncols), lambda i: (0, 0), pipeline_mode=pl.Buffered(1)),
            pl.BlockSpec((1, ncols), lambda i: (0, 0)),
            pl.BlockSpec((1, ncols), lambda i: (0, 0)),
        ],
        out_specs=pl.BlockSpec((tm, ncols), lambda i: (i, 0)),
        out_shape=jax.ShapeDtypeStruct((n, ncols), F32),
        scratch_shapes=[pltpu.VMEM((SUBLANES, ncols), F32)],
        compiler_params=_cparams(1),
        name="proj",
    )(x, g.reshape(1, d), w_all, mu_all, b_all)


def _bdot(a, b):
    return lax.dot_general(a, b, (((2,), (1,)), ((0,), (0,))), preferred_element_type=F32)


def _bdot_nt(a, b):
    return lax.dot_general(a, b, (((2,), (2,)), ((0,), (0,))), preferred_element_type=F32)


def _inv_unit_lower(nmat):
    L = nmat.shape[-1]
    r = lax.broadcasted_iota(jnp.int32, (L, L), 0)
    c = lax.broadcasted_iota(jnp.int32, (L, L), 1)
    eye = jnp.where(r == c, 1.0, 0.0)
    t = eye + nmat
    pw = nmat.astype(BF16)
    pw = _bdot(pw, pw).astype(BF16)
    steps = L.bit_length() - 2
    for i in range(steps):
        if i + 1 < steps:
            both = _bdot(jnp.concatenate([t.astype(BF16), pw], axis=1), pw)
            t = t + both[:, :L]
            pw = both[:, L:].astype(BF16)
        else:
            t = t + _bdot(t.astype(BF16), pw)
    return t


def _rwkv_kernel(*refs, seqs, **static):
    s_ref = refs[15]

    @pl.when(pl.program_id(1) == 0)
    def _():
        s_ref[...] = jnp.zeros(s_ref.shape, F32)

    for q in range(seqs):
        _rwkv_sequence(q, *refs, **static)


def _rwkv_sequence(q, r_ref, k_ref, v_ref, lora_ref, w0_ref, a0_ref, kk_ref, ka_ref, rk_ref, gw_ref, gb_ref,
                   du_ref, au_ref, gu_ref,
                   o_ref,
                   s_ref, pr_ref, pk_ref, pv_ref, pa_ref, pg_ref, plp_ref, plw_ref,
                   *, n_tiles, dlp, alp):
    L = CHUNK
    tiles = slice(q * n_tiles, (q + 1) * n_tiles)

    lora = lora_ref[q]
    wd = jnp.tanh(lora[:, :dlp])
    ad = lora[:, dlp:dlp + alp]
    gd = jax.nn.sigmoid(lora[:, dlp + alp:])
    w = -_softplus(-(w0_ref[...] + _dot(wd.astype(BF16), du_ref[...]))) - 0.5
    lw = -jnp.exp(w)
    asig = jax.nn.sigmoid(a0_ref[...] + _dot(ad.astype(BF16), au_ref[...]))
    gate = _dot(gd.astype(BF16), gu_ref[...])
    row = lax.broadcasted_iota(jnp.int32, (L, L), 0)
    col = lax.broadcasted_iota(jnp.int32, (L, L), 1)
    incl = row >= col
    strict = row > col
    tril = jnp.where(incl, 1.0, 0.0).astype(BF16)
    h1 = lw.astype(BF16)
    r1 = lw - h1.astype(F32)
    h2 = r1.astype(BF16)
    h3 = (r1 - h2.astype(F32)).astype(BF16)
    logp = _dot(tril, h1) + _dot(tril, h2) + _dot(tril, h3)
    for p in range(n_tiles):
        sl = slice(p * LANES, (p + 1) * LANES)
        i = q * n_tiles + p
        pr_ref[i] = r_ref[q, :, sl]
        pk_ref[i] = k_ref[q, :, sl]
        pv_ref[i] = v_ref[q, :, sl]
        pa_ref[i] = asig[:, sl]
        pg_ref[i] = gate[:, sl]
        plp_ref[i] = logp[:, sl]
        plw_ref[i] = lw[:, sl]

    lane = lax.broadcasted_iota(jnp.int32, (1, LANES), 1)
    lo = lane < HEAD_DIM
    same_head = jnp.where(row < HEAD_DIM, 0, 1) == jnp.where(col < HEAD_DIM, 0, 1)
    inv_hd = 1.0 / HEAD_DIM

    def first(x):
        return jnp.where(lo, x, 0.0)

    def second(x):
        return jnp.where(lo, 0.0, x)

    def segsum(x):
        return jnp.where(lo, jnp.sum(first(x), axis=-1, keepdims=True),
                         jnp.sum(second(x), axis=-1, keepdims=True))

    r = pr_ref[tiles]
    k = pk_ref[tiles]
    v = pv_ref[tiles]
    a_s = pa_ref[tiles]
    lp = plp_ref[tiles]
    kk = k * kk_ref[...]
    ss = segsum(kk * kk)
    kk = kk / jnp.maximum(jnp.sqrt(ss), 1e-12)
    a = -kk
    b = kk * a_s
    km = k * (1.0 + (a_s - 1.0) * ka_ref[...])
    cmid = lp[:, L // 2 - 1:L // 2, :]
    clast = lp[:, L - 1:L, :]
    lpe = lp - plw_ref[tiles]
    e_inv = jnp.exp(cmid - lp)
    at = a * jnp.exp(lpe - cmid)
    a_abs = a * jnp.exp(lpe)
    rt = r * jnp.exp(lp - cmid)
    r_abs = r * jnp.exp(lp)
    bt = b * e_inv
    kt = km * e_inv
    e_l = jnp.exp(clast - cmid)
    e_p = jnp.exp(clast)
    lhs = jnp.concatenate([first(at), second(at), first(rt), second(rt)], axis=1).astype(BF16)
    gmat = _bdot_nt(lhs, jnp.concatenate([bt, kt], axis=1).astype(BF16))
    n_lo = jnp.where(strict, gmat[:, 0:L, 0:L], 0.0)
    ak_lo = jnp.where(strict, gmat[:, 0:L, L:], 0.0)
    n_hi = jnp.where(strict, gmat[:, L:2 * L, 0:L], 0.0)
    ak_hi = jnp.where(strict, gmat[:, L:2 * L, L:], 0.0)
    rb_lo = jnp.where(incl, gmat[:, 2 * L:3 * L, 0:L], 0.0)
    rk_lo = jnp.where(incl, gmat[:, 2 * L:3 * L, L:], 0.0)
    rb_hi = jnp.where(incl, gmat[:, 3 * L:, 0:L], 0.0)
    rk_hi = jnp.where(incl, gmat[:, 3 * L:, L:], 0.0)
    t_all = _inv_unit_lower(jnp.concatenate([n_lo, n_hi], axis=0))
    t_lo, t_hi = t_all[:n_tiles], t_all[n_tiles:]
    s = s_ref[tiles]
    s_b = s.astype(BF16)
    v_lohi = jnp.concatenate([first(v), second(v)], axis=1).astype(BF16)
    rhs = _bdot_nt(a_abs.astype(BF16), s_b) + _bdot(jnp.concatenate([ak_lo, ak_hi], axis=2).astype(BF16), v_lohi)
    u = _bdot(jnp.concatenate([t_lo, t_hi], axis=2).astype(BF16),
              jnp.concatenate([first(rhs), second(rhs)], axis=1).astype(BF16))
    u_lohi = jnp.concatenate([first(u), second(u)], axis=1).astype(BF16)
    y = _bdot_nt(r_abs.astype(BF16), s_b) + _bdot(
        jnp.concatenate([rb_lo, rb_hi, rk_lo, rk_hi], axis=2).astype(BF16),
        jnp.concatenate([u_lohi, v_lohi], axis=1))
    upd = _bdot(jnp.concatenate([jnp.swapaxes(u, 1, 2), jnp.swapaxes(v, 1, 2)], axis=2).astype(BF16),
                jnp.concatenate([bt * e_l, kt * e_l], axis=1).astype(BF16))
    s_ref[tiles] = s * e_p + jnp.where(same_head, upd, 0.0)
    mean = segsum(y) * inv_hd
    dev = y - mean
    var = segsum(dev * dev) * inv_hd
    yn = dev * lax.rsqrt(var + GN_EPS) * gw_ref[...] + gb_ref[...]
    bonus = segsum(r * km * rk_ref[...]) * v
    out = ((yn + bonus) * pg_ref[tiles]).astype(BF16)
    for p in range(n_tiles):
        o_ref[q, :, p * LANES:(p + 1) * LANES] = out[p]


def _rwkv(z, batch, seq, c, lora_off, lora_w, dlp, alp, vecs, loras):
    n = z.shape[0]
    nc = seq // CHUNK
    n_tiles = c // LANES
    seqs = 1
    assert lora_off % lora_w == 0
    lora_blk = lora_off // lora_w
    z3 = z.reshape(batch, seq, z.shape[1])
    vec_spec = pl.BlockSpec((n_tiles, 1, LANES), lambda b, t: (0, 0, 0))
    in_specs = [
        pl.BlockSpec((seqs, CHUNK, c), lambda b, t: (b, t, 0)),
        pl.BlockSpec((seqs, CHUNK, c), lambda b, t: (b, t, 1)),
        pl.BlockSpec((seqs, CHUNK, c), lambda b, t: (b, t, 2)),
        pl.BlockSpec((seqs, CHUNK, lora_w), lambda b, t: (b, t, lora_blk)),
        pl.BlockSpec((1, c), lambda b, t: (0, 0)),
        pl.BlockSpec((1, c), lambda b, t: (0, 0)),
    ] + [vec_spec] * 5 + [pl.BlockSpec(w.shape, lambda b, t: (0, 0)) for w in loras]
    tile_f32 = pltpu.VMEM((seqs * n_tiles, CHUNK, LANES), F32)
    return pl.pallas_call(
        functools.partial(_rwkv_kernel, seqs=seqs, n_tiles=n_tiles, dlp=dlp, alp=alp),
        grid=(batch // seqs, nc),
        in_specs=in_specs,
        out_specs=pl.BlockSpec((seqs, CHUNK, c), lambda b, t: (b, t, 0)),
        out_shape=jax.ShapeDtypeStruct((batch, seq, c), BF16),
        scratch_shapes=[pltpu.VMEM((seqs * n_tiles, LANES, LANES), F32)] + [tile_f32] * 7,
        compiler_params=_cparams(2),
        name="rwkv",
    )(z3, z3, z3, z3, *vecs, *loras).reshape(n, c)


def _swa_kernel(sink_ref, q_ref, kvc_ref, kvp_ref, cosc_ref, sinc_ref, cosp_ref, sinp_ref, o_ref,
                *, n_tiles, group, qb):
    nblk = pl.program_id(1)
    lane = lax.broadcasted_iota(jnp.int32, (1, LANES), 1)
    lo = lane < HEAD_DIM
    rot_lo = jnp.bitwise_and(lane, HEAD_DIM - 1) < HEAD_DIM // 2

    def rope(x, cos, sin_signed):
        partner = jnp.where(rot_lo, pltpu.roll(x, LANES - HEAD_DIM // 2, 1), pltpu.roll(x, HEAD_DIM // 2, 1))
        return x * cos + partner * sin_signed

    cosc = cosc_ref[...]
    sinc = sinc_ref[...]
    kvc = kvc_ref[...]
    kvp = kvp_ref[...]
    keys = jnp.concatenate([rope(kvp[:, :LANES], cosp_ref[...], sinp_ref[...]),
                            rope(kvc[:, :LANES], cosc, sinc)], axis=0)
    vals = jnp.concatenate([kvp[:, LANES:], kvc[:, LANES:]], axis=0)
    keys_sw = pltpu.roll(keys, HEAD_DIM, 1)
    vals_sw = pltpu.roll(vals, HEAD_DIM, 1)
    k_first = [jnp.where(lo, keys, 0.0).astype(BF16), jnp.where(lo, keys_sw, 0.0).astype(BF16)]
    k_second = [jnp.where(lo, 0.0, keys_sw).astype(BF16), jnp.where(lo, 0.0, keys).astype(BF16)]
    v_first = [jnp.where(lo, vals, 0.0).astype(BF16), jnp.where(lo, vals_sw, 0.0).astype(BF16)]
    v_second = [jnp.where(lo, 0.0, vals_sw).astype(BF16), jnp.where(lo, 0.0, vals).astype(BF16)]
    qi = lax.broadcasted_iota(jnp.int32, (BLOCK, 2 * BLOCK), 0)
    ki = lax.broadcasted_iota(jnp.int32, (BLOCK, 2 * BLOCK), 1)
    window = (ki > qi) & (ki <= qi + BLOCK)
    valid = [window & ((nblk > 0) | (ki >= BLOCK))] + [window] * (qb - 1)
    scale = HEAD_DIM ** -0.5
    ones_kv = jnp.ones((2 * BLOCK, LANES), BF16)
    heads = [(j, p, half) for j in range(qb) for p in range(n_tiles) for half in range(HEADS_PER_TILE)]
    scores = []
    for j in range(qb):
        qrows = slice(j * BLOCK, (j + 1) * BLOCK)
        krows = slice(j * BLOCK, (j + 2) * BLOCK)
        for p in range(n_tiles):
            g = (p * HEADS_PER_TILE) // group
            qp = (rope(q_ref[qrows, p * LANES:(p + 1) * LANES], cosc[qrows], sinc[qrows]) * scale).astype(BF16)
            scores += [_dot_nt(qp, k_first[g][krows]), _dot_nt(qp, k_second[g][krows])]
    probs, sink_terms = [], []
    for (j, p, half), s in zip(heads, scores):
        s = jnp.where(valid[j], s, NEG_INF)
        sink = sink_ref[p * HEADS_PER_TILE + half]
        m = jnp.maximum(jnp.max(s, axis=-1, keepdims=True), sink)
        probs.append(jnp.exp(s - m).astype(BF16))
        sink_terms.append(jnp.exp(sink - m))
    outs = []
    for (j, p, half), e, st in zip(heads, probs, sink_terms):
        g = (p * HEADS_PER_TILE) // group
        krows = slice(j * BLOCK, (j + 2) * BLOCK)
        den = _dot(e, ones_kv) + st
        outs.append(_dot(e, (v_second[g] if half else v_first[g])[krows]) * (1.0 / den))
    for j in range(qb):
        for p in range(n_tiles):
            i = (j * n_tiles + p) * HEADS_PER_TILE
            o_ref[j * BLOCK:(j + 1) * BLOCK, p * LANES:(p + 1) * LANES] = (outs[i] + outs[i + 1]).astype(BF16)


def _swa(z, sinks, cos_t, sin_t, batch, seq, q_off, qw, kv_off, kvw, group):
    n = z.shape[0]
    nb = seq // BLOCK
    assert q_off % qw == 0 and kv_off % kvw == 0 and kvw == 2 * LANES
    q_blk, kv_blk = q_off // qw, kv_off // kvw
    qb = 2 if nb % 2 == 0 else 1
    steps = nb // qb
    prev = lambda t: jnp.maximum(t * qb - 1, 0)
    return pl.pallas_call(
        functools.partial(_swa_kernel, n_tiles=qw // LANES, group=group, qb=qb),
        grid=(batch, steps),
        in_specs=[
            pl.BlockSpec(memory_space=pltpu.SMEM),
            pl.BlockSpec((qb * BLOCK, qw), lambda b, t: (b * steps + t, q_blk)),
            pl.BlockSpec((qb * BLOCK, kvw), lambda b, t: (b * steps + t, kv_blk)),
            pl.BlockSpec((BLOCK, kvw), lambda b, t: (b * nb + prev(t), kv_blk)),
            pl.BlockSpec((qb * BLOCK, LANES), lambda b, t: (t, 0)),
            pl.BlockSpec((qb * BLOCK, LANES), lambda b, t: (t, 0)),
            pl.BlockSpec((BLOCK, LANES), lambda b, t: (prev(t), 0)),
            pl.BlockSpec((BLOCK, LANES), lambda b, t: (prev(t), 0)),
        ],
        out_specs=pl.BlockSpec((qb * BLOCK, qw), lambda b, t: (b * steps + t, 0)),
        out_shape=jax.ShapeDtypeStruct((n, qw), BF16),
        compiler_params=_cparams(2),
        name="swa",
    )(sinks, z, z, z, cos_t, sin_t, cos_t, sin_t)


def _memkv_kernel(m_ref, g_ref, w_ref, o_ref):
    h = _rms(m_ref[...], g_ref[...]).astype(BF16)
    o_ref[...] = _dot(h, w_ref[...].astype(BF16)).astype(BF16)


def _memkv(mem, g, w):
    n, d = mem.shape
    ncols = w.shape[1]
    tm = _pick(n, (512, 256, 128))
    tn = _pick(ncols, (1024, 512, 256, 128))
    return pl.pallas_call(
        _memkv_kernel,
        grid=(ncols // tn, n // tm),
        in_specs=[
            pl.BlockSpec((tm, d), lambda j, i: (i, 0)),
            pl.BlockSpec((1, d), lambda j, i: (0, 0)),
            pl.BlockSpec((d, tn), lambda j, i: (0, j)),
        ],
        out_specs=pl.BlockSpec((tm, tn), lambda j, i: (i, j)),
        out_shape=jax.ShapeDtypeStruct((n, ncols), BF16),
        compiler_params=_cparams(2),
        name="memkv",
    )(mem, g.reshape(1, d), w)


def _mix_kernel(x_ref, yr_ref, ys_ref, wo_ref, bo_ref, g_ref, wq_ref, k_ref, v_ref, wxo_ref, o_ref):
    c = yr_ref.shape[1]
    d = x_ref.shape[1]
    hd = d // XATTN_HEADS
    x2 = x_ref[...] + _dot(yr_ref[...], wo_ref[:c, :]) + _dot(ys_ref[...], wo_ref[c:, :]) + bo_ref[...]
    q = _dot(_rms(x2, g_ref[...]).astype(BF16), wq_ref[...]).astype(BF16)
    scale = hd ** -0.5
    outs = []
    for h in range(XATTN_HEADS):
        sl = slice(h * hd, (h + 1) * hd)
        s = _dot_nt(q[:, sl], k_ref[:, sl]) * scale
        m = jnp.max(s, axis=-1, keepdims=True)
        e = jnp.exp(s - m)
        den = jnp.sum(e, axis=-1, keepdims=True)
        outs.append((_dot(e.astype(BF16), v_ref[:, sl]) * (1.0 / den)).astype(BF16))
    o_ref[...] = x2 + _dot(jnp.concatenate(outs, axis=1), wxo_ref[...])


def _mix(x, yr, ys, wo, bo, g, wq, kv, wxo, seq, mlen):
    n, d = x.shape
    c = yr.shape[1]
    tm = _pick(seq, (256, 128))
    per_seq = seq // tm
    once = pl.Buffered(1)
    return pl.pallas_call(
        _mix_kernel,
        grid=(n // tm,),
        in_specs=[
            pl.BlockSpec((tm, d), lambda i: (i, 0)),
            pl.BlockSpec((tm, c), lambda i: (i, 0)),
            pl.BlockSpec((tm, d - c), lambda i: (i, 0)),
            pl.BlockSpec((d, d), lambda i: (0, 0), pipeline_mode=once),
            pl.BlockSpec((1, d), lambda i: (0, 0)),
            pl.BlockSpec((1, d), lambda i: (0, 0)),
            pl.BlockSpec((d, d), lambda i: (0, 0), pipeline_mode=once),
            pl.BlockSpec((mlen, d), lambda i: (i // per_seq, 0)),
            pl.BlockSpec((mlen, d), lambda i: (i // per_seq, 1)),
            pl.BlockSpec((d, d), lambda i: (0, 0), pipeline_mode=once),
        ],
        out_specs=pl.BlockSpec((tm, d), lambda i: (i, 0)),
        out_shape=jax.ShapeDtypeStruct((n, d), F32),
        compiler_params=_cparams(1),
        name="mix",
    )(x, yr, ys, wo, bo.reshape(1, d), g.reshape(1, d), wq, kv, kv, wxo)


def _pad_cols(w, width):
    return jnp.pad(w, ((0, 0), (0, width - w.shape[1])))


def _pad_rows(w, height):
    return jnp.pad(w, ((0, height - w.shape[0]), (0, 0)))


def _rope_tables(seq):
    half = HEAD_DIM // 2
    lane = jnp.arange(LANES)
    inv_freq = ROPE_THETA ** (-jnp.arange(0, HEAD_DIM, 2, dtype=F32) / HEAD_DIM)
    ang = jnp.arange(seq, dtype=F32)[:, None] * inv_freq[lane % half][None, :]
    sign = jnp.where((lane % HEAD_DIM) < half, -1.0, 1.0)
    return jnp.cos(ang), jnp.sin(ang) * sign[None, :]


def kernel(x, mem, f1_norm, f1_gate, f1_up, f1_down, mix_norm, w_in, b_in_attn, rw_mu, rw_w0, rw_decay_up, rw_a0, rw_aaa_up, rw_gate_up, rw_k_k, rw_k_a, rw_r_k, rw_lnx_w, rw_lnx_b, attn_sinks, w_out, b_out, xa_norm, mem_norm, w_xq, w_xkv, w_xo, f2_norm, f2_gate, f2_up, f2_down, final_norm):
    batch, seq, d = x.shape
    mlen = mem.shape[1]
    depth = f1_norm.shape[0]
    c = rw_w0.shape[1]
    sw = d - c
    dl, al, gl = rw_decay_up.shape[1], rw_aaa_up.shape[1], rw_gate_up.shape[1]
    dlp, alp, glp = (_round_up(v, LANES) for v in (dl, al, gl))
    kvw = b_in_attn.shape[1] - sw
    q_heads = sw // HEAD_DIM
    group = q_heads // (kvw // (2 * HEAD_DIM))
    n_tiles = c // LANES
    n = batch * seq
    q_off = 3 * c
    lora_off = q_off + sw
    lora_w = dlp + alp + glp
    kv_off = lora_off + lora_w
    ncols = _round_up(kv_off + kvw, 2 * LANES)
    cos_t, sin_t = _rope_tables(seq)

    xf = x.reshape(n, d)
    memf = mem.reshape(batch * mlen, d)
    for l in range(depth):
        xf = _ffn(xf, f1_norm[l], f1_gate[l].astype(BF16), f1_up[l].astype(BF16), f1_down[l].astype(BF16), None)

        wl = w_in[l]
        o1, o2, o3 = 3 * c, 3 * c + dl, 3 * c + dl + al
        shift = o3 + gl
        w_all = jnp.concatenate([
            wl[:, :o1], wl[:, shift:shift + sw],
            _pad_cols(wl[:, o1:o2], dlp), _pad_cols(wl[:, o2:o3], alp), _pad_cols(wl[:, o3:shift], glp),
            wl[:, shift + sw:]], axis=1)
        w_all = _pad_cols(w_all, ncols).astype(BF16)
        mu = rw_mu[l][None, :]
        mu_all = _pad_cols(jnp.concatenate([
            mu[:, :o1], jnp.zeros((1, sw), F32),
            _pad_cols(mu[:, o1:o2], dlp), _pad_cols(mu[:, o2:o3], alp), _pad_cols(mu[:, o3:shift], glp)],
            axis=1), ncols)
        bia = b_in_attn[l][None, :]
        b_all = _pad_cols(jnp.concatenate([
            jnp.zeros((1, q_off), F32), bia[:, :sw], jnp.zeros((1, lora_w), F32), bia[:, sw:]], axis=1), ncols)
        z = _proj(xf, mix_norm[l], w_all, mu_all, b_all, seq)

        tiles = lambda v: v.reshape(n_tiles, 1, LANES)
        vecs = [rw_w0[l].reshape(1, c), rw_a0[l].reshape(1, c), tiles(rw_k_k[l]), tiles(rw_k_a[l]),
                tiles(rw_r_k[l]), tiles(rw_lnx_w[l]), tiles(rw_lnx_b[l])]
        loras = [_pad_rows(rw_decay_up[l], dlp).astype(BF16), _pad_rows(rw_aaa_up[l], alp).astype(BF16),
                 _pad_rows(rw_gate_up[l], glp).astype(BF16)]
        y_rwkv = _rwkv(z, batch, seq, c, lora_off, lora_w, dlp, alp, vecs, loras)
        y_swa = _swa(z, attn_sinks[l], cos_t, sin_t, batch, seq, q_off, sw, kv_off, kvw, group)

        kv_mem = _memkv(memf, mem_norm[l], w_xkv[l])
        xf = _mix(xf, y_rwkv, y_swa, w_out[l].astype(BF16), b_out[l], xa_norm[l], w_xq[l].astype(BF16),
                  kv_mem, w_xo[l].astype(BF16), seq, mlen)

        last = l == depth - 1
        xf = _ffn(xf, f2_norm[l], f2_gate[l].astype(BF16), f2_up[l].astype(BF16), f2_down[l].astype(BF16),
                  final_norm if last else None)
    return xf.reshape(batch, seq, d)
```

```python
import functools

import jax
import jax.numpy as jnp
from jax import lax
from jax.experimental import pallas as pl
from jax.experimental.pallas import tpu as pltpu

F32, BF16 = jnp.float32, jnp.bfloat16

LANES = 128
SUBLANES = 8
VMEM_LIMIT_BYTES = 56 * 1024 * 1024

HEAD_DIM = 64
HEADS_PER_TILE = LANES // HEAD_DIM
CHUNK = 128
BLOCK = 128
XATTN_HEADS = 4
RMS_EPS = 1e-6
GN_EPS = 64e-5
NEG_INF = -1e30
ROPE_THETA = 10000.0


def _round_up(n, m):
    return (n + m - 1) // m * m


def _pick(n, prefs):
    for p in prefs:
        if n % p == 0:
            return p
    raise ValueError(f"no tile in {prefs} divides {n}")


def _cparams(n_axes):
    return pltpu.CompilerParams(dimension_semantics=("arbitrary",) * n_axes,
                                vmem_limit_bytes=VMEM_LIMIT_BYTES)


def _dot(a, b):
    return jnp.dot(a, b, preferred_element_type=F32)


def _dot_nt(a, b):
    return lax.dot_general(a, b, (((1,), (1,)), ((), ())), preferred_element_type=F32)


def _cast_specs(weights, nsteps, flat_step):
    in_specs, out_specs, out_shapes = [], [], []
    for w in weights:
        units = w.shape[0] // (2 * SUBLANES)
        nblk = max(d for d in range(1, nsteps + 1) if units % d == 0)
        imap = lambda *g, nblk=nblk: (jnp.minimum(flat_step(*g), nblk - 1), 0)
        spec = pl.BlockSpec((w.shape[0] // nblk, w.shape[1]), imap)
        in_specs.append(spec)
        out_specs.append(spec)
        out_shapes.append(jax.ShapeDtypeStruct(w.shape, BF16))
    return in_specs, out_specs, out_shapes


def _cast_blocks(src_refs, dst_refs):
    for src, dst in zip(src_refs, dst_refs):
        dst[...] = src[...].astype(BF16)


def _rms(x, g):
    ms = jnp.mean(x * x, axis=-1, keepdims=True)
    return x * lax.rsqrt(ms + RMS_EPS) * g


def _softplus(x):
    return jnp.maximum(x, 0.0) + jnp.log(1.0 + jnp.exp(-jnp.abs(x)))


def _ffn_kernel(x_ref, g_ref, wg_ref, wu_ref, wd_ref, fg_ref, o_ref, h_ref, *, final_norm, rows):
    j = pl.program_id(1)
    tm = x_ref.shape[0]

    @pl.when(j == 0)
    def _():
        for r0 in range(0, tm, rows):
            x = x_ref[r0:r0 + rows, :]
            h_ref[r0:r0 + rows, :] = _rms(x, g_ref[...]).astype(BF16)
            o_ref[r0:r0 + rows, :] = x

    for r0 in range(0, tm, rows):
        h = h_ref[r0:r0 + rows, :]
        gate = _dot(h, wg_ref[...])
        up = _dot(h, wu_ref[...])
        act = (gate * jax.nn.sigmoid(gate) * up).astype(BF16)
        o_ref[r0:r0 + rows, :] += 0.5 * _dot(act, wd_ref[...])

    if final_norm:
        @pl.when(j == pl.num_programs(1) - 1)
        def _():
            for r0 in range(0, tm, rows):
                o_ref[r0:r0 + rows, :] = _rms(o_ref[r0:r0 + rows, :], fg_ref[...])


def _ffn(x, g, wg, wu, wd, fg):
    n, d = x.shape
    f = wg.shape[1]
    tm = _pick(n, (1024, 512, 256, 128))
    tf = _pick(f, (512, 256, 128))
    rows = min(tm, 512)
    final_norm = fg is not None
    fg = g if fg is None else fg
    return pl.pallas_call(
        functools.partial(_ffn_kernel, final_norm=final_norm, rows=rows),
        grid=(n // tm, f // tf),
        in_specs=[
            pl.BlockSpec((tm, d), lambda i, j: (i, 0)),
            pl.BlockSpec((1, d), lambda i, j: (0, 0)),
            pl.BlockSpec((d, tf), lambda i, j: (0, j)),
            pl.BlockSpec((d, tf), lambda i, j: (0, j)),
            pl.BlockSpec((tf, d), lambda i, j: (j, 0)),
            pl.BlockSpec((1, d), lambda i, j: (0, 0)),
        ],
        out_specs=pl.BlockSpec((tm, d), lambda i, j: (i, 0)),
        out_shape=jax.ShapeDtypeStruct((n, d), F32),
        scratch_shapes=[pltpu.VMEM((tm, d), BF16)],
        compiler_params=_cparams(2),
        name="ffn",
    )(x, g.reshape(1, d), wg, wu, wd, fg.reshape(1, d))


def _proj_kernel(x_ref, g_ref, w_ref, mu_ref, b_ref, o_ref, carry_ref, *, tiles_per_seq, tn):
    i = pl.program_id(0)
    tm = x_ref.shape[0]
    ncols = w_ref.shape[1]

    @pl.when(i % tiles_per_seq == 0)
    def _():
        carry_ref[...] = jnp.zeros(carry_ref.shape, F32)

    h = _rms(x_ref[...], g_ref[...]).astype(BF16)
    row = lax.broadcasted_iota(jnp.int32, (SUBLANES, tn), 0)
    for c0 in range(0, ncols, tn):
        cs = slice(c0, c0 + tn)
        z = _dot(h, w_ref[:, cs])
        prev_tail = carry_ref[:, cs]
        carry_ref[:, cs] = z[tm - SUBLANES:, :]
        zs = pltpu.roll(z, 1, 0)
        head = jnp.where(row == 0, pltpu.roll(prev_tail, 1, 0), zs[:SUBLANES])
        mu = mu_ref[:, cs]
        b = b_ref[:, cs]
        z0 = z[:SUBLANES]
        o_ref[:SUBLANES, cs] = z0 + (head - z0) * mu + b
        z1 = z[SUBLANES:]
        o_ref[SUBLANES:, cs] = z1 + (zs[SUBLANES:] - z1) * mu + b


def _proj(x, g, w_all, mu_all, b_all, seq):
    n, d = x.shape
    ncols = w_all.shape[1]
    tm = _pick(seq, (256, 128))
    tn = _pick(ncols, (1024, 512, 256, 128))
    return pl.pallas_call(
        functools.partial(_proj_kernel, tiles_per_seq=seq // tm, tn=tn),
        grid=(n // tm,),
        in_specs=[
            pl.BlockSpec((tm, d), lambda i: (i, 0)),
            pl.BlockSpec((1, d), lambda i: (0, 0)),
            pl.BlockSpec((d, ncols), lambda i: (0, 0), pipeline_mode=pl.Buffered(1)),
            pl.BlockSpec((1, ncols), lambda i: (0, 0)),
            pl.BlockSpec((1, ncols), lambda i: (0, 0)),
        ],
        out_specs=pl.BlockSpec((tm, ncols), lambda i: (i, 0)),
        out_shape=jax.ShapeDtypeStruct((n, ncols), F32),
        scratch_shapes=[pltpu.VMEM((SUBLANES, ncols), F32)],
        compiler_params=_cparams(1),
        name="proj",
    )(x, g.reshape(1, d), w_all, mu_all, b_all)


def _bdot(a, b):
    return lax.dot_general(a, b, (((2,), (1,)), ((0,), (0,))), preferred_element_type=F32)


def _bdot_nt(a, b):
    return lax.dot_general(a, b, (((2,), (2,)), ((0,), (0,))), preferred_element_type=F32)


def _inv_unit_lower(nmat):
    L = nmat.shape[-1]
    r = lax.broadcasted_iota(jnp.int32, (L, L), 0)
    c = lax.broadcasted_iota(jnp.int32, (L, L), 1)
    eye = jnp.where(r == c, 1.0, 0.0)
    t = eye + nmat
    pw = nmat.astype(BF16)
    pw = _bdot(pw, pw).astype(BF16)
    steps = L.bit_length() - 2
    for i in range(steps):
        if i + 1 < steps:
            both = _bdot(jnp.concatenate([t.astype(BF16), pw], axis=1), pw)
            t = t + both[:, :L]
            pw = both[:, L:].astype(BF16)
        else:
            t = t + _bdot(t.astype(BF16), pw)
    return t


RWKV_INPUTS = 14


def _rwkv_kernel(*refs, seqs, n_cast, **static):
    cast_in = refs[RWKV_INPUTS:RWKV_INPUTS + n_cast]
    cast_out = refs[RWKV_INPUTS + n_cast + 1:RWKV_INPUTS + 2 * n_cast + 1]
    seq_refs = refs[:RWKV_INPUTS] + (refs[RWKV_INPUTS + n_cast],) + refs[RWKV_INPUTS + 2 * n_cast + 1:]
    s_ref = seq_refs[RWKV_INPUTS + 1]

    @pl.when(pl.program_id(1) == 0)
    def _():
        s_ref[...] = jnp.zeros(s_ref.shape, F32)

    for q in range(seqs):
        _rwkv_sequence(q, *seq_refs, **static)
    _cast_blocks(cast_in, cast_out)


def _rwkv_sequence(q, r_ref, k_ref, v_ref, lora_ref, w0_ref, a0_ref, kk_ref, ka_ref, rk_ref, gw_ref, gb_ref,
                   du_ref, au_ref, gu_ref,
                   o_ref,
                   s_ref, pr_ref, pk_ref, pv_ref, pa_ref, pg_ref, plp_ref, plw_ref,
                   *, n_tiles, dlp, alp):
    L = CHUNK
    tiles = slice(q * n_tiles, (q + 1) * n_tiles)

    lora = lora_ref[q]
    wd = jnp.tanh(lora[:, :dlp])
    ad = lora[:, dlp:dlp + alp]
    gd = jax.nn.sigmoid(lora[:, dlp + alp:])
    w = -_softplus(-(w0_ref[...] + _dot(wd.astype(BF16), du_ref[...]))) - 0.5
    lw = -jnp.exp(w)
    asig = jax.nn.sigmoid(a0_ref[...] + _dot(ad.astype(BF16), au_ref[...]))
    gate = _dot(gd.astype(BF16), gu_ref[...])
    row = lax.broadcasted_iota(jnp.int32, (L, L), 0)
    col = lax.broadcasted_iota(jnp.int32, (L, L), 1)
    incl = row >= col
    strict = row > col
    tril = jnp.where(incl, 1.0, 0.0).astype(BF16)
    h1 = lw.astype(BF16)
    r1 = lw - h1.astype(F32)
    h2 = r1.astype(BF16)
    h3 = (r1 - h2.astype(F32)).astype(BF16)
    logp = _dot(tril, h1) + _dot(tril, h2) + _dot(tril, h3)
    for p in range(n_tiles):
        sl = slice(p * LANES, (p + 1) * LANES)
        i = q * n_tiles + p
        pr_ref[i] = r_ref[q, :, sl]
        pk_ref[i] = k_ref[q, :, sl]
        pv_ref[i] = v_ref[q, :, sl]
        pa_ref[i] = asig[:, sl]
        pg_ref[i] = gate[:, sl]
        plp_ref[i] = logp[:, sl]
        plw_ref[i] = lw[:, sl]

    lane = lax.broadcasted_iota(jnp.int32, (1, LANES), 1)
    lo = lane < HEAD_DIM
    same_head = jnp.where(row < HEAD_DIM, 0, 1) == jnp.where(col < HEAD_DIM, 0, 1)
    inv_hd = 1.0 / HEAD_DIM

    def first(x):
        return jnp.where(lo, x, 0.0)

    def second(x):
        return jnp.where(lo, 0.0, x)

    def segsum(x):
        return jnp.where(lo, jnp.sum(first(x), axis=-1, keepdims=True),
                         jnp.sum(second(x), axis=-1, keepdims=True))

    r = pr_ref[tiles]
    k = pk_ref[tiles]
    v = pv_ref[tiles]
    a_s = pa_ref[tiles]
    lp = plp_ref[tiles]
    kk = k * kk_ref[...]
    ss = segsum(kk * kk)
    kk = kk / jnp.maximum(jnp.sqrt(ss), 1e-12)
    a = -kk
    b = kk * a_s
    km = k * (1.0 + (a_s - 1.0) * ka_ref[...])
    cmid = lp[:, L // 2 - 1:L // 2, :]
    clast = lp[:, L - 1:L, :]
    lpe = lp - plw_ref[tiles]
    e_inv = jnp.exp(cmid - lp)
    at = a * jnp.exp(lpe - cmid)
    a_abs = a * jnp.exp(lpe)
    rt = r * jnp.exp(lp - cmid)
    r_abs = r * jnp.exp(lp)
    bt = b * e_inv
    kt = km * e_inv
    e_l = jnp.exp(clast - cmid)
    e_p = jnp.exp(clast)
    lhs = jnp.concatenate([first(at), second(at), first(rt), second(rt)], axis=1).astype(BF16)
    gmat = _bdot_nt(lhs, jnp.concatenate([bt, kt], axis=1).astype(BF16))
    n_lo = jnp.where(strict, gmat[:, 0:L, 0:L], 0.0)
    ak_lo = jnp.where(strict, gmat[:, 0:L, L:], 0.0)
    n_hi = jnp.where(strict, gmat[:, L:2 * L, 0:L], 0.0)
    ak_hi = jnp.where(strict, gmat[:, L:2 * L, L:], 0.0)
    rb_lo = jnp.where(incl, gmat[:, 2 * L:3 * L, 0:L], 0.0)
    rk_lo = jnp.where(incl, gmat[:, 2 * L:3 * L, L:], 0.0)
    rb_hi = jnp.where(incl, gmat[:, 3 * L:, 0:L], 0.0)
    rk_hi = jnp.where(incl, gmat[:, 3 * L:, L:], 0.0)
    t_all = _inv_unit_lower(jnp.concatenate([n_lo, n_hi], axis=0))
    t_lo, t_hi = t_all[:n_tiles], t_all[n_tiles:]
    s = s_ref[tiles]
    s_b = s.astype(BF16)
    v_lohi = jnp.concatenate([first(v), second(v)], axis=1).astype(BF16)
    rhs = _bdot_nt(a_abs.astype(BF16), s_b) + _bdot(jnp.concatenate([ak_lo, ak_hi], axis=2).astype(BF16), v_lohi)
    u = _bdot(jnp.concatenate([t_lo, t_hi], axis=2).astype(BF16),
              jnp.concatenate([first(rhs), second(rhs)], axis=1).astype(BF16))
    u_lohi = jnp.concatenate([first(u), second(u)], axis=1).astype(BF16)
    y = _bdot_nt(r_abs.astype(BF16), s_b) + _bdot(
        jnp.concatenate([rb_lo, rb_hi, rk_lo, rk_hi], axis=2).astype(BF16),
        jnp.concatenate([u_lohi, v_lohi], axis=1))
    upd = _bdot(jnp.concatenate([jnp.swapaxes(u, 1, 2), jnp.swapaxes(v, 1, 2)], axis=2).astype(BF16),
                jnp.concatenate([bt * e_l, kt * e_l], axis=1).astype(BF16))
    s_ref[tiles] = s * e_p + jnp.where(same_head, upd, 0.0)
    mean = segsum(y) * inv_hd
    dev = y - mean
    var = segsum(dev * dev) * inv_hd
    yn = dev * lax.rsqrt(var + GN_EPS) * gw_ref[...] + gb_ref[...]
    bonus = segsum(r * km * rk_ref[...]) * v
    out = ((yn + bonus) * pg_ref[tiles]).astype(BF16)
    for p in range(n_tiles):
        o_ref[q, :, p * LANES:(p + 1) * LANES] = out[p]


def _rwkv(z, batch, seq, c, lora_off, lora_w, dlp, alp, vecs, loras, casts):
    n = z.shape[0]
    nc = seq // CHUNK
    n_tiles = c // LANES
    seqs = 1
    assert lora_off % lora_w == 0 and len(vecs) + len(loras) + 4 == RWKV_INPUTS
    lora_blk = lora_off // lora_w
    z3 = z.reshape(batch, seq, z.shape[1])
    cast_in, cast_out, cast_shapes = _cast_specs(casts, (batch // seqs) * nc, lambda b, t: b * nc + t)
    vec_spec = pl.BlockSpec((n_tiles, 1, LANES), lambda b, t: (0, 0, 0))
    in_specs = [
        pl.BlockSpec((seqs, CHUNK, c), lambda b, t: (b, t, 0)),
        pl.BlockSpec((seqs, CHUNK, c), lambda b, t: (b, t, 1)),
        pl.BlockSpec((seqs, CHUNK, c), lambda b, t: (b, t, 2)),
        pl.BlockSpec((seqs, CHUNK, lora_w), lambda b, t: (b, t, lora_blk)),
        pl.BlockSpec((1, c), lambda b, t: (0, 0)),
        pl.BlockSpec((1, c), lambda b, t: (0, 0)),
    ] + [vec_spec] * 5 + [pl.BlockSpec(w.shape, lambda b, t: (0, 0)) for w in loras]
    tile_f32 = pltpu.VMEM((seqs * n_tiles, CHUNK, LANES), F32)
    y, *cast = pl.pallas_call(
        functools.partial(_rwkv_kernel, seqs=seqs, n_cast=len(casts), n_tiles=n_tiles, dlp=dlp, alp=alp),
        grid=(batch // seqs, nc),
        in_specs=in_specs + cast_in,
        out_specs=[pl.BlockSpec((seqs, CHUNK, c), lambda b, t: (b, t, 0))] + cast_out,
        out_shape=[jax.ShapeDtypeStruct((batch, seq, c), BF16)] + cast_shapes,
        scratch_shapes=[pltpu.VMEM((seqs * n_tiles, LANES, LANES), F32)] + [tile_f32] * 7,
        compiler_params=_cparams(2),
        name="rwkv",
    )(z3, z3, z3, z3, *vecs, *loras, *casts)
    return y.reshape(n, c), cast


def _swa_kernel(sink_ref, q_ref, kvc_ref, kvp_ref, cosc_ref, sinc_ref, cosp_ref, sinp_ref, *rest,
                n_tiles, group, qb):
    n_cast = len(rest) // 2
    o_ref = rest[n_cast]
    _cast_blocks(rest[:n_cast], rest[n_cast + 1:])
    nblk = pl.program_id(1)
    lane = lax.broadcasted_iota(jnp.int32, (1, LANES), 1)
    lo = lane < HEAD_DIM
    rot_lo = jnp.bitwise_and(lane, HEAD_DIM - 1) < HEAD_DIM // 2

    def rope(x, cos, sin_signed):
        partner = jnp.where(rot_lo, pltpu.roll(x, LANES - HEAD_DIM // 2, 1), pltpu.roll(x, HEAD_DIM // 2, 1))
        return x * cos + partner * sin_signed

    cosc = cosc_ref[...]
    sinc = sinc_ref[...]
    kvc = kvc_ref[...]
    kvp = kvp_ref[...]
    keys = jnp.concatenate([rope(kvp[:, :LANES], cosp_ref[...], sinp_ref[...]),
                            rope(kvc[:, :LANES], cosc, sinc)], axis=0)
    vals = jnp.concatenate([kvp[:, LANES:], kvc[:, LANES:]], axis=0)
    keys_sw = pltpu.roll(keys, HEAD_DIM, 1)
    vals_sw = pltpu.roll(vals, HEAD_DIM, 1)
    k_first = [jnp.where(lo, keys, 0.0).astype(BF16), jnp.where(lo, keys_sw, 0.0).astype(BF16)]
    k_second = [jnp.where(lo, 0.0, keys_sw).astype(BF16), jnp.where(lo, 0.0, keys).astype(BF16)]
    v_first = [jnp.where(lo, vals, 0.0).astype(BF16), jnp.where(lo, vals_sw, 0.0).astype(BF16)]
    v_second = [jnp.where(lo, 0.0, vals_sw).astype(BF16), jnp.where(lo, 0.0, vals).astype(BF16)]
    qi = lax.broadcasted_iota(jnp.int32, (BLOCK, 2 * BLOCK), 0)
    ki = lax.broadcasted_iota(jnp.int32, (BLOCK, 2 * BLOCK), 1)
    window = (ki > qi) & (ki <= qi + BLOCK)
    valid = [window & ((nblk > 0) | (ki >= BLOCK))] + [window] * (qb - 1)
    scale = HEAD_DIM ** -0.5
    ones_kv = jnp.ones((2 * BLOCK, LANES), BF16)
    heads = [(j, p, half) for j in range(qb) for p in range(n_tiles) for half in range(HEADS_PER_TILE)]
    scores = []
    for j in range(qb):
        qrows = slice(j * BLOCK, (j + 1) * BLOCK)
        krows = slice(j * BLOCK, (j + 2) * BLOCK)
        for p in range(n_tiles):
            g = (p * HEADS_PER_TILE) // group
            qp = (rope(q_ref[qrows, p * LANES:(p + 1) * LANES], cosc[qrows], sinc[qrows]) * scale).astype(BF16)
            scores += [_dot_nt(qp, k_first[g][krows]), _dot_nt(qp, k_second[g][krows])]
    probs, sink_terms = [], []
    for (j, p, half), s in zip(heads, scores):
        s = jnp.where(valid[j], s, NEG_INF)
        sink = sink_ref[p * HEADS_PER_TILE + half]
        m = jnp.maximum(jnp.max(s, axis=-1, keepdims=True), sink)
        probs.append(jnp.exp(s - m).astype(BF16))
        sink_terms.append(jnp.exp(sink - m))
    outs = []
    for (j, p, half), e, st in zip(heads, probs, sink_terms):
        g = (p * HEADS_PER_TILE) // group
        krows = slice(j * BLOCK, (j + 2) * BLOCK)
        den = _dot(e, ones_kv) + st
        outs.append(_dot(e, (v_second[g] if half else v_first[g])[krows]) * (1.0 / den))
    for j in range(qb):
        for p in range(n_tiles):
            i = (j * n_tiles + p) * HEADS_PER_TILE
            o_ref[j * BLOCK:(j + 1) * BLOCK, p * LANES:(p + 1) * LANES] = (outs[i] + outs[i + 1]).astype(BF16)


def _swa(z, sinks, cos_t, sin_t, batch, seq, q_off, qw, kv_off, kvw, group, casts):
    n = z.shape[0]
    nb = seq // BLOCK
    assert q_off % qw == 0 and kv_off % kvw == 0 and kvw == 2 * LANES
    q_blk, kv_blk = q_off // qw, kv_off // kvw
    qb = 2 if nb % 2 == 0 else 1
    steps = nb // qb
    prev = lambda t: jnp.maximum(t * qb - 1, 0)
    cast_in, cast_out, cast_shapes = _cast_specs(casts, batch * steps, lambda b, t: b * steps + t)
    y, *cast = pl.pallas_call(
        functools.partial(_swa_kernel, n_tiles=qw // LANES, group=group, qb=qb),
        grid=(batch, steps),
        in_specs=[
            pl.BlockSpec(memory_space=pltpu.SMEM),
            pl.BlockSpec((qb * BLOCK, qw), lambda b, t: (b * steps + t, q_blk)),
            pl.BlockSpec((qb * BLOCK, kvw), lambda b, t: (b * steps + t, kv_blk)),
            pl.BlockSpec((BLOCK, kvw), lambda b, t: (b * nb + prev(t), kv_blk)),
            pl.BlockSpec((qb * BLOCK, LANES), lambda b, t: (t, 0)),
            pl.BlockSpec((qb * BLOCK, LANES), lambda b, t: (t, 0)),
            pl.BlockSpec((BLOCK, LANES), lambda b, t: (prev(t), 0)),
            pl.BlockSpec((BLOCK, LANES), lambda b, t: (prev(t), 0)),
        ] + cast_in,
        out_specs=[pl.BlockSpec((qb * BLOCK, qw), lambda b, t: (b * steps + t, 0))] + cast_out,
        out_shape=[jax.ShapeDtypeStruct((n, qw), BF16)] + cast_shapes,
        compiler_params=_cparams(2),
        name="swa",
    )(sinks, z, z, z, cos_t, sin_t, cos_t, sin_t, *casts)
    return y, cast


def _memkv_kernel(m_ref, g_ref, w_ref, o_ref):
    h = _rms(m_ref[...], g_ref[...]).astype(BF16)
    o_ref[...] = _dot(h, w_ref[...].astype(BF16)).astype(BF16)


def _memkv(mem, g, w):
    n, d = mem.shape
    ncols = w.shape[1]
    tm = _pick(n, (512, 256, 128))
    tn = _pick(ncols, (1024, 512, 256, 128))
    return pl.pallas_call(
        _memkv_kernel,
        grid=(ncols // tn, n // tm),
        in_specs=[
            pl.BlockSpec((tm, d), lambda j, i: (i, 0)),
            pl.BlockSpec((1, d), lambda j, i: (0, 0)),
            pl.BlockSpec((d, tn), lambda j, i: (0, j)),
        ],
        out_specs=pl.BlockSpec((tm, tn), lambda j, i: (i, j)),
        out_shape=jax.ShapeDtypeStruct((n, ncols), BF16),
        compiler_params=_cparams(2),
        name="memkv",
    )(mem, g.reshape(1, d), w)


def _mix_kernel(x_ref, yr_ref, ys_ref, wo_ref, bo_ref, g_ref, wq_ref, k_ref, v_ref, wxo_ref, o_ref):
    c = yr_ref.shape[1]
    d = x_ref.shape[1]
    hd = d // XATTN_HEADS
    x2 = x_ref[...] + _dot(yr_ref[...], wo_ref[:c, :]) + _dot(ys_ref[...], wo_ref[c:, :]) + bo_ref[...]
    q = _dot(_rms(x2, g_ref[...]).astype(BF16), wq_ref[...]).astype(BF16)
    scale = hd ** -0.5
    outs = []
    for h in range(XATTN_HEADS):
        sl = slice(h * hd, (h + 1) * hd)
        s = _dot_nt(q[:, sl], k_ref[:, sl]) * scale
        m = jnp.max(s, axis=-1, keepdims=True)
        e = jnp.exp(s - m)
        den = jnp.sum(e, axis=-1, keepdims=True)
        outs.append((_dot(e.astype(BF16), v_ref[:, sl]) * (1.0 / den)).astype(BF16))
    o_ref[...] = x2 + _dot(jnp.concatenate(outs, axis=1), wxo_ref[...])


def _mix(x, yr, ys, wo, bo, g, wq, kv, wxo, seq, mlen):
    n, d = x.shape
    c = yr.shape[1]
    tm = _pick(seq, (256, 128))
    per_seq = seq // tm
    once = pl.Buffered(1)
    return pl.pallas_call(
        _mix_kernel,
        grid=(n // tm,),
        in_specs=[
            pl.BlockSpec((tm, d), lambda i: (i, 0)),
            pl.BlockSpec((tm, c), lambda i: (i, 0)),
            pl.BlockSpec((tm, d - c), lambda i: (i, 0)),
            pl.BlockSpec((d, d), lambda i: (0, 0), pipeline_mode=once),
            pl.BlockSpec((1, d), lambda i: (0, 0)),
            pl.BlockSpec((1, d), lambda i: (0, 0)),
            pl.BlockSpec((d, d), lambda i: (0, 0), pipeline_mode=once),
            pl.BlockSpec((mlen, d), lambda i: (i // per_seq, 0)),
            pl.BlockSpec((mlen, d), lambda i: (i // per_seq, 1)),
            pl.BlockSpec((d, d), lambda i: (0, 0), pipeline_mode=once),
        ],
        out_specs=pl.BlockSpec((tm, d), lambda i: (i, 0)),
        out_shape=jax.ShapeDtypeStruct((n, d), F32),
        compiler_params=_cparams(1),
        name="mix",
    )(x, yr, ys, wo, bo.reshape(1, d), g.reshape(1, d), wq, kv, kv, wxo)


def _pad_cols(w, width):
    return jnp.pad(w, ((0, 0), (0, width - w.shape[1])))


def _pad_rows(w, height):
    return jnp.pad(w, ((0, height - w.shape[0]), (0, 0)))


def _rope_tables(seq):
    half = HEAD_DIM // 2
    lane = jnp.arange(LANES)
    inv_freq = ROPE_THETA ** (-jnp.arange(0, HEAD_DIM, 2, dtype=F32) / HEAD_DIM)
    ang = jnp.arange(seq, dtype=F32)[:, None] * inv_freq[lane % half][None, :]
    sign = jnp.where((lane % HEAD_DIM) < half, -1.0, 1.0)
    return jnp.cos(ang), jnp.sin(ang) * sign[None, :]


def kernel(x, mem, f1_norm, f1_gate, f1_up, f1_down, mix_norm, w_in, b_in_attn, rw_mu, rw_w0, rw_decay_up, rw_a0, rw_aaa_up, rw_gate_up, rw_k_k, rw_k_a, rw_r_k, rw_lnx_w, rw_lnx_b, attn_sinks, w_out, b_out, xa_norm, mem_norm, w_xq, w_xkv, w_xo, f2_norm, f2_gate, f2_up, f2_down, final_norm):
    batch, seq, d = x.shape
    mlen = mem.shape[1]
    depth = f1_norm.shape[0]
    c = rw_w0.shape[1]
    sw = d - c
    dl, al, gl = rw_decay_up.shape[1], rw_aaa_up.shape[1], rw_gate_up.shape[1]
    dlp, alp, glp = (_round_up(v, LANES) for v in (dl, al, gl))
    kvw = b_in_attn.shape[1] - sw
    q_heads = sw // HEAD_DIM
    group = q_heads // (kvw // (2 * HEAD_DIM))
    n_tiles = c // LANES
    n = batch * seq
    q_off = 3 * c
    lora_off = q_off + sw
    lora_w = dlp + alp + glp
    kv_off = lora_off + lora_w
    ncols = _round_up(kv_off + kvw, 2 * LANES)
    cos_t, sin_t = _rope_tables(seq)

    xf = x.reshape(n, d)
    memf = mem.reshape(batch * mlen, d)
    for l in range(depth):
        xf = _ffn(xf, f1_norm[l], f1_gate[l].astype(BF16), f1_up[l].astype(BF16), f1_down[l].astype(BF16), None)

        wl = w_in[l]
        o1, o2, o3 = 3 * c, 3 * c + dl, 3 * c + dl + al
        shift = o3 + gl
        w_all = jnp.concatenate([
            wl[:, :o1], wl[:, shift:shift + sw],
            _pad_cols(wl[:, o1:o2], dlp), _pad_cols(wl[:, o2:o3], alp), _pad_cols(wl[:, o3:shift], glp),
            wl[:, shift + sw:]], axis=1)
        w_all = _pad_cols(w_all, ncols).astype(BF16)
        mu = rw_mu[l][None, :]
        mu_all = _pad_cols(jnp.concatenate([
            mu[:, :o1], jnp.zeros((1, sw), F32),
            _pad_cols(mu[:, o1:o2], dlp), _pad_cols(mu[:, o2:o3], alp), _pad_cols(mu[:, o3:shift], glp)],
            axis=1), ncols)
        bia = b_in_attn[l][None, :]
        b_all = _pad_cols(jnp.concatenate([
            jnp.zeros((1, q_off), F32), bia[:, :sw], jnp.zeros((1, lora_w), F32), bia[:, sw:]], axis=1), ncols)
        z = _proj(xf, mix_norm[l], w_all, mu_all, b_all, seq)

        tiles = lambda v: v.reshape(n_tiles, 1, LANES)
        vecs = [rw_w0[l].reshape(1, c), rw_a0[l].reshape(1, c), tiles(rw_k_k[l]), tiles(rw_k_a[l]),
                tiles(rw_r_k[l]), tiles(rw_lnx_w[l]), tiles(rw_lnx_b[l])]
        loras = [_pad_rows(rw_decay_up[l], dlp).astype(BF16), _pad_rows(rw_aaa_up[l], alp).astype(BF16),
                 _pad_rows(rw_gate_up[l], glp).astype(BF16)]
        y_rwkv, (f2_down_b, w_out_b, w_xq_b, w_xo_b) = _rwkv(
            z, batch, seq, c, lora_off, lora_w, dlp, alp, vecs, loras,
            [f2_down[l], w_out[l], w_xq[l], w_xo[l]])
        y_swa, (f2_gate_b, f2_up_b) = _swa(z, attn_sinks[l], cos_t, sin_t, batch, seq, q_off, sw, kv_off, kvw,
                                           group, [f2_gate[l], f2_up[l]])

        kv_mem = _memkv(memf, mem_norm[l], w_xkv[l])
        xf = _mix(xf, y_rwkv, y_swa, w_out_b, b_out[l], xa_norm[l], w_xq_b, kv_mem, w_xo_b, seq, mlen)

        last = l == depth - 1
        xf = _ffn(xf, f2_norm[l], f2_gate_b, f2_up_b, f2_down_b, final_norm if last else None)
    return xf.reshape(batch, seq, d)
```

```python
import functools

import jax
import jax.numpy as jnp
from jax import lax
from jax.experimental import pallas as pl
from jax.experimental.pallas import tpu as pltpu

F32, BF16 = jnp.float32, jnp.bfloat16

LANES = 128
SUBLANES = 8
VMEM_LIMIT_BYTES = 56 * 1024 * 1024

HEAD_DIM = 64
HEADS_PER_TILE = LANES // HEAD_DIM
CHUNK = 128
BLOCK = 128
XATTN_HEADS = 4
RMS_EPS = 1e-6
GN_EPS = 64e-5
NEG_INF = -1e30
ROPE_THETA = 10000.0


def _round_up(n, m):
    return (n + m - 1) // m * m


def _pick(n, prefs):
    for p in prefs:
        if n % p == 0:
            return p
    raise ValueError(f"no tile in {prefs} divides {n}")


def _cparams(n_axes):
    return pltpu.CompilerParams(dimension_semantics=("arbitrary",) * n_axes,
                                vmem_limit_bytes=VMEM_LIMIT_BYTES)


def _dot(a, b):
    return jnp.dot(a, b, preferred_element_type=F32)


def _dot_nt(a, b):
    return lax.dot_general(a, b, (((1,), (1,)), ((), ())), preferred_element_type=F32)


def _cast_specs(weights, nsteps, flat_step):
    in_specs, out_specs, out_shapes = [], [], []
    for w in weights:
        units = w.shape[0] // (2 * SUBLANES)
        nblk = max(d for d in range(1, nsteps + 1) if units % d == 0)
        imap = lambda *g, nblk=nblk: (jnp.minimum(flat_step(*g), nblk - 1), 0)
        spec = pl.BlockSpec((w.shape[0] // nblk, w.shape[1]), imap)
        in_specs.append(spec)
        out_specs.append(spec)
        out_shapes.append(jax.ShapeDtypeStruct(w.shape, BF16))
    return in_specs, out_specs, out_shapes


def _cast_blocks(src_refs, dst_refs):
    for src, dst in zip(src_refs, dst_refs):
        dst[...] = src[...].astype(BF16)


def _rms(x, g):
    ms = jnp.mean(x * x, axis=-1, keepdims=True)
    return x * lax.rsqrt(ms + RMS_EPS) * g


def _softplus(x):
    return jnp.maximum(x, 0.0) + jnp.log(1.0 + jnp.exp(-jnp.abs(x)))


def _ffn_kernel(x_ref, g_ref, wg_ref, wu_ref, wd_ref, fg_ref, o_ref, h_ref, *, final_norm, rows):
    j = pl.program_id(1)
    tm = x_ref.shape[0]

    @pl.when(j == 0)
    def _():
        for r0 in range(0, tm, rows):
            x = x_ref[r0:r0 + rows, :]
            h_ref[r0:r0 + rows, :] = _rms(x, g_ref[...]).astype(BF16)
            o_ref[r0:r0 + rows, :] = x

    for r0 in range(0, tm, rows):
        h = h_ref[r0:r0 + rows, :]
        gate = _dot(h, wg_ref[...])
        up = _dot(h, wu_ref[...])
        act = (gate * jax.nn.sigmoid(gate) * up).astype(BF16)
        o_ref[r0:r0 + rows, :] += 0.5 * _dot(act, wd_ref[...])

    if final_norm:
        @pl.when(j == pl.num_programs(1) - 1)
        def _():
            for r0 in range(0, tm, rows):
                o_ref[r0:r0 + rows, :] = _rms(o_ref[r0:r0 + rows, :], fg_ref[...])


def _ffn(x, g, wg, wu, wd, fg):
    n, d = x.shape
    f = wg.shape[1]
    tm = _pick(n, (1024, 512, 256, 128))
    tf = _pick(f, (512, 256, 128))
    rows = min(tm, 512)
    final_norm = fg is not None
    fg = g if fg is None else fg
    return pl.pallas_call(
        functools.partial(_ffn_kernel, final_norm=final_norm, rows=rows),
        grid=(n // tm, f // tf),
        in_specs=[
            pl.BlockSpec((tm, d), lambda i, j: (i, 0)),
            pl.BlockSpec((1, d), lambda i, j: (0, 0)),
            pl.BlockSpec((d, tf), lambda i, j: (0, j)),
            pl.BlockSpec((d, tf), lambda i, j: (0, j)),
            pl.BlockSpec((tf, d), lambda i, j: (j, 0)),
            pl.BlockSpec((1, d), lambda i, j: (0, 0)),
        ],
        out_specs=pl.BlockSpec((tm, d), lambda i, j: (i, 0)),
        out_shape=jax.ShapeDtypeStruct((n, d), F32),
        scratch_shapes=[pltpu.VMEM((tm, d), BF16)],
        compiler_params=_cparams(2),
        name="ffn",
    )(x, g.reshape(1, d), wg, wu, wd, fg.reshape(1, d))


def _proj_kernel(x_ref, g_ref, w_ref, mu_ref, b_ref, o_ref, carry_ref, *, tiles_per_seq, tn):
    i = pl.program_id(0)
    tm = x_ref.shape[0]
    ncols = w_ref.shape[1]

    @pl.when(i % tiles_per_seq == 0)
    def _():
        carry_ref[...] = jnp.zeros(carry_ref.shape, F32)

    h = _rms(x_ref[...], g_ref[...]).astype(BF16)
    row = lax.broadcasted_iota(jnp.int32, (SUBLANES, tn), 0)
    for c0 in range(0, ncols, tn):
        cs = slice(c0, c0 + tn)
        z = _dot(h, w_ref[:, cs])
        prev_tail = carry_ref[:, cs]
        carry_ref[:, cs] = z[tm - SUBLANES:, :]
        zs = pltpu.roll(z, 1, 0)
        head = jnp.where(row == 0, pltpu.roll(prev_tail, 1, 0), zs[:SUBLANES])
        mu = mu_ref[:, cs]
        b = b_ref[:, cs]
        z0 = z[:SUBLANES]
        o_ref[:SUBLANES, cs] = z0 + (head - z0) * mu + b
        z1 = z[SUBLANES:]
        o_ref[SUBLANES:, cs] = z1 + (zs[SUBLANES:] - z1) * mu + b


def _proj(x, g, w_all, mu_all, b_all, seq):
    n, d = x.shape
    ncols = w_all.shape[1]
    tm = _pick(seq, (256, 128))
    tn = _pick(ncols, (1024, 512, 256, 128))
    return pl.pallas_call(
        functools.partial(_proj_kernel, tiles_per_seq=seq // tm, tn=tn),
        grid=(n // tm,),
        in_specs=[
            pl.BlockSpec((tm, d), lambda i: (i, 0)),
            pl.BlockSpec((1, d), lambda i: (0, 0)),
            pl.BlockSpec((d, ncols), lambda i: (0, 0), pipeline_mode=pl.Buffered(1)),
            pl.BlockSpec((1, ncols), lambda i: (0, 0)),
            pl.BlockSpec((1, ncols), lambda i: (0, 0)),
        ],
        out_specs=pl.BlockSpec((tm, ncols), lambda i: (i, 0)),
        out_shape=jax.ShapeDtypeStruct((n, ncols), F32),
        scratch_shapes=[pltpu.VMEM((SUBLANES, ncols), F32)],
        compiler_params=_cparams(1),
        name="proj",
    )(x, g.reshape(1, d), w_all, mu_all, b_all)


def _bdot(a, b):
    return lax.dot_general(a, b, (((2,), (1,)), ((0,), (0,))), preferred_element_type=F32)


def _bdot_nt(a, b):
    return lax.dot_general(a, b, (((2,), (2,)), ((0,), (0,))), preferred_element_type=F32)


def _inv_unit_lower(nmat):
    L = nmat.shape[-1]
    r = lax.broadcasted_iota(jnp.int32, (L, L), 0)
    c = lax.broadcasted_iota(jnp.int32, (L, L), 1)
    eye = jnp.where(r == c, 1.0, 0.0)
    t = eye + nmat
    pw = nmat.astype(BF16)
    pw = _bdot(pw, pw).astype(BF16)
    steps = L.bit_length() - 2
    for i in range(steps):
        if i + 1 < steps:
            both = _bdot(jnp.concatenate([t.astype(BF16), pw], axis=1), pw)
            t = t + both[:, :L]
            pw = both[:, L:].astype(BF16)
        else:
            t = t + _bdot(t.astype(BF16), pw)
    return t


RWKV_INPUTS = 14


def _rwkv_kernel(*refs, seqs, n_cast, **static):
    cast_in = refs[RWKV_INPUTS:RWKV_INPUTS + n_cast]
    cast_out = refs[RWKV_INPUTS + n_cast + 1:RWKV_INPUTS + 2 * n_cast + 1]
    seq_refs = refs[:RWKV_INPUTS] + (refs[RWKV_INPUTS + n_cast],) + refs[RWKV_INPUTS + 2 * n_cast + 1:]
    s_ref = seq_refs[RWKV_INPUTS + 1]

    @pl.when(pl.program_id(1) == 0)
    def _():
        s_ref[...] = jnp.zeros(s_ref.shape, F32)

    for q in range(seqs):
        _rwkv_sequence(q, *seq_refs, **static)
    _cast_blocks(cast_in, cast_out)


def _rwkv_sequence(q, r_ref, k_ref, v_ref, lora_ref, w0_ref, a0_ref, kk_ref, ka_ref, rk_ref, gw_ref, gb_ref,
                   du_ref, au_ref, gu_ref,
                   o_ref,
                   s_ref, pr_ref, pk_ref, pv_ref, pa_ref, pg_ref, plp_ref, plw_ref,
                   *, n_tiles, dlp, alp):
    L = CHUNK
    tiles = slice(q * n_tiles, (q + 1) * n_tiles)

    lora = lora_ref[q]
    wd = jnp.tanh(lora[:, :dlp])
    ad = lora[:, dlp:dlp + alp]
    gd = jax.nn.sigmoid(lora[:, dlp + alp:])
    w = -_softplus(-(w0_ref[...] + _dot(wd.astype(BF16), du_ref[...]))) - 0.5
    lw = -jnp.exp(w)
    asig = jax.nn.sigmoid(a0_ref[...] + _dot(ad.astype(BF16), au_ref[...]))
    gate = _dot(gd.astype(BF16), gu_ref[...])
    row = lax.broadcasted_iota(jnp.int32, (L, L), 0)
    col = lax.broadcasted_iota(jnp.int32, (L, L), 1)
    incl = row >= col
    strict = row > col
    tril = jnp.where(incl, 1.0, 0.0).astype(BF16)
    h1 = lw.astype(BF16)
    r1 = lw - h1.astype(F32)
    h2 = r1.astype(BF16)
    h3 = (r1 - h2.astype(F32)).astype(BF16)
    logp = _dot(tril, h1) + _dot(tril, h2) + _dot(tril, h3)
    for p in range(n_tiles):
        sl = slice(p * LANES, (p + 1) * LANES)
        i = q * n_tiles + p
        pr_ref[i] = r_ref[q, :, sl]
        pk_ref[i] = k_ref[q, :, sl]
        pv_ref[i] = v_ref[q, :, sl]
        pa_ref[i] = asig[:, sl]
        pg_ref[i] = gate[:, sl]
        plp_ref[i] = logp[:, sl]
        plw_ref[i] = lw[:, sl]

    lane = lax.broadcasted_iota(jnp.int32, (1, LANES), 1)
    lo = lane < HEAD_DIM
    same_head = jnp.where(row < HEAD_DIM, 0, 1) == jnp.where(col < HEAD_DIM, 0, 1)
    inv_hd = 1.0 / HEAD_DIM

    def first(x):
        return jnp.where(lo, x, 0.0)

    def second(x):
        return jnp.where(lo, 0.0, x)

    def segsum(x):
        return jnp.where(lo, jnp.sum(first(x), axis=-1, keepdims=True),
                         jnp.sum(second(x), axis=-1, keepdims=True))

    r = pr_ref[tiles]
    k = pk_ref[tiles]
    v = pv_ref[tiles]
    a_s = pa_ref[tiles]
    lp = plp_ref[tiles]
    kk = k * kk_ref[...]
    ss = segsum(kk * kk)
    kk = kk / jnp.maximum(jnp.sqrt(ss), 1e-12)
    a = -kk
    b = kk * a_s
    km = k * (1.0 + (a_s - 1.0) * ka_ref[...])
    cmid = lp[:, L // 2 - 1:L // 2, :]
    clast = lp[:, L - 1:L, :]
    lpe = lp - plw_ref[tiles]
    e_inv = jnp.exp(cmid - lp)
    e_mid = jnp.exp(cmid)
    at = a * jnp.exp(lpe - cmid)
    a_abs = at * e_mid
    rt = r * jnp.exp(lp - cmid)
    r_abs = rt * e_mid
    bt = b * e_inv
    kt = km * e_inv
    e_l = jnp.exp(clast - cmid)
    e_p = jnp.exp(clast)
    lhs = jnp.concatenate([first(at), second(at), first(rt), second(rt)], axis=1).astype(BF16)
    gmat = _bdot_nt(lhs, jnp.concatenate([bt, kt], axis=1).astype(BF16))
    n_lo = jnp.where(strict, gmat[:, 0:L, 0:L], 0.0)
    ak_lo = jnp.where(strict, gmat[:, 0:L, L:], 0.0)
    n_hi = jnp.where(strict, gmat[:, L:2 * L, 0:L], 0.0)
    ak_hi = jnp.where(strict, gmat[:, L:2 * L, L:], 0.0)
    rb_lo = jnp.where(incl, gmat[:, 2 * L:3 * L, 0:L], 0.0)
    rk_lo = jnp.where(incl, gmat[:, 2 * L:3 * L, L:], 0.0)
    rb_hi = jnp.where(incl, gmat[:, 3 * L:, 0:L], 0.0)
    rk_hi = jnp.where(incl, gmat[:, 3 * L:, L:], 0.0)
    t_all = _inv_unit_lower(jnp.concatenate([n_lo, n_hi], axis=0))
    t_lo, t_hi = t_all[:n_tiles], t_all[n_tiles:]
    s = s_ref[tiles]
    s_b = s.astype(BF16)
    v_lohi = jnp.concatenate([first(v), second(v)], axis=1).astype(BF16)
    rhs = _bdot_nt(a_abs.astype(BF16), s_b) + _bdot(jnp.concatenate([ak_lo, ak_hi], axis=2).astype(BF16), v_lohi)
    u = _bdot(jnp.concatenate([t_lo, t_hi], axis=2).astype(BF16),
              jnp.concatenate([first(rhs), second(rhs)], axis=1).astype(BF16))
    u_lohi = jnp.concatenate([first(u), second(u)], axis=1).astype(BF16)
    y = _bdot_nt(r_abs.astype(BF16), s_b) + _bdot(
        jnp.concatenate([rb_lo, rb_hi, rk_lo, rk_hi], axis=2).astype(BF16),
        jnp.concatenate([u_lohi, v_lohi], axis=1))
    upd = _bdot(jnp.concatenate([jnp.swapaxes(u, 1, 2), jnp.swapaxes(v, 1, 2)], axis=2).astype(BF16),
                jnp.concatenate([bt * e_l, kt * e_l], axis=1).astype(BF16))
    s_ref[tiles] = s * e_p + jnp.where(same_head, upd, 0.0)
    mean = segsum(y) * inv_hd
    dev = y - mean
    var = segsum(dev * dev) * inv_hd
    yn = dev * lax.rsqrt(var + GN_EPS) * gw_ref[...] + gb_ref[...]
    bonus = segsum(r * km * rk_ref[...]) * v
    out = ((yn + bonus) * pg_ref[tiles]).astype(BF16)
    for p in range(n_tiles):
        o_ref[q, :, p * LANES:(p + 1) * LANES] = out[p]


def _rwkv(z, batch, seq, c, lora_off, lora_w, dlp, alp, vecs, loras, casts):
    n = z.shape[0]
    nc = seq // CHUNK
    n_tiles = c // LANES
    seqs = 1
    assert lora_off % lora_w == 0 and len(vecs) + len(loras) + 4 == RWKV_INPUTS
    lora_blk = lora_off // lora_w
    z3 = z.reshape(batch, seq, z.shape[1])
    cast_in, cast_out, cast_shapes = _cast_specs(casts, (batch // seqs) * nc, lambda b, t: b * nc + t)
    vec_spec = pl.BlockSpec((n_tiles, 1, LANES), lambda b, t: (0, 0, 0))
    in_specs = [
        pl.BlockSpec((seqs, CHUNK, c), lambda b, t: (b, t, 0)),
        pl.BlockSpec((seqs, CHUNK, c), lambda b, t: (b, t, 1)),
        pl.BlockSpec((seqs, CHUNK, c), lambda b, t: (b, t, 2)),
        pl.BlockSpec((seqs, CHUNK, lora_w), lambda b, t: (b, t, lora_blk)),
        pl.BlockSpec((1, c), lambda b, t: (0, 0)),
        pl.BlockSpec((1, c), lambda b, t: (0, 0)),
    ] + [vec_spec] * 5 + [pl.BlockSpec(w.shape, lambda b, t: (0, 0)) for w in loras]
    tile_f32 = pltpu.VMEM((seqs * n_tiles, CHUNK, LANES), F32)
    y, *cast = pl.pallas_call(
        functools.partial(_rwkv_kernel, seqs=seqs, n_cast=len(casts), n_tiles=n_tiles, dlp=dlp, alp=alp),
        grid=(batch // seqs, nc),
        in_specs=in_specs + cast_in,
        out_specs=[pl.BlockSpec((seqs, CHUNK, c), lambda b, t: (b, t, 0))] + cast_out,
        out_shape=[jax.ShapeDtypeStruct((batch, seq, c), BF16)] + cast_shapes,
        scratch_shapes=[pltpu.VMEM((seqs * n_tiles, LANES, LANES), F32)] + [tile_f32] * 7,
        compiler_params=_cparams(2),
        name="rwkv",
    )(z3, z3, z3, z3, *vecs, *loras, *casts)
    return y.reshape(n, c), cast


def _swa_kernel(sink_ref, q_ref, kvc_ref, kvp_ref, cosc_ref, sinc_ref, cosp_ref, sinp_ref, *rest,
                n_tiles, group, qb):
    n_cast = len(rest) // 2
    o_ref = rest[n_cast]
    _cast_blocks(rest[:n_cast], rest[n_cast + 1:])
    nblk = pl.program_id(1)
    lane = lax.broadcasted_iota(jnp.int32, (1, LANES), 1)
    lo = lane < HEAD_DIM
    rot_lo = jnp.bitwise_and(lane, HEAD_DIM - 1) < HEAD_DIM // 2

    def rope(x, cos, sin_signed):
        partner = jnp.where(rot_lo, pltpu.roll(x, LANES - HEAD_DIM // 2, 1), pltpu.roll(x, HEAD_DIM // 2, 1))
        return x * cos + partner * sin_signed

    cosc = cosc_ref[...]
    sinc = sinc_ref[...]
    kvc = kvc_ref[...]
    kvp = kvp_ref[...]
    keys = jnp.concatenate([rope(kvp[:, :LANES], cosp_ref[...], sinp_ref[...]),
                            rope(kvc[:, :LANES], cosc, sinc)], axis=0)
    vals = jnp.concatenate([kvp[:, LANES:], kvc[:, LANES:]], axis=0)
    keys_sw = pltpu.roll(keys, HEAD_DIM, 1)
    vals_sw = pltpu.roll(vals, HEAD_DIM, 1)
    k_first = [jnp.where(lo, keys, 0.0).astype(BF16), jnp.where(lo, keys_sw, 0.0).astype(BF16)]
    k_second = [jnp.where(lo, 0.0, keys_sw).astype(BF16), jnp.where(lo, 0.0, keys).astype(BF16)]
    ones_kv = jnp.ones(vals.shape, BF16)
    with_ones = lambda x: jnp.concatenate([x.astype(BF16), ones_kv], axis=1)
    v_first = [with_ones(jnp.where(lo, vals, 0.0)), with_ones(jnp.where(lo, vals_sw, 0.0))]
    v_second = [with_ones(jnp.where(lo, 0.0, vals_sw)), with_ones(jnp.where(lo, 0.0, vals))]
    qi = lax.broadcasted_iota(jnp.int32, (BLOCK, 2 * BLOCK), 0)
    ki = lax.broadcasted_iota(jnp.int32, (BLOCK, 2 * BLOCK), 1)
    window = (ki > qi) & (ki <= qi + BLOCK)
    valid = [window & ((nblk > 0) | (ki >= BLOCK))] + [window] * (qb - 1)
    scale = HEAD_DIM ** -0.5
    heads = [(j, p, half) for j in range(qb) for p in range(n_tiles) for half in range(HEADS_PER_TILE)]
    scores = []
    for j in range(qb):
        qrows = slice(j * BLOCK, (j + 1) * BLOCK)
        krows = slice(j * BLOCK, (j + 2) * BLOCK)
        for p in range(n_tiles):
            g = (p * HEADS_PER_TILE) // group
            qp = (rope(q_ref[qrows, p * LANES:(p + 1) * LANES], cosc[qrows], sinc[qrows]) * scale).astype(BF16)
            scores += [_dot_nt(qp, k_first[g][krows]), _dot_nt(qp, k_second[g][krows])]
    probs, sink_terms = [], []
    for (j, p, half), s in zip(heads, scores):
        s = jnp.where(valid[j], s, NEG_INF)
        sink = sink_ref[p * HEADS_PER_TILE + half]
        m = jnp.maximum(jnp.max(s, axis=-1, keepdims=True), sink)
        probs.append(jnp.exp(s - m).astype(BF16))
        sink_terms.append(jnp.exp(sink - m))
    outs = []
    for (j, p, half), e, st in zip(heads, probs, sink_terms):
        g = (p * HEADS_PER_TILE) // group
        krows = slice(j * BLOCK, (j + 2) * BLOCK)
        pv = _dot(e, (v_second[g] if half else v_first[g])[krows])
        outs.append(pv[:, :LANES] * (1.0 / (pv[:, LANES:] + st)))
    for j in range(qb):
        for p in range(n_tiles):
            i = (j * n_tiles + p) * HEADS_PER_TILE
            o_ref[j * BLOCK:(j + 1) * BLOCK, p * LANES:(p + 1) * LANES] = (outs[i] + outs[i + 1]).astype(BF16)


def _swa(z, sinks, cos_t, sin_t, batch, seq, q_off, qw, kv_off, kvw, group, casts):
    n = z.shape[0]
    nb = seq // BLOCK
    assert q_off % qw == 0 and kv_off % kvw == 0 and kvw == 2 * LANES
    q_blk, kv_blk = q_off // qw, kv_off // kvw
    qb = _pick(nb, (4, 2, 1))
    steps = nb // qb
    prev = lambda t: jnp.maximum(t * qb - 1, 0)
    cast_in, cast_out, cast_shapes = _cast_specs(casts, batch * steps, lambda b, t: b * steps + t)
    y, *cast = pl.pallas_call(
        functools.partial(_swa_kernel, n_tiles=qw // LANES, group=group, qb=qb),
        grid=(batch, steps),
        in_specs=[
            pl.BlockSpec(memory_space=pltpu.SMEM),
            pl.BlockSpec((qb * BLOCK, qw), lambda b, t: (b * steps + t, q_blk)),
            pl.BlockSpec((qb * BLOCK, kvw), lambda b, t: (b * steps + t, kv_blk)),
            pl.BlockSpec((BLOCK, kvw), lambda b, t: (b * nb + prev(t), kv_blk)),
            pl.BlockSpec((qb * BLOCK, LANES), lambda b, t: (t, 0)),
            pl.BlockSpec((qb * BLOCK, LANES), lambda b, t: (t, 0)),
            pl.BlockSpec((BLOCK, LANES), lambda b, t: (prev(t), 0)),
            pl.BlockSpec((BLOCK, LANES), lambda b, t: (prev(t), 0)),
        ] + cast_in,
        out_specs=[pl.BlockSpec((qb * BLOCK, qw), lambda b, t: (b * steps + t, 0))] + cast_out,
        out_shape=[jax.ShapeDtypeStruct((n, qw), BF16)] + cast_shapes,
        compiler_params=_cparams(2),
        name="swa",
    )(sinks, z, z, z, cos_t, sin_t, cos_t, sin_t, *casts)
    return y, cast


def _memkv_kernel(m_ref, g_ref, w_ref, o_ref):
    h = _rms(m_ref[...], g_ref[...]).astype(BF16)
    o_ref[...] = _dot(h, w_ref[...].astype(BF16)).astype(BF16)


def _memkv(mem, g, w):
    n, d = mem.shape
    ncols = w.shape[1]
    tm = _pick(n, (512, 256, 128))
    tn = _pick(ncols, (1024, 512, 256, 128))
    return pl.pallas_call(
        _memkv_kernel,
        grid=(ncols // tn, n // tm),
        in_specs=[
            pl.BlockSpec((tm, d), lambda j, i: (i, 0)),
            pl.BlockSpec((1, d), lambda j, i: (0, 0)),
            pl.BlockSpec((d, tn), lambda j, i: (0, j)),
        ],
        out_specs=pl.BlockSpec((tm, tn), lambda j, i: (i, j)),
        out_shape=jax.ShapeDtypeStruct((n, ncols), BF16),
        compiler_params=_cparams(2),
        name="memkv",
    )(mem, g.reshape(1, d), w)


def _mix_kernel(x_ref, yr_ref, ys_ref, wo_ref, bo_ref, g_ref, wq_ref, k_ref, v_ref, wxo_ref, o_ref):
    c = yr_ref.shape[1]
    d = x_ref.shape[1]
    hd = d // XATTN_HEADS
    x2 = x_ref[...] + _dot(yr_ref[...], wo_ref[:c, :]) + _dot(ys_ref[...], wo_ref[c:, :]) + bo_ref[...]
    q = _dot(_rms(x2, g_ref[...]).astype(BF16), wq_ref[...]).astype(BF16)
    scale = hd ** -0.5
    outs = []
    for h in range(XATTN_HEADS):
        sl = slice(h * hd, (h + 1) * hd)
        s = _dot_nt(q[:, sl], k_ref[:, sl]) * scale
        m = jnp.max(s, axis=-1, keepdims=True)
        e = jnp.exp(s - m)
        den = jnp.sum(e, axis=-1, keepdims=True)
        outs.append((_dot(e.astype(BF16), v_ref[:, sl]) * (1.0 / den)).astype(BF16))
    o_ref[...] = x2 + _dot(jnp.concatenate(outs, axis=1), wxo_ref[...])


def _mix(x, yr, ys, wo, bo, g, wq, kv, wxo, seq, mlen):
    n, d = x.shape
    c = yr.shape[1]
    tm = _pick(seq, (256, 128))
    per_seq = seq // tm
    once = pl.Buffered(1)
    return pl.pallas_call(
        _mix_kernel,
        grid=(n // tm,),
        in_specs=[
            pl.BlockSpec((tm, d), lambda i: (i, 0)),
            pl.BlockSpec((tm, c), lambda i: (i, 0)),
            pl.BlockSpec((tm, d - c), lambda i: (i, 0)),
            pl.BlockSpec((d, d), lambda i: (0, 0), pipeline_mode=once),
            pl.BlockSpec((1, d), lambda i: (0, 0)),
            pl.BlockSpec((1, d), lambda i: (0, 0)),
            pl.BlockSpec((d, d), lambda i: (0, 0), pipeline_mode=once),
            pl.BlockSpec((mlen, d), lambda i: (i // per_seq, 0)),
            pl.BlockSpec((mlen, d), lambda i: (i // per_seq, 1)),
            pl.BlockSpec((d, d), lambda i: (0, 0), pipeline_mode=once),
        ],
        out_specs=pl.BlockSpec((tm, d), lambda i: (i, 0)),
        out_shape=jax.ShapeDtypeStruct((n, d), F32),
        compiler_params=_cparams(1),
        name="mix",
    )(x, yr, ys, wo, bo.reshape(1, d), g.reshape(1, d), wq, kv, kv, wxo)


def _pad_cols(w, width):
    return jnp.pad(w, ((0, 0), (0, width - w.shape[1])))


def _pad_rows(w, height):
    return jnp.pad(w, ((0, height - w.shape[0]), (0, 0)))


def _rope_tables(seq):
    half = HEAD_DIM // 2
    lane = jnp.arange(LANES)
    inv_freq = ROPE_THETA ** (-jnp.arange(0, HEAD_DIM, 2, dtype=F32) / HEAD_DIM)
    ang = jnp.arange(seq, dtype=F32)[:, None] * inv_freq[lane % half][None, :]
    sign = jnp.where((lane % HEAD_DIM) < half, -1.0, 1.0)
    return jnp.cos(ang), jnp.sin(ang) * sign[None, :]


def kernel(x, mem, f1_norm, f1_gate, f1_up, f1_down, mix_norm, w_in, b_in_attn, rw_mu, rw_w0, rw_decay_up, rw_a0, rw_aaa_up, rw_gate_up, rw_k_k, rw_k_a, rw_r_k, rw_lnx_w, rw_lnx_b, attn_sinks, w_out, b_out, xa_norm, mem_norm, w_xq, w_xkv, w_xo, f2_norm, f2_gate, f2_up, f2_down, final_norm):
    batch, seq, d = x.shape
    mlen = mem.shape[1]
    depth = f1_norm.shape[0]
    c = rw_w0.shape[1]
    sw = d - c
    dl, al, gl = rw_decay_up.shape[1], rw_aaa_up.shape[1], rw_gate_up.shape[1]
    dlp, alp, glp = (_round_up(v, LANES) for v in (dl, al, gl))
    kvw = b_in_attn.shape[1] - sw
    q_heads = sw // HEAD_DIM
    group = q_heads // (kvw // (2 * HEAD_DIM))
    n_tiles = c // LANES
    n = batch * seq
    q_off = 3 * c
    lora_off = q_off + sw
    lora_w = dlp + alp + glp
    kv_off = lora_off + lora_w
    ncols = _round_up(kv_off + kvw, 2 * LANES)
    cos_t, sin_t = _rope_tables(seq)

    xf = x.reshape(n, d)
    memf = mem.reshape(batch * mlen, d)
    for l in range(depth):
        xf = _ffn(xf, f1_norm[l], f1_gate[l].astype(BF16), f1_up[l].astype(BF16), f1_down[l].astype(BF16), None)

        wl = w_in[l]
        o1, o2, o3 = 3 * c, 3 * c + dl, 3 * c + dl + al
        shift = o3 + gl
        w_all = jnp.concatenate([
            wl[:, :o1], wl[:, shift:shift + sw],
            _pad_cols(wl[:, o1:o2], dlp), _pad_cols(wl[:, o2:o3], alp), _pad_cols(wl[:, o3:shift], glp),
            wl[:, shift + sw:]], axis=1)
        w_all = _pad_cols(w_all, ncols).astype(BF16)
        mu = rw_mu[l][None, :]
        mu_all = _pad_cols(jnp.concatenate([
            mu[:, :o1], jnp.zeros((1, sw), F32),
            _pad_cols(mu[:, o1:o2], dlp), _pad_cols(mu[:, o2:o3], alp), _pad_cols(mu[:, o3:shift], glp)],
            axis=1), ncols)
        bia = b_in_attn[l][None, :]
        b_all = _pad_cols(jnp.concatenate([
            jnp.zeros((1, q_off), F32), bia[:, :sw], jnp.zeros((1, lora_w), F32), bia[:, sw:]], axis=1), ncols)
        z = _proj(xf, mix_norm[l], w_all, mu_all, b_all, seq)

        tiles = lambda v: v.reshape(n_tiles, 1, LANES)
        vecs = [rw_w0[l].reshape(1, c), rw_a0[l].reshape(1, c), tiles(rw_k_k[l]), tiles(rw_k_a[l]),
                tiles(rw_r_k[l]), tiles(rw_lnx_w[l]), tiles(rw_lnx_b[l])]
        loras = [_pad_rows(rw_decay_up[l], dlp).astype(BF16), _pad_rows(rw_aaa_up[l], alp).astype(BF16),
                 _pad_rows(rw_gate_up[l], glp).astype(BF16)]
        y_rwkv, (f2_down_b, w_out_b, w_xq_b, w_xo_b) = _rwkv(
            z, batch, seq, c, lora_off, lora_w, dlp, alp, vecs, loras,
            [f2_down[l], w_out[l], w_xq[l], w_xo[l]])
        y_swa, (f2_gate_b, f2_up_b) = _swa(z, attn_sinks[l], cos_t, sin_t, batch, seq, q_off, sw, kv_off, kvw,
                                           group, [f2_gate[l], f2_up[l]])

        kv_mem = _memkv(memf, mem_norm[l], w_xkv[l])
        xf = _mix(xf, y_rwkv, y_swa, w_out_b, b_out[l], xa_norm[l], w_xq_b, kv_mem, w_xo_b, seq, mlen)

        last = l == depth - 1
        xf = _ffn(xf, f2_norm[l], f2_gate_b, f2_up_b, f2_down_b, final_norm if last else None)
    return xf.reshape(batch, seq, d)
```

```python
import functools

import jax
import jax.numpy as jnp
from jax import lax
from jax.experimental import pallas as pl
from jax.experimental.pallas import tpu as pltpu

F32, BF16 = jnp.float32, jnp.bfloat16

LANES = 128
SUBLANES = 8
VMEM_LIMIT_BYTES = 56 * 1024 * 1024

HEAD_DIM = 64
HEADS_PER_TILE = LANES // HEAD_DIM
CHUNK = 128
BLOCK = 128
XATTN_HEADS = 4
RMS_EPS = 1e-6
GN_EPS = 64e-5
NEG_INF = -1e30
ROPE_THETA = 10000.0


def _round_up(n, m):
    return (n + m - 1) // m * m


def _pick(n, prefs):
    for p in prefs:
        if n % p == 0:
            return p
    raise ValueError(f"no tile in {prefs} divides {n}")


def _cparams(n_axes):
    return pltpu.CompilerParams(dimension_semantics=("arbitrary",) * n_axes,
                                vmem_limit_bytes=VMEM_LIMIT_BYTES)


def _dot(a, b):
    return jnp.dot(a, b, preferred_element_type=F32)


def _dot_nt(a, b):
    return lax.dot_general(a, b, (((1,), (1,)), ((), ())), preferred_element_type=F32)


def _cast_specs(weights, nsteps, flat_step):
    in_specs, out_specs, out_shapes = [], [], []
    for w in weights:
        units = w.shape[0] // (2 * SUBLANES)
        nblk = max(d for d in range(1, nsteps + 1) if units % d == 0)
        imap = lambda *g, nblk=nblk: (jnp.minimum(flat_step(*g), nblk - 1), 0)
        spec = pl.BlockSpec((w.shape[0] // nblk, w.shape[1]), imap)
        in_specs.append(spec)
        out_specs.append(spec)
        out_shapes.append(jax.ShapeDtypeStruct(w.shape, BF16))
    return in_specs, out_specs, out_shapes


def _cast_blocks(src_refs, dst_refs):
    for src, dst in zip(src_refs, dst_refs):
        dst[...] = src[...].astype(BF16)


def _rms(x, g):
    ms = jnp.mean(x * x, axis=-1, keepdims=True)
    return x * lax.rsqrt(ms + RMS_EPS) * g


def _softplus(x):
    return jnp.maximum(x, 0.0) + jnp.log(1.0 + jnp.exp(-jnp.abs(x)))


def _ffn_kernel(x_ref, g_ref, wg_ref, wu_ref, wd_ref, fg_ref, o_ref, h_ref, *, final_norm, rows):
    j = pl.program_id(1)
    tm = x_ref.shape[0]

    @pl.when(j == 0)
    def _():
        for r0 in range(0, tm, rows):
            x = x_ref[r0:r0 + rows, :]
            h_ref[r0:r0 + rows, :] = _rms(x, g_ref[...]).astype(BF16)
            o_ref[r0:r0 + rows, :] = x

    for r0 in range(0, tm, rows):
        h = h_ref[r0:r0 + rows, :]
        gate = _dot(h, wg_ref[...])
        up = _dot(h, wu_ref[...])
        act = (gate * jax.nn.sigmoid(gate) * up).astype(BF16)
        o_ref[r0:r0 + rows, :] += 0.5 * _dot(act, wd_ref[...])

    if final_norm:
        @pl.when(j == pl.num_programs(1) - 1)
        def _():
            for r0 in range(0, tm, rows):
                o_ref[r0:r0 + rows, :] = _rms(o_ref[r0:r0 + rows, :], fg_ref[...])


def _ffn(x, g, wg, wu, wd, fg):
    n, d = x.shape
    f = wg.shape[1]
    tm = _pick(n, (1024, 512, 256, 128))
    tf = _pick(f, (512, 256, 128))
    rows = min(tm, 512)
    final_norm = fg is not None
    fg = g if fg is None else fg
    return pl.pallas_call(
        functools.partial(_ffn_kernel, final_norm=final_norm, rows=rows),
        grid=(n // tm, f // tf),
        in_specs=[
            pl.BlockSpec((tm, d), lambda i, j: (i, 0)),
            pl.BlockSpec((1, d), lambda i, j: (0, 0)),
            pl.BlockSpec((d, tf), lambda i, j: (0, j)),
            pl.BlockSpec((d, tf), lambda i, j: (0, j)),
            pl.BlockSpec((tf, d), lambda i, j: (j, 0)),
            pl.BlockSpec((1, d), lambda i, j: (0, 0)),
        ],
        out_specs=pl.BlockSpec((tm, d), lambda i, j: (i, 0)),
        out_shape=jax.ShapeDtypeStruct((n, d), F32),
        scratch_shapes=[pltpu.VMEM((tm, d), BF16)],
        compiler_params=_cparams(2),
        name="ffn",
    )(x, g.reshape(1, d), wg, wu, wd, fg.reshape(1, d))


def _proj_kernel(x_ref, g_ref, w_ref, mu_ref, b_ref, o_ref, carry_ref, *, tiles_per_seq, tn):
    i = pl.program_id(0)
    tm = x_ref.shape[0]
    ncols = w_ref.shape[1]

    @pl.when(i % tiles_per_seq == 0)
    def _():
        carry_ref[...] = jnp.zeros(carry_ref.shape, F32)

    h = _rms(x_ref[...], g_ref[...]).astype(BF16)
    row = lax.broadcasted_iota(jnp.int32, (SUBLANES, tn), 0)
    for c0 in range(0, ncols, tn):
        cs = slice(c0, c0 + tn)
        z = _dot(h, w_ref[:, cs])
        prev_tail = carry_ref[:, cs]
        carry_ref[:, cs] = z[tm - SUBLANES:, :]
        zs = pltpu.roll(z, 1, 0)
        head = jnp.where(row == 0, pltpu.roll(prev_tail, 1, 0), zs[:SUBLANES])
        mu = mu_ref[:, cs]
        b = b_ref[:, cs]
        z0 = z[:SUBLANES]
        o_ref[:SUBLANES, cs] = z0 + (head - z0) * mu + b
        z1 = z[SUBLANES:]
        o_ref[SUBLANES:, cs] = z1 + (zs[SUBLANES:] - z1) * mu + b


def _proj(x, g, w_all, mu_all, b_all, seq):
    n, d = x.shape
    ncols = w_all.shape[1]
    tm = _pick(seq, (256, 128))
    tn = _pick(ncols, (1024, 512, 256, 128))
    return pl.pallas_call(
        functools.partial(_proj_kernel, tiles_per_seq=seq // tm, tn=tn),
        grid=(n // tm,),
        in_specs=[
            pl.BlockSpec((tm, d), lambda i: (i, 0)),
            pl.BlockSpec((1, d), lambda i: (0, 0)),
            pl.BlockSpec((d, ncols), lambda i: (0, 0), pipeline_mode=pl.Buffered(1)),
            pl.BlockSpec((1, ncols), lambda i: (0, 0)),
            pl.BlockSpec((1, ncols), lambda i: (0, 0)),
        ],
        out_specs=pl.BlockSpec((tm, ncols), lambda i: (i, 0)),
        out_shape=jax.ShapeDtypeStruct((n, ncols), F32),
        scratch_shapes=[pltpu.VMEM((SUBLANES, ncols), F32)],
        compiler_params=_cparams(1),
        name="proj",
    )(x, g.reshape(1, d), w_all, mu_all, b_all)


def _bdot(a, b):
    return lax.dot_general(a, b, (((2,), (1,)), ((0,), (0,))), preferred_element_type=F32)


def _bdot_nt(a, b):
    return lax.dot_general(a, b, (((2,), (2,)), ((0,), (0,))), preferred_element_type=F32)


def _inv_unit_lower(nmat):
    L = nmat.shape[-1]
    r = lax.broadcasted_iota(jnp.int32, (L, L), 0)
    c = lax.broadcasted_iota(jnp.int32, (L, L), 1)
    eye = jnp.where(r == c, 1.0, 0.0)
    t = eye + nmat
    pw = nmat.astype(BF16)
    pw = _bdot(pw, pw).astype(BF16)
    steps = L.bit_length() - 2
    for i in range(steps):
        if i + 1 < steps:
            both = _bdot(jnp.concatenate([t.astype(BF16), pw], axis=1), pw)
            t = t + both[:, :L]
            pw = both[:, L:].astype(BF16)
        else:
            t = t + _bdot(t.astype(BF16), pw)
    return t


RWKV_INPUTS = 14


def _rwkv_kernel(*refs, seqs, n_cast, **static):
    cast_in = refs[RWKV_INPUTS:RWKV_INPUTS + n_cast]
    cast_out = refs[RWKV_INPUTS + n_cast + 1:RWKV_INPUTS + 2 * n_cast + 1]
    seq_refs = refs[:RWKV_INPUTS] + (refs[RWKV_INPUTS + n_cast],) + refs[RWKV_INPUTS + 2 * n_cast + 1:]
    s_ref = seq_refs[RWKV_INPUTS + 1]

    @pl.when(pl.program_id(1) == 0)
    def _():
        s_ref[...] = jnp.zeros(s_ref.shape, F32)

    for q in range(seqs):
        _rwkv_sequence(q, *seq_refs, **static)
    _cast_blocks(cast_in, cast_out)


def _rwkv_sequence(q, r_ref, k_ref, v_ref, lora_ref, w0_ref, a0_ref, kk_ref, ka_ref, rk_ref, gw_ref, gb_ref,
                   du_ref, au_ref, gu_ref,
                   o_ref,
                   s_ref, pr_ref, pk_ref, pv_ref, pa_ref, pg_ref, plp_ref, plw_ref,
                   *, n_tiles, dlp, alp):
    L = CHUNK
    tiles = slice(q * n_tiles, (q + 1) * n_tiles)

    lora = lora_ref[q]
    wd = jnp.tanh(lora[:, :dlp])
    ad = lora[:, dlp:dlp + alp]
    gd = jax.nn.sigmoid(lora[:, dlp + alp:])
    w = -_softplus(-(w0_ref[...] + _dot(wd.astype(BF16), du_ref[...]))) - 0.5
    lw = -jnp.exp(w)
    asig = jax.nn.sigmoid(a0_ref[...] + _dot(ad.astype(BF16), au_ref[...]))
    gate = _dot(gd.astype(BF16), gu_ref[...])
    row = lax.broadcasted_iota(jnp.int32, (L, L), 0)
    col = lax.broadcasted_iota(jnp.int32, (L, L), 1)
    incl = row >= col
    strict = row > col
    tril = jnp.where(incl, 1.0, 0.0).astype(BF16)
    h1 = lw.astype(BF16)
    r1 = lw - h1.astype(F32)
    h2 = r1.astype(BF16)
    h3 = (r1 - h2.astype(F32)).astype(BF16)
    logp = _dot(tril, h1) + _dot(tril, h2) + _dot(tril, h3)
    for p in range(n_tiles):
        sl = slice(p * LANES, (p + 1) * LANES)
        i = q * n_tiles + p
        pr_ref[i] = r_ref[q, :, sl]
        pk_ref[i] = k_ref[q, :, sl]
        pv_ref[i] = v_ref[q, :, sl]
        pa_ref[i] = asig[:, sl]
        pg_ref[i] = gate[:, sl]
        plp_ref[i] = logp[:, sl]
        plw_ref[i] = lw[:, sl]

    lane = lax.broadcasted_iota(jnp.int32, (1, LANES), 1)
    lo = lane < HEAD_DIM
    same_head = jnp.where(row < HEAD_DIM, 0, 1) == jnp.where(col < HEAD_DIM, 0, 1)
    inv_hd = 1.0 / HEAD_DIM

    def first(x):
        return jnp.where(lo, x, 0.0)

    def second(x):
        return jnp.where(lo, 0.0, x)

    def segsum(x):
        return jnp.where(lo, jnp.sum(first(x), axis=-1, keepdims=True),
                         jnp.sum(second(x), axis=-1, keepdims=True))

    r = pr_ref[tiles]
    k = pk_ref[tiles]
    v = pv_ref[tiles]
    a_s = pa_ref[tiles]
    lp = plp_ref[tiles]
    kk = k * kk_ref[...]
    ss = segsum(kk * kk)
    kk = kk / jnp.maximum(jnp.sqrt(ss), 1e-12)
    a = -kk
    b = kk * a_s
    km = k * (1.0 + (a_s - 1.0) * ka_ref[...])
    cmid = lp[:, L // 2 - 1:L // 2, :]
    clast = lp[:, L - 1:L, :]
    lpe = lp - plw_ref[tiles]
    e_inv = jnp.exp(cmid - lp)
    e_mid = jnp.exp(cmid)
    at = a * jnp.exp(lpe - cmid)
    a_abs = at * e_mid
    rt = r * jnp.exp(lp - cmid)
    r_abs = rt * e_mid
    bt = b * e_inv
    kt = km * e_inv
    e_l = jnp.exp(clast - cmid)
    e_p = jnp.exp(clast)
    lhs = jnp.concatenate([first(at), second(at), first(rt), second(rt)], axis=1).astype(BF16)
    gmat = _bdot_nt(lhs, jnp.concatenate([bt, kt], axis=1).astype(BF16))
    n_lo = jnp.where(strict, gmat[:, 0:L, 0:L], 0.0)
    ak_lo = jnp.where(strict, gmat[:, 0:L, L:], 0.0)
    n_hi = jnp.where(strict, gmat[:, L:2 * L, 0:L], 0.0)
    ak_hi = jnp.where(strict, gmat[:, L:2 * L, L:], 0.0)
    rb_lo = jnp.where(incl, gmat[:, 2 * L:3 * L, 0:L], 0.0)
    rk_lo = jnp.where(incl, gmat[:, 2 * L:3 * L, L:], 0.0)
    rb_hi = jnp.where(incl, gmat[:, 3 * L:, 0:L], 0.0)
    rk_hi = jnp.where(incl, gmat[:, 3 * L:, L:], 0.0)
    t_all = _inv_unit_lower(jnp.concatenate([n_lo, n_hi], axis=0))
    t_lo, t_hi = t_all[:n_tiles], t_all[n_tiles:]
    s = s_ref[tiles]
    s_b = s.astype(BF16)
    v_lohi = jnp.concatenate([first(v), second(v)], axis=1).astype(BF16)
    rhs = _bdot_nt(a_abs.astype(BF16), s_b) + _bdot(jnp.concatenate([ak_lo, ak_hi], axis=2).astype(BF16), v_lohi)
    u = _bdot(jnp.concatenate([t_lo, t_hi], axis=2).astype(BF16),
              jnp.concatenate([first(rhs), second(rhs)], axis=1).astype(BF16))
    u_lohi = jnp.concatenate([first(u), second(u)], axis=1).astype(BF16)
    y = _bdot_nt(r_abs.astype(BF16), s_b) + _bdot(
        jnp.concatenate([rb_lo, rb_hi, rk_lo, rk_hi], axis=2).astype(BF16),
        jnp.concatenate([u_lohi, v_lohi], axis=1))
    upd = _bdot(jnp.concatenate([jnp.swapaxes(u, 1, 2), jnp.swapaxes(v, 1, 2)], axis=2).astype(BF16),
                jnp.concatenate([bt * e_l, kt * e_l], axis=1).astype(BF16))
    s_ref[tiles] = s * e_p + jnp.where(same_head, upd, 0.0)
    mean = segsum(y) * inv_hd
    dev = y - mean
    var = segsum(dev * dev) * inv_hd
    yn = dev * lax.rsqrt(var + GN_EPS) * gw_ref[...] + gb_ref[...]
    bonus = segsum(r * km * rk_ref[...]) * v
    out = ((yn + bonus) * pg_ref[tiles]).astype(BF16)
    for p in range(n_tiles):
        o_ref[q, :, p * LANES:(p + 1) * LANES] = out[p]


def _rwkv(z, batch, seq, c, lora_off, lora_w, dlp, alp, vecs, loras, casts):
    n = z.shape[0]
    nc = seq // CHUNK
    n_tiles = c // LANES
    seqs = 1
    assert lora_off % lora_w == 0 and len(vecs) + len(loras) + 4 == RWKV_INPUTS
    lora_blk = lora_off // lora_w
    z3 = z.reshape(batch, seq, z.shape[1])
    cast_in, cast_out, cast_shapes = _cast_specs(casts, (batch // seqs) * nc, lambda b, t: b * nc + t)
    vec_spec = pl.BlockSpec((n_tiles, 1, LANES), lambda b, t: (0, 0, 0))
    in_specs = [
        pl.BlockSpec((seqs, CHUNK, c), lambda b, t: (b, t, 0)),
        pl.BlockSpec((seqs, CHUNK, c), lambda b, t: (b, t, 1)),
        pl.BlockSpec((seqs, CHUNK, c), lambda b, t: (b, t, 2)),
        pl.BlockSpec((seqs, CHUNK, lora_w), lambda b, t: (b, t, lora_blk)),
        pl.BlockSpec((1, c), lambda b, t: (0, 0)),
        pl.BlockSpec((1, c), lambda b, t: (0, 0)),
    ] + [vec_spec] * 5 + [pl.BlockSpec(w.shape, lambda b, t: (0, 0)) for w in loras]
    tile_f32 = pltpu.VMEM((seqs * n_tiles, CHUNK, LANES), F32)
    y, *cast = pl.pallas_call(
        functools.partial(_rwkv_kernel, seqs=seqs, n_cast=len(casts), n_tiles=n_tiles, dlp=dlp, alp=alp),
        grid=(batch // seqs, nc),
        in_specs=in_specs + cast_in,
        out_specs=[pl.BlockSpec((seqs, CHUNK, c), lambda b, t: (b, t, 0))] + cast_out,
        out_shape=[jax.ShapeDtypeStruct((batch, seq, c), BF16)] + cast_shapes,
        scratch_shapes=[pltpu.VMEM((seqs * n_tiles, LANES, LANES), F32)] + [tile_f32] * 7,
        compiler_params=_cparams(2),
        name="rwkv",
    )(z3, z3, z3, z3, *vecs, *loras, *casts)
    return y.reshape(n, c), cast


def _swa_kernel(sink_ref, q_ref, kvc_ref, kvp_ref, cosc_ref, sinc_ref, cosp_ref, sinp_ref, *rest,
                n_tiles, group, qb):
    n_cast = len(rest) // 2
    o_ref = rest[n_cast]
    _cast_blocks(rest[:n_cast], rest[n_cast + 1:])
    nblk = pl.program_id(1)
    lane = lax.broadcasted_iota(jnp.int32, (1, LANES), 1)
    lo = lane < HEAD_DIM
    rot_lo = jnp.bitwise_and(lane, HEAD_DIM - 1) < HEAD_DIM // 2

    def rope(x, cos, sin_signed):
        partner = jnp.where(rot_lo, pltpu.roll(x, LANES - HEAD_DIM // 2, 1), pltpu.roll(x, HEAD_DIM // 2, 1))
        return x * cos + partner * sin_signed

    cosc = cosc_ref[...]
    sinc = sinc_ref[...]
    kvc = kvc_ref[...]
    kvp = kvp_ref[...]
    keys = jnp.concatenate([rope(kvp[:, :LANES], cosp_ref[...], sinp_ref[...]),
                            rope(kvc[:, :LANES], cosc, sinc)], axis=0)
    vals = jnp.concatenate([kvp[:, LANES:], kvc[:, LANES:]], axis=0)
    keys_sw = pltpu.roll(keys, HEAD_DIM, 1)
    vals_sw = pltpu.roll(vals, HEAD_DIM, 1)
    k_first = [jnp.where(lo, keys, 0.0).astype(BF16), jnp.where(lo, keys_sw, 0.0).astype(BF16)]
    k_second = [jnp.where(lo, 0.0, keys_sw).astype(BF16), jnp.where(lo, 0.0, keys).astype(BF16)]
    ones_kv = jnp.ones(vals.shape, BF16)
    with_ones = lambda x: jnp.concatenate([x.astype(BF16), ones_kv], axis=1)
    v_first = [with_ones(jnp.where(lo, vals, 0.0)), with_ones(jnp.where(lo, vals_sw, 0.0))]
    v_second = [with_ones(jnp.where(lo, 0.0, vals_sw)), with_ones(jnp.where(lo, 0.0, vals))]
    qi = lax.broadcasted_iota(jnp.int32, (BLOCK, 2 * BLOCK), 0)
    ki = lax.broadcasted_iota(jnp.int32, (BLOCK, 2 * BLOCK), 1)
    window = (ki > qi) & (ki <= qi + BLOCK)
    valid = [window & ((nblk > 0) | (ki >= BLOCK))] + [window] * (qb - 1)
    scale = HEAD_DIM ** -0.5
    heads = [(j, p, half) for j in range(qb) for p in range(n_tiles) for half in range(HEADS_PER_TILE)]
    scores = []
    for j in range(qb):
        qrows = slice(j * BLOCK, (j + 1) * BLOCK)
        krows = slice(j * BLOCK, (j + 2) * BLOCK)
        for p in range(n_tiles):
            g = (p * HEADS_PER_TILE) // group
            qp = (rope(q_ref[qrows, p * LANES:(p + 1) * LANES], cosc[qrows], sinc[qrows]) * scale).astype(BF16)
            scores += [_dot_nt(qp, k_first[g][krows]), _dot_nt(qp, k_second[g][krows])]
    probs, sink_terms = [], []
    for (j, p, half), s in zip(heads, scores):
        s = jnp.where(valid[j], s, NEG_INF)
        sink = sink_ref[p * HEADS_PER_TILE + half]
        m = jnp.maximum(jnp.max(s, axis=-1, keepdims=True), sink)
        probs.append(jnp.exp(s - m).astype(BF16))
        sink_terms.append(jnp.exp(sink - m))
    outs = []
    for (j, p, half), e, st in zip(heads, probs, sink_terms):
        g = (p * HEADS_PER_TILE) // group
        krows = slice(j * BLOCK, (j + 2) * BLOCK)
        pv = _dot(e, (v_second[g] if half else v_first[g])[krows])
        outs.append(pv[:, :LANES] * (1.0 / (pv[:, LANES:] + st)))
    for j in range(qb):
        for p in range(n_tiles):
            i = (j * n_tiles + p) * HEADS_PER_TILE
            o_ref[j * BLOCK:(j + 1) * BLOCK, p * LANES:(p + 1) * LANES] = (outs[i] + outs[i + 1]).astype(BF16)


def _swa(z, sinks, cos_t, sin_t, batch, seq, q_off, qw, kv_off, kvw, group, casts):
    n = z.shape[0]
    nb = seq // BLOCK
    assert q_off % qw == 0 and kv_off % kvw == 0 and kvw == 2 * LANES
    q_blk, kv_blk = q_off // qw, kv_off // kvw
    qb = _pick(nb, (4, 2, 1))
    steps = nb // qb
    prev = lambda t: jnp.maximum(t * qb - 1, 0)
    cast_in, cast_out, cast_shapes = _cast_specs(casts, batch * steps, lambda b, t: b * steps + t)
    y, *cast = pl.pallas_call(
        functools.partial(_swa_kernel, n_tiles=qw // LANES, group=group, qb=qb),
        grid=(batch, steps),
        in_specs=[
            pl.BlockSpec(memory_space=pltpu.SMEM),
            pl.BlockSpec((qb * BLOCK, qw), lambda b, t: (b * steps + t, q_blk)),
            pl.BlockSpec((qb * BLOCK, kvw), lambda b, t: (b * steps + t, kv_blk)),
            pl.BlockSpec((BLOCK, kvw), lambda b, t: (b * nb + prev(t), kv_blk)),
            pl.BlockSpec((qb * BLOCK, LANES), lambda b, t: (t, 0)),
            pl.BlockSpec((qb * BLOCK, LANES), lambda b, t: (t, 0)),
            pl.BlockSpec((BLOCK, LANES), lambda b, t: (prev(t), 0)),
            pl.BlockSpec((BLOCK, LANES), lambda b, t: (prev(t), 0)),
        ] + cast_in,
        out_specs=[pl.BlockSpec((qb * BLOCK, qw), lambda b, t: (b * steps + t, 0))] + cast_out,
        out_shape=[jax.ShapeDtypeStruct((n, qw), BF16)] + cast_shapes,
        compiler_params=_cparams(2),
        name="swa",
    )(sinks, z, z, z, cos_t, sin_t, cos_t, sin_t, *casts)
    return y, cast


def _memkv_kernel(m_ref, g_ref, w_ref, o_ref):
    h = _rms(m_ref[...], g_ref[...]).astype(BF16)
    o_ref[...] = _dot(h, w_ref[...].astype(BF16)).astype(BF16)


def _memkv(mem, g, w):
    n, d = mem.shape
    ncols = w.shape[1]
    tm = _pick(n, (512, 256, 128))
    tn = _pick(ncols, (1024, 512, 256, 128))
    return pl.pallas_call(
        _memkv_kernel,
        grid=(ncols // tn, n // tm),
        in_specs=[
            pl.BlockSpec((tm, d), lambda j, i: (i, 0)),
            pl.BlockSpec((1, d), lambda j, i: (0, 0)),
            pl.BlockSpec((d, tn), lambda j, i: (0, j)),
        ],
        out_specs=pl.BlockSpec((tm, tn), lambda j, i: (i, j)),
        out_shape=jax.ShapeDtypeStruct((n, ncols), BF16),
        compiler_params=_cparams(2),
        name="memkv",
    )(mem, g.reshape(1, d), w)


def _mix_kernel(x_ref, yr_ref, ys_ref, wo_ref, bo_ref, g_ref, wq_ref, k_ref, v_ref, wxo_ref, o_ref):
    c = yr_ref.shape[1]
    d = x_ref.shape[1]
    hd = d // XATTN_HEADS
    x2 = x_ref[...] + _dot(yr_ref[...], wo_ref[:c, :]) + _dot(ys_ref[...], wo_ref[c:, :]) + bo_ref[...]
    q = _dot(_rms(x2, g_ref[...]).astype(BF16), wq_ref[...]).astype(BF16)
    scale = hd ** -0.5
    heads = [slice(h * hd, (h + 1) * hd) for h in range(XATTN_HEADS)]
    scores = [_dot_nt(q[:, sl], k_ref[:, sl]) * scale for sl in heads]
    probs, dens = [], []
    for s in scores:
        e = jnp.exp(s - jnp.max(s, axis=-1, keepdims=True))
        probs.append(e.astype(BF16))
        dens.append(jnp.sum(e, axis=-1, keepdims=True))
    outs = [(_dot(e, v_ref[:, sl]) * (1.0 / den)).astype(BF16) for e, den, sl in zip(probs, dens, heads)]
    o_ref[...] = x2 + _dot(jnp.concatenate(outs, axis=1), wxo_ref[...])


def _mix(x, yr, ys, wo, bo, g, wq, kv, wxo, seq, mlen):
    n, d = x.shape
    c = yr.shape[1]
    tm = _pick(seq, (256, 128))
    per_seq = seq // tm
    once = pl.Buffered(1)
    return pl.pallas_call(
        _mix_kernel,
        grid=(n // tm,),
        in_specs=[
            pl.BlockSpec((tm, d), lambda i: (i, 0)),
            pl.BlockSpec((tm, c), lambda i: (i, 0)),
            pl.BlockSpec((tm, d - c), lambda i: (i, 0)),
            pl.BlockSpec((d, d), lambda i: (0, 0), pipeline_mode=once),
            pl.BlockSpec((1, d), lambda i: (0, 0)),
            pl.BlockSpec((1, d), lambda i: (0, 0)),
            pl.BlockSpec((d, d), lambda i: (0, 0), pipeline_mode=once),
            pl.BlockSpec((mlen, d), lambda i: (i // per_seq, 0)),
            pl.BlockSpec((mlen, d), lambda i: (i // per_seq, 1)),
            pl.BlockSpec((d, d), lambda i: (0, 0), pipeline_mode=once),
        ],
        out_specs=pl.BlockSpec((tm, d), lambda i: (i, 0)),
        out_shape=jax.ShapeDtypeStruct((n, d), F32),
        compiler_params=_cparams(1),
        name="mix",
    )(x, yr, ys, wo, bo.reshape(1, d), g.reshape(1, d), wq, kv, kv, wxo)


def _pad_cols(w, width):
    return jnp.pad(w, ((0, 0), (0, width - w.shape[1])))


def _pad_rows(w, height):
    return jnp.pad(w, ((0, height - w.shape[0]), (0, 0)))


def _rope_tables(seq):
    half = HEAD_DIM // 2
    lane = jnp.arange(LANES)
    inv_freq = ROPE_THETA ** (-jnp.arange(0, HEAD_DIM, 2, dtype=F32) / HEAD_DIM)
    ang = jnp.arange(seq, dtype=F32)[:, None] * inv_freq[lane % half][None, :]
    sign = jnp.where((lane % HEAD_DIM) < half, -1.0, 1.0)
    return jnp.cos(ang), jnp.sin(ang) * sign[None, :]


def kernel(x, mem, f1_norm, f1_gate, f1_up, f1_down, mix_norm, w_in, b_in_attn, rw_mu, rw_w0, rw_decay_up, rw_a0, rw_aaa_up, rw_gate_up, rw_k_k, rw_k_a, rw_r_k, rw_lnx_w, rw_lnx_b, attn_sinks, w_out, b_out, xa_norm, mem_norm, w_xq, w_xkv, w_xo, f2_norm, f2_gate, f2_up, f2_down, final_norm):
    batch, seq, d = x.shape
    mlen = mem.shape[1]
    depth = f1_norm.shape[0]
    c = rw_w0.shape[1]
    sw = d - c
    dl, al, gl = rw_decay_up.shape[1], rw_aaa_up.shape[1], rw_gate_up.shape[1]
    dlp, alp, glp = (_round_up(v, LANES) for v in (dl, al, gl))
    kvw = b_in_attn.shape[1] - sw
    q_heads = sw // HEAD_DIM
    group = q_heads // (kvw // (2 * HEAD_DIM))
    n_tiles = c // LANES
    n = batch * seq
    q_off = 3 * c
    lora_off = q_off + sw
    lora_w = dlp + alp + glp
    kv_off = lora_off + lora_w
    ncols = _round_up(kv_off + kvw, 2 * LANES)
    cos_t, sin_t = _rope_tables(seq)

    xf = x.reshape(n, d)
    memf = mem.reshape(batch * mlen, d)
    for l in range(depth):
        xf = _ffn(xf, f1_norm[l], f1_gate[l].astype(BF16), f1_up[l].astype(BF16), f1_down[l].astype(BF16), None)

        wl = w_in[l]
        o1, o2, o3 = 3 * c, 3 * c + dl, 3 * c + dl + al
        shift = o3 + gl
        w_all = jnp.concatenate([
            wl[:, :o1], wl[:, shift:shift + sw],
            _pad_cols(wl[:, o1:o2], dlp), _pad_cols(wl[:, o2:o3], alp), _pad_cols(wl[:, o3:shift], glp),
            wl[:, shift + sw:]], axis=1)
        w_all = _pad_cols(w_all, ncols).astype(BF16)
        mu = rw_mu[l][None, :]
        mu_all = _pad_cols(jnp.concatenate([
            mu[:, :o1], jnp.zeros((1, sw), F32),
            _pad_cols(mu[:, o1:o2], dlp), _pad_cols(mu[:, o2:o3], alp), _pad_cols(mu[:, o3:shift], glp)],
            axis=1), ncols)
        bia = b_in_attn[l][None, :]
        b_all = _pad_cols(jnp.concatenate([
            jnp.zeros((1, q_off), F32), bia[:, :sw], jnp.zeros((1, lora_w), F32), bia[:, sw:]], axis=1), ncols)
        z = _proj(xf, mix_norm[l], w_all, mu_all, b_all, seq)

        tiles = lambda v: v.reshape(n_tiles, 1, LANES)
        vecs = [rw_w0[l].reshape(1, c), rw_a0[l].reshape(1, c), tiles(rw_k_k[l]), tiles(rw_k_a[l]),
                tiles(rw_r_k[l]), tiles(rw_lnx_w[l]), tiles(rw_lnx_b[l])]
        loras = [_pad_rows(rw_decay_up[l], dlp).astype(BF16), _pad_rows(rw_aaa_up[l], alp).astype(BF16),
                 _pad_rows(rw_gate_up[l], glp).astype(BF16)]
        y_rwkv, (f2_down_b, w_out_b, w_xq_b, w_xo_b) = _rwkv(
            z, batch, seq, c, lora_off, lora_w, dlp, alp, vecs, loras,
            [f2_down[l], w_out[l], w_xq[l], w_xo[l]])
        y_swa, (f2_gate_b, f2_up_b) = _swa(z, attn_sinks[l], cos_t, sin_t, batch, seq, q_off, sw, kv_off, kvw,
                                           group, [f2_gate[l], f2_up[l]])

        kv_mem = _memkv(memf, mem_norm[l], w_xkv[l])
        xf = _mix(xf, y_rwkv, y_swa, w_out_b, b_out[l], xa_norm[l], w_xq_b, kv_mem, w_xo_b, seq, mlen)

        last = l == depth - 1
        xf = _ffn(xf, f2_norm[l], f2_gate_b, f2_up_b, f2_down_b, final_norm if last else None)
    return xf.reshape(batch, seq, d)
```

```python
import functools

import jax
import jax.numpy as jnp
from jax import lax
from jax.experimental import pallas as pl
from jax.experimental.pallas import tpu as pltpu

F32, BF16 = jnp.float32, jnp.bfloat16

LANES = 128
SUBLANES = 8
VMEM_LIMIT_BYTES = 56 * 1024 * 1024

HEAD_DIM = 64
HEADS_PER_TILE = LANES // HEAD_DIM
CHUNK = 128
BLOCK = 128
XATTN_HEADS = 4
RMS_EPS = 1e-6
GN_EPS = 64e-5
NEG_INF = -1e30
ROPE_THETA = 10000.0


def _round_up(n, m):
    return (n + m - 1) // m * m


def _pick(n, prefs):
    for p in prefs:
        if n % p == 0:
            return p
    raise ValueError(f"no tile in {prefs} divides {n}")


def _cparams(n_axes):
    return pltpu.CompilerParams(dimension_semantics=("arbitrary",) * n_axes,
                                vmem_limit_bytes=VMEM_LIMIT_BYTES)


def _dot(a, b):
    return jnp.dot(a, b, preferred_element_type=F32)


def _dot_nt(a, b):
    return lax.dot_general(a, b, (((1,), (1,)), ((), ())), preferred_element_type=F32)


def _cast_specs(weights, nsteps, flat_step):
    in_specs, out_specs, out_shapes = [], [], []
    for w in weights:
        units = w.shape[0] // (2 * SUBLANES)
        nblk = max(d for d in range(1, nsteps + 1) if units % d == 0)
        imap = lambda *g, nblk=nblk: (jnp.minimum(flat_step(*g), nblk - 1), 0)
        spec = pl.BlockSpec((w.shape[0] // nblk, w.shape[1]), imap)
        in_specs.append(spec)
        out_specs.append(spec)
        out_shapes.append(jax.ShapeDtypeStruct(w.shape, BF16))
    return in_specs, out_specs, out_shapes


def _cast_blocks(src_refs, dst_refs):
    for src, dst in zip(src_refs, dst_refs):
        dst[...] = src[...].astype(BF16)


def _rms(x, g):
    ms = jnp.mean(x * x, axis=-1, keepdims=True)
    return x * lax.rsqrt(ms + RMS_EPS) * g


def _softplus(x):
    return jnp.maximum(x, 0.0) + jnp.log(1.0 + jnp.exp(-jnp.abs(x)))


def _ffn_kernel(x_ref, g_ref, wg_ref, wu_ref, wd_ref, fg_ref, o_ref, h_ref, *, final_norm, rows):
    j = pl.program_id(1)
    tm = x_ref.shape[0]

    @pl.when(j == 0)
    def _():
        for r0 in range(0, tm, rows):
            x = x_ref[r0:r0 + rows, :]
            h_ref[r0:r0 + rows, :] = _rms(x, g_ref[...]).astype(BF16)
            o_ref[r0:r0 + rows, :] = x

    for r0 in range(0, tm, rows):
        h = h_ref[r0:r0 + rows, :]
        gate = _dot(h, wg_ref[...])
        up = _dot(h, wu_ref[...])
        act = (gate * jax.nn.sigmoid(gate) * up).astype(BF16)
        o_ref[r0:r0 + rows, :] += 0.5 * _dot(act, wd_ref[...])

    if final_norm:
        @pl.when(j == pl.num_programs(1) - 1)
        def _():
            for r0 in range(0, tm, rows):
                o_ref[r0:r0 + rows, :] = _rms(o_ref[r0:r0 + rows, :], fg_ref[...])


def _ffn(x, g, wg, wu, wd, fg):
    n, d = x.shape
    f = wg.shape[1]
    tm = _pick(n, (1024, 512, 256, 128))
    tf = _pick(f, (512, 256, 128))
    rows = min(tm, 512)
    final_norm = fg is not None
    fg = g if fg is None else fg
    return pl.pallas_call(
        functools.partial(_ffn_kernel, final_norm=final_norm, rows=rows),
        grid=(n // tm, f // tf),
        in_specs=[
            pl.BlockSpec((tm, d), lambda i, j: (i, 0)),
            pl.BlockSpec((1, d), lambda i, j: (0, 0)),
            pl.BlockSpec((d, tf), lambda i, j: (0, j)),
            pl.BlockSpec((d, tf), lambda i, j: (0, j)),
            pl.BlockSpec((tf, d), lambda i, j: (j, 0)),
            pl.BlockSpec((1, d), lambda i, j: (0, 0)),
        ],
        out_specs=pl.BlockSpec((tm, d), lambda i, j: (i, 0)),
        out_shape=jax.ShapeDtypeStruct((n, d), F32),
        scratch_shapes=[pltpu.VMEM((tm, d), BF16)],
        compiler_params=_cparams(2),
        name="ffn",
    )(x, g.reshape(1, d), wg, wu, wd, fg.reshape(1, d))


def _proj_kernel(x_ref, g_ref, w_ref, mu_ref, b_ref, o_ref, carry_ref, *, tiles_per_seq, tn):
    i = pl.program_id(0)
    tm = x_ref.shape[0]
    ncols = w_ref.shape[1]

    @pl.when(i % tiles_per_seq == 0)
    def _():
        carry_ref[...] = jnp.zeros(carry_ref.shape, F32)

    h = _rms(x_ref[...], g_ref[...]).astype(BF16)
    row = lax.broadcasted_iota(jnp.int32, (SUBLANES, tn), 0)
    for c0 in range(0, ncols, tn):
        cs = slice(c0, c0 + tn)
        z = _dot(h, w_ref[:, cs])
        prev_tail = carry_ref[:, cs]
        carry_ref[:, cs] = z[tm - SUBLANES:, :]
        zs = pltpu.roll(z, 1, 0)
        head = jnp.where(row == 0, pltpu.roll(prev_tail, 1, 0), zs[:SUBLANES])
        mu = mu_ref[:, cs]
        b = b_ref[:, cs]
        z0 = z[:SUBLANES]
        o_ref[:SUBLANES, cs] = z0 + (head - z0) * mu + b
        z1 = z[SUBLANES:]
        o_ref[SUBLANES:, cs] = z1 + (zs[SUBLANES:] - z1) * mu + b


def _proj(x, g, w_all, mu_all, b_all, seq):
    n, d = x.shape
    ncols = w_all.shape[1]
    tm = _pick(seq, (512, 256, 128))
    tn = _pick(ncols, (1024, 512, 256, 128))
    return pl.pallas_call(
        functools.partial(_proj_kernel, tiles_per_seq=seq // tm, tn=tn),
        grid=(n // tm,),
        in_specs=[
            pl.BlockSpec((tm, d), lambda i: (i, 0)),
            pl.BlockSpec((1, d), lambda i: (0, 0)),
            pl.BlockSpec((d, ncols), lambda i: (0, 0), pipeline_mode=pl.Buffered(1)),
            pl.BlockSpec((1, ncols), lambda i: (0, 0)),
            pl.BlockSpec((1, ncols), lambda i: (0, 0)),
        ],
        out_specs=pl.BlockSpec((tm, ncols), lambda i: (i, 0)),
        out_shape=jax.ShapeDtypeStruct((n, ncols), F32),
        scratch_shapes=[pltpu.VMEM((SUBLANES, ncols), F32)],
        compiler_params=_cparams(1),
        name="proj",
    )(x, g.reshape(1, d), w_all, mu_all, b_all)


def _bdot(a, b):
    return lax.dot_general(a, b, (((2,), (1,)), ((0,), (0,))), preferred_element_type=F32)


def _bdot_nt(a, b):
    return lax.dot_general(a, b, (((2,), (2,)), ((0,), (0,))), preferred_element_type=F32)


def _inv_unit_lower(nmat):
    L = nmat.shape[-1]
    r = lax.broadcasted_iota(jnp.int32, (L, L), 0)
    c = lax.broadcasted_iota(jnp.int32, (L, L), 1)
    eye = jnp.where(r == c, 1.0, 0.0)
    t = eye + nmat
    pw = nmat.astype(BF16)
    pw = _bdot(pw, pw).astype(BF16)
    steps = L.bit_length() - 2
    for i in range(steps):
        if i + 1 < steps:
            both = _bdot(jnp.concatenate([t.astype(BF16), pw], axis=1), pw)
            t = t + both[:, :L]
            pw = both[:, L:].astype(BF16)
        else:
            t = t + _bdot(t.astype(BF16), pw)
    return t


RWKV_INPUTS = 14


def _rwkv_kernel(*refs, seqs, n_cast, **static):
    cast_in = refs[RWKV_INPUTS:RWKV_INPUTS + n_cast]
    cast_out = refs[RWKV_INPUTS + n_cast + 1:RWKV_INPUTS + 2 * n_cast + 1]
    seq_refs = refs[:RWKV_INPUTS] + (refs[RWKV_INPUTS + n_cast],) + refs[RWKV_INPUTS + 2 * n_cast + 1:]
    s_ref = seq_refs[RWKV_INPUTS + 1]

    @pl.when(pl.program_id(1) == 0)
    def _():
        s_ref[...] = jnp.zeros(s_ref.shape, F32)

    for q in range(seqs):
        _rwkv_sequence(q, *seq_refs, **static)
    _cast_blocks(cast_in, cast_out)


def _rwkv_sequence(q, r_ref, k_ref, v_ref, lora_ref, w0_ref, a0_ref, kk_ref, ka_ref, rk_ref, gw_ref, gb_ref,
                   du_ref, au_ref, gu_ref,
                   o_ref,
                   s_ref, pr_ref, pk_ref, pv_ref, pa_ref, pg_ref, plp_ref, plw_ref,
                   *, n_tiles, dlp, alp):
    L = CHUNK
    tiles = slice(q * n_tiles, (q + 1) * n_tiles)

    lora = lora_ref[q]
    wd = jnp.tanh(lora[:, :dlp])
    ad = lora[:, dlp:dlp + alp]
    gd = jax.nn.sigmoid(lora[:, dlp + alp:])
    w = -_softplus(-(w0_ref[...] + _dot(wd.astype(BF16), du_ref[...]))) - 0.5
    lw = -jnp.exp(w)
    asig = jax.nn.sigmoid(a0_ref[...] + _dot(ad.astype(BF16), au_ref[...]))
    gate = _dot(gd.astype(BF16), gu_ref[...])
    row = lax.broadcasted_iota(jnp.int32, (L, L), 0)
    col = lax.broadcasted_iota(jnp.int32, (L, L), 1)
    incl = row >= col
    strict = row > col
    tril = jnp.where(incl, 1.0, 0.0).astype(BF16)
    h1 = lw.astype(BF16)
    r1 = lw - h1.astype(F32)
    h2 = r1.astype(BF16)
    h3 = (r1 - h2.astype(F32)).astype(BF16)
    logp = _dot(tril, h1) + _dot(tril, h2) + _dot(tril, h3)
    for p in range(n_tiles):
        sl = slice(p * LANES, (p + 1) * LANES)
        i = q * n_tiles + p
        pr_ref[i] = r_ref[q, :, sl]
        pk_ref[i] = k_ref[q, :, sl]
        pv_ref[i] = v_ref[q, :, sl]
        pa_ref[i] = asig[:, sl]
        pg_ref[i] = gate[:, sl]
        plp_ref[i] = logp[:, sl]
        plw_ref[i] = lw[:, sl]

    lane = lax.broadcasted_iota(jnp.int32, (1, LANES), 1)
    lo = lane < HEAD_DIM
    same_head = jnp.where(row < HEAD_DIM, 0, 1) == jnp.where(col < HEAD_DIM, 0, 1)
    inv_hd = 1.0 / HEAD_DIM

    def first(x):
        return jnp.where(lo, x, 0.0)

    def second(x):
        return jnp.where(lo, 0.0, x)

    def segsum(x):
        return jnp.where(lo, jnp.sum(first(x), axis=-1, keepdims=True),
                         jnp.sum(second(x), axis=-1, keepdims=True))

    r = pr_ref[tiles]
    k = pk_ref[tiles]
    v = pv_ref[tiles]
    a_s = pa_ref[tiles]
    lp = plp_ref[tiles]
    kk = k * kk_ref[...]
    ss = segsum(kk * kk)
    kk = kk / jnp.maximum(jnp.sqrt(ss), 1e-12)
    a = -kk
    b = kk * a_s
    km = k * (1.0 + (a_s - 1.0) * ka_ref[...])
    cmid = lp[:, L // 2 - 1:L // 2, :]
    clast = lp[:, L - 1:L, :]
    lpe = lp - plw_ref[tiles]
    e_inv = jnp.exp(cmid - lp)
    e_mid = jnp.exp(cmid)
    at = a * jnp.exp(lpe - cmid)
    a_abs = at * e_mid
    rt = r * jnp.exp(lp - cmid)
    r_abs = rt * e_mid
    bt = b * e_inv
    kt = km * e_inv
    e_l = jnp.exp(clast - cmid)
    e_p = jnp.exp(clast)
    lhs = jnp.concatenate([first(at), second(at), first(rt), second(rt)], axis=1).astype(BF16)
    gmat = _bdot_nt(lhs, jnp.concatenate([bt, kt], axis=1).astype(BF16))
    n_lo = jnp.where(strict, gmat[:, 0:L, 0:L], 0.0)
    ak_lo = jnp.where(strict, gmat[:, 0:L, L:], 0.0)
    n_hi = jnp.where(strict, gmat[:, L:2 * L, 0:L], 0.0)
    ak_hi = jnp.where(strict, gmat[:, L:2 * L, L:], 0.0)
    rb_lo = jnp.where(incl, gmat[:, 2 * L:3 * L, 0:L], 0.0)
    rk_lo = jnp.where(incl, gmat[:, 2 * L:3 * L, L:], 0.0)
    rb_hi = jnp.where(incl, gmat[:, 3 * L:, 0:L], 0.0)
    rk_hi = jnp.where(incl, gmat[:, 3 * L:, L:], 0.0)
    t_all = _inv_unit_lower(jnp.concatenate([n_lo, n_hi], axis=0))
    t_lo, t_hi = t_all[:n_tiles], t_all[n_tiles:]
    s = s_ref[tiles]
    s_b = s.astype(BF16)
    v_lohi = jnp.concatenate([first(v), second(v)], axis=1).astype(BF16)
    rhs = _bdot_nt(a_abs.astype(BF16), s_b) + _bdot(jnp.concatenate([ak_lo, ak_hi], axis=2).astype(BF16), v_lohi)
    u = _bdot(jnp.concatenate([t_lo, t_hi], axis=2).astype(BF16),
              jnp.concatenate([first(rhs), second(rhs)], axis=1).astype(BF16))
    u_lohi = jnp.concatenate([first(u), second(u)], axis=1).astype(BF16)
    y = _bdot_nt(r_abs.astype(BF16), s_b) + _bdot(
        jnp.concatenate([rb_lo, rb_hi, rk_lo, rk_hi], axis=2).astype(BF16),
        jnp.concatenate([u_lohi, v_lohi], axis=1))
    upd = _bdot(jnp.concatenate([jnp.swapaxes(u, 1, 2), jnp.swapaxes(v, 1, 2)], axis=2).astype(BF16),
                jnp.concatenate([bt * e_l, kt * e_l], axis=1).astype(BF16))
    s_ref[tiles] = s * e_p + jnp.where(same_head, upd, 0.0)
    mean = segsum(y) * inv_hd
    dev = y - mean
    var = segsum(dev * dev) * inv_hd
    yn = dev * lax.rsqrt(var + GN_EPS) * gw_ref[...] + gb_ref[...]
    bonus = segsum(r * km * rk_ref[...]) * v
    out = ((yn + bonus) * pg_ref[tiles]).astype(BF16)
    for p in range(n_tiles):
        o_ref[q, :, p * LANES:(p + 1) * LANES] = out[p]


def _rwkv(z, batch, seq, c, lora_off, lora_w, dlp, alp, vecs, loras, casts):
    n = z.shape[0]
    nc = seq // CHUNK
    n_tiles = c // LANES
    seqs = 1
    assert lora_off % lora_w == 0 and len(vecs) + len(loras) + 4 == RWKV_INPUTS
    lora_blk = lora_off // lora_w
    z3 = z.reshape(batch, seq, z.shape[1])
    cast_in, cast_out, cast_shapes = _cast_specs(casts, (batch // seqs) * nc, lambda b, t: b * nc + t)
    vec_spec = pl.BlockSpec((n_tiles, 1, LANES), lambda b, t: (0, 0, 0))
    in_specs = [
        pl.BlockSpec((seqs, CHUNK, c), lambda b, t: (b, t, 0)),
        pl.BlockSpec((seqs, CHUNK, c), lambda b, t: (b, t, 1)),
        pl.BlockSpec((seqs, CHUNK, c), lambda b, t: (b, t, 2)),
        pl.BlockSpec((seqs, CHUNK, lora_w), lambda b, t: (b, t, lora_blk)),
        pl.BlockSpec((1, c), lambda b, t: (0, 0)),
        pl.BlockSpec((1, c), lambda b, t: (0, 0)),
    ] + [vec_spec] * 5 + [pl.BlockSpec(w.shape, lambda b, t: (0, 0)) for w in loras]
    tile_f32 = pltpu.VMEM((seqs * n_tiles, CHUNK, LANES), F32)
    y, *cast = pl.pallas_call(
        functools.partial(_rwkv_kernel, seqs=seqs, n_cast=len(casts), n_tiles=n_tiles, dlp=dlp, alp=alp),
        grid=(batch // seqs, nc),
        in_specs=in_specs + cast_in,
        out_specs=[pl.BlockSpec((seqs, CHUNK, c), lambda b, t: (b, t, 0))] + cast_out,
        out_shape=[jax.ShapeDtypeStruct((batch, seq, c), BF16)] + cast_shapes,
        scratch_shapes=[pltpu.VMEM((seqs * n_tiles, LANES, LANES), F32)] + [tile_f32] * 7,
        compiler_params=_cparams(2),
        name="rwkv",
    )(z3, z3, z3, z3, *vecs, *loras, *casts)
    return y.reshape(n, c), cast


def _swa_kernel(sink_ref, q_ref, kvc_ref, kvp_ref, cosc_ref, sinc_ref, cosp_ref, sinp_ref, *rest,
                n_tiles, group, qb):
    n_cast = len(rest) // 2
    o_ref = rest[n_cast]
    _cast_blocks(rest[:n_cast], rest[n_cast + 1:])
    nblk = pl.program_id(1)
    lane = lax.broadcasted_iota(jnp.int32, (1, LANES), 1)
    lo = lane < HEAD_DIM
    rot_lo = jnp.bitwise_and(lane, HEAD_DIM - 1) < HEAD_DIM // 2

    def rope(x, cos, sin_signed):
        partner = jnp.where(rot_lo, pltpu.roll(x, LANES - HEAD_DIM // 2, 1), pltpu.roll(x, HEAD_DIM // 2, 1))
        return x * cos + partner * sin_signed

    cosc = cosc_ref[...]
    sinc = sinc_ref[...]
    kvc = kvc_ref[...]
    kvp = kvp_ref[...]
    keys = jnp.concatenate([rope(kvp[:, :LANES], cosp_ref[...], sinp_ref[...]),
                            rope(kvc[:, :LANES], cosc, sinc)], axis=0)
    vals = jnp.concatenate([kvp[:, LANES:], kvc[:, LANES:]], axis=0)
    keys_sw = pltpu.roll(keys, HEAD_DIM, 1)
    vals_sw = pltpu.roll(vals, HEAD_DIM, 1)
    k_first = [jnp.where(lo, keys, 0.0).astype(BF16), jnp.where(lo, keys_sw, 0.0).astype(BF16)]
    k_second = [jnp.where(lo, 0.0, keys_sw).astype(BF16), jnp.where(lo, 0.0, keys).astype(BF16)]
    ones_kv = jnp.ones(vals.shape, BF16)
    with_ones = lambda x: jnp.concatenate([x.astype(BF16), ones_kv], axis=1)
    v_first = [with_ones(jnp.where(lo, vals, 0.0)), with_ones(jnp.where(lo, vals_sw, 0.0))]
    v_second = [with_ones(jnp.where(lo, 0.0, vals_sw)), with_ones(jnp.where(lo, 0.0, vals))]
    qi = lax.broadcasted_iota(jnp.int32, (BLOCK, 2 * BLOCK), 0)
    ki = lax.broadcasted_iota(jnp.int32, (BLOCK, 2 * BLOCK), 1)
    window = (ki > qi) & (ki <= qi + BLOCK)
    valid = [window & ((nblk > 0) | (ki >= BLOCK))] + [window] * (qb - 1)
    scale = HEAD_DIM ** -0.5
    heads = [(j, p, half) for j in range(qb) for p in range(n_tiles) for half in range(HEADS_PER_TILE)]
    scores = []
    for j in range(qb):
        qrows = slice(j * BLOCK, (j + 1) * BLOCK)
        krows = slice(j * BLOCK, (j + 2) * BLOCK)
        for p in range(n_tiles):
            g = (p * HEADS_PER_TILE) // group
            qp = (rope(q_ref[qrows, p * LANES:(p + 1) * LANES], cosc[qrows], sinc[qrows]) * scale).astype(BF16)
            scores += [_dot_nt(qp, k_first[g][krows]), _dot_nt(qp, k_second[g][krows])]
    probs, sink_terms = [], []
    for (j, p, half), s in zip(heads, scores):
        s = jnp.where(valid[j], s, NEG_INF)
        sink = sink_ref[p * HEADS_PER_TILE + half]
        m = jnp.maximum(jnp.max(s, axis=-1, keepdims=True), sink)
        probs.append(jnp.exp(s - m).astype(BF16))
        sink_terms.append(jnp.exp(sink - m))
    outs = []
    for (j, p, half), e, st in zip(heads, probs, sink_terms):
        g = (p * HEADS_PER_TILE) // group
        krows = slice(j * BLOCK, (j + 2) * BLOCK)
        pv = _dot(e, (v_second[g] if half else v_first[g])[krows])
        outs.append(pv[:, :LANES] * (1.0 / (pv[:, LANES:] + st)))
    for j in range(qb):
        for p in range(n_tiles):
            i = (j * n_tiles + p) * HEADS_PER_TILE
            o_ref[j * BLOCK:(j + 1) * BLOCK, p * LANES:(p + 1) * LANES] = (outs[i] + outs[i + 1]).astype(BF16)


def _swa(z, sinks, cos_t, sin_t, batch, seq, q_off, qw, kv_off, kvw, group, casts):
    n = z.shape[0]
    nb = seq // BLOCK
    assert q_off % qw == 0 and kv_off % kvw == 0 and kvw == 2 * LANES
    q_blk, kv_blk = q_off // qw, kv_off // kvw
    qb = _pick(nb, (4, 2, 1))
    steps = nb // qb
    prev = lambda t: jnp.maximum(t * qb - 1, 0)
    cast_in, cast_out, cast_shapes = _cast_specs(casts, batch * steps, lambda b, t: b * steps + t)
    y, *cast = pl.pallas_call(
        functools.partial(_swa_kernel, n_tiles=qw // LANES, group=group, qb=qb),
        grid=(batch, steps),
        in_specs=[
            pl.BlockSpec(memory_space=pltpu.SMEM),
            pl.BlockSpec((qb * BLOCK, qw), lambda b, t: (b * steps + t, q_blk)),
            pl.BlockSpec((qb * BLOCK, kvw), lambda b, t: (b * steps + t, kv_blk)),
            pl.BlockSpec((BLOCK, kvw), lambda b, t: (b * nb + prev(t), kv_blk)),
            pl.BlockSpec((qb * BLOCK, LANES), lambda b, t: (t, 0)),
            pl.BlockSpec((qb * BLOCK, LANES), lambda b, t: (t, 0)),
            pl.BlockSpec((BLOCK, LANES), lambda b, t: (prev(t), 0)),
            pl.BlockSpec((BLOCK, LANES), lambda b, t: (prev(t), 0)),
        ] + cast_in,
        out_specs=[pl.BlockSpec((qb * BLOCK, qw), lambda b, t: (b * steps + t, 0))] + cast_out,
        out_shape=[jax.ShapeDtypeStruct((n, qw), BF16)] + cast_shapes,
        compiler_params=_cparams(2),
        name="swa",
    )(sinks, z, z, z, cos_t, sin_t, cos_t, sin_t, *casts)
    return y, cast


def _memkv_kernel(m_ref, g_ref, w_ref, o_ref):
    h = _rms(m_ref[...], g_ref[...]).astype(BF16)
    o_ref[...] = _dot(h, w_ref[...].astype(BF16)).astype(BF16)


def _memkv(mem, g, w):
    n, d = mem.shape
    ncols = w.shape[1]
    tm = _pick(n, (512, 256, 128))
    tn = _pick(ncols, (1024, 512, 256, 128))
    return pl.pallas_call(
        _memkv_kernel,
        grid=(ncols // tn, n // tm),
        in_specs=[
            pl.BlockSpec((tm, d), lambda j, i: (i, 0)),
            pl.BlockSpec((1, d), lambda j, i: (0, 0)),
            pl.BlockSpec((d, tn), lambda j, i: (0, j)),
        ],
        out_specs=pl.BlockSpec((tm, tn), lambda j, i: (i, j)),
        out_shape=jax.ShapeDtypeStruct((n, ncols), BF16),
        compiler_params=_cparams(2),
        name="memkv",
    )(mem, g.reshape(1, d), w)


def _mix_kernel(x_ref, yr_ref, ys_ref, wo_ref, bo_ref, g_ref, wq_ref, k_ref, v_ref, wxo_ref, o_ref):
    c = yr_ref.shape[1]
    d = x_ref.shape[1]
    hd = d // XATTN_HEADS
    x2 = x_ref[...] + _dot(yr_ref[...], wo_ref[:c, :]) + _dot(ys_ref[...], wo_ref[c:, :]) + bo_ref[...]
    q = _dot(_rms(x2, g_ref[...]).astype(BF16), wq_ref[...]).astype(BF16)
    scale = hd ** -0.5
    heads = [slice(h * hd, (h + 1) * hd) for h in range(XATTN_HEADS)]
    scores = [_dot_nt(q[:, sl], k_ref[:, sl]) * scale for sl in heads]
    probs, dens = [], []
    for s in scores:
        e = jnp.exp(s - jnp.max(s, axis=-1, keepdims=True))
        probs.append(e.astype(BF16))
        dens.append(jnp.sum(e, axis=-1, keepdims=True))
    outs = [(_dot(e, v_ref[:, sl]) * (1.0 / den)).astype(BF16) for e, den, sl in zip(probs, dens, heads)]
    o_ref[...] = x2 + _dot(jnp.concatenate(outs, axis=1), wxo_ref[...])


def _mix(x, yr, ys, wo, bo, g, wq, kv, wxo, seq, mlen):
    n, d = x.shape
    c = yr.shape[1]
    tm = _pick(seq, (256, 128))
    per_seq = seq // tm
    once = pl.Buffered(1)
    return pl.pallas_call(
        _mix_kernel,
        grid=(n // tm,),
        in_specs=[
            pl.BlockSpec((tm, d), lambda i: (i, 0)),
            pl.BlockSpec((tm, c), lambda i: (i, 0)),
            pl.BlockSpec((tm, d - c), lambda i: (i, 0)),
            pl.BlockSpec((d, d), lambda i: (0, 0), pipeline_mode=once),
            pl.BlockSpec((1, d), lambda i: (0, 0)),
            pl.BlockSpec((1, d), lambda i: (0, 0)),
            pl.BlockSpec((d, d), lambda i: (0, 0), pipeline_mode=once),
            pl.BlockSpec((mlen, d), lambda i: (i // per_seq, 0)),
            pl.BlockSpec((mlen, d), lambda i: (i // per_seq, 1)),
            pl.BlockSpec((d, d), lambda i: (0, 0), pipeline_mode=once),
        ],
        out_specs=pl.BlockSpec((tm, d), lambda i: (i, 0)),
        out_shape=jax.ShapeDtypeStruct((n, d), F32),
        compiler_params=_cparams(1),
        name="mix",
    )(x, yr, ys, wo, bo.reshape(1, d), g.reshape(1, d), wq, kv, kv, wxo)


def _pad_cols(w, width):
    return jnp.pad(w, ((0, 0), (0, width - w.shape[1])))


def _pad_rows(w, height):
    return jnp.pad(w, ((0, height - w.shape[0]), (0, 0)))


def _rope_tables(seq):
    half = HEAD_DIM // 2
    lane = jnp.arange(LANES)
    inv_freq = ROPE_THETA ** (-jnp.arange(0, HEAD_DIM, 2, dtype=F32) / HEAD_DIM)
    ang = jnp.arange(seq, dtype=F32)[:, None] * inv_freq[lane % half][None, :]
    sign = jnp.where((lane % HEAD_DIM) < half, -1.0, 1.0)
    return jnp.cos(ang), jnp.sin(ang) * sign[None, :]


def kernel(x, mem, f1_norm, f1_gate, f1_up, f1_down, mix_norm, w_in, b_in_attn, rw_mu, rw_w0, rw_decay_up, rw_a0, rw_aaa_up, rw_gate_up, rw_k_k, rw_k_a, rw_r_k, rw_lnx_w, rw_lnx_b, attn_sinks, w_out, b_out, xa_norm, mem_norm, w_xq, w_xkv, w_xo, f2_norm, f2_gate, f2_up, f2_down, final_norm):
    batch, seq, d = x.shape
    mlen = mem.shape[1]
    depth = f1_norm.shape[0]
    c = rw_w0.shape[1]
    sw = d - c
    dl, al, gl = rw_decay_up.shape[1], rw_aaa_up.shape[1], rw_gate_up.shape[1]
    dlp, alp, glp = (_round_up(v, LANES) for v in (dl, al, gl))
    kvw = b_in_attn.shape[1] - sw
    q_heads = sw // HEAD_DIM
    group = q_heads // (kvw // (2 * HEAD_DIM))
    n_tiles = c // LANES
    n = batch * seq
    q_off = 3 * c
    lora_off = q_off + sw
    lora_w = dlp + alp + glp
    kv_off = lora_off + lora_w
    ncols = _round_up(kv_off + kvw, 2 * LANES)
    cos_t, sin_t = _rope_tables(seq)

    xf = x.reshape(n, d)
    memf = mem.reshape(batch * mlen, d)
    for l in range(depth):
        xf = _ffn(xf, f1_norm[l], f1_gate[l].astype(BF16), f1_up[l].astype(BF16), f1_down[l].astype(BF16), None)

        wl = w_in[l]
        o1, o2, o3 = 3 * c, 3 * c + dl, 3 * c + dl + al
        shift = o3 + gl
        w_all = jnp.concatenate([
            wl[:, :o1], wl[:, shift:shift + sw],
            _pad_cols(wl[:, o1:o2], dlp), _pad_cols(wl[:, o2:o3], alp), _pad_cols(wl[:, o3:shift], glp),
            wl[:, shift + sw:]], axis=1)
        w_all = _pad_cols(w_all, ncols).astype(BF16)
        mu = rw_mu[l][None, :]
        mu_all = _pad_cols(jnp.concatenate([
            mu[:, :o1], jnp.zeros((1, sw), F32),
            _pad_cols(mu[:, o1:o2], dlp), _pad_cols(mu[:, o2:o3], alp), _pad_cols(mu[:, o3:shift], glp)],
            axis=1), ncols)
        bia = b_in_attn[l][None, :]
        b_all = _pad_cols(jnp.concatenate([
            jnp.zeros((1, q_off), F32), bia[:, :sw], jnp.zeros((1, lora_w), F32), bia[:, sw:]], axis=1), ncols)
        z = _proj(xf, mix_norm[l], w_all, mu_all, b_all, seq)

        tiles = lambda v: v.reshape(n_tiles, 1, LANES)
        vecs = [rw_w0[l].reshape(1, c), rw_a0[l].reshape(1, c), tiles(rw_k_k[l]), tiles(rw_k_a[l]),
                tiles(rw_r_k[l]), tiles(rw_lnx_w[l]), tiles(rw_lnx_b[l])]
        loras = [_pad_rows(rw_decay_up[l], dlp).astype(BF16), _pad_rows(rw_aaa_up[l], alp).astype(BF16),
                 _pad_rows(rw_gate_up[l], glp).astype(BF16)]
        y_rwkv, (f2_down_b, w_out_b, w_xq_b, w_xo_b) = _rwkv(
            z, batch, seq, c, lora_off, lora_w, dlp, alp, vecs, loras,
            [f2_down[l], w_out[l], w_xq[l], w_xo[l]])
        y_swa, (f2_gate_b, f2_up_b) = _swa(z, attn_sinks[l], cos_t, sin_t, batch, seq, q_off, sw, kv_off, kvw,
                                           group, [f2_gate[l], f2_up[l]])

        kv_mem = _memkv(memf, mem_norm[l], w_xkv[l])
        xf = _mix(xf, y_rwkv, y_swa, w_out_b, b_out[l], xa_norm[l], w_xq_b, kv_mem, w_xo_b, seq, mlen)

        last = l == depth - 1
        xf = _ffn(xf, f2_norm[l], f2_gate_b, f2_up_b, f2_down_b, final_norm if last else None)
    return xf.reshape(batch, seq, d)
```

```python
import functools

import jax
import jax.numpy as jnp
from jax import lax
from jax.experimental import pallas as pl
from jax.experimental.pallas import tpu as pltpu

F32, BF16 = jnp.float32, jnp.bfloat16

LANES = 128
SUBLANES = 8
VMEM_LIMIT_BYTES = 56 * 1024 * 1024

HEAD_DIM = 64
HEADS_PER_TILE = LANES // HEAD_DIM
CHUNK = 128
BLOCK = 128
XATTN_HEADS = 4
RMS_EPS = 1e-6
GN_EPS = 64e-5
NEG_INF = -1e30
ROPE_THETA = 10000.0


def _round_up(n, m):
    return (n + m - 1) // m * m


def _pick(n, prefs):
    for p in prefs:
        if n % p == 0:
            return p
    raise ValueError(f"no tile in {prefs} divides {n}")


def _cparams(n_axes):
    return pltpu.CompilerParams(dimension_semantics=("arbitrary",) * n_axes,
                                vmem_limit_bytes=VMEM_LIMIT_BYTES)


def _dot(a, b):
    return jnp.dot(a, b, preferred_element_type=F32)


def _dot_nt(a, b):
    return lax.dot_general(a, b, (((1,), (1,)), ((), ())), preferred_element_type=F32)


def _cast_specs(weights, nsteps, flat_step):
    in_specs, out_specs, out_shapes = [], [], []
    for w in weights:
        units = w.shape[0] // (2 * SUBLANES)
        nblk = max(d for d in range(1, nsteps + 1) if units % d == 0)
        imap = lambda *g, nblk=nblk: (jnp.minimum(flat_step(*g), nblk - 1), 0)
        spec = pl.BlockSpec((w.shape[0] // nblk, w.shape[1]), imap)
        in_specs.append(spec)
        out_specs.append(spec)
        out_shapes.append(jax.ShapeDtypeStruct(w.shape, BF16))
    return in_specs, out_specs, out_shapes


def _cast_blocks(src_refs, dst_refs):
    for src, dst in zip(src_refs, dst_refs):
        dst[...] = src[...].astype(BF16)


def _rms(x, g):
    ms = jnp.mean(x * x, axis=-1, keepdims=True)
    return x * lax.rsqrt(ms + RMS_EPS) * g


def _softplus(x):
    return jnp.maximum(x, 0.0) + jnp.log(1.0 + jnp.exp(-jnp.abs(x)))


def _ffn_kernel(x_ref, g_ref, wg_hbm, wu_hbm, wd_hbm, fg_ref, o_ref, h_ref, wg_buf, wu_buf, wd_buf, sem,
                *, final_norm, rows, tf, n_tf):
    i = pl.program_id(0)
    tm = x_ref.shape[0]

    def weight_copies(j, slot):
        col = pl.multiple_of(j * tf, tf)
        return (pltpu.make_async_copy(wg_hbm.at[:, pl.ds(col, tf)], wg_buf.at[slot], sem.at[0, slot]),
                pltpu.make_async_copy(wu_hbm.at[:, pl.ds(col, tf)], wu_buf.at[slot], sem.at[1, slot]),
                pltpu.make_async_copy(wd_hbm.at[pl.ds(col, tf), :], wd_buf.at[slot], sem.at[2, slot]))

    @pl.when(i == 0)
    def _():
        for cp in weight_copies(0, 0):
            cp.start()

    for r0 in range(0, tm, rows):
        x = x_ref[r0:r0 + rows, :]
        h_ref[r0:r0 + rows, :] = _rms(x, g_ref[...]).astype(BF16)
        o_ref[r0:r0 + rows, :] = x

    def hidden_tile(j, carry):
        slot = jnp.bitwise_and(i * n_tf + j, 1)
        for cp in weight_copies(j, slot):
            cp.wait()

        @pl.when((j + 1 < n_tf) | (i + 1 < pl.num_programs(0)))
        def _():
            for cp in weight_copies(jnp.where(j + 1 < n_tf, j + 1, 0), 1 - slot):
                cp.start()

        wg = wg_buf[slot]
        wu = wu_buf[slot]
        wd = wd_buf[slot]
        for r0 in range(0, tm, rows):
            h = h_ref[r0:r0 + rows, :]
            gate = _dot(h, wg)
            up = _dot(h, wu)
            act = (gate * jax.nn.sigmoid(gate) * up).astype(BF16)
            o_ref[r0:r0 + rows, :] += 0.5 * _dot(act, wd)
        return carry

    lax.fori_loop(0, n_tf, hidden_tile, 0)

    if final_norm:
        for r0 in range(0, tm, rows):
            o_ref[r0:r0 + rows, :] = _rms(o_ref[r0:r0 + rows, :], fg_ref[...])


def _ffn(x, g, wg, wu, wd, fg):
    n, d = x.shape
    f = wg.shape[1]
    tm = _pick(n, (1024, 512, 256, 128))
    tf = _pick(f, (512, 256, 128))
    rows = min(tm, 512)
    final_norm = fg is not None
    fg = g if fg is None else fg
    hbm = pl.BlockSpec(memory_space=pl.ANY)
    return pl.pallas_call(
        functools.partial(_ffn_kernel, final_norm=final_norm, rows=rows, tf=tf, n_tf=f // tf),
        grid=(n // tm,),
        in_specs=[
            pl.BlockSpec((tm, d), lambda i: (i, 0)),
            pl.BlockSpec((1, d), lambda i: (0, 0)),
            hbm, hbm, hbm,
            pl.BlockSpec((1, d), lambda i: (0, 0)),
        ],
        out_specs=pl.BlockSpec((tm, d), lambda i: (i, 0)),
        out_shape=jax.ShapeDtypeStruct((n, d), F32),
        scratch_shapes=[pltpu.VMEM((tm, d), BF16), pltpu.VMEM((2, d, tf), BF16), pltpu.VMEM((2, d, tf), BF16),
                        pltpu.VMEM((2, tf, d), BF16), pltpu.SemaphoreType.DMA((3, 2))],
        compiler_params=_cparams(1),
        name="ffn",
    )(x, g.reshape(1, d), wg, wu, wd, fg.reshape(1, d))


def _proj_kernel(x_ref, g_ref, w_ref, mu_ref, b_ref, o_ref, carry_ref, *, tiles_per_seq, tn):
    i = pl.program_id(0)
    tm = x_ref.shape[0]
    ncols = w_ref.shape[1]

    @pl.when(i % tiles_per_seq == 0)
    def _():
        carry_ref[...] = jnp.zeros(carry_ref.shape, F32)

    h = _rms(x_ref[...], g_ref[...]).astype(BF16)
    row = lax.broadcasted_iota(jnp.int32, (SUBLANES, tn), 0)
    for c0 in range(0, ncols, tn):
        cs = slice(c0, c0 + tn)
        z = _dot(h, w_ref[:, cs])
        prev_tail = carry_ref[:, cs]
        carry_ref[:, cs] = z[tm - SUBLANES:, :]
        zs = pltpu.roll(z, 1, 0)
        head = jnp.where(row == 0, pltpu.roll(prev_tail, 1, 0), zs[:SUBLANES])
        mu = mu_ref[:, cs]
        b = b_ref[:, cs]
        z0 = z[:SUBLANES]
        o_ref[:SUBLANES, cs] = z0 + (head - z0) * mu + b
        z1 = z[SUBLANES:]
        o_ref[SUBLANES:, cs] = z1 + (zs[SUBLANES:] - z1) * mu + b


def _proj(x, g, w_all, mu_all, b_all, seq):
    n, d = x.shape
    ncols = w_all.shape[1]
    tm = _pick(seq, (512, 256, 128))
    tn = _pick(ncols, (1024, 512, 256, 128))
    return pl.pallas_call(
        functools.partial(_proj_kernel, tiles_per_seq=seq // tm, tn=tn),
        grid=(n // tm,),
        in_specs=[
            pl.BlockSpec((tm, d), lambda i: (i, 0)),
            pl.BlockSpec((1, d), lambda i: (0, 0)),
            pl.BlockSpec((d, ncols), lambda i: (0, 0), pipeline_mode=pl.Buffered(1)),
            pl.BlockSpec((1, ncols), lambda i: (0, 0)),
            pl.BlockSpec((1, ncols), lambda i: (0, 0)),
        ],
        out_specs=pl.BlockSpec((tm, ncols), lambda i: (i, 0)),
        out_shape=jax.ShapeDtypeStruct((n, ncols), F32),
        scratch_shapes=[pltpu.VMEM((SUBLANES, ncols), F32)],
        compiler_params=_cparams(1),
        name="proj",
    )(x, g.reshape(1, d), w_all, mu_all, b_all)


def _bdot(a, b):
    return lax.dot_general(a, b, (((2,), (1,)), ((0,), (0,))), preferred_element_type=F32)


def _bdot_nt(a, b):
    return lax.dot_general(a, b, (((2,), (2,)), ((0,), (0,))), preferred_element_type=F32)


def _inv_unit_lower(nmat):
    L = nmat.shape[-1]
    r = lax.broadcasted_iota(jnp.int32, (L, L), 0)
    c = lax.broadcasted_iota(jnp.int32, (L, L), 1)
    eye = jnp.where(r == c, 1.0, 0.0)
    t = eye + nmat
    pw = nmat.astype(BF16)
    pw = _bdot(pw, pw).astype(BF16)
    steps = L.bit_length() - 2
    for i in range(steps):
        if i + 1 < steps:
            both = _bdot(jnp.concatenate([t.astype(BF16), pw], axis=1), pw)
            t = t + both[:, :L]
            pw = both[:, L:].astype(BF16)
        else:
            t = t + _bdot(t.astype(BF16), pw)
    return t


RWKV_INPUTS = 14


def _rwkv_kernel(*refs, seqs, n_cast, **static):
    cast_in = refs[RWKV_INPUTS:RWKV_INPUTS + n_cast]
    cast_out = refs[RWKV_INPUTS + n_cast + 1:RWKV_INPUTS + 2 * n_cast + 1]
    seq_refs = refs[:RWKV_INPUTS] + (refs[RWKV_INPUTS + n_cast],) + refs[RWKV_INPUTS + 2 * n_cast + 1:]
    s_ref = seq_refs[RWKV_INPUTS + 1]

    @pl.when(pl.program_id(1) == 0)
    def _():
        s_ref[...] = jnp.zeros(s_ref.shape, F32)

    for q in range(seqs):
        _rwkv_sequence(q, *seq_refs, **static)
    _cast_blocks(cast_in, cast_out)


def _rwkv_sequence(q, r_ref, k_ref, v_ref, lora_ref, w0_ref, a0_ref, kk_ref, ka_ref, rk_ref, gw_ref, gb_ref,
                   du_ref, au_ref, gu_ref,
                   o_ref,
                   s_ref, pr_ref, pk_ref, pv_ref, pa_ref, pg_ref, plp_ref, plw_ref,
                   *, n_tiles, dlp, alp):
    L = CHUNK
    tiles = slice(q * n_tiles, (q + 1) * n_tiles)

    lora = lora_ref[q]
    wd = jnp.tanh(lora[:, :dlp])
    ad = lora[:, dlp:dlp + alp]
    gd = jax.nn.sigmoid(lora[:, dlp + alp:])
    w = -_softplus(-(w0_ref[...] + _dot(wd.astype(BF16), du_ref[...]))) - 0.5
    lw = -jnp.exp(w)
    asig = jax.nn.sigmoid(a0_ref[...] + _dot(ad.astype(BF16), au_ref[...]))
    gate = _dot(gd.astype(BF16), gu_ref[...])
    row = lax.broadcasted_iota(jnp.int32, (L, L), 0)
    col = lax.broadcasted_iota(jnp.int32, (L, L), 1)
    incl = row >= col
    strict = row > col
    tril = jnp.where(incl, 1.0, 0.0).astype(BF16)
    h1 = lw.astype(BF16)
    r1 = lw - h1.astype(F32)
    h2 = r1.astype(BF16)
    h3 = (r1 - h2.astype(F32)).astype(BF16)
    logp = _dot(tril, h1) + _dot(tril, h2) + _dot(tril, h3)
    for p in range(n_tiles):
        sl = slice(p * LANES, (p + 1) * LANES)
        i = q * n_tiles + p
        pr_ref[i] = r_ref[q, :, sl]
        pk_ref[i] = k_ref[q, :, sl]
        pv_ref[i] = v_ref[q, :, sl]
        pa_ref[i] = asig[:, sl]
        pg_ref[i] = gate[:, sl]
        plp_ref[i] = logp[:, sl]
        plw_ref[i] = lw[:, sl]

    lane = lax.broadcasted_iota(jnp.int32, (1, LANES), 1)
    lo = lane < HEAD_DIM
    same_head = jnp.where(row < HEAD_DIM, 0, 1) == jnp.where(col < HEAD_DIM, 0, 1)
    inv_hd = 1.0 / HEAD_DIM

    def first(x):
        return jnp.where(lo, x, 0.0)

    def second(x):
        return jnp.where(lo, 0.0, x)

    def segsum(x):
        return jnp.where(lo, jnp.sum(first(x), axis=-1, keepdims=True),
                         jnp.sum(second(x), axis=-1, keepdims=True))

    r = pr_ref[tiles]
    k = pk_ref[tiles]
    v = pv_ref[tiles]
    a_s = pa_ref[tiles]
    lp = plp_ref[tiles]
    kk = k * kk_ref[...]
    ss = segsum(kk * kk)
    kk = kk / jnp.maximum(jnp.sqrt(ss), 1e-12)
    a = -kk
    b = kk * a_s
    km = k * (1.0 + (a_s - 1.0) * ka_ref[...])
    cmid = lp[:, L // 2 - 1:L // 2, :]
    clast = lp[:, L - 1:L, :]
    lpe = lp - plw_ref[tiles]
    e_inv = jnp.exp(cmid - lp)
    e_mid = jnp.exp(cmid)
    at = a * jnp.exp(lpe - cmid)
    a_abs = at * e_mid
    rt = r * jnp.exp(lp - cmid)
    r_abs = rt * e_mid
    bt = b * e_inv
    kt = km * e_inv
    e_l = jnp.exp(clast - cmid)
    e_p = jnp.exp(clast)
    lhs = jnp.concatenate([first(at), second(at), first(rt), second(rt)], axis=1).astype(BF16)
    gmat = _bdot_nt(lhs, jnp.concatenate([bt, kt], axis=1).astype(BF16))
    n_lo = jnp.where(strict, gmat[:, 0:L, 0:L], 0.0)
    ak_lo = jnp.where(strict, gmat[:, 0:L, L:], 0.0)
    n_hi = jnp.where(strict, gmat[:, L:2 * L, 0:L], 0.0)
    ak_hi = jnp.where(strict, gmat[:, L:2 * L, L:], 0.0)
    rb_lo = jnp.where(incl, gmat[:, 2 * L:3 * L, 0:L], 0.0)
    rk_lo = jnp.where(incl, gmat[:, 2 * L:3 * L, L:], 0.0)
    rb_hi = jnp.where(incl, gmat[:, 3 * L:, 0:L], 0.0)
    rk_hi = jnp.where(incl, gmat[:, 3 * L:, L:], 0.0)
    t_all = _inv_unit_lower(jnp.concatenate([n_lo, n_hi], axis=0))
    t_lo, t_hi = t_all[:n_tiles], t_all[n_tiles:]
    s = s_ref[tiles]
    s_b = s.astype(BF16)
    v_lohi = jnp.concatenate([first(v), second(v)], axis=1).astype(BF16)
    rhs = _bdot_nt(a_abs.astype(BF16), s_b) + _bdot(jnp.concatenate([ak_lo, ak_hi], axis=2).astype(BF16), v_lohi)
    u = _bdot(jnp.concatenate([t_lo, t_hi], axis=2).astype(BF16),
              jnp.concatenate([first(rhs), second(rhs)], axis=1).astype(BF16))
    u_lohi = jnp.concatenate([first(u), second(u)], axis=1).astype(BF16)
    y = _bdot_nt(r_abs.astype(BF16), s_b) + _bdot(
        jnp.concatenate([rb_lo, rb_hi, rk_lo, rk_hi], axis=2).astype(BF16),
        jnp.concatenate([u_lohi, v_lohi], axis=1))
    upd = _bdot(jnp.concatenate([jnp.swapaxes(u, 1, 2), jnp.swapaxes(v, 1, 2)], axis=2).astype(BF16),
                jnp.concatenate([bt * e_l, kt * e_l], axis=1).astype(BF16))
    s_ref[tiles] = s * e_p + jnp.where(same_head, upd, 0.0)
    mean = segsum(y) * inv_hd
    dev = y - mean
    var = segsum(dev * dev) * inv_hd
    yn = dev * lax.rsqrt(var + GN_EPS) * gw_ref[...] + gb_ref[...]
    bonus = segsum(r * km * rk_ref[...]) * v
    out = ((yn + bonus) * pg_ref[tiles]).astype(BF16)
    for p in range(n_tiles):
        o_ref[q, :, p * LANES:(p + 1) * LANES] = out[p]


def _rwkv(z, batch, seq, c, lora_off, lora_w, dlp, alp, vecs, loras, casts):
    n = z.shape[0]
    nc = seq // CHUNK
    n_tiles = c // LANES
    seqs = 1
    assert lora_off % lora_w == 0 and len(vecs) + len(loras) + 4 == RWKV_INPUTS
    lora_blk = lora_off // lora_w
    z3 = z.reshape(batch, seq, z.shape[1])
    cast_in, cast_out, cast_shapes = _cast_specs(casts, (batch // seqs) * nc, lambda b, t: b * nc + t)
    vec_spec = pl.BlockSpec((n_tiles, 1, LANES), lambda b, t: (0, 0, 0))
    in_specs = [
        pl.BlockSpec((seqs, CHUNK, c), lambda b, t: (b, t, 0)),
        pl.BlockSpec((seqs, CHUNK, c), lambda b, t: (b, t, 1)),
        pl.BlockSpec((seqs, CHUNK, c), lambda b, t: (b, t, 2)),
        pl.BlockSpec((seqs, CHUNK, lora_w), lambda b, t: (b, t, lora_blk)),
        pl.BlockSpec((1, c), lambda b, t: (0, 0)),
        pl.BlockSpec((1, c), lambda b, t: (0, 0)),
    ] + [vec_spec] * 5 + [pl.BlockSpec(w.shape, lambda b, t: (0, 0)) for w in loras]
    tile_f32 = pltpu.VMEM((seqs * n_tiles, CHUNK, LANES), F32)
    y, *cast = pl.pallas_call(
        functools.partial(_rwkv_kernel, seqs=seqs, n_cast=len(casts), n_tiles=n_tiles, dlp=dlp, alp=alp),
        grid=(batch // seqs, nc),
        in_specs=in_specs + cast_in,
        out_specs=[pl.BlockSpec((seqs, CHUNK, c), lambda b, t: (b, t, 0))] + cast_out,
        out_shape=[jax.ShapeDtypeStruct((batch, seq, c), BF16)] + cast_shapes,
        scratch_shapes=[pltpu.VMEM((seqs * n_tiles, LANES, LANES), F32)] + [tile_f32] * 7,
        compiler_params=_cparams(2),
        name="rwkv",
    )(z3, z3, z3, z3, *vecs, *loras, *casts)
    return y.reshape(n, c), cast


def _swa_kernel(sink_ref, q_ref, kvc_ref, kvp_ref, cosc_ref, sinc_ref, cosp_ref, sinp_ref, *rest,
                n_tiles, group, qb):
    n_cast = len(rest) // 2
    o_ref = rest[n_cast]
    _cast_blocks(rest[:n_cast], rest[n_cast + 1:])
    nblk = pl.program_id(1)
    lane = lax.broadcasted_iota(jnp.int32, (1, LANES), 1)
    lo = lane < HEAD_DIM
    rot_lo = jnp.bitwise_and(lane, HEAD_DIM - 1) < HEAD_DIM // 2

    def rope(x, cos, sin_signed):
        partner = jnp.where(rot_lo, pltpu.roll(x, LANES - HEAD_DIM // 2, 1), pltpu.roll(x, HEAD_DIM // 2, 1))
        return x * cos + partner * sin_signed

    cosc = cosc_ref[...]
    sinc = sinc_ref[...]
    kvc = kvc_ref[...]
    kvp = kvp_ref[...]
    keys = jnp.concatenate([rope(kvp[:, :LANES], cosp_ref[...], sinp_ref[...]),
                            rope(kvc[:, :LANES], cosc, sinc)], axis=0)
    vals = jnp.concatenate([kvp[:, LANES:], kvc[:, LANES:]], axis=0)
    keys_sw = pltpu.roll(keys, HEAD_DIM, 1)
    vals_sw = pltpu.roll(vals, HEAD_DIM, 1)
    k_first = [jnp.where(lo, keys, 0.0).astype(BF16), jnp.where(lo, keys_sw, 0.0).astype(BF16)]
    k_second = [jnp.where(lo, 0.0, keys_sw).astype(BF16), jnp.where(lo, 0.0, keys).astype(BF16)]
    ones_kv = jnp.ones(vals.shape, BF16)
    with_ones = lambda x: jnp.concatenate([x.astype(BF16), ones_kv], axis=1)
    v_first = [with_ones(jnp.where(lo, vals, 0.0)), with_ones(jnp.where(lo, vals_sw, 0.0))]
    v_second = [with_ones(jnp.where(lo, 0.0, vals_sw)), with_ones(jnp.where(lo, 0.0, vals))]
    qi = lax.broadcasted_iota(jnp.int32, (BLOCK, 2 * BLOCK), 0)
    ki = lax.broadcasted_iota(jnp.int32, (BLOCK, 2 * BLOCK), 1)
    window = (ki > qi) & (ki <= qi + BLOCK)
    valid = [window & ((nblk > 0) | (ki >= BLOCK))] + [window] * (qb - 1)
    scale = HEAD_DIM ** -0.5
    heads = [(j, p, half) for j in range(qb) for p in range(n_tiles) for half in range(HEADS_PER_TILE)]
    scores = []
    for j in range(qb):
        qrows = slice(j * BLOCK, (j + 1) * BLOCK)
        krows = slice(j * BLOCK, (j + 2) * BLOCK)
        for p in range(n_tiles):
            g = (p * HEADS_PER_TILE) // group
            qp = (rope(q_ref[qrows, p * LANES:(p + 1) * LANES], cosc[qrows], sinc[qrows]) * scale).astype(BF16)
            scores += [_dot_nt(qp, k_first[g][krows]), _dot_nt(qp, k_second[g][krows])]
    probs, sink_terms = [], []
    for (j, p, half), s in zip(heads, scores):
        s = jnp.where(valid[j], s, NEG_INF)
        sink = sink_ref[p * HEADS_PER_TILE + half]
        m = jnp.maximum(jnp.max(s, axis=-1, keepdims=True), sink)
        probs.append(jnp.exp(s - m).astype(BF16))
        sink_terms.append(jnp.exp(sink - m))
    outs = []
    for (j, p, half), e, st in zip(heads, probs, sink_terms):
        g = (p * HEADS_PER_TILE) // group
        krows = slice(j * BLOCK, (j + 2) * BLOCK)
        pv = _dot(e, (v_second[g] if half else v_first[g])[krows])
        outs.append(pv[:, :LANES] * (1.0 / (pv[:, LANES:] + st)))
    for j in range(qb):
        for p in range(n_tiles):
            i = (j * n_tiles + p) * HEADS_PER_TILE
            o_ref[j * BLOCK:(j + 1) * BLOCK, p * LANES:(p + 1) * LANES] = (outs[i] + outs[i + 1]).astype(BF16)


def _swa(z, sinks, cos_t, sin_t, batch, seq, q_off, qw, kv_off, kvw, group, casts):
    n = z.shape[0]
    nb = seq // BLOCK
    assert q_off % qw == 0 and kv_off % kvw == 0 and kvw == 2 * LANES
    q_blk, kv_blk = q_off // qw, kv_off // kvw
    qb = _pick(nb, (4, 2, 1))
    steps = nb // qb
    prev = lambda t: jnp.maximum(t * qb - 1, 0)
    cast_in, cast_out, cast_shapes = _cast_specs(casts, batch * steps, lambda b, t: b * steps + t)
    y, *cast = pl.pallas_call(
        functools.partial(_swa_kernel, n_tiles=qw // LANES, group=group, qb=qb),
        grid=(batch, steps),
        in_specs=[
            pl.BlockSpec(memory_space=pltpu.SMEM),
            pl.BlockSpec((qb * BLOCK, qw), lambda b, t: (b * steps + t, q_blk)),
            pl.BlockSpec((qb * BLOCK, kvw), lambda b, t: (b * steps + t, kv_blk)),
            pl.BlockSpec((BLOCK, kvw), lambda b, t: (b * nb + prev(t), kv_blk)),
            pl.BlockSpec((qb * BLOCK, LANES), lambda b, t: (t, 0)),
            pl.BlockSpec((qb * BLOCK, LANES), lambda b, t: (t, 0)),
            pl.BlockSpec((BLOCK, LANES), lambda b, t: (prev(t), 0)),
            pl.BlockSpec((BLOCK, LANES), lambda b, t: (prev(t), 0)),
        ] + cast_in,
        out_specs=[pl.BlockSpec((qb * BLOCK, qw), lambda b, t: (b * steps + t, 0))] + cast_out,
        out_shape=[jax.ShapeDtypeStruct((n, qw), BF16)] + cast_shapes,
        compiler_params=_cparams(2),
        name="swa",
    )(sinks, z, z, z, cos_t, sin_t, cos_t, sin_t, *casts)
    return y, cast


def _memkv_kernel(m_ref, g_ref, w_ref, o_ref):
    h = _rms(m_ref[...], g_ref[...]).astype(BF16)
    o_ref[...] = _dot(h, w_ref[...].astype(BF16)).astype(BF16)


def _memkv(mem, g, w):
    n, d = mem.shape
    ncols = w.shape[1]
    tm = _pick(n, (512, 256, 128))
    tn = _pick(ncols, (1024, 512, 256, 128))
    return pl.pallas_call(
        _memkv_kernel,
        grid=(ncols // tn, n // tm),
        in_specs=[
            pl.BlockSpec((tm, d), lambda j, i: (i, 0)),
            pl.BlockSpec((1, d), lambda j, i: (0, 0)),
            pl.BlockSpec((d, tn), lambda j, i: (0, j)),
        ],
        out_specs=pl.BlockSpec((tm, tn), lambda j, i: (i, j)),
        out_shape=jax.ShapeDtypeStruct((n, ncols), BF16),
        compiler_params=_cparams(2),
        name="memkv",
    )(mem, g.reshape(1, d), w)


def _mix_kernel(x_ref, yr_ref, ys_ref, wo_ref, bo_ref, g_ref, wq_ref, k_ref, v_ref, wxo_ref, o_ref):
    c = yr_ref.shape[1]
    d = x_ref.shape[1]
    hd = d // XATTN_HEADS
    x2 = x_ref[...] + _dot(yr_ref[...], wo_ref[:c, :]) + _dot(ys_ref[...], wo_ref[c:, :]) + bo_ref[...]
    q = _dot(_rms(x2, g_ref[...]).astype(BF16), wq_ref[...]).astype(BF16)
    scale = hd ** -0.5
    heads = [slice(h * hd, (h + 1) * hd) for h in range(XATTN_HEADS)]
    scores = [_dot_nt(q[:, sl], k_ref[:, sl]) * scale for sl in heads]
    probs, dens = [], []
    for s in scores:
        e = jnp.exp(s - jnp.max(s, axis=-1, keepdims=True))
        probs.append(e.astype(BF16))
        dens.append(jnp.sum(e, axis=-1, keepdims=True))
    outs = [(_dot(e, v_ref[:, sl]) * (1.0 / den)).astype(BF16) for e, den, sl in zip(probs, dens, heads)]
    o_ref[...] = x2 + _dot(jnp.concatenate(outs, axis=1), wxo_ref[...])


def _mix(x, yr, ys, wo, bo, g, wq, kv, wxo, seq, mlen):
    n, d = x.shape
    c = yr.shape[1]
    tm = _pick(seq, (256, 128))
    per_seq = seq // tm
    once = pl.Buffered(1)
    return pl.pallas_call(
        _mix_kernel,
        grid=(n // tm,),
        in_specs=[
            pl.BlockSpec((tm, d), lambda i: (i, 0)),
            pl.BlockSpec((tm, c), lambda i: (i, 0)),
            pl.BlockSpec((tm, d - c), lambda i: (i, 0)),
            pl.BlockSpec((d, d), lambda i: (0, 0), pipeline_mode=once),
            pl.BlockSpec((1, d), lambda i: (0, 0)),
            pl.BlockSpec((1, d), lambda i: (0, 0)),
            pl.BlockSpec((d, d), lambda i: (0, 0), pipeline_mode=once),
            pl.BlockSpec((mlen, d), lambda i: (i // per_seq, 0)),
            pl.BlockSpec((mlen, d), lambda i: (i // per_seq, 1)),
            pl.BlockSpec((d, d), lambda i: (0, 0), pipeline_mode=once),
        ],
        out_specs=pl.BlockSpec((tm, d), lambda i: (i, 0)),
        out_shape=jax.ShapeDtypeStruct((n, d), F32),
        compiler_params=_cparams(1),
        name="mix",
    )(x, yr, ys, wo, bo.reshape(1, d), g.reshape(1, d), wq, kv, kv, wxo)


def _pad_cols(w, width):
    return jnp.pad(w, ((0, 0), (0, width - w.shape[1])))


def _pad_rows(w, height):
    return jnp.pad(w, ((0, height - w.shape[0]), (0, 0)))


def _rope_tables(seq):
    half = HEAD_DIM // 2
    lane = jnp.arange(LANES)
    inv_freq = ROPE_THETA ** (-jnp.arange(0, HEAD_DIM, 2, dtype=F32) / HEAD_DIM)
    ang = jnp.arange(seq, dtype=F32)[:, None] * inv_freq[lane % half][None, :]
    sign = jnp.where((lane % HEAD_DIM) < half, -1.0, 1.0)
    return jnp.cos(ang), jnp.sin(ang) * sign[None, :]


def kernel(x, mem, f1_norm, f1_gate, f1_up, f1_down, mix_norm, w_in, b_in_attn, rw_mu, rw_w0, rw_decay_up, rw_a0, rw_aaa_up, rw_gate_up, rw_k_k, rw_k_a, rw_r_k, rw_lnx_w, rw_lnx_b, attn_sinks, w_out, b_out, xa_norm, mem_norm, w_xq, w_xkv, w_xo, f2_norm, f2_gate, f2_up, f2_down, final_norm):
    batch, seq, d = x.shape
    mlen = mem.shape[1]
    depth = f1_norm.shape[0]
    c = rw_w0.shape[1]
    sw = d - c
    dl, al, gl = rw_decay_up.shape[1], rw_aaa_up.shape[1], rw_gate_up.shape[1]
    dlp, alp, glp = (_round_up(v, LANES) for v in (dl, al, gl))
    kvw = b_in_attn.shape[1] - sw
    q_heads = sw // HEAD_DIM
    group = q_heads // (kvw // (2 * HEAD_DIM))
    n_tiles = c // LANES
    n = batch * seq
    q_off = 3 * c
    lora_off = q_off + sw
    lora_w = dlp + alp + glp
    kv_off = lora_off + lora_w
    ncols = _round_up(kv_off + kvw, 2 * LANES)
    cos_t, sin_t = _rope_tables(seq)

    xf = x.reshape(n, d)
    memf = mem.reshape(batch * mlen, d)
    for l in range(depth):
        xf = _ffn(xf, f1_norm[l], f1_gate[l].astype(BF16), f1_up[l].astype(BF16), f1_down[l].astype(BF16), None)

        wl = w_in[l]
        o1, o2, o3 = 3 * c, 3 * c + dl, 3 * c + dl + al
        shift = o3 + gl
        w_all = jnp.concatenate([
            wl[:, :o1], wl[:, shift:shift + sw],
            _pad_cols(wl[:, o1:o2], dlp), _pad_cols(wl[:, o2:o3], alp), _pad_cols(wl[:, o3:shift], glp),
            wl[:, shift + sw:]], axis=1)
        w_all = _pad_cols(w_all, ncols).astype(BF16)
        mu = rw_mu[l][None, :]
        mu_all = _pad_cols(jnp.concatenate([
            mu[:, :o1], jnp.zeros((1, sw), F32),
            _pad_cols(mu[:, o1:o2], dlp), _pad_cols(mu[:, o2:o3], alp), _pad_cols(mu[:, o3:shift], glp)],
            axis=1), ncols)
        bia = b_in_attn[l][None, :]
        b_all = _pad_cols(jnp.concatenate([
            jnp.zeros((1, q_off), F32), bia[:, :sw], jnp.zeros((1, lora_w), F32), bia[:, sw:]], axis=1), ncols)
        z = _proj(xf, mix_norm[l], w_all, mu_all, b_all, seq)

        tiles = lambda v: v.reshape(n_tiles, 1, LANES)
        vecs = [rw_w0[l].reshape(1, c), rw_a0[l].reshape(1, c), tiles(rw_k_k[l]), tiles(rw_k_a[l]),
                tiles(rw_r_k[l]), tiles(rw_lnx_w[l]), tiles(rw_lnx_b[l])]
        loras = [_pad_rows(rw_decay_up[l], dlp).astype(BF16), _pad_rows(rw_aaa_up[l], alp).astype(BF16),
                 _pad_rows(rw_gate_up[l], glp).astype(BF16)]
        y_rwkv, (f2_down_b, w_out_b, w_xq_b, w_xo_b) = _rwkv(
            z, batch, seq, c, lora_off, lora_w, dlp, alp, vecs, loras,
            [f2_down[l], w_out[l], w_xq[l], w_xo[l]])
        y_swa, (f2_gate_b, f2_up_b) = _swa(z, attn_sinks[l], cos_t, sin_t, batch, seq, q_off, sw, kv_off, kvw,
                                           group, [f2_gate[l], f2_up[l]])

        kv_mem = _memkv(memf, mem_norm[l], w_xkv[l])
        xf = _mix(xf, y_rwkv, y_swa, w_out_b, b_out[l], xa_norm[l], w_xq_b, kv_mem, w_xo_b, seq, mlen)

        last = l == depth - 1
        xf = _ffn(xf, f2_norm[l], f2_gate_b, f2_up_b, f2_down_b, final_norm if last else None)
    return xf.reshape(batch, seq, d)
```

```python
import functools

import jax
import jax.numpy as jnp
from jax import lax
from jax.experimental import pallas as pl
from jax.experimental.pallas import tpu as pltpu

F32, BF16 = jnp.float32, jnp.bfloat16

LANES = 128
SUBLANES = 8
VMEM_LIMIT_BYTES = 56 * 1024 * 1024

HEAD_DIM = 64
HEADS_PER_TILE = LANES // HEAD_DIM
CHUNK = 128
BLOCK = 128
XATTN_HEADS = 4
RMS_EPS = 1e-6
GN_EPS = 64e-5
NEG_INF = -1e30
ROPE_THETA = 10000.0


def _round_up(n, m):
    return (n + m - 1) // m * m


def _pick(n, prefs):
    for p in prefs:
        if n % p == 0:
            return p
    raise ValueError(f"no tile in {prefs} divides {n}")


def _cparams(n_axes):
    return pltpu.CompilerParams(dimension_semantics=("arbitrary",) * n_axes,
                                vmem_limit_bytes=VMEM_LIMIT_BYTES)


def _dot(a, b):
    return jnp.dot(a, b, preferred_element_type=F32)


def _dot_nt(a, b):
    return lax.dot_general(a, b, (((1,), (1,)), ((), ())), preferred_element_type=F32)


def _cast_specs(weights, nsteps, flat_step):
    in_specs, out_specs, out_shapes = [], [], []
    for w in weights:
        units = w.shape[0] // (2 * SUBLANES)
        nblk = max(d for d in range(1, nsteps + 1) if units % d == 0)
        imap = lambda *g, nblk=nblk: (jnp.minimum(flat_step(*g), nblk - 1), 0)
        spec = pl.BlockSpec((w.shape[0] // nblk, w.shape[1]), imap)
        in_specs.append(spec)
        out_specs.append(spec)
        out_shapes.append(jax.ShapeDtypeStruct(w.shape, BF16))
    return in_specs, out_specs, out_shapes


def _cast_blocks(src_refs, dst_refs):
    for src, dst in zip(src_refs, dst_refs):
        dst[...] = src[...].astype(BF16)


def _rms(x, g):
    ms = jnp.mean(x * x, axis=-1, keepdims=True)
    return x * lax.rsqrt(ms + RMS_EPS) * g


def _softplus(x):
    return jnp.maximum(x, 0.0) + jnp.log(1.0 + jnp.exp(-jnp.abs(x)))


def _ffn_kernel(x_ref, g_ref, wg_ref, wu_ref, wd_ref, fg_ref, *rest, final_norm, rows, pieces):
    j = pl.program_id(1)
    tm = x_ref.shape[0]
    if pieces:
        src_ref, o_ref, dst_ref, h_ref = rest
        blk = src_ref[...]
        cols = []
        for start, width, pad in pieces:
            cols.append(blk[:, start:start + width])
            if pad:
                cols.append(jnp.zeros((blk.shape[0], pad), blk.dtype))
        dst_ref[...] = jnp.concatenate(cols, axis=1).astype(BF16)
    else:
        o_ref, h_ref = rest

    @pl.when(j == 0)
    def _():
        for r0 in range(0, tm, rows):
            x = x_ref[r0:r0 + rows, :]
            h_ref[r0:r0 + rows, :] = _rms(x, g_ref[...]).astype(BF16)
            o_ref[r0:r0 + rows, :] = x

    for r0 in range(0, tm, rows):
        h = h_ref[r0:r0 + rows, :]
        gate = _dot(h, wg_ref[...])
        up = _dot(h, wu_ref[...])
        act = (gate * jax.nn.sigmoid(gate) * up).astype(BF16)
        o_ref[r0:r0 + rows, :] += 0.5 * _dot(act, wd_ref[...])

    if final_norm:
        @pl.when(j == pl.num_programs(1) - 1)
        def _():
            for r0 in range(0, tm, rows):
                o_ref[r0:r0 + rows, :] = _rms(o_ref[r0:r0 + rows, :], fg_ref[...])


def _ffn(x, g, wg, wu, wd, fg, relayout=None):
    n, d = x.shape
    f = wg.shape[1]
    tm = _pick(n, (1024, 512, 256, 128))
    tf = _pick(f, (512, 256, 128))
    n_tf = f // tf
    rows = min(tm, 512)
    final_norm = fg is not None
    fg = g if fg is None else fg
    in_specs = [
        pl.BlockSpec((tm, d), lambda i, j: (i, 0)),
        pl.BlockSpec((1, d), lambda i, j: (0, 0)),
        pl.BlockSpec((d, tf), lambda i, j: (0, j)),
        pl.BlockSpec((d, tf), lambda i, j: (0, j)),
        pl.BlockSpec((tf, d), lambda i, j: (j, 0)),
        pl.BlockSpec((1, d), lambda i, j: (0, 0)),
    ]
    out_specs = [pl.BlockSpec((tm, d), lambda i, j: (i, 0))]
    out_shape = [jax.ShapeDtypeStruct((n, d), F32)]
    args = [x, g.reshape(1, d), wg, wu, wd, fg.reshape(1, d)]
    pieces = ()
    if relayout is not None:
        w_src, pieces = relayout
        src_spec, _, _ = _cast_specs([w_src], (n // tm) * n_tf, lambda i, j: i * n_tf + j)
        out_cols = sum(width + pad for _, width, pad in pieces)
        in_specs += src_spec
        out_specs.append(pl.BlockSpec((src_spec[0].block_shape[0], out_cols), src_spec[0].index_map))
        out_shape.append(jax.ShapeDtypeStruct((w_src.shape[0], out_cols), BF16))
        args.append(w_src)
    outs = pl.pallas_call(
        functools.partial(_ffn_kernel, final_norm=final_norm, rows=rows, pieces=tuple(pieces)),
        grid=(n // tm, n_tf),
        in_specs=in_specs,
        out_specs=out_specs,
        out_shape=out_shape,
        scratch_shapes=[pltpu.VMEM((tm, d), BF16)],
        compiler_params=_cparams(2),
        name="ffn",
    )(*args)
    return outs if relayout is not None else outs[0]


def _proj_kernel(x_ref, g_ref, w_ref, mu_ref, b_ref, o_ref, carry_ref, *, tiles_per_seq, tn):
    i = pl.program_id(0)
    tm = x_ref.shape[0]
    ncols = w_ref.shape[1]

    @pl.when(i % tiles_per_seq == 0)
    def _():
        carry_ref[...] = jnp.zeros(carry_ref.shape, F32)

    h = _rms(x_ref[...], g_ref[...]).astype(BF16)
    row = lax.broadcasted_iota(jnp.int32, (SUBLANES, tn), 0)
    for c0 in range(0, ncols, tn):
        cs = slice(c0, c0 + tn)
        z = _dot(h, w_ref[:, cs])
        prev_tail = carry_ref[:, cs]
        carry_ref[:, cs] = z[tm - SUBLANES:, :]
        zs = pltpu.roll(z, 1, 0)
        head = jnp.where(row == 0, pltpu.roll(prev_tail, 1, 0), zs[:SUBLANES])
        mu = mu_ref[:, cs]
        b = b_ref[:, cs]
        z0 = z[:SUBLANES]
        o_ref[:SUBLANES, cs] = z0 + (head - z0) * mu + b
        z1 = z[SUBLANES:]
        o_ref[SUBLANES:, cs] = z1 + (zs[SUBLANES:] - z1) * mu + b


def _proj(x, g, w_all, mu_all, b_all, seq):
    n, d = x.shape
    ncols = w_all.shape[1]
    tm = _pick(seq, (512, 256, 128))
    tn = _pick(ncols, (1024, 512, 256, 128))
    return pl.pallas_call(
        functools.partial(_proj_kernel, tiles_per_seq=seq // tm, tn=tn),
        grid=(n // tm,),
        in_specs=[
            pl.BlockSpec((tm, d), lambda i: (i, 0)),
            pl.BlockSpec((1, d), lambda i: (0, 0)),
            pl.BlockSpec((d, ncols), lambda i: (0, 0), pipeline_mode=pl.Buffered(1)),
            pl.BlockSpec((1, ncols), lambda i: (0, 0)),
            pl.BlockSpec((1, ncols), lambda i: (0, 0)),
        ],
        out_specs=pl.BlockSpec((tm, ncols), lambda i: (i, 0)),
        out_shape=jax.ShapeDtypeStruct((n, ncols), F32),
        scratch_shapes=[pltpu.VMEM((SUBLANES, ncols), F32)],
        compiler_params=_cparams(1),
        name="proj",
    )(x, g.reshape(1, d), w_all, mu_all, b_all)


def _bdot(a, b):
    return lax.dot_general(a, b, (((2,), (1,)), ((0,), (0,))), preferred_element_type=F32)


def _bdot_nt(a, b):
    return lax.dot_general(a, b, (((2,), (2,)), ((0,), (0,))), preferred_element_type=F32)


def _inv_unit_lower(nmat):
    L = nmat.shape[-1]
    r = lax.broadcasted_iota(jnp.int32, (L, L), 0)
    c = lax.broadcasted_iota(jnp.int32, (L, L), 1)
    eye = jnp.where(r == c, 1.0, 0.0)
    t = eye + nmat
    pw = nmat.astype(BF16)
    pw = _bdot(pw, pw).astype(BF16)
    steps = L.bit_length() - 2
    for i in range(steps):
        if i + 1 < steps:
            both = _bdot(jnp.concatenate([t.astype(BF16), pw], axis=1), pw)
            t = t + both[:, :L]
            pw = both[:, L:].astype(BF16)
        else:
            t = t + _bdot(t.astype(BF16), pw)
    return t


RWKV_INPUTS = 14


def _rwkv_kernel(*refs, seqs, n_cast, **static):
    cast_in = refs[RWKV_INPUTS:RWKV_INPUTS + n_cast]
    cast_out = refs[RWKV_INPUTS + n_cast + 1:RWKV_INPUTS + 2 * n_cast + 1]
    seq_refs = refs[:RWKV_INPUTS] + (refs[RWKV_INPUTS + n_cast],) + refs[RWKV_INPUTS + 2 * n_cast + 1:]
    s_ref = seq_refs[RWKV_INPUTS + 1]

    @pl.when(pl.program_id(1) == 0)
    def _():
        s_ref[...] = jnp.zeros(s_ref.shape, F32)

    for q in range(seqs):
        _rwkv_sequence(q, *seq_refs, **static)
    _cast_blocks(cast_in, cast_out)


def _rwkv_sequence(q, r_ref, k_ref, v_ref, lora_ref, w0_ref, a0_ref, kk_ref, ka_ref, rk_ref, gw_ref, gb_ref,
                   du_ref, au_ref, gu_ref,
                   o_ref,
                   s_ref, pr_ref, pk_ref, pv_ref, pa_ref, pg_ref, plp_ref, plw_ref,
                   *, n_tiles, dlp, alp):
    L = CHUNK
    tiles = slice(q * n_tiles, (q + 1) * n_tiles)

    lora = lora_ref[q]
    wd = jnp.tanh(lora[:, :dlp])
    ad = lora[:, dlp:dlp + alp]
    gd = jax.nn.sigmoid(lora[:, dlp + alp:])
    w = -_softplus(-(w0_ref[...] + _dot(wd.astype(BF16), du_ref[...]))) - 0.5
    lw = -jnp.exp(w)
    asig = jax.nn.sigmoid(a0_ref[...] + _dot(ad.astype(BF16), au_ref[...]))
    gate = _dot(gd.astype(BF16), gu_ref[...])
    row = lax.broadcasted_iota(jnp.int32, (L, L), 0)
    col = lax.broadcasted_iota(jnp.int32, (L, L), 1)
    incl = row >= col
    strict = row > col
    tril = jnp.where(incl, 1.0, 0.0).astype(BF16)
    h1 = lw.astype(BF16)
    r1 = lw - h1.astype(F32)
    h2 = r1.astype(BF16)
    h3 = (r1 - h2.astype(F32)).astype(BF16)
    logp = _dot(tril, h1) + _dot(tril, h2) + _dot(tril, h3)
    for p in range(n_tiles):
        sl = slice(p * LANES, (p + 1) * LANES)
        i = q * n_tiles + p
        pr_ref[i] = r_ref[q, :, sl]
        pk_ref[i] = k_ref[q, :, sl]
        pv_ref[i] = v_ref[q, :, sl]
        pa_ref[i] = asig[:, sl]
        pg_ref[i] = gate[:, sl]
        plp_ref[i] = logp[:, sl]
        plw_ref[i] = lw[:, sl]

    lane = lax.broadcasted_iota(jnp.int32, (1, LANES), 1)
    lo = lane < HEAD_DIM
    same_head = jnp.where(row < HEAD_DIM, 0, 1) == jnp.where(col < HEAD_DIM, 0, 1)
    inv_hd = 1.0 / HEAD_DIM

    def first(x):
        return jnp.where(lo, x, 0.0)

    def second(x):
        return jnp.where(lo, 0.0, x)

    def segsum(x):
        return jnp.where(lo, jnp.sum(first(x), axis=-1, keepdims=True),
                         jnp.sum(second(x), axis=-1, keepdims=True))

    r = pr_ref[tiles]
    k = pk_ref[tiles]
    v = pv_ref[tiles]
    a_s = pa_ref[tiles]
    lp = plp_ref[tiles]
    kk = k * kk_ref[...]
    ss = segsum(kk * kk)
    kk = kk / jnp.maximum(jnp.sqrt(ss), 1e-12)
    a = -kk
    b = kk * a_s
    km = k * (1.0 + (a_s - 1.0) * ka_ref[...])
    cmid = lp[:, L // 2 - 1:L // 2, :]
    clast = lp[:, L - 1:L, :]
    lpe = lp - plw_ref[tiles]
    e_inv = jnp.exp(cmid - lp)
    e_mid = jnp.exp(cmid)
    at = a * jnp.exp(lpe - cmid)
    a_abs = at * e_mid
    rt = r * jnp.exp(lp - cmid)
    r_abs = rt * e_mid
    bt = b * e_inv
    kt = km * e_inv
    e_l = jnp.exp(clast - cmid)
    e_p = jnp.exp(clast)
    lhs = jnp.concatenate([first(at), second(at), first(rt), second(rt)], axis=1).astype(BF16)
    gmat = _bdot_nt(lhs, jnp.concatenate([bt, kt], axis=1).astype(BF16))
    n_lo = jnp.where(strict, gmat[:, 0:L, 0:L], 0.0)
    ak_lo = jnp.where(strict, gmat[:, 0:L, L:], 0.0)
    n_hi = jnp.where(strict, gmat[:, L:2 * L, 0:L], 0.0)
    ak_hi = jnp.where(strict, gmat[:, L:2 * L, L:], 0.0)
    rb_lo = jnp.where(incl, gmat[:, 2 * L:3 * L, 0:L], 0.0)
    rk_lo = jnp.where(incl, gmat[:, 2 * L:3 * L, L:], 0.0)
    rb_hi = jnp.where(incl, gmat[:, 3 * L:, 0:L], 0.0)
    rk_hi = jnp.where(incl, gmat[:, 3 * L:, L:], 0.0)
    t_all = _inv_unit_lower(jnp.concatenate([n_lo, n_hi], axis=0))
    t_lo, t_hi = t_all[:n_tiles], t_all[n_tiles:]
    s = s_ref[tiles]
    s_b = s.astype(BF16)
    v_lohi = jnp.concatenate([first(v), second(v)], axis=1).astype(BF16)
    rhs = _bdot_nt(a_abs.astype(BF16), s_b) + _bdot(jnp.concatenate([ak_lo, ak_hi], axis=2).astype(BF16), v_lohi)
    u = _bdot(jnp.concatenate([t_lo, t_hi], axis=2).astype(BF16),
              jnp.concatenate([first(rhs), second(rhs)], axis=1).astype(BF16))
    u_lohi = jnp.concatenate([first(u), second(u)], axis=1).astype(BF16)
    y = _bdot_nt(r_abs.astype(BF16), s_b) + _bdot(
        jnp.concatenate([rb_lo, rb_hi, rk_lo, rk_hi], axis=2).astype(BF16),
        jnp.concatenate([u_lohi, v_lohi], axis=1))
    upd = _bdot(jnp.concatenate([jnp.swapaxes(u, 1, 2), jnp.swapaxes(v, 1, 2)], axis=2).astype(BF16),
                jnp.concatenate([bt * e_l, kt * e_l], axis=1).astype(BF16))
    s_ref[tiles] = s * e_p + jnp.where(same_head, upd, 0.0)
    mean = segsum(y) * inv_hd
    dev = y - mean
    var = segsum(dev * dev) * inv_hd
    yn = dev * lax.rsqrt(var + GN_EPS) * gw_ref[...] + gb_ref[...]
    bonus = segsum(r * km * rk_ref[...]) * v
    out = ((yn + bonus) * pg_ref[tiles]).astype(BF16)
    for p in range(n_tiles):
        o_ref[q, :, p * LANES:(p + 1) * LANES] = out[p]


def _rwkv(z, batch, seq, c, lora_off, lora_w, dlp, alp, vecs, loras, casts):
    n = z.shape[0]
    nc = seq // CHUNK
    n_tiles = c // LANES
    seqs = 1
    assert lora_off % lora_w == 0 and len(vecs) + len(loras) + 4 == RWKV_INPUTS
    lora_blk = lora_off // lora_w
    z3 = z.reshape(batch, seq, z.shape[1])
    cast_in, cast_out, cast_shapes = _cast_specs(casts, (batch // seqs) * nc, lambda b, t: b * nc + t)
    vec_spec = pl.BlockSpec((n_tiles, 1, LANES), lambda b, t: (0, 0, 0))
    in_specs = [
        pl.BlockSpec((seqs, CHUNK, c), lambda b, t: (b, t, 0)),
        pl.BlockSpec((seqs, CHUNK, c), lambda b, t: (b, t, 1)),
        pl.BlockSpec((seqs, CHUNK, c), lambda b, t: (b, t, 2)),
        pl.BlockSpec((seqs, CHUNK, lora_w), lambda b, t: (b, t, lora_blk)),
        pl.BlockSpec((1, c), lambda b, t: (0, 0)),
        pl.BlockSpec((1, c), lambda b, t: (0, 0)),
    ] + [vec_spec] * 5 + [pl.BlockSpec(w.shape, lambda b, t: (0, 0)) for w in loras]
    tile_f32 = pltpu.VMEM((seqs * n_tiles, CHUNK, LANES), F32)
    y, *cast = pl.pallas_call(
        functools.partial(_rwkv_kernel, seqs=seqs, n_cast=len(casts), n_tiles=n_tiles, dlp=dlp, alp=alp),
        grid=(batch // seqs, nc),
        in_specs=in_specs + cast_in,
        out_specs=[pl.BlockSpec((seqs, CHUNK, c), lambda b, t: (b, t, 0))] + cast_out,
        out_shape=[jax.ShapeDtypeStruct((batch, seq, c), BF16)] + cast_shapes,
        scratch_shapes=[pltpu.VMEM((seqs * n_tiles, LANES, LANES), F32)] + [tile_f32] * 7,
        compiler_params=_cparams(2),
        name="rwkv",
    )(z3, z3, z3, z3, *vecs, *loras, *casts)
    return y.reshape(n, c), cast


def _swa_kernel(sink_ref, q_ref, kvc_ref, kvp_ref, cosc_ref, sinc_ref, cosp_ref, sinp_ref, *rest,
                n_tiles, group, qb):
    n_cast = len(rest) // 2
    o_ref = rest[n_cast]
    _cast_blocks(rest[:n_cast], rest[n_cast + 1:])
    nblk = pl.program_id(1)
    lane = lax.broadcasted_iota(jnp.int32, (1, LANES), 1)
    lo = lane < HEAD_DIM
    rot_lo = jnp.bitwise_and(lane, HEAD_DIM - 1) < HEAD_DIM // 2

    def rope(x, cos, sin_signed):
        partner = jnp.where(rot_lo, pltpu.roll(x, LANES - HEAD_DIM // 2, 1), pltpu.roll(x, HEAD_DIM // 2, 1))
        return x * cos + partner * sin_signed

    cosc = cosc_ref[...]
    sinc = sinc_ref[...]
    kvc = kvc_ref[...]
    kvp = kvp_ref[...]
    keys = jnp.concatenate([rope(kvp[:, :LANES], cosp_ref[...], sinp_ref[...]),
                            rope(kvc[:, :LANES], cosc, sinc)], axis=0)
    vals = jnp.concatenate([kvp[:, LANES:], kvc[:, LANES:]], axis=0)
    keys_sw = pltpu.roll(keys, HEAD_DIM, 1)
    vals_sw = pltpu.roll(vals, HEAD_DIM, 1)
    k_first = [jnp.where(lo, keys, 0.0).astype(BF16), jnp.where(lo, keys_sw, 0.0).astype(BF16)]
    k_second = [jnp.where(lo, 0.0, keys_sw).astype(BF16), jnp.where(lo, 0.0, keys).astype(BF16)]
    ones_kv = jnp.ones(vals.shape, BF16)
    with_ones = lambda x: jnp.concatenate([x.astype(BF16), ones_kv], axis=1)
    v_first = [with_ones(jnp.where(lo, vals, 0.0)), with_ones(jnp.where(lo, vals_sw, 0.0))]
    v_second = [with_ones(jnp.where(lo, 0.0, vals_sw)), with_ones(jnp.where(lo, 0.0, vals))]
    qi = lax.broadcasted_iota(jnp.int32, (BLOCK, 2 * BLOCK), 0)
    ki = lax.broadcasted_iota(jnp.int32, (BLOCK, 2 * BLOCK), 1)
    window = (ki > qi) & (ki <= qi + BLOCK)
    valid = [window & ((nblk > 0) | (ki >= BLOCK))] + [window] * (qb - 1)
    scale = HEAD_DIM ** -0.5
    heads = [(j, p, half) for j in range(qb) for p in range(n_tiles) for half in range(HEADS_PER_TILE)]
    scores = []
    for j in range(qb):
        qrows = slice(j * BLOCK, (j + 1) * BLOCK)
        krows = slice(j * BLOCK, (j + 2) * BLOCK)
        for p in range(n_tiles):
            g = (p * HEADS_PER_TILE) // group
            qp = (rope(q_ref[qrows, p * LANES:(p + 1) * LANES], cosc[qrows], sinc[qrows]) * scale).astype(BF16)
            scores += [_dot_nt(qp, k_first[g][krows]), _dot_nt(qp, k_second[g][krows])]
    probs, sink_terms = [], []
    for (j, p, half), s in zip(heads, scores):
        s = jnp.where(valid[j], s, NEG_INF)
        sink = sink_ref[p * HEADS_PER_TILE + half]
        m = jnp.maximum(jnp.max(s, axis=-1, keepdims=True), sink)
        probs.append(jnp.exp(s - m).astype(BF16))
        sink_terms.append(jnp.exp(sink - m))
    outs = []
    for (j, p, half), e, st in zip(heads, probs, sink_terms):
        g = (p * HEADS_PER_TILE) // group
        krows = slice(j * BLOCK, (j + 2) * BLOCK)
        pv = _dot(e, (v_second[g] if half else v_first[g])[krows])
        outs.append(pv[:, :LANES] * (1.0 / (pv[:, LANES:] + st)))
    for j in range(qb):
        for p in range(n_tiles):
            i = (j * n_tiles + p) * HEADS_PER_TILE
            o_ref[j * BLOCK:(j + 1) * BLOCK, p * LANES:(p + 1) * LANES] = (outs[i] + outs[i + 1]).astype(BF16)


def _swa(z, sinks, cos_t, sin_t, batch, seq, q_off, qw, kv_off, kvw, group, casts):
    n = z.shape[0]
    nb = seq // BLOCK
    assert q_off % qw == 0 and kv_off % kvw == 0 and kvw == 2 * LANES
    q_blk, kv_blk = q_off // qw, kv_off // kvw
    qb = _pick(nb, (4, 2, 1))
    steps = nb // qb
    prev = lambda t: jnp.maximum(t * qb - 1, 0)
    cast_in, cast_out, cast_shapes = _cast_specs(casts, batch * steps, lambda b, t: b * steps + t)
    y, *cast = pl.pallas_call(
        functools.partial(_swa_kernel, n_tiles=qw // LANES, group=group, qb=qb),
        grid=(batch, steps),
        in_specs=[
            pl.BlockSpec(memory_space=pltpu.SMEM),
            pl.BlockSpec((qb * BLOCK, qw), lambda b, t: (b * steps + t, q_blk)),
            pl.BlockSpec((qb * BLOCK, kvw), lambda b, t: (b * steps + t, kv_blk)),
            pl.BlockSpec((BLOCK, kvw), lambda b, t: (b * nb + prev(t), kv_blk)),
            pl.BlockSpec((qb * BLOCK, LANES), lambda b, t: (t, 0)),
            pl.BlockSpec((qb * BLOCK, LANES), lambda b, t: (t, 0)),
            pl.BlockSpec((BLOCK, LANES), lambda b, t: (prev(t), 0)),
            pl.BlockSpec((BLOCK, LANES), lambda b, t: (prev(t), 0)),
        ] + cast_in,
        out_specs=[pl.BlockSpec((qb * BLOCK, qw), lambda b, t: (b * steps + t, 0))] + cast_out,
        out_shape=[jax.ShapeDtypeStruct((n, qw), BF16)] + cast_shapes,
        compiler_params=_cparams(2),
        name="swa",
    )(sinks, z, z, z, cos_t, sin_t, cos_t, sin_t, *casts)
    return y, cast


def _memkv_kernel(m_ref, g_ref, w_ref, o_ref):
    h = _rms(m_ref[...], g_ref[...]).astype(BF16)
    o_ref[...] = _dot(h, w_ref[...].astype(BF16)).astype(BF16)


def _memkv(mem, g, w):
    n, d = mem.shape
    ncols = w.shape[1]
    tm = _pick(n, (512, 256, 128))
    tn = _pick(ncols, (1024, 512, 256, 128))
    return pl.pallas_call(
        _memkv_kernel,
        grid=(ncols // tn, n // tm),
        in_specs=[
            pl.BlockSpec((tm, d), lambda j, i: (i, 0)),
            pl.BlockSpec((1, d), lambda j, i: (0, 0)),
            pl.BlockSpec((d, tn), lambda j, i: (0, j)),
        ],
        out_specs=pl.BlockSpec((tm, tn), lambda j, i: (i, j)),
        out_shape=jax.ShapeDtypeStruct((n, ncols), BF16),
        compiler_params=_cparams(2),
        name="memkv",
    )(mem, g.reshape(1, d), w)


def _mix_kernel(x_ref, yr_ref, ys_ref, wo_ref, bo_ref, g_ref, wq_ref, k_ref, v_ref, wxo_ref, o_ref):
    c = yr_ref.shape[1]
    d = x_ref.shape[1]
    hd = d // XATTN_HEADS
    x2 = x_ref[...] + _dot(yr_ref[...], wo_ref[:c, :]) + _dot(ys_ref[...], wo_ref[c:, :]) + bo_ref[...]
    q = _dot(_rms(x2, g_ref[...]).astype(BF16), wq_ref[...]).astype(BF16)
    scale = hd ** -0.5
    heads = [slice(h * hd, (h + 1) * hd) for h in range(XATTN_HEADS)]
    scores = [_dot_nt(q[:, sl], k_ref[:, sl]) * scale for sl in heads]
    probs, dens = [], []
    for s in scores:
        e = jnp.exp(s - jnp.max(s, axis=-1, keepdims=True))
        probs.append(e.astype(BF16))
        dens.append(jnp.sum(e, axis=-1, keepdims=True))
    outs = [(_dot(e, v_ref[:, sl]) * (1.0 / den)).astype(BF16) for e, den, sl in zip(probs, dens, heads)]
    o_ref[...] = x2 + _dot(jnp.concatenate(outs, axis=1), wxo_ref[...])


def _mix(x, yr, ys, wo, bo, g, wq, kv, wxo, seq, mlen):
    n, d = x.shape
    c = yr.shape[1]
    tm = _pick(seq, (256, 128))
    per_seq = seq // tm
    once = pl.Buffered(1)
    return pl.pallas_call(
        _mix_kernel,
        grid=(n // tm,),
        in_specs=[
            pl.BlockSpec((tm, d), lambda i: (i, 0)),
            pl.BlockSpec((tm, c), lambda i: (i, 0)),
            pl.BlockSpec((tm, d - c), lambda i: (i, 0)),
            pl.BlockSpec((d, d), lambda i: (0, 0), pipeline_mode=once),
            pl.BlockSpec((1, d), lambda i: (0, 0)),
            pl.BlockSpec((1, d), lambda i: (0, 0)),
            pl.BlockSpec((d, d), lambda i: (0, 0), pipeline_mode=once),
            pl.BlockSpec((mlen, d), lambda i: (i // per_seq, 0)),
            pl.BlockSpec((mlen, d), lambda i: (i // per_seq, 1)),
            pl.BlockSpec((d, d), lambda i: (0, 0), pipeline_mode=once),
        ],
        out_specs=pl.BlockSpec((tm, d), lambda i: (i, 0)),
        out_shape=jax.ShapeDtypeStruct((n, d), F32),
        compiler_params=_cparams(1),
        name="mix",
    )(x, yr, ys, wo, bo.reshape(1, d), g.reshape(1, d), wq, kv, kv, wxo)


def _pad_cols(w, width):
    return jnp.pad(w, ((0, 0), (0, width - w.shape[1])))


def _pad_rows(w, height):
    return jnp.pad(w, ((0, height - w.shape[0]), (0, 0)))


def _rope_tables(seq):
    half = HEAD_DIM // 2
    lane = jnp.arange(LANES)
    inv_freq = ROPE_THETA ** (-jnp.arange(0, HEAD_DIM, 2, dtype=F32) / HEAD_DIM)
    ang = jnp.arange(seq, dtype=F32)[:, None] * inv_freq[lane % half][None, :]
    sign = jnp.where((lane % HEAD_DIM) < half, -1.0, 1.0)
    return jnp.cos(ang), jnp.sin(ang) * sign[None, :]


def kernel(x, mem, f1_norm, f1_gate, f1_up, f1_down, mix_norm, w_in, b_in_attn, rw_mu, rw_w0, rw_decay_up, rw_a0, rw_aaa_up, rw_gate_up, rw_k_k, rw_k_a, rw_r_k, rw_lnx_w, rw_lnx_b, attn_sinks, w_out, b_out, xa_norm, mem_norm, w_xq, w_xkv, w_xo, f2_norm, f2_gate, f2_up, f2_down, final_norm):
    batch, seq, d = x.shape
    mlen = mem.shape[1]
    depth = f1_norm.shape[0]
    c = rw_w0.shape[1]
    sw = d - c
    dl, al, gl = rw_decay_up.shape[1], rw_aaa_up.shape[1], rw_gate_up.shape[1]
    dlp, alp, glp = (_round_up(v, LANES) for v in (dl, al, gl))
    kvw = b_in_attn.shape[1] - sw
    q_heads = sw // HEAD_DIM
    group = q_heads // (kvw // (2 * HEAD_DIM))
    n_tiles = c // LANES
    n = batch * seq
    q_off = 3 * c
    lora_off = q_off + sw
    lora_w = dlp + alp + glp
    kv_off = lora_off + lora_w
    ncols = _round_up(kv_off + kvw, 2 * LANES)
    cos_t, sin_t = _rope_tables(seq)

    xf = x.reshape(n, d)
    memf = mem.reshape(batch * mlen, d)
    for l in range(depth):
        o1, o2, o3 = 3 * c, 3 * c + dl, 3 * c + dl + al
        shift = o3 + gl
        pieces = [(0, o1, 0), (shift, sw, 0), (o1, dl, dlp - dl), (o2, al, alp - al), (o3, gl, glp - gl),
                  (shift + sw, kvw, ncols - kv_off - kvw)]
        xf, w_all = _ffn(xf, f1_norm[l], f1_gate[l].astype(BF16), f1_up[l].astype(BF16), f1_down[l].astype(BF16),
                         None, relayout=(w_in[l], pieces))
        mu = rw_mu[l][None, :]
        mu_all = _pad_cols(jnp.concatenate([
            mu[:, :o1], jnp.zeros((1, sw), F32),
            _pad_cols(mu[:, o1:o2], dlp), _pad_cols(mu[:, o2:o3], alp), _pad_cols(mu[:, o3:shift], glp)],
            axis=1), ncols)
        bia = b_in_attn[l][None, :]
        b_all = _pad_cols(jnp.concatenate([
            jnp.zeros((1, q_off), F32), bia[:, :sw], jnp.zeros((1, lora_w), F32), bia[:, sw:]], axis=1), ncols)
        z = _proj(xf, mix_norm[l], w_all, mu_all, b_all, seq)

        tiles = lambda v: v.reshape(n_tiles, 1, LANES)
        vecs = [rw_w0[l].reshape(1, c), rw_a0[l].reshape(1, c), tiles(rw_k_k[l]), tiles(rw_k_a[l]),
                tiles(rw_r_k[l]), tiles(rw_lnx_w[l]), tiles(rw_lnx_b[l])]
        loras = [_pad_rows(rw_decay_up[l], dlp).astype(BF16), _pad_rows(rw_aaa_up[l], alp).astype(BF16),
                 _pad_rows(rw_gate_up[l], glp).astype(BF16)]
        y_rwkv, (f2_down_b, w_out_b, w_xq_b, w_xo_b) = _rwkv(
            z, batch, seq, c, lora_off, lora_w, dlp, alp, vecs, loras,
            [f2_down[l], w_out[l], w_xq[l], w_xo[l]])
        y_swa, (f2_gate_b, f2_up_b) = _swa(z, attn_sinks[l], cos_t, sin_t, batch, seq, q_off, sw, kv_off, kvw,
                                           group, [f2_gate[l], f2_up[l]])

        kv_mem = _memkv(memf, mem_norm[l], w_xkv[l])
        xf = _mix(xf, y_rwkv, y_swa, w_out_b, b_out[l], xa_norm[l], w_xq_b, kv_mem, w_xo_b, seq, mlen)

        last = l == depth - 1
        xf = _ffn(xf, f2_norm[l], f2_gate_b, f2_up_b, f2_down_b, final_norm if last else None)
    return xf.reshape(batch, seq, d)
```

```python
import functools

import jax
import jax.numpy as jnp
from jax import lax
from jax.experimental import pallas as pl
from jax.experimental.pallas import tpu as pltpu

F32, BF16 = jnp.float32, jnp.bfloat16

LANES = 128
SUBLANES = 8
VMEM_LIMIT_BYTES = 56 * 1024 * 1024

HEAD_DIM = 64
HEADS_PER_TILE = LANES // HEAD_DIM
CHUNK = 128
BLOCK = 128
XATTN_HEADS = 4
RMS_EPS = 1e-6
GN_EPS = 64e-5
NEG_INF = -1e30
ROPE_THETA = 10000.0
EXP_M_HALF = 0.6065306597126334


def _round_up(n, m):
    return (n + m - 1) // m * m


def _pick(n, prefs):
    for p in prefs:
        if n % p == 0:
            return p
    raise ValueError(f"no tile in {prefs} divides {n}")


def _cparams(n_axes):
    return pltpu.CompilerParams(dimension_semantics=("arbitrary",) * n_axes,
                                vmem_limit_bytes=VMEM_LIMIT_BYTES)


def _dot(a, b):
    return jnp.dot(a, b, preferred_element_type=F32)


def _dot_nt(a, b):
    return lax.dot_general(a, b, (((1,), (1,)), ((), ())), preferred_element_type=F32)


def _cast_specs(weights, nsteps, flat_step):
    in_specs, out_specs, out_shapes = [], [], []
    for w in weights:
        units = w.shape[0] // (2 * SUBLANES)
        nblk = max(d for d in range(1, nsteps + 1) if units % d == 0)
        imap = lambda *g, nblk=nblk: (jnp.minimum(flat_step(*g), nblk - 1), 0)
        spec = pl.BlockSpec((w.shape[0] // nblk, w.shape[1]), imap)
        in_specs.append(spec)
        out_specs.append(spec)
        out_shapes.append(jax.ShapeDtypeStruct(w.shape, BF16))
    return in_specs, out_specs, out_shapes


def _cast_blocks(src_refs, dst_refs):
    for src, dst in zip(src_refs, dst_refs):
        dst[...] = src[...].astype(BF16)


def _rms(x, g):
    ms = jnp.mean(x * x, axis=-1, keepdims=True)
    return x * lax.rsqrt(ms + RMS_EPS) * g


def _ffn_kernel(x_ref, g_ref, wg_ref, wu_ref, wd_ref, fg_ref, o_ref, h_ref, *, final_norm, rows):
    j = pl.program_id(1)
    tm = x_ref.shape[0]

    @pl.when(j == 0)
    def _():
        for r0 in range(0, tm, rows):
            x = x_ref[r0:r0 + rows, :]
            h_ref[r0:r0 + rows, :] = _rms(x, g_ref[...]).astype(BF16)
            o_ref[r0:r0 + rows, :] = x

    for r0 in range(0, tm, rows):
        h = h_ref[r0:r0 + rows, :]
        gate = _dot(h, wg_ref[...])
        up = _dot(h, wu_ref[...])
        act = (gate * jax.nn.sigmoid(gate) * up).astype(BF16)
        o_ref[r0:r0 + rows, :] += 0.5 * _dot(act, wd_ref[...])

    if final_norm:
        @pl.when(j == pl.num_programs(1) - 1)
        def _():
            for r0 in range(0, tm, rows):
                o_ref[r0:r0 + rows, :] = _rms(o_ref[r0:r0 + rows, :], fg_ref[...])


def _ffn(x, g, wg, wu, wd, fg):
    n, d = x.shape
    f = wg.shape[1]
    tm = _pick(n, (1024, 512, 256, 128))
    tf = _pick(f, (512, 256, 128))
    rows = min(tm, 512)
    final_norm = fg is not None
    fg = g if fg is None else fg
    return pl.pallas_call(
        functools.partial(_ffn_kernel, final_norm=final_norm, rows=rows),
        grid=(n // tm, f // tf),
        in_specs=[
            pl.BlockSpec((tm, d), lambda i, j: (i, 0)),
            pl.BlockSpec((1, d), lambda i, j: (0, 0)),
            pl.BlockSpec((d, tf), lambda i, j: (0, j)),
            pl.BlockSpec((d, tf), lambda i, j: (0, j)),
            pl.BlockSpec((tf, d), lambda i, j: (j, 0)),
            pl.BlockSpec((1, d), lambda i, j: (0, 0)),
        ],
        out_specs=pl.BlockSpec((tm, d), lambda i, j: (i, 0)),
        out_shape=jax.ShapeDtypeStruct((n, d), F32),
        scratch_shapes=[pltpu.VMEM((tm, d), BF16)],
        compiler_params=_cparams(2),
        name="ffn",
    )(x, g.reshape(1, d), wg, wu, wd, fg.reshape(1, d))


def _proj_kernel(x_ref, g_ref, w_ref, mu_ref, b_ref, o_ref, carry_ref, *, tiles_per_seq, tn):
    i = pl.program_id(0)
    tm = x_ref.shape[0]
    ncols = w_ref.shape[1]

    @pl.when(i % tiles_per_seq == 0)
    def _():
        carry_ref[...] = jnp.zeros(carry_ref.shape, F32)

    h = _rms(x_ref[...], g_ref[...]).astype(BF16)
    row = lax.broadcasted_iota(jnp.int32, (SUBLANES, tn), 0)
    for c0 in range(0, ncols, tn):
        cs = slice(c0, c0 + tn)
        z = _dot(h, w_ref[:, cs])
        prev_tail = carry_ref[:, cs]
        carry_ref[:, cs] = z[tm - SUBLANES:, :]
        zs = pltpu.roll(z, 1, 0)
        head = jnp.where(row == 0, pltpu.roll(prev_tail, 1, 0), zs[:SUBLANES])
        mu = mu_ref[:, cs]
        b = b_ref[:, cs]
        z0 = z[:SUBLANES]
        o_ref[:SUBLANES, cs] = z0 + (head - z0) * mu + b
        z1 = z[SUBLANES:]
        o_ref[SUBLANES:, cs] = z1 + (zs[SUBLANES:] - z1) * mu + b


def _proj(x, g, w_all, mu_all, b_all, seq):
    n, d = x.shape
    ncols = w_all.shape[1]
    tm = _pick(seq, (512, 256, 128))
    tn = _pick(ncols, (1024, 512, 256, 128))
    return pl.pallas_call(
        functools.partial(_proj_kernel, tiles_per_seq=seq // tm, tn=tn),
        grid=(n // tm,),
        in_specs=[
            pl.BlockSpec((tm, d), lambda i: (i, 0)),
            pl.BlockSpec((1, d), lambda i: (0, 0)),
            pl.BlockSpec((d, ncols), lambda i: (0, 0), pipeline_mode=pl.Buffered(1)),
            pl.BlockSpec((1, ncols), lambda i: (0, 0)),
            pl.BlockSpec((1, ncols), lambda i: (0, 0)),
        ],
        out_specs=pl.BlockSpec((tm, ncols), lambda i: (i, 0)),
        out_shape=jax.ShapeDtypeStruct((n, ncols), F32),
        scratch_shapes=[pltpu.VMEM((SUBLANES, ncols), F32)],
        compiler_params=_cparams(1),
        name="proj",
    )(x, g.reshape(1, d), w_all, mu_all, b_all)


def _bdot(a, b):
    return lax.dot_general(a, b, (((2,), (1,)), ((0,), (0,))), preferred_element_type=F32)


def _bdot_nt(a, b):
    return lax.dot_general(a, b, (((2,), (2,)), ((0,), (0,))), preferred_element_type=F32)


def _inv_unit_lower(nmat):
    L = nmat.shape[-1]
    r = lax.broadcasted_iota(jnp.int32, (L, L), 0)
    c = lax.broadcasted_iota(jnp.int32, (L, L), 1)
    eye = jnp.where(r == c, 1.0, 0.0)
    t = eye + nmat
    pw = nmat.astype(BF16)
    pw = _bdot(pw, pw).astype(BF16)
    steps = L.bit_length() - 2
    for i in range(steps):
        if i + 1 < steps:
            both = _bdot(jnp.concatenate([t.astype(BF16), pw], axis=1), pw)
            t = t + both[:, :L]
            pw = both[:, L:].astype(BF16)
        else:
            t = t + _bdot(t.astype(BF16), pw)
    return t


RWKV_INPUTS = 14


def _rwkv_kernel(*refs, seqs, n_cast, **static):
    cast_in = refs[RWKV_INPUTS:RWKV_INPUTS + n_cast]
    cast_out = refs[RWKV_INPUTS + n_cast + 1:RWKV_INPUTS + 2 * n_cast + 1]
    seq_refs = refs[:RWKV_INPUTS] + (refs[RWKV_INPUTS + n_cast],) + refs[RWKV_INPUTS + 2 * n_cast + 1:]
    s_ref = seq_refs[RWKV_INPUTS + 1]

    @pl.when(pl.program_id(1) == 0)
    def _():
        s_ref[...] = jnp.zeros(s_ref.shape, F32)

    for q in range(seqs):
        _rwkv_sequence(q, *seq_refs, **static)
    _cast_blocks(cast_in, cast_out)


def _rwkv_sequence(q, r_ref, k_ref, v_ref, lora_ref, w0_ref, a0_ref, kk_ref, ka_ref, rk_ref, gw_ref, gb_ref,
                   du_ref, au_ref, gu_ref,
                   o_ref,
                   s_ref, pr_ref, pk_ref, pv_ref, pa_ref, pg_ref, plp_ref, plw_ref,
                   *, n_tiles, dlp, alp):
    L = CHUNK
    tiles = slice(q * n_tiles, (q + 1) * n_tiles)

    lora = lora_ref[q]
    wd = jnp.tanh(lora[:, :dlp])
    ad = lora[:, dlp:dlp + alp]
    gd = jax.nn.sigmoid(lora[:, dlp + alp:])
    lw = -EXP_M_HALF * jax.nn.sigmoid(w0_ref[...] + _dot(wd.astype(BF16), du_ref[...]))
    asig = jax.nn.sigmoid(a0_ref[...] + _dot(ad.astype(BF16), au_ref[...]))
    gate = _dot(gd.astype(BF16), gu_ref[...])
    row = lax.broadcasted_iota(jnp.int32, (L, L), 0)
    col = lax.broadcasted_iota(jnp.int32, (L, L), 1)
    incl = row >= col
    strict = row > col
    tril = jnp.where(incl, 1.0, 0.0).astype(BF16)
    h1 = lw.astype(BF16)
    h2 = (lw - h1.astype(F32)).astype(BF16)
    logp = _dot(tril, h1) + _dot(tril, h2)
    for p in range(n_tiles):
        sl = slice(p * LANES, (p + 1) * LANES)
        i = q * n_tiles + p
        pr_ref[i] = r_ref[q, :, sl]
        pk_ref[i] = k_ref[q, :, sl]
        pv_ref[i] = v_ref[q, :, sl]
        pa_ref[i] = asig[:, sl]
        pg_ref[i] = gate[:, sl]
        plp_ref[i] = logp[:, sl]
        plw_ref[i] = lw[:, sl]

    lane = lax.broadcasted_iota(jnp.int32, (1, LANES), 1)
    lo = lane < HEAD_DIM
    same_head = jnp.where(row < HEAD_DIM, 0, 1) == jnp.where(col < HEAD_DIM, 0, 1)
    inv_hd = 1.0 / HEAD_DIM

    def first(x):
        return jnp.where(lo, x, 0.0)

    def second(x):
        return jnp.where(lo, 0.0, x)

    def segsum(x):
        return jnp.where(lo, jnp.sum(first(x), axis=-1, keepdims=True),
                         jnp.sum(second(x), axis=-1, keepdims=True))

    r = pr_ref[tiles]
    k = pk_ref[tiles]
    v = pv_ref[tiles]
    a_s = pa_ref[tiles]
    lp = plp_ref[tiles]
    kk = k * kk_ref[...]
    ss = segsum(kk * kk)
    kk = kk * lax.rsqrt(jnp.maximum(ss, 1e-24))
    a = -kk
    b = kk * a_s
    km = k * (1.0 + (a_s - 1.0) * ka_ref[...])
    cmid = lp[:, L // 2 - 1:L // 2, :]
    clast = lp[:, L - 1:L, :]
    lpe = lp - plw_ref[tiles]
    e_inv = jnp.exp(cmid - lp)
    e_mid = jnp.exp(cmid)
    at = a * jnp.exp(lpe - cmid)
    a_abs = at * e_mid
    rt = r * jnp.exp(lp - cmid)
    r_abs = rt * e_mid
    bt = b * e_inv
    kt = km * e_inv
    e_l = jnp.exp(clast - cmid)
    e_p = jnp.exp(clast)
    lhs = jnp.concatenate([first(at), second(at), first(rt), second(rt)], axis=1).astype(BF16)
    gmat = _bdot_nt(lhs, jnp.concatenate([bt, kt], axis=1).astype(BF16))
    n_lo = jnp.where(strict, gmat[:, 0:L, 0:L], 0.0)
    ak_lo = jnp.where(strict, gmat[:, 0:L, L:], 0.0)
    n_hi = jnp.where(strict, gmat[:, L:2 * L, 0:L], 0.0)
    ak_hi = jnp.where(strict, gmat[:, L:2 * L, L:], 0.0)
    rb_lo = jnp.where(incl, gmat[:, 2 * L:3 * L, 0:L], 0.0)
    rk_lo = jnp.where(incl, gmat[:, 2 * L:3 * L, L:], 0.0)
    rb_hi = jnp.where(incl, gmat[:, 3 * L:, 0:L], 0.0)
    rk_hi = jnp.where(incl, gmat[:, 3 * L:, L:], 0.0)
    t_all = _inv_unit_lower(jnp.concatenate([n_lo, n_hi], axis=0))
    t_lo, t_hi = t_all[:n_tiles], t_all[n_tiles:]
    s = s_ref[tiles]
    s_b = s.astype(BF16)
    v_lohi = jnp.concatenate([first(v), second(v)], axis=1).astype(BF16)
    rhs = _bdot_nt(a_abs.astype(BF16), s_b) + _bdot(jnp.concatenate([ak_lo, ak_hi], axis=2).astype(BF16), v_lohi)
    u = _bdot(jnp.concatenate([t_lo, t_hi], axis=2).astype(BF16),
              jnp.concatenate([first(rhs), second(rhs)], axis=1).astype(BF16))
    u_lohi = jnp.concatenate([first(u), second(u)], axis=1).astype(BF16)
    y = _bdot_nt(r_abs.astype(BF16), s_b) + _bdot(
        jnp.concatenate([rb_lo, rb_hi, rk_lo, rk_hi], axis=2).astype(BF16),
        jnp.concatenate([u_lohi, v_lohi], axis=1))
    upd = _bdot(jnp.concatenate([jnp.swapaxes(u, 1, 2), jnp.swapaxes(v, 1, 2)], axis=2).astype(BF16),
                jnp.concatenate([bt * e_l, kt * e_l], axis=1).astype(BF16))
    s_ref[tiles] = s * e_p + jnp.where(same_head, upd, 0.0)
    mean = segsum(y) * inv_hd
    dev = y - mean
    var = segsum(dev * dev) * inv_hd
    yn = dev * lax.rsqrt(var + GN_EPS) * gw_ref[...] + gb_ref[...]
    bonus = segsum(r * km * rk_ref[...]) * v
    out = ((yn + bonus) * pg_ref[tiles]).astype(BF16)
    for p in range(n_tiles):
        o_ref[q, :, p * LANES:(p + 1) * LANES] = out[p]


def _rwkv(z, batch, seq, c, lora_off, lora_w, dlp, alp, vecs, loras, casts):
    n = z.shape[0]
    nc = seq // CHUNK
    n_tiles = c // LANES
    seqs = 1
    assert lora_off % lora_w == 0 and len(vecs) + len(loras) + 4 == RWKV_INPUTS
    lora_blk = lora_off // lora_w
    z3 = z.reshape(batch, seq, z.shape[1])
    cast_in, cast_out, cast_shapes = _cast_specs(casts, (batch // seqs) * nc, lambda b, t: b * nc + t)
    vec_spec = pl.BlockSpec((n_tiles, 1, LANES), lambda b, t: (0, 0, 0))
    in_specs = [
        pl.BlockSpec((seqs, CHUNK, c), lambda b, t: (b, t, 0)),
        pl.BlockSpec((seqs, CHUNK, c), lambda b, t: (b, t, 1)),
        pl.BlockSpec((seqs, CHUNK, c), lambda b, t: (b, t, 2)),
        pl.BlockSpec((seqs, CHUNK, lora_w), lambda b, t: (b, t, lora_blk)),
        pl.BlockSpec((1, c), lambda b, t: (0, 0)),
        pl.BlockSpec((1, c), lambda b, t: (0, 0)),
    ] + [vec_spec] * 5 + [pl.BlockSpec(w.shape, lambda b, t: (0, 0)) for w in loras]
    tile_f32 = pltpu.VMEM((seqs * n_tiles, CHUNK, LANES), F32)
    y, *cast = pl.pallas_call(
        functools.partial(_rwkv_kernel, seqs=seqs, n_cast=len(casts), n_tiles=n_tiles, dlp=dlp, alp=alp),
        grid=(batch // seqs, nc),
        in_specs=in_specs + cast_in,
        out_specs=[pl.BlockSpec((seqs, CHUNK, c), lambda b, t: (b, t, 0))] + cast_out,
        out_shape=[jax.ShapeDtypeStruct((batch, seq, c), BF16)] + cast_shapes,
        scratch_shapes=[pltpu.VMEM((seqs * n_tiles, LANES, LANES), F32)] + [tile_f32] * 7,
        compiler_params=_cparams(2),
        name="rwkv",
    )(z3, z3, z3, z3, *vecs, *loras, *casts)
    return y.reshape(n, c), cast


def _swa_kernel(sink_ref, q_ref, kvc_ref, kvp_ref, cosc_ref, sinc_ref, cosp_ref, sinp_ref, *rest,
                n_tiles, group, qb):
    n_cast = len(rest) // 2
    o_ref = rest[n_cast]
    _cast_blocks(rest[:n_cast], rest[n_cast + 1:])
    nblk = pl.program_id(1)
    lane = lax.broadcasted_iota(jnp.int32, (1, LANES), 1)
    lo = lane < HEAD_DIM
    rot_lo = jnp.bitwise_and(lane, HEAD_DIM - 1) < HEAD_DIM // 2

    def rope(x, cos, sin_signed):
        partner = jnp.where(rot_lo, pltpu.roll(x, LANES - HEAD_DIM // 2, 1), pltpu.roll(x, HEAD_DIM // 2, 1))
        return x * cos + partner * sin_signed

    cosc = cosc_ref[...]
    sinc = sinc_ref[...]
    kvc = kvc_ref[...]
    kvp = kvp_ref[...]
    keys = jnp.concatenate([rope(kvp[:, :LANES], cosp_ref[...], sinp_ref[...]),
                            rope(kvc[:, :LANES], cosc, sinc)], axis=0)
    vals = jnp.concatenate([kvp[:, LANES:], kvc[:, LANES:]], axis=0)
    keys_sw = pltpu.roll(keys, HEAD_DIM, 1)
    vals_sw = pltpu.roll(vals, HEAD_DIM, 1)
    k_first = [jnp.where(lo, keys, 0.0).astype(BF16), jnp.where(lo, keys_sw, 0.0).astype(BF16)]
    k_second = [jnp.where(lo, 0.0, keys_sw).astype(BF16), jnp.where(lo, 0.0, keys).astype(BF16)]
    ones_kv = jnp.ones(vals.shape, BF16)
    with_ones = lambda x: jnp.concatenate([x.astype(BF16), ones_kv], axis=1)
    v_first = [with_ones(jnp.where(lo, vals, 0.0)), with_ones(jnp.where(lo, vals_sw, 0.0))]
    v_second = [with_ones(jnp.where(lo, 0.0, vals_sw)), with_ones(jnp.where(lo, 0.0, vals))]
    qi = lax.broadcasted_iota(jnp.int32, (BLOCK, 2 * BLOCK), 0)
    ki = lax.broadcasted_iota(jnp.int32, (BLOCK, 2 * BLOCK), 1)
    window = (ki > qi) & (ki <= qi + BLOCK)
    valid = [window & ((nblk > 0) | (ki >= BLOCK))] + [window] * (qb - 1)
    scale = HEAD_DIM ** -0.5
    heads = [(j, p, half) for j in range(qb) for p in range(n_tiles) for half in range(HEADS_PER_TILE)]
    scores = []
    for j in range(qb):
        qrows = slice(j * BLOCK, (j + 1) * BLOCK)
        krows = slice(j * BLOCK, (j + 2) * BLOCK)
        for p in range(n_tiles):
            g = (p * HEADS_PER_TILE) // group
            qp = (rope(q_ref[qrows, p * LANES:(p + 1) * LANES], cosc[qrows], sinc[qrows]) * scale).astype(BF16)
            scores += [_dot_nt(qp, k_first[g][krows]), _dot_nt(qp, k_second[g][krows])]
    probs, sink_terms = [], []
    for (j, p, half), s in zip(heads, scores):
        s = jnp.where(valid[j], s, NEG_INF)
        sink = sink_ref[p * HEADS_PER_TILE + half]
        m = jnp.maximum(jnp.max(s, axis=-1, keepdims=True), sink)
        probs.append(jnp.exp(s - m).astype(BF16))
        sink_terms.append(jnp.exp(sink - m))
    outs = []
    for (j, p, half), e, st in zip(heads, probs, sink_terms):
        g = (p * HEADS_PER_TILE) // group
        krows = slice(j * BLOCK, (j + 2) * BLOCK)
        pv = _dot(e, (v_second[g] if half else v_first[g])[krows])
        outs.append(pv[:, :LANES] * (1.0 / (pv[:, LANES:] + st)))
    for j in range(qb):
        for p in range(n_tiles):
            i = (j * n_tiles + p) * HEADS_PER_TILE
            o_ref[j * BLOCK:(j + 1) * BLOCK, p * LANES:(p + 1) * LANES] = (outs[i] + outs[i + 1]).astype(BF16)


def _swa(z, sinks, cos_t, sin_t, batch, seq, q_off, qw, kv_off, kvw, group, casts):
    n = z.shape[0]
    nb = seq // BLOCK
    assert q_off % qw == 0 and kv_off % kvw == 0 and kvw == 2 * LANES
    q_blk, kv_blk = q_off // qw, kv_off // kvw
    qb = _pick(nb, (4, 2, 1))
    steps = nb // qb
    prev = lambda t: jnp.maximum(t * qb - 1, 0)
    cast_in, cast_out, cast_shapes = _cast_specs(casts, batch * steps, lambda b, t: b * steps + t)
    y, *cast = pl.pallas_call(
        functools.partial(_swa_kernel, n_tiles=qw // LANES, group=group, qb=qb),
        grid=(batch, steps),
        in_specs=[
            pl.BlockSpec(memory_space=pltpu.SMEM),
            pl.BlockSpec((qb * BLOCK, qw), lambda b, t: (b * steps + t, q_blk)),
            pl.BlockSpec((qb * BLOCK, kvw), lambda b, t: (b * steps + t, kv_blk)),
            pl.BlockSpec((BLOCK, kvw), lambda b, t: (b * nb + prev(t), kv_blk)),
            pl.BlockSpec((qb * BLOCK, LANES), lambda b, t: (t, 0)),
            pl.BlockSpec((qb * BLOCK, LANES), lambda b, t: (t, 0)),
            pl.BlockSpec((BLOCK, LANES), lambda b, t: (prev(t), 0)),
            pl.BlockSpec((BLOCK, LANES), lambda b, t: (prev(t), 0)),
        ] + cast_in,
        out_specs=[pl.BlockSpec((qb * BLOCK, qw), lambda b, t: (b * steps + t, 0))] + cast_out,
        out_shape=[jax.ShapeDtypeStruct((n, qw), BF16)] + cast_shapes,
        compiler_params=_cparams(2),
        name="swa",
    )(sinks, z, z, z, cos_t, sin_t, cos_t, sin_t, *casts)
    return y, cast


def _memkv_kernel(m_ref, g_ref, w_ref, o_ref):
    h = _rms(m_ref[...], g_ref[...]).astype(BF16)
    o_ref[...] = _dot(h, w_ref[...].astype(BF16)).astype(BF16)


def _memkv(mem, g, w):
    n, d = mem.shape
    ncols = w.shape[1]
    tm = _pick(n, (512, 256, 128))
    tn = _pick(ncols, (1024, 512, 256, 128))
    return pl.pallas_call(
        _memkv_kernel,
        grid=(ncols // tn, n // tm),
        in_specs=[
            pl.BlockSpec((tm, d), lambda j, i: (i, 0)),
            pl.BlockSpec((1, d), lambda j, i: (0, 0)),
            pl.BlockSpec((d, tn), lambda j, i: (0, j)),
        ],
        out_specs=pl.BlockSpec((tm, tn), lambda j, i: (i, j)),
        out_shape=jax.ShapeDtypeStruct((n, ncols), BF16),
        compiler_params=_cparams(2),
        name="memkv",
    )(mem, g.reshape(1, d), w)


def _mix_kernel(x_ref, yr_ref, ys_ref, wo_ref, bo_ref, g_ref, wq_ref, k_ref, v_ref, wxo_ref, o_ref):
    c = yr_ref.shape[1]
    d = x_ref.shape[1]
    hd = d // XATTN_HEADS
    x2 = x_ref[...] + _dot(yr_ref[...], wo_ref[:c, :]) + _dot(ys_ref[...], wo_ref[c:, :]) + bo_ref[...]
    q = _dot(_rms(x2, g_ref[...]).astype(BF16), wq_ref[...]).astype(BF16)
    scale = hd ** -0.5
    heads = [slice(h * hd, (h + 1) * hd) for h in range(XATTN_HEADS)]
    scores = [_dot_nt(q[:, sl], k_ref[:, sl]) * scale for sl in heads]
    probs, dens = [], []
    for s in scores:
        e = jnp.exp(s - jnp.max(s, axis=-1, keepdims=True))
        probs.append(e.astype(BF16))
        dens.append(jnp.sum(e, axis=-1, keepdims=True))
    outs = [(_dot(e, v_ref[:, sl]) * (1.0 / den)).astype(BF16) for e, den, sl in zip(probs, dens, heads)]
    o_ref[...] = x2 + _dot(jnp.concatenate(outs, axis=1), wxo_ref[...])


def _mix(x, yr, ys, wo, bo, g, wq, kv, wxo, seq, mlen):
    n, d = x.shape
    c = yr.shape[1]
    tm = _pick(seq, (256, 128))
    per_seq = seq // tm
    once = pl.Buffered(1)
    return pl.pallas_call(
        _mix_kernel,
        grid=(n // tm,),
        in_specs=[
            pl.BlockSpec((tm, d), lambda i: (i, 0)),
            pl.BlockSpec((tm, c), lambda i: (i, 0)),
            pl.BlockSpec((tm, d - c), lambda i: (i, 0)),
            pl.BlockSpec((d, d), lambda i: (0, 0), pipeline_mode=once),
            pl.BlockSpec((1, d), lambda i: (0, 0)),
            pl.BlockSpec((1, d), lambda i: (0, 0)),
            pl.BlockSpec((d, d), lambda i: (0, 0), pipeline_mode=once),
            pl.BlockSpec((mlen, d), lambda i: (i // per_seq, 0)),
            pl.BlockSpec((mlen, d), lambda i: (i // per_seq, 1)),
            pl.BlockSpec((d, d), lambda i: (0, 0), pipeline_mode=once),
        ],
        out_specs=pl.BlockSpec((tm, d), lambda i: (i, 0)),
        out_shape=jax.ShapeDtypeStruct((n, d), F32),
        compiler_params=_cparams(1),
        name="mix",
    )(x, yr, ys, wo, bo.reshape(1, d), g.reshape(1, d), wq, kv, kv, wxo)


def _pad_cols(w, width):
    return jnp.pad(w, ((0, 0), (0, width - w.shape[1])))


def _pad_rows(w, height):
    return jnp.pad(w, ((0, height - w.shape[0]), (0, 0)))


def _rope_tables(seq):
    half = HEAD_DIM // 2
    lane = jnp.arange(LANES)
    inv_freq = ROPE_THETA ** (-jnp.arange(0, HEAD_DIM, 2, dtype=F32) / HEAD_DIM)
    ang = jnp.arange(seq, dtype=F32)[:, None] * inv_freq[lane % half][None, :]
    sign = jnp.where((lane % HEAD_DIM) < half, -1.0, 1.0)
    return jnp.cos(ang), jnp.sin(ang) * sign[None, :]


def kernel(x, mem, f1_norm, f1_gate, f1_up, f1_down, mix_norm, w_in, b_in_attn, rw_mu, rw_w0, rw_decay_up, rw_a0, rw_aaa_up, rw_gate_up, rw_k_k, rw_k_a, rw_r_k, rw_lnx_w, rw_lnx_b, attn_sinks, w_out, b_out, xa_norm, mem_norm, w_xq, w_xkv, w_xo, f2_norm, f2_gate, f2_up, f2_down, final_norm):
    batch, seq, d = x.shape
    mlen = mem.shape[1]
    depth = f1_norm.shape[0]
    c = rw_w0.shape[1]
    sw = d - c
    dl, al, gl = rw_decay_up.shape[1], rw_aaa_up.shape[1], rw_gate_up.shape[1]
    dlp, alp, glp = (_round_up(v, LANES) for v in (dl, al, gl))
    kvw = b_in_attn.shape[1] - sw
    q_heads = sw // HEAD_DIM
    group = q_heads // (kvw // (2 * HEAD_DIM))
    n_tiles = c // LANES
    n = batch * seq
    q_off = 3 * c
    lora_off = q_off + sw
    lora_w = dlp + alp + glp
    kv_off = lora_off + lora_w
    ncols = _round_up(kv_off + kvw, 2 * LANES)
    cos_t, sin_t = _rope_tables(seq)

    xf = x.reshape(n, d)
    memf = mem.reshape(batch * mlen, d)
    for l in range(depth):
        xf = _ffn(xf, f1_norm[l], f1_gate[l].astype(BF16), f1_up[l].astype(BF16), f1_down[l].astype(BF16), None)

        wl = w_in[l]
        o1, o2, o3 = 3 * c, 3 * c + dl, 3 * c + dl + al
        shift = o3 + gl
        w_all = jnp.concatenate([
            wl[:, :o1], wl[:, shift:shift + sw],
            _pad_cols(wl[:, o1:o2], dlp), _pad_cols(wl[:, o2:o3], alp), _pad_cols(wl[:, o3:shift], glp),
            wl[:, shift + sw:]], axis=1)
        w_all = _pad_cols(w_all, ncols).astype(BF16)
        mu = rw_mu[l][None, :]
        mu_all = _pad_cols(jnp.concatenate([
            mu[:, :o1], jnp.zeros((1, sw), F32),
            _pad_cols(mu[:, o1:o2], dlp), _pad_cols(mu[:, o2:o3], alp), _pad_cols(mu[:, o3:shift], glp)],
            axis=1), ncols)
        bia = b_in_attn[l][None, :]
        b_all = _pad_cols(jnp.concatenate([
            jnp.zeros((1, q_off), F32), bia[:, :sw], jnp.zeros((1, lora_w), F32), bia[:, sw:]], axis=1), ncols)
        z = _proj(xf, mix_norm[l], w_all, mu_all, b_all, seq)

        tiles = lambda v: v.reshape(n_tiles, 1, LANES)
        vecs = [rw_w0[l].reshape(1, c), rw_a0[l].reshape(1, c), tiles(rw_k_k[l]), tiles(rw_k_a[l]),
                tiles(rw_r_k[l]), tiles(rw_lnx_w[l]), tiles(rw_lnx_b[l])]
        loras = [_pad_rows(rw_decay_up[l], dlp).astype(BF16), _pad_rows(rw_aaa_up[l], alp).astype(BF16),
                 _pad_rows(rw_gate_up[l], glp).astype(BF16)]
        y_rwkv, (f2_down_b, w_out_b, w_xq_b, w_xo_b) = _rwkv(
            z, batch, seq, c, lora_off, lora_w, dlp, alp, vecs, loras,
            [f2_down[l], w_out[l], w_xq[l], w_xo[l]])
        y_swa, (f2_gate_b, f2_up_b) = _swa(z, attn_sinks[l], cos_t, sin_t, batch, seq, q_off, sw, kv_off, kvw,
                                           group, [f2_gate[l], f2_up[l]])

        kv_mem = _memkv(memf, mem_norm[l], w_xkv[l])
        xf = _mix(xf, y_rwkv, y_swa, w_out_b, b_out[l], xa_norm[l], w_xq_b, kv_mem, w_xo_b, seq, mlen)

        last = l == depth - 1
        xf = _ffn(xf, f2_norm[l], f2_gate_b, f2_up_b, f2_down_b, final_norm if last else None)
    return xf.reshape(batch, seq, d)
```

```python
import functools

import jax
import jax.numpy as jnp
from jax import lax
from jax.experimental import pallas as pl
from jax.experimental.pallas import tpu as pltpu

F32, BF16 = jnp.float32, jnp.bfloat16

LANES = 128
SUBLANES = 8
VMEM_LIMIT_BYTES = 56 * 1024 * 1024

HEAD_DIM = 64
HEADS_PER_TILE = LANES // HEAD_DIM
CHUNK = 128
BLOCK = 128
XATTN_HEADS = 4
RMS_EPS = 1e-6
GN_EPS = 64e-5
NEG_INF = -1e30
ROPE_THETA = 10000.0
EXP_M_HALF = 0.6065306597126334
LOG2_E = 1.4426950408889634


def _round_up(n, m):
    return (n + m - 1) // m * m


def _pick(n, prefs):
    for p in prefs:
        if n % p == 0:
            return p
    raise ValueError(f"no tile in {prefs} divides {n}")


def _cparams(n_axes):
    return pltpu.CompilerParams(dimension_semantics=("arbitrary",) * n_axes,
                                vmem_limit_bytes=VMEM_LIMIT_BYTES)


def _dot(a, b):
    return jnp.dot(a, b, preferred_element_type=F32)


def _dot_nt(a, b):
    return lax.dot_general(a, b, (((1,), (1,)), ((), ())), preferred_element_type=F32)


def _cast_specs(weights, nsteps, flat_step):
    in_specs, out_specs, out_shapes = [], [], []
    for w in weights:
        units = w.shape[0] // (2 * SUBLANES)
        nblk = max(d for d in range(1, nsteps + 1) if units % d == 0)
        imap = lambda *g, nblk=nblk: (jnp.minimum(flat_step(*g), nblk - 1), 0)
        spec = pl.BlockSpec((w.shape[0] // nblk, w.shape[1]), imap)
        in_specs.append(spec)
        out_specs.append(spec)
        out_shapes.append(jax.ShapeDtypeStruct(w.shape, BF16))
    return in_specs, out_specs, out_shapes


def _cast_blocks(src_refs, dst_refs):
    for src, dst in zip(src_refs, dst_refs):
        dst[...] = src[...].astype(BF16)


def _rms(x, g):
    ms = jnp.mean(x * x, axis=-1, keepdims=True)
    return x * lax.rsqrt(ms + RMS_EPS) * g


def _ffn_kernel(x_ref, g_ref, wg_ref, wu_ref, wd_ref, fg_ref, o_ref, h_ref, *, final_norm, rows):
    j = pl.program_id(1)
    tm = x_ref.shape[0]

    @pl.when(j == 0)
    def _():
        for r0 in range(0, tm, rows):
            x = x_ref[r0:r0 + rows, :]
            h_ref[r0:r0 + rows, :] = _rms(x, g_ref[...]).astype(BF16)
            o_ref[r0:r0 + rows, :] = x

    for r0 in range(0, tm, rows):
        h = h_ref[r0:r0 + rows, :]
        gate = _dot(h, wg_ref[...])
        up = _dot(h, wu_ref[...])
        act = (gate * jax.nn.sigmoid(gate) * up).astype(BF16)
        o_ref[r0:r0 + rows, :] += 0.5 * _dot(act, wd_ref[...])

    if final_norm:
        @pl.when(j == pl.num_programs(1) - 1)
        def _():
            for r0 in range(0, tm, rows):
                o_ref[r0:r0 + rows, :] = _rms(o_ref[r0:r0 + rows, :], fg_ref[...])


def _ffn(x, g, wg, wu, wd, fg):
    n, d = x.shape
    f = wg.shape[1]
    tm = _pick(n, (1024, 512, 256, 128))
    tf = _pick(f, (512, 256, 128))
    rows = min(tm, 512)
    final_norm = fg is not None
    fg = g if fg is None else fg
    return pl.pallas_call(
        functools.partial(_ffn_kernel, final_norm=final_norm, rows=rows),
        grid=(n // tm, f // tf),
        in_specs=[
            pl.BlockSpec((tm, d), lambda i, j: (i, 0)),
            pl.BlockSpec((1, d), lambda i, j: (0, 0)),
            pl.BlockSpec((d, tf), lambda i, j: (0, j)),
            pl.BlockSpec((d, tf), lambda i, j: (0, j)),
            pl.BlockSpec((tf, d), lambda i, j: (j, 0)),
            pl.BlockSpec((1, d), lambda i, j: (0, 0)),
        ],
        out_specs=pl.BlockSpec((tm, d), lambda i, j: (i, 0)),
        out_shape=jax.ShapeDtypeStruct((n, d), F32),
        scratch_shapes=[pltpu.VMEM((tm, d), BF16)],
        compiler_params=_cparams(2),
        name="ffn",
    )(x, g.reshape(1, d), wg, wu, wd, fg.reshape(1, d))


def _proj_kernel(x_ref, g_ref, w_ref, mu_ref, b_ref, o_ref, carry_ref, *, tiles_per_seq, tn):
    i = pl.program_id(0)
    tm = x_ref.shape[0]
    ncols = w_ref.shape[1]

    @pl.when(i % tiles_per_seq == 0)
    def _():
        carry_ref[...] = jnp.zeros(carry_ref.shape, F32)

    h = _rms(x_ref[...], g_ref[...]).astype(BF16)
    row = lax.broadcasted_iota(jnp.int32, (SUBLANES, tn), 0)
    for c0 in range(0, ncols, tn):
        cs = slice(c0, c0 + tn)
        z = _dot(h, w_ref[:, cs])
        prev_tail = carry_ref[:, cs]
        carry_ref[:, cs] = z[tm - SUBLANES:, :]
        zs = pltpu.roll(z, 1, 0)
        head = jnp.where(row == 0, pltpu.roll(prev_tail, 1, 0), zs[:SUBLANES])
        mu = mu_ref[:, cs]
        b = b_ref[:, cs]
        z0 = z[:SUBLANES]
        o_ref[:SUBLANES, cs] = z0 + (head - z0) * mu + b
        z1 = z[SUBLANES:]
        o_ref[SUBLANES:, cs] = z1 + (zs[SUBLANES:] - z1) * mu + b


def _proj(x, g, w_all, mu_all, b_all, seq):
    n, d = x.shape
    ncols = w_all.shape[1]
    tm = _pick(seq, (512, 256, 128))
    tn = _pick(ncols, (1024, 512, 256, 128))
    return pl.pallas_call(
        functools.partial(_proj_kernel, tiles_per_seq=seq // tm, tn=tn),
        grid=(n // tm,),
        in_specs=[
            pl.BlockSpec((tm, d), lambda i: (i, 0)),
            pl.BlockSpec((1, d), lambda i: (0, 0)),
            pl.BlockSpec((d, ncols), lambda i: (0, 0), pipeline_mode=pl.Buffered(1)),
            pl.BlockSpec((1, ncols), lambda i: (0, 0)),
            pl.BlockSpec((1, ncols), lambda i: (0, 0)),
        ],
        out_specs=pl.BlockSpec((tm, ncols), lambda i: (i, 0)),
        out_shape=jax.ShapeDtypeStruct((n, ncols), F32),
        scratch_shapes=[pltpu.VMEM((SUBLANES, ncols), F32)],
        compiler_params=_cparams(1),
        name="proj",
    )(x, g.reshape(1, d), w_all, mu_all, b_all)


def _bdot(a, b):
    return lax.dot_general(a, b, (((2,), (1,)), ((0,), (0,))), preferred_element_type=F32)


def _bdot_nt(a, b):
    return lax.dot_general(a, b, (((2,), (2,)), ((0,), (0,))), preferred_element_type=F32)


def _inv_unit_lower(nmat):
    L = nmat.shape[-1]
    r = lax.broadcasted_iota(jnp.int32, (L, L), 0)
    c = lax.broadcasted_iota(jnp.int32, (L, L), 1)
    eye = jnp.where(r == c, 1.0, 0.0)
    t = eye + nmat
    pw = nmat.astype(BF16)
    pw = _bdot(pw, pw).astype(BF16)
    steps = L.bit_length() - 2
    for i in range(steps):
        if i + 1 < steps:
            both = _bdot(jnp.concatenate([t.astype(BF16), pw], axis=1), pw)
            t = t + both[:, :L]
            pw = both[:, L:].astype(BF16)
        else:
            t = t + _bdot(t.astype(BF16), pw)
    return t


RWKV_INPUTS = 14


def _rwkv_kernel(*refs, n_cast, **static):
    cast_in = refs[RWKV_INPUTS:RWKV_INPUTS + n_cast]
    cast_out = refs[RWKV_INPUTS + n_cast + 1:RWKV_INPUTS + 2 * n_cast + 1]
    chunk_refs = refs[:RWKV_INPUTS] + (refs[RWKV_INPUTS + n_cast],) + refs[RWKV_INPUTS + 2 * n_cast + 1:]
    _rwkv_chunk(*chunk_refs, **static)
    _cast_blocks(cast_in, cast_out)


def _rwkv_chunk(r_ref, k_ref, v_ref, lora_ref, w0_ref, a0_ref, kk_ref, ka_ref, rk_ref, gw_ref, gb_ref,
                du_ref, au_ref, gu_ref,
                o_ref,
                s_ref, pr_ref, pk_ref, pv_ref, pa_ref, pg_ref, plp_ref, plw_ref,
                *, n_tiles, dlp, alp):
    L = CHUNK

    @pl.when(pl.program_id(1) == 0)
    def _():
        s_ref[...] = jnp.zeros(s_ref.shape, F32)

    lora = lora_ref[...]
    wd = jnp.tanh(lora[:, :dlp])
    ad = lora[:, dlp:dlp + alp]
    gd = jax.nn.sigmoid(lora[:, dlp + alp:])
    lw = -EXP_M_HALF * jax.nn.sigmoid(w0_ref[...] + _dot(wd.astype(BF16), du_ref[...]))
    asig = jax.nn.sigmoid(a0_ref[...] + _dot(ad.astype(BF16), au_ref[...]))
    gate = _dot(gd.astype(BF16), gu_ref[...])
    row = lax.broadcasted_iota(jnp.int32, (L, L), 0)
    col = lax.broadcasted_iota(jnp.int32, (L, L), 1)
    incl = row >= col
    strict = row > col
    tril = jnp.where(incl, 1.0, 0.0).astype(BF16)
    h1 = lw.astype(BF16)
    h2 = (lw - h1.astype(F32)).astype(BF16)
    logp = _dot(tril, h1) + _dot(tril, h2)
    for p in range(n_tiles):
        sl = slice(p * LANES, (p + 1) * LANES)
        pr_ref[p] = r_ref[:, sl]
        pk_ref[p] = k_ref[:, sl]
        pv_ref[p] = v_ref[:, sl]
        pa_ref[p] = asig[:, sl]
        pg_ref[p] = gate[:, sl]
        plp_ref[p] = logp[:, sl]
        plw_ref[p] = lw[:, sl]

    lane = lax.broadcasted_iota(jnp.int32, (1, LANES), 1)
    lo = lane < HEAD_DIM
    same_head = jnp.where(row < HEAD_DIM, 0, 1) == jnp.where(col < HEAD_DIM, 0, 1)
    inv_hd = 1.0 / HEAD_DIM

    def first(x):
        return jnp.where(lo, x, 0.0)

    def second(x):
        return jnp.where(lo, 0.0, x)

    def segsum(x):
        return jnp.where(lo, jnp.sum(first(x), axis=-1, keepdims=True),
                         jnp.sum(second(x), axis=-1, keepdims=True))

    r = pr_ref[...]
    k = pk_ref[...]
    v = pv_ref[...]
    a_s = pa_ref[...]
    lp = plp_ref[...]
    kk = k * kk_ref[...]
    ss = segsum(kk * kk)
    kk = kk * lax.rsqrt(jnp.maximum(ss, 1e-24))
    a = -kk
    b = kk * a_s
    km = k * (1.0 + (a_s - 1.0) * ka_ref[...])
    cmid = lp[:, L // 2 - 1:L // 2, :]
    clast = lp[:, L - 1:L, :]
    lpe = lp - plw_ref[...]
    e_inv = jnp.exp(cmid - lp)
    e_mid = jnp.exp(cmid)
    at = a * jnp.exp(lpe - cmid)
    a_abs = at * e_mid
    rt = r * jnp.exp(lp - cmid)
    r_abs = rt * e_mid
    bt = b * e_inv
    kt = km * e_inv
    e_l = jnp.exp(clast - cmid)
    e_p = jnp.exp(clast)
    lhs = jnp.concatenate([first(at), second(at), first(rt), second(rt)], axis=1).astype(BF16)
    gmat = _bdot_nt(lhs, jnp.concatenate([bt, kt], axis=1).astype(BF16))
    n_lo = jnp.where(strict, gmat[:, 0:L, 0:L], 0.0)
    ak_lo = jnp.where(strict, gmat[:, 0:L, L:], 0.0)
    n_hi = jnp.where(strict, gmat[:, L:2 * L, 0:L], 0.0)
    ak_hi = jnp.where(strict, gmat[:, L:2 * L, L:], 0.0)
    rb_lo = jnp.where(incl, gmat[:, 2 * L:3 * L, 0:L], 0.0)
    rk_lo = jnp.where(incl, gmat[:, 2 * L:3 * L, L:], 0.0)
    rb_hi = jnp.where(incl, gmat[:, 3 * L:, 0:L], 0.0)
    rk_hi = jnp.where(incl, gmat[:, 3 * L:, L:], 0.0)
    t_all = _inv_unit_lower(jnp.concatenate([n_lo, n_hi], axis=0))
    t_lo, t_hi = t_all[:n_tiles], t_all[n_tiles:]
    s = s_ref[...]
    s_b = s.astype(BF16)
    v_lohi = jnp.concatenate([first(v), second(v)], axis=1).astype(BF16)
    rhs = _bdot_nt(a_abs.astype(BF16), s_b) + _bdot(jnp.concatenate([ak_lo, ak_hi], axis=2).astype(BF16), v_lohi)
    u = _bdot(jnp.concatenate([t_lo, t_hi], axis=2).astype(BF16),
              jnp.concatenate([first(rhs), second(rhs)], axis=1).astype(BF16))
    u_lohi = jnp.concatenate([first(u), second(u)], axis=1).astype(BF16)
    y = _bdot_nt(r_abs.astype(BF16), s_b) + _bdot(
        jnp.concatenate([rb_lo, rb_hi, rk_lo, rk_hi], axis=2).astype(BF16),
        jnp.concatenate([u_lohi, v_lohi], axis=1))
    upd = _bdot(jnp.concatenate([jnp.swapaxes(u, 1, 2), jnp.swapaxes(v, 1, 2)], axis=2).astype(BF16),
                jnp.concatenate([bt * e_l, kt * e_l], axis=1).astype(BF16))
    s_ref[...] = s * e_p + jnp.where(same_head, upd, 0.0)
    mean = segsum(y) * inv_hd
    dev = y - mean
    var = segsum(dev * dev) * inv_hd
    yn = dev * lax.rsqrt(var + GN_EPS) * gw_ref[...] + gb_ref[...]
    bonus = segsum(r * km * rk_ref[...]) * v
    out = ((yn + bonus) * pg_ref[...]).astype(BF16)
    for p in range(n_tiles):
        o_ref[:, p * LANES:(p + 1) * LANES] = out[p]


def _rwkv(z, batch, seq, c, lora_off, lora_w, dlp, alp, vecs, loras, casts):
    n = z.shape[0]
    nc = seq // CHUNK
    n_tiles = c // LANES
    assert lora_off % lora_w == 0 and len(vecs) + len(loras) + 4 == RWKV_INPUTS
    lora_blk = lora_off // lora_w
    cast_in, cast_out, cast_shapes = _cast_specs(casts, batch * nc, lambda b, t: b * nc + t)
    vec_spec = pl.BlockSpec((n_tiles, 1, LANES), lambda b, t: (0, 0, 0))
    in_specs = [
        pl.BlockSpec((CHUNK, c), lambda b, t: (b * nc + t, 0)),
        pl.BlockSpec((CHUNK, c), lambda b, t: (b * nc + t, 1)),
        pl.BlockSpec((CHUNK, c), lambda b, t: (b * nc + t, 2)),
        pl.BlockSpec((CHUNK, lora_w), lambda b, t: (b * nc + t, lora_blk)),
        pl.BlockSpec((1, c), lambda b, t: (0, 0)),
        pl.BlockSpec((1, c), lambda b, t: (0, 0)),
    ] + [vec_spec] * 5 + [pl.BlockSpec(w.shape, lambda b, t: (0, 0)) for w in loras]
    tile_f32 = pltpu.VMEM((n_tiles, CHUNK, LANES), F32)
    y, *cast = pl.pallas_call(
        functools.partial(_rwkv_kernel, n_cast=len(casts), n_tiles=n_tiles, dlp=dlp, alp=alp),
        grid=(batch, nc),
        in_specs=in_specs + cast_in,
        out_specs=[pl.BlockSpec((CHUNK, c), lambda b, t: (b * nc + t, 0))] + cast_out,
        out_shape=[jax.ShapeDtypeStruct((n, c), BF16)] + cast_shapes,
        scratch_shapes=[pltpu.VMEM((n_tiles, LANES, LANES), F32)] + [tile_f32] * 7,
        compiler_params=_cparams(2),
        name="rwkv",
    )(z, z, z, z, *vecs, *loras, *casts)
    return y, cast


def _swa_kernel(sink_ref, q_ref, kvc_ref, kvp_ref, cosc_ref, sinc_ref, cosp_ref, sinp_ref, *rest,
                n_tiles, group, qb):
    n_cast = len(rest) // 2
    o_ref = rest[n_cast]
    _cast_blocks(rest[:n_cast], rest[n_cast + 1:])
    nblk = pl.program_id(1)
    lane = lax.broadcasted_iota(jnp.int32, (1, LANES), 1)
    lo = lane < HEAD_DIM
    rot_lo = jnp.bitwise_and(lane, HEAD_DIM - 1) < HEAD_DIM // 2

    def rope(x, cos, sin_signed):
        partner = jnp.where(rot_lo, pltpu.roll(x, LANES - HEAD_DIM // 2, 1), pltpu.roll(x, HEAD_DIM // 2, 1))
        return x * cos + partner * sin_signed

    cosc = cosc_ref[...]
    sinc = sinc_ref[...]
    kvc = kvc_ref[...]
    kvp = kvp_ref[...]
    keys = jnp.concatenate([rope(kvp[:, :LANES], cosp_ref[...], sinp_ref[...]),
                            rope(kvc[:, :LANES], cosc, sinc)], axis=0)
    vals = jnp.concatenate([kvp[:, LANES:], kvc[:, LANES:]], axis=0)
    keys_sw = pltpu.roll(keys, HEAD_DIM, 1)
    vals_sw = pltpu.roll(vals, HEAD_DIM, 1)
    k_first = [jnp.where(lo, keys, 0.0).astype(BF16), jnp.where(lo, keys_sw, 0.0).astype(BF16)]
    k_second = [jnp.where(lo, 0.0, keys_sw).astype(BF16), jnp.where(lo, 0.0, keys).astype(BF16)]
    ones_kv = jnp.ones(vals.shape, BF16)
    with_ones = lambda x: jnp.concatenate([x.astype(BF16), ones_kv], axis=1)
    v_first = [with_ones(jnp.where(lo, vals, 0.0)), with_ones(jnp.where(lo, vals_sw, 0.0))]
    v_second = [with_ones(jnp.where(lo, 0.0, vals_sw)), with_ones(jnp.where(lo, 0.0, vals))]
    qi = lax.broadcasted_iota(jnp.int32, (BLOCK, 2 * BLOCK), 0)
    ki = lax.broadcasted_iota(jnp.int32, (BLOCK, 2 * BLOCK), 1)
    window = (ki > qi) & (ki <= qi + BLOCK)
    valid = [window & ((nblk > 0) | (ki >= BLOCK))] + [window] * (qb - 1)
    scale = HEAD_DIM ** -0.5 * LOG2_E
    heads = [(j, p, half) for j in range(qb) for p in range(n_tiles) for half in range(HEADS_PER_TILE)]
    scores = []
    for j in range(qb):
        qrows = slice(j * BLOCK, (j + 1) * BLOCK)
        krows = slice(j * BLOCK, (j + 2) * BLOCK)
        for p in range(n_tiles):
            g = (p * HEADS_PER_TILE) // group
            qp = (rope(q_ref[qrows, p * LANES:(p + 1) * LANES], cosc[qrows], sinc[qrows]) * scale).astype(BF16)
            scores += [_dot_nt(qp, k_first[g][krows]), _dot_nt(qp, k_second[g][krows])]
    probs, sink_terms = [], []
    for (j, p, half), s in zip(heads, scores):
        s = jnp.where(valid[j], s, NEG_INF)
        sink = sink_ref[p * HEADS_PER_TILE + half] * LOG2_E
        m = jnp.maximum(jnp.max(s, axis=-1, keepdims=True), sink)
        probs.append(jnp.exp2(s - m).astype(BF16))
        sink_terms.append(jnp.exp2(sink - m))
    outs = []
    for (j, p, half), e, st in zip(heads, probs, sink_terms):
        g = (p * HEADS_PER_TILE) // group
        krows = slice(j * BLOCK, (j + 2) * BLOCK)
        pv = _dot(e, (v_second[g] if half else v_first[g])[krows])
        outs.append(pv[:, :LANES] * (1.0 / (pv[:, LANES:] + st)))
    for j in range(qb):
        for p in range(n_tiles):
            i = (j * n_tiles + p) * HEADS_PER_TILE
            o_ref[j * BLOCK:(j + 1) * BLOCK, p * LANES:(p + 1) * LANES] = (outs[i] + outs[i + 1]).astype(BF16)


def _swa(z, sinks, cos_t, sin_t, batch, seq, q_off, qw, kv_off, kvw, group, casts):
    n = z.shape[0]
    nb = seq // BLOCK
    assert q_off % qw == 0 and kv_off % kvw == 0 and kvw == 2 * LANES
    q_blk, kv_blk = q_off // qw, kv_off // kvw
    qb = _pick(nb, (4, 2, 1))
    steps = nb // qb
    prev = lambda t: jnp.maximum(t * qb - 1, 0)
    cast_in, cast_out, cast_shapes = _cast_specs(casts, batch * steps, lambda b, t: b * steps + t)
    y, *cast = pl.pallas_call(
        functools.partial(_swa_kernel, n_tiles=qw // LANES, group=group, qb=qb),
        grid=(batch, steps),
        in_specs=[
            pl.BlockSpec(memory_space=pltpu.SMEM),
            pl.BlockSpec((qb * BLOCK, qw), lambda b, t: (b * steps + t, q_blk)),
            pl.BlockSpec((qb * BLOCK, kvw), lambda b, t: (b * steps + t, kv_blk)),
            pl.BlockSpec((BLOCK, kvw), lambda b, t: (b * nb + prev(t), kv_blk)),
            pl.BlockSpec((qb * BLOCK, LANES), lambda b, t: (t, 0)),
            pl.BlockSpec((qb * BLOCK, LANES), lambda b, t: (t, 0)),
            pl.BlockSpec((BLOCK, LANES), lambda b, t: (prev(t), 0)),
            pl.BlockSpec((BLOCK, LANES), lambda b, t: (prev(t), 0)),
        ] + cast_in,
        out_specs=[pl.BlockSpec((qb * BLOCK, qw), lambda b, t: (b * steps + t, 0))] + cast_out,
        out_shape=[jax.ShapeDtypeStruct((n, qw), BF16)] + cast_shapes,
        compiler_params=_cparams(2),
        name="swa",
    )(sinks, z, z, z, cos_t, sin_t, cos_t, sin_t, *casts)
    return y, cast


def _memkv_kernel(m_ref, g_ref, w_ref, o_ref):
    h = _rms(m_ref[...], g_ref[...]).astype(BF16)
    o_ref[...] = _dot(h, w_ref[...].astype(BF16)).astype(BF16)


def _memkv(mem, g, w):
    n, d = mem.shape
    ncols = w.shape[1]
    tm = _pick(n, (512, 256, 128))
    tn = _pick(ncols, (1024, 512, 256, 128))
    return pl.pallas_call(
        _memkv_kernel,
        grid=(ncols // tn, n // tm),
        in_specs=[
            pl.BlockSpec((tm, d), lambda j, i: (i, 0)),
            pl.BlockSpec((1, d), lambda j, i: (0, 0)),
            pl.BlockSpec((d, tn), lambda j, i: (0, j)),
        ],
        out_specs=pl.BlockSpec((tm, tn), lambda j, i: (i, j)),
        out_shape=jax.ShapeDtypeStruct((n, ncols), BF16),
        compiler_params=_cparams(2),
        name="memkv",
    )(mem, g.reshape(1, d), w)


def _mix_kernel(x_ref, yr_ref, ys_ref, wo_ref, bo_ref, g_ref, wq_ref, k_ref, v_ref, wxo_ref, o_ref):
    c = yr_ref.shape[1]
    d = x_ref.shape[1]
    hd = d // XATTN_HEADS
    x2 = x_ref[...] + _dot(yr_ref[...], wo_ref[:c, :]) + _dot(ys_ref[...], wo_ref[c:, :]) + bo_ref[...]
    q = _dot(_rms(x2, g_ref[...]).astype(BF16), wq_ref[...]).astype(BF16)
    scale = hd ** -0.5
    heads = [slice(h * hd, (h + 1) * hd) for h in range(XATTN_HEADS)]
    scores = [_dot_nt(q[:, sl], k_ref[:, sl]) * scale for sl in heads]
    probs, dens = [], []
    for s in scores:
        e = jnp.exp(s - jnp.max(s, axis=-1, keepdims=True))
        probs.append(e.astype(BF16))
        dens.append(jnp.sum(e, axis=-1, keepdims=True))
    outs = [(_dot(e, v_ref[:, sl]) * (1.0 / den)).astype(BF16) for e, den, sl in zip(probs, dens, heads)]
    o_ref[...] = x2 + _dot(jnp.concatenate(outs, axis=1), wxo_ref[...])


def _mix(x, yr, ys, wo, bo, g, wq, kv, wxo, seq, mlen):
    n, d = x.shape
    c = yr.shape[1]
    tm = _pick(seq, (256, 128))
    per_seq = seq // tm
    once = pl.Buffered(1)
    return pl.pallas_call(
        _mix_kernel,
        grid=(n // tm,),
        in_specs=[
            pl.BlockSpec((tm, d), lambda i: (i, 0)),
            pl.BlockSpec((tm, c), lambda i: (i, 0)),
            pl.BlockSpec((tm, d - c), lambda i: (i, 0)),
            pl.BlockSpec((d, d), lambda i: (0, 0), pipeline_mode=once),
            pl.BlockSpec((1, d), lambda i: (0, 0)),
            pl.BlockSpec((1, d), lambda i: (0, 0)),
            pl.BlockSpec((d, d), lambda i: (0, 0), pipeline_mode=once),
            pl.BlockSpec((mlen, d), lambda i: (i // per_seq, 0)),
            pl.BlockSpec((mlen, d), lambda i: (i // per_seq, 1)),
            pl.BlockSpec((d, d), lambda i: (0, 0), pipeline_mode=once),
        ],
        out_specs=pl.BlockSpec((tm, d), lambda i: (i, 0)),
        out_shape=jax.ShapeDtypeStruct((n, d), F32),
        compiler_params=_cparams(1),
        name="mix",
    )(x, yr, ys, wo, bo.reshape(1, d), g.reshape(1, d), wq, kv, kv, wxo)


def _pad_cols(w, width):
    return jnp.pad(w, ((0, 0), (0, width - w.shape[1])))


def _pad_rows(w, height):
    return jnp.pad(w, ((0, height - w.shape[0]), (0, 0)))


def _rope_tables(seq):
    half = HEAD_DIM // 2
    lane = jnp.arange(LANES)
    inv_freq = ROPE_THETA ** (-jnp.arange(0, HEAD_DIM, 2, dtype=F32) / HEAD_DIM)
    ang = jnp.arange(seq, dtype=F32)[:, None] * inv_freq[lane % half][None, :]
    sign = jnp.where((lane % HEAD_DIM) < half, -1.0, 1.0)
    return jnp.cos(ang), jnp.sin(ang) * sign[None, :]


def kernel(x, mem, f1_norm, f1_gate, f1_up, f1_down, mix_norm, w_in, b_in_attn, rw_mu, rw_w0, rw_decay_up, rw_a0, rw_aaa_up, rw_gate_up, rw_k_k, rw_k_a, rw_r_k, rw_lnx_w, rw_lnx_b, attn_sinks, w_out, b_out, xa_norm, mem_norm, w_xq, w_xkv, w_xo, f2_norm, f2_gate, f2_up, f2_down, final_norm):
    batch, seq, d = x.shape
    mlen = mem.shape[1]
    depth = f1_norm.shape[0]
    c = rw_w0.shape[1]
    sw = d - c
    dl, al, gl = rw_decay_up.shape[1], rw_aaa_up.shape[1], rw_gate_up.shape[1]
    dlp, alp, glp = (_round_up(v, LANES) for v in (dl, al, gl))
    kvw = b_in_attn.shape[1] - sw
    q_heads = sw // HEAD_DIM
    group = q_heads // (kvw // (2 * HEAD_DIM))
    n_tiles = c // LANES
    n = batch * seq
    q_off = 3 * c
    lora_off = q_off + sw
    lora_w = dlp + alp + glp
    kv_off = lora_off + lora_w
    ncols = _round_up(kv_off + kvw, 2 * LANES)
    cos_t, sin_t = _rope_tables(seq)

    xf = x.reshape(n, d)
    memf = mem.reshape(batch * mlen, d)
    for l in range(depth):
        xf = _ffn(xf, f1_norm[l], f1_gate[l].astype(BF16), f1_up[l].astype(BF16), f1_down[l].astype(BF16), None)

        wl = w_in[l]
        o1, o2, o3 = 3 * c, 3 * c + dl, 3 * c + dl + al
        shift = o3 + gl
        w_all = jnp.concatenate([
            wl[:, :o1], wl[:, shift:shift + sw],
            _pad_cols(wl[:, o1:o2], dlp), _pad_cols(wl[:, o2:o3], alp), _pad_cols(wl[:, o3:shift], glp),
            wl[:, shift + sw:]], axis=1)
        w_all = _pad_cols(w_all, ncols).astype(BF16)
        mu = rw_mu[l][None, :]
        mu_all = _pad_cols(jnp.concatenate([
            mu[:, :o1], jnp.zeros((1, sw), F32),
            _pad_cols(mu[:, o1:o2], dlp), _pad_cols(mu[:, o2:o3], alp), _pad_cols(mu[:, o3:shift], glp)],
            axis=1), ncols)
        bia = b_in_attn[l][None, :]
        b_all = _pad_cols(jnp.concatenate([
            jnp.zeros((1, q_off), F32), bia[:, :sw], jnp.zeros((1, lora_w), F32), bia[:, sw:]], axis=1), ncols)
        z = _proj(xf, mix_norm[l], w_all, mu_all, b_all, seq)

        tiles = lambda v: v.reshape(n_tiles, 1, LANES)
        vecs = [rw_w0[l].reshape(1, c), rw_a0[l].reshape(1, c), tiles(rw_k_k[l]), tiles(rw_k_a[l]),
                tiles(rw_r_k[l]), tiles(rw_lnx_w[l]), tiles(rw_lnx_b[l])]
        loras = [_pad_rows(rw_decay_up[l], dlp).astype(BF16), _pad_rows(rw_aaa_up[l], alp).astype(BF16),
                 _pad_rows(rw_gate_up[l], glp).astype(BF16)]
        y_rwkv, (f2_down_b, w_out_b, w_xq_b, w_xo_b) = _rwkv(
            z, batch, seq, c, lora_off, lora_w, dlp, alp, vecs, loras,
            [f2_down[l], w_out[l], w_xq[l], w_xo[l]])
        y_swa, (f2_gate_b, f2_up_b) = _swa(z, attn_sinks[l], cos_t, sin_t, batch, seq, q_off, sw, kv_off, kvw,
                                           group, [f2_gate[l], f2_up[l]])

        kv_mem = _memkv(memf, mem_norm[l], w_xkv[l])
        xf = _mix(xf, y_rwkv, y_swa, w_out_b, b_out[l], xa_norm[l], w_xq_b, kv_mem, w_xo_b, seq, mlen)

        last = l == depth - 1
        xf = _ffn(xf, f2_norm[l], f2_gate_b, f2_up_b, f2_down_b, final_norm if last else None)
    return xf.reshape(batch, seq, d)
```

```python
import functools

import jax
import jax.numpy as jnp
from jax import lax
from jax.experimental import pallas as pl
from jax.experimental.pallas import tpu as pltpu

F32, BF16 = jnp.float32, jnp.bfloat16

LANES = 128
SUBLANES = 8
VMEM_LIMIT_BYTES = 56 * 1024 * 1024

HEAD_DIM = 64
HEADS_PER_TILE = LANES // HEAD_DIM
CHUNK = 128
BLOCK = 128
XATTN_HEADS = 4
RMS_EPS = 1e-6
GN_EPS = 64e-5
NEG_INF = -1e30
ROPE_THETA = 10000.0
EXP_M_HALF = 0.6065306597126334
LOG2_E = 1.4426950408889634


def _round_up(n, m):
    return (n + m - 1) // m * m


def _pick(n, prefs):
    for p in prefs:
        if n % p == 0:
            return p
    raise ValueError(f"no tile in {prefs} divides {n}")


def _cparams(n_axes):
    return pltpu.CompilerParams(dimension_semantics=("arbitrary",) * n_axes,
                                vmem_limit_bytes=VMEM_LIMIT_BYTES)


def _dot(a, b):
    return jnp.dot(a, b, preferred_element_type=F32)


def _dot_nt(a, b):
    return lax.dot_general(a, b, (((1,), (1,)), ((), ())), preferred_element_type=F32)


def _cast_specs(weights, nsteps, flat_step):
    in_specs, out_specs, out_shapes = [], [], []
    for w in weights:
        units = w.shape[0] // (2 * SUBLANES)
        nblk = max(d for d in range(1, nsteps + 1) if units % d == 0)
        imap = lambda *g, nblk=nblk: (jnp.minimum(flat_step(*g), nblk - 1), 0)
        spec = pl.BlockSpec((w.shape[0] // nblk, w.shape[1]), imap)
        in_specs.append(spec)
        out_specs.append(spec)
        out_shapes.append(jax.ShapeDtypeStruct(w.shape, BF16))
    return in_specs, out_specs, out_shapes


def _cast_blocks(src_refs, dst_refs):
    for src, dst in zip(src_refs, dst_refs):
        dst[...] = src[...].astype(BF16)


def _rms(x, g):
    ms = jnp.mean(x * x, axis=-1, keepdims=True)
    return x * lax.rsqrt(ms + RMS_EPS) * g


def _ffn_kernel(x_ref, g_ref, wg_ref, wu_ref, wd_ref, fg_ref, o_ref, h_ref, *, final_norm, rows):
    j = pl.program_id(1)
    tm = x_ref.shape[0]

    @pl.when(j == 0)
    def _():
        for r0 in range(0, tm, rows):
            x = x_ref[r0:r0 + rows, :]
            h_ref[r0:r0 + rows, :] = _rms(x, g_ref[...]).astype(BF16)
            o_ref[r0:r0 + rows, :] = x

    for r0 in range(0, tm, rows):
        h = h_ref[r0:r0 + rows, :]
        gate = _dot(h, wg_ref[...])
        up = _dot(h, wu_ref[...])
        act = (gate * jax.nn.sigmoid(gate) * up).astype(BF16)
        o_ref[r0:r0 + rows, :] += 0.5 * _dot(act, wd_ref[...])

    if final_norm:
        @pl.when(j == pl.num_programs(1) - 1)
        def _():
            for r0 in range(0, tm, rows):
                o_ref[r0:r0 + rows, :] = _rms(o_ref[r0:r0 + rows, :], fg_ref[...])


def _ffn(x, g, wg, wu, wd, fg):
    n, d = x.shape
    f = wg.shape[1]
    tm = _pick(n, (1024, 512, 256, 128))
    tf = _pick(f, (512, 256, 128))
    rows = min(tm, 512)
    final_norm = fg is not None
    fg = g if fg is None else fg
    return pl.pallas_call(
        functools.partial(_ffn_kernel, final_norm=final_norm, rows=rows),
        grid=(n // tm, f // tf),
        in_specs=[
            pl.BlockSpec((tm, d), lambda i, j: (i, 0)),
            pl.BlockSpec((1, d), lambda i, j: (0, 0)),
            pl.BlockSpec((d, tf), lambda i, j: (0, j)),
            pl.BlockSpec((d, tf), lambda i, j: (0, j)),
            pl.BlockSpec((tf, d), lambda i, j: (j, 0)),
            pl.BlockSpec((1, d), lambda i, j: (0, 0)),
        ],
        out_specs=pl.BlockSpec((tm, d), lambda i, j: (i, 0)),
        out_shape=jax.ShapeDtypeStruct((n, d), F32),
        scratch_shapes=[pltpu.VMEM((tm, d), BF16)],
        compiler_params=_cparams(2),
        name="ffn",
    )(x, g.reshape(1, d), wg, wu, wd, fg.reshape(1, d))


def _proj_kernel(x_ref, g_ref, w_ref, mu_ref, b_ref, o_ref, carry_ref, *, tiles_per_seq, tn):
    i = pl.program_id(0)
    tm = x_ref.shape[0]
    ncols = w_ref.shape[1]

    @pl.when(i % tiles_per_seq == 0)
    def _():
        carry_ref[...] = jnp.zeros(carry_ref.shape, F32)

    h = _rms(x_ref[...], g_ref[...]).astype(BF16)
    row = lax.broadcasted_iota(jnp.int32, (SUBLANES, tn), 0)
    for c0 in range(0, ncols, tn):
        cs = slice(c0, c0 + tn)
        z = _dot(h, w_ref[:, cs])
        prev_tail = carry_ref[:, cs]
        carry_ref[:, cs] = z[tm - SUBLANES:, :]
        zs = pltpu.roll(z, 1, 0)
        head = jnp.where(row == 0, pltpu.roll(prev_tail, 1, 0), zs[:SUBLANES])
        mu = mu_ref[:, cs]
        b = b_ref[:, cs]
        z0 = z[:SUBLANES]
        o_ref[:SUBLANES, cs] = z0 + (head - z0) * mu + b
        z1 = z[SUBLANES:]
        o_ref[SUBLANES:, cs] = z1 + (zs[SUBLANES:] - z1) * mu + b


def _proj(x, g, w_all, mu_all, b_all, seq):
    n, d = x.shape
    ncols = w_all.shape[1]
    tm = _pick(seq, (512, 256, 128))
    tn = _pick(ncols, (1024, 512, 256, 128))
    return pl.pallas_call(
        functools.partial(_proj_kernel, tiles_per_seq=seq // tm, tn=tn),
        grid=(n // tm,),
        in_specs=[
            pl.BlockSpec((tm, d), lambda i: (i, 0)),
            pl.BlockSpec((1, d), lambda i: (0, 0)),
            pl.BlockSpec((d, ncols), lambda i: (0, 0), pipeline_mode=pl.Buffered(1)),
            pl.BlockSpec((1, ncols), lambda i: (0, 0)),
            pl.BlockSpec((1, ncols), lambda i: (0, 0)),
        ],
        out_specs=pl.BlockSpec((tm, ncols), lambda i: (i, 0)),
        out_shape=jax.ShapeDtypeStruct((n, ncols), F32),
        scratch_shapes=[pltpu.VMEM((SUBLANES, ncols), F32)],
        compiler_params=_cparams(1),
        name="proj",
    )(x, g.reshape(1, d), w_all, mu_all, b_all)


def _bdot(a, b):
    return lax.dot_general(a, b, (((2,), (1,)), ((0,), (0,))), preferred_element_type=F32)


def _bdot_nt(a, b):
    return lax.dot_general(a, b, (((2,), (2,)), ((0,), (0,))), preferred_element_type=F32)


def _inv_unit_lower(nmat):
    L = nmat.shape[-1]
    r = lax.broadcasted_iota(jnp.int32, (L, L), 0)
    c = lax.broadcasted_iota(jnp.int32, (L, L), 1)
    eye = jnp.where(r == c, 1.0, 0.0)
    t = eye + nmat
    pw = nmat.astype(BF16)
    pw = _bdot(pw, pw).astype(BF16)
    steps = L.bit_length() - 2
    for i in range(steps):
        if i + 1 < steps:
            both = _bdot(jnp.concatenate([t.astype(BF16), pw], axis=1), pw)
            t = t + both[:, :L]
            pw = both[:, L:].astype(BF16)
        else:
            t = t + _bdot(t.astype(BF16), pw)
    return t


RWKV_INPUTS = 14


def _rwkv_kernel(*refs, n_cast, **static):
    cast_in = refs[RWKV_INPUTS:RWKV_INPUTS + n_cast]
    cast_out = refs[RWKV_INPUTS + n_cast + 1:RWKV_INPUTS + 2 * n_cast + 1]
    chunk_refs = refs[:RWKV_INPUTS] + (refs[RWKV_INPUTS + n_cast],) + refs[RWKV_INPUTS + 2 * n_cast + 1:]
    _rwkv_chunk(*chunk_refs, **static)
    _cast_blocks(cast_in, cast_out)


def _rwkv_chunk(r_ref, k_ref, v_ref, lora_ref, w0_ref, a0_ref, kk_ref, ka_ref, rk_ref, gw_ref, gb_ref,
                du_ref, au_ref, gu_ref,
                o_ref,
                s_ref, pr_ref, pk_ref, pv_ref, pa_ref, pg_ref, plp_ref, plw_ref,
                *, n_tiles, sub, dlp, alp):
    L = CHUNK
    nt = sub * n_tiles

    @pl.when(pl.program_id(1) == 0)
    def _():
        s_ref[...] = jnp.zeros(s_ref.shape, F32)

    lora = lora_ref[...]
    wd = jnp.tanh(lora[:, :dlp])
    ad = lora[:, dlp:dlp + alp]
    gd = jax.nn.sigmoid(lora[:, dlp + alp:])
    lw = -EXP_M_HALF * jax.nn.sigmoid(w0_ref[...] + _dot(wd.astype(BF16), du_ref[...]))
    asig = jax.nn.sigmoid(a0_ref[...] + _dot(ad.astype(BF16), au_ref[...]))
    gate = _dot(gd.astype(BF16), gu_ref[...])
    row = lax.broadcasted_iota(jnp.int32, (L, L), 0)
    col = lax.broadcasted_iota(jnp.int32, (L, L), 1)
    incl = row >= col
    strict = row > col
    tril = jnp.where(incl, 1.0, 0.0).astype(BF16)
    h1 = lw.astype(BF16)
    h2 = (lw - h1.astype(F32)).astype(BF16)
    for c in range(sub):
        rows = slice(c * L, (c + 1) * L)
        logp = _dot(tril, h1[rows]) + _dot(tril, h2[rows])
        for p in range(n_tiles):
            sl = slice(p * LANES, (p + 1) * LANES)
            i = c * n_tiles + p
            pr_ref[i] = r_ref[rows, sl]
            pk_ref[i] = k_ref[rows, sl]
            pv_ref[i] = v_ref[rows, sl]
            pa_ref[i] = asig[rows, sl]
            pg_ref[i] = gate[rows, sl]
            plp_ref[i] = logp[:, sl]
            plw_ref[i] = lw[rows, sl]

    lane = lax.broadcasted_iota(jnp.int32, (1, LANES), 1)
    lo = lane < HEAD_DIM
    same_head = jnp.where(row < HEAD_DIM, 0, 1) == jnp.where(col < HEAD_DIM, 0, 1)
    inv_hd = 1.0 / HEAD_DIM

    def first(x):
        return jnp.where(lo, x, 0.0)

    def second(x):
        return jnp.where(lo, 0.0, x)

    def segsum(x):
        return jnp.where(lo, jnp.sum(first(x), axis=-1, keepdims=True),
                         jnp.sum(second(x), axis=-1, keepdims=True))

    def per_tile(ref):
        return jnp.concatenate([ref[...]] * sub, axis=0)

    r = pr_ref[...]
    k = pk_ref[...]
    v = pv_ref[...]
    a_s = pa_ref[...]
    lp = plp_ref[...]
    kk = k * per_tile(kk_ref)
    ss = segsum(kk * kk)
    kk = kk * lax.rsqrt(jnp.maximum(ss, 1e-24))
    a = -kk
    b = kk * a_s
    km = k * (1.0 + (a_s - 1.0) * per_tile(ka_ref))
    cmid = lp[:, L // 2 - 1:L // 2, :]
    clast = lp[:, L - 1:L, :]
    lpe = lp - plw_ref[...]
    e_inv = jnp.exp(cmid - lp)
    e_mid = jnp.exp(cmid)
    at = a * jnp.exp(lpe - cmid)
    a_abs = at * e_mid
    rt = r * jnp.exp(lp - cmid)
    r_abs = rt * e_mid
    bt = b * e_inv
    kt = km * e_inv
    e_l = jnp.exp(clast - cmid)
    e_p = jnp.exp(clast)
    lhs = jnp.concatenate([first(at), second(at), first(rt), second(rt)], axis=1).astype(BF16)
    gmat = _bdot_nt(lhs, jnp.concatenate([bt, kt], axis=1).astype(BF16))
    n_lo = jnp.where(strict, gmat[:, 0:L, 0:L], 0.0)
    ak_lo = jnp.where(strict, gmat[:, 0:L, L:], 0.0)
    n_hi = jnp.where(strict, gmat[:, L:2 * L, 0:L], 0.0)
    ak_hi = jnp.where(strict, gmat[:, L:2 * L, L:], 0.0)
    rb_lo = jnp.where(incl, gmat[:, 2 * L:3 * L, 0:L], 0.0)
    rk_lo = jnp.where(incl, gmat[:, 2 * L:3 * L, L:], 0.0)
    rb_hi = jnp.where(incl, gmat[:, 3 * L:, 0:L], 0.0)
    rk_hi = jnp.where(incl, gmat[:, 3 * L:, L:], 0.0)
    t_all = _inv_unit_lower(jnp.concatenate([n_lo, n_hi], axis=0))
    t_wide = jnp.concatenate([t_all[:nt], t_all[nt:]], axis=2).astype(BF16)
    ak_wide = jnp.concatenate([ak_lo, ak_hi], axis=2).astype(BF16)
    read_wide = jnp.concatenate([rb_lo, rb_hi, rk_lo, rk_hi], axis=2).astype(BF16)
    v_lohi = jnp.concatenate([first(v), second(v)], axis=1).astype(BF16)
    vt = jnp.swapaxes(v, 1, 2)
    bk_last = jnp.concatenate([bt * e_l, kt * e_l], axis=1).astype(BF16)
    a_abs_b = a_abs.astype(BF16)
    r_abs_b = r_abs.astype(BF16)
    ys = []
    for c in range(sub):
        tl = slice(c * n_tiles, (c + 1) * n_tiles)
        s = s_ref[...]
        s_b = s.astype(BF16)
        rhs = _bdot_nt(a_abs_b[tl], s_b) + _bdot(ak_wide[tl], v_lohi[tl])
        u = _bdot(t_wide[tl], jnp.concatenate([first(rhs), second(rhs)], axis=1).astype(BF16))
        u_lohi = jnp.concatenate([first(u), second(u)], axis=1).astype(BF16)
        ys.append(_bdot_nt(r_abs_b[tl], s_b) + _bdot(read_wide[tl], jnp.concatenate([u_lohi, v_lohi[tl]], axis=1)))
        upd = _bdot(jnp.concatenate([jnp.swapaxes(u, 1, 2), vt[tl]], axis=2).astype(BF16), bk_last[tl])
        s_ref[...] = s * e_p[tl] + jnp.where(same_head, upd, 0.0)
    y = jnp.concatenate(ys, axis=0)
    mean = segsum(y) * inv_hd
    dev = y - mean
    var = segsum(dev * dev) * inv_hd
    yn = dev * lax.rsqrt(var + GN_EPS) * per_tile(gw_ref) + per_tile(gb_ref)
    bonus = segsum(r * km * per_tile(rk_ref)) * v
    out = ((yn + bonus) * pg_ref[...]).astype(BF16)
    for c in range(sub):
        for p in range(n_tiles):
            o_ref[c * L:(c + 1) * L, p * LANES:(p + 1) * LANES] = out[c * n_tiles + p]


def _rwkv(z, batch, seq, c, lora_off, lora_w, dlp, alp, vecs, loras, casts):
    n = z.shape[0]
    nc = seq // CHUNK
    n_tiles = c // LANES
    sub = _pick(nc, (2, 1))
    steps = nc // sub
    rows = sub * CHUNK
    assert lora_off % lora_w == 0 and len(vecs) + len(loras) + 4 == RWKV_INPUTS
    lora_blk = lora_off // lora_w
    cast_in, cast_out, cast_shapes = _cast_specs(casts, batch * steps, lambda b, t: b * steps + t)
    vec_spec = pl.BlockSpec((n_tiles, 1, LANES), lambda b, t: (0, 0, 0))
    in_specs = [
        pl.BlockSpec((rows, c), lambda b, t: (b * steps + t, 0)),
        pl.BlockSpec((rows, c), lambda b, t: (b * steps + t, 1)),
        pl.BlockSpec((rows, c), lambda b, t: (b * steps + t, 2)),
        pl.BlockSpec((rows, lora_w), lambda b, t: (b * steps + t, lora_blk)),
        pl.BlockSpec((1, c), lambda b, t: (0, 0)),
        pl.BlockSpec((1, c), lambda b, t: (0, 0)),
    ] + [vec_spec] * 5 + [pl.BlockSpec(w.shape, lambda b, t: (0, 0)) for w in loras]
    tile_f32 = pltpu.VMEM((sub * n_tiles, CHUNK, LANES), F32)
    y, *cast = pl.pallas_call(
        functools.partial(_rwkv_kernel, n_cast=len(casts), n_tiles=n_tiles, sub=sub, dlp=dlp, alp=alp),
        grid=(batch, steps),
        in_specs=in_specs + cast_in,
        out_specs=[pl.BlockSpec((rows, c), lambda b, t: (b * steps + t, 0))] + cast_out,
        out_shape=[jax.ShapeDtypeStruct((n, c), BF16)] + cast_shapes,
        scratch_shapes=[pltpu.VMEM((n_tiles, LANES, LANES), F32)] + [tile_f32] * 7,
        compiler_params=_cparams(2),
        name="rwkv",
    )(z, z, z, z, *vecs, *loras, *casts)
    return y, cast


def _swa_kernel(sink_ref, q_ref, kvc_ref, kvp_ref, cosc_ref, sinc_ref, cosp_ref, sinp_ref, *rest,
                n_tiles, group, qb):
    n_cast = len(rest) // 2
    o_ref = rest[n_cast]
    _cast_blocks(rest[:n_cast], rest[n_cast + 1:])
    nblk = pl.program_id(1)
    lane = lax.broadcasted_iota(jnp.int32, (1, LANES), 1)
    lo = lane < HEAD_DIM
    rot_lo = jnp.bitwise_and(lane, HEAD_DIM - 1) < HEAD_DIM // 2

    def rope(x, cos, sin_signed):
        partner = jnp.where(rot_lo, pltpu.roll(x, LANES - HEAD_DIM // 2, 1), pltpu.roll(x, HEAD_DIM // 2, 1))
        return x * cos + partner * sin_signed

    cosc = cosc_ref[...]
    sinc = sinc_ref[...]
    kvc = kvc_ref[...]
    kvp = kvp_ref[...]
    keys = jnp.concatenate([rope(kvp[:, :LANES], cosp_ref[...], sinp_ref[...]),
                            rope(kvc[:, :LANES], cosc, sinc)], axis=0)
    vals = jnp.concatenate([kvp[:, LANES:], kvc[:, LANES:]], axis=0)
    keys_sw = pltpu.roll(keys, HEAD_DIM, 1)
    vals_sw = pltpu.roll(vals, HEAD_DIM, 1)
    k_first = [jnp.where(lo, keys, 0.0).astype(BF16), jnp.where(lo, keys_sw, 0.0).astype(BF16)]
    k_second = [jnp.where(lo, 0.0, keys_sw).astype(BF16), jnp.where(lo, 0.0, keys).astype(BF16)]
    ones_kv = jnp.ones(vals.shape, BF16)
    with_ones = lambda x: jnp.concatenate([x.astype(BF16), ones_kv], axis=1)
    v_first = [with_ones(jnp.where(lo, vals, 0.0)), with_ones(jnp.where(lo, vals_sw, 0.0))]
    v_second = [with_ones(jnp.where(lo, 0.0, vals_sw)), with_ones(jnp.where(lo, 0.0, vals))]
    qi = lax.broadcasted_iota(jnp.int32, (BLOCK, 2 * BLOCK), 0)
    ki = lax.broadcasted_iota(jnp.int32, (BLOCK, 2 * BLOCK), 1)
    window = (ki > qi) & (ki <= qi + BLOCK)
    valid = [window & ((nblk > 0) | (ki >= BLOCK))] + [window] * (qb - 1)
    scale = HEAD_DIM ** -0.5 * LOG2_E
    heads = [(j, p, half) for j in range(qb) for p in range(n_tiles) for half in range(HEADS_PER_TILE)]
    scores = []
    for j in range(qb):
        qrows = slice(j * BLOCK, (j + 1) * BLOCK)
        krows = slice(j * BLOCK, (j + 2) * BLOCK)
        for p in range(n_tiles):
            g = (p * HEADS_PER_TILE) // group
            qp = (rope(q_ref[qrows, p * LANES:(p + 1) * LANES], cosc[qrows], sinc[qrows]) * scale).astype(BF16)
            scores += [_dot_nt(qp, k_first[g][krows]), _dot_nt(qp, k_second[g][krows])]
    probs, sink_terms = [], []
    for (j, p, half), s in zip(heads, scores):
        s = jnp.where(valid[j], s, NEG_INF)
        sink = sink_ref[p * HEADS_PER_TILE + half] * LOG2_E
        m = jnp.maximum(jnp.max(s, axis=-1, keepdims=True), sink)
        probs.append(jnp.exp2(s - m).astype(BF16))
        sink_terms.append(jnp.exp2(sink - m))
    outs = []
    for (j, p, half), e, st in zip(heads, probs, sink_terms):
        g = (p * HEADS_PER_TILE) // group
        krows = slice(j * BLOCK, (j + 2) * BLOCK)
        pv = _dot(e, (v_second[g] if half else v_first[g])[krows])
        outs.append(pv[:, :LANES] * (1.0 / (pv[:, LANES:] + st)))
    for j in range(qb):
        for p in range(n_tiles):
            i = (j * n_tiles + p) * HEADS_PER_TILE
            o_ref[j * BLOCK:(j + 1) * BLOCK, p * LANES:(p + 1) * LANES] = (outs[i] + outs[i + 1]).astype(BF16)


def _swa(z, sinks, cos_t, sin_t, batch, seq, q_off, qw, kv_off, kvw, group, casts):
    n = z.shape[0]
    nb = seq // BLOCK
    assert q_off % qw == 0 and kv_off % kvw == 0 and kvw == 2 * LANES
    q_blk, kv_blk = q_off // qw, kv_off // kvw
    qb = _pick(nb, (4, 2, 1))
    steps = nb // qb
    prev = lambda t: jnp.maximum(t * qb - 1, 0)
    cast_in, cast_out, cast_shapes = _cast_specs(casts, batch * steps, lambda b, t: b * steps + t)
    y, *cast = pl.pallas_call(
        functools.partial(_swa_kernel, n_tiles=qw // LANES, group=group, qb=qb),
        grid=(batch, steps),
        in_specs=[
            pl.BlockSpec(memory_space=pltpu.SMEM),
            pl.BlockSpec((qb * BLOCK, qw), lambda b, t: (b * steps + t, q_blk)),
            pl.BlockSpec((qb * BLOCK, kvw), lambda b, t: (b * steps + t, kv_blk)),
            pl.BlockSpec((BLOCK, kvw), lambda b, t: (b * nb + prev(t), kv_blk)),
            pl.BlockSpec((qb * BLOCK, LANES), lambda b, t: (t, 0)),
            pl.BlockSpec((qb * BLOCK, LANES), lambda b, t: (t, 0)),
            pl.BlockSpec((BLOCK, LANES), lambda b, t: (prev(t), 0)),
            pl.BlockSpec((BLOCK, LANES), lambda b, t: (prev(t), 0)),
        ] + cast_in,
        out_specs=[pl.BlockSpec((qb * BLOCK, qw), lambda b, t: (b * steps + t, 0))] + cast_out,
        out_shape=[jax.ShapeDtypeStruct((n, qw), BF16)] + cast_shapes,
        compiler_params=_cparams(2),
        name="swa",
    )(sinks, z, z, z, cos_t, sin_t, cos_t, sin_t, *casts)
    return y, cast


def _memkv_kernel(m_ref, g_ref, w_ref, o_ref):
    h = _rms(m_ref[...], g_ref[...]).astype(BF16)
    o_ref[...] = _dot(h, w_ref[...].astype(BF16)).astype(BF16)


def _memkv(mem, g, w):
    n, d = mem.shape
    ncols = w.shape[1]
    tm = _pick(n, (512, 256, 128))
    tn = _pick(ncols, (1024, 512, 256, 128))
    return pl.pallas_call(
        _memkv_kernel,
        grid=(ncols // tn, n // tm),
        in_specs=[
            pl.BlockSpec((tm, d), lambda j, i: (i, 0)),
            pl.BlockSpec((1, d), lambda j, i: (0, 0)),
            pl.BlockSpec((d, tn), lambda j, i: (0, j)),
        ],
        out_specs=pl.BlockSpec((tm, tn), lambda j, i: (i, j)),
        out_shape=jax.ShapeDtypeStruct((n, ncols), BF16),
        compiler_params=_cparams(2),
        name="memkv",
    )(mem, g.reshape(1, d), w)


def _mix_kernel(x_ref, yr_ref, ys_ref, wo_ref, bo_ref, g_ref, wq_ref, k_ref, v_ref, wxo_ref, o_ref):
    c = yr_ref.shape[1]
    d = x_ref.shape[1]
    hd = d // XATTN_HEADS
    x2 = x_ref[...] + _dot(yr_ref[...], wo_ref[:c, :]) + _dot(ys_ref[...], wo_ref[c:, :]) + bo_ref[...]
    q = _dot(_rms(x2, g_ref[...]).astype(BF16), wq_ref[...]).astype(BF16)
    scale = hd ** -0.5
    heads = [slice(h * hd, (h + 1) * hd) for h in range(XATTN_HEADS)]
    scores = [_dot_nt(q[:, sl], k_ref[:, sl]) * scale for sl in heads]
    probs, dens = [], []
    for s in scores:
        e = jnp.exp(s - jnp.max(s, axis=-1, keepdims=True))
        probs.append(e.astype(BF16))
        dens.append(jnp.sum(e, axis=-1, keepdims=True))
    outs = [(_dot(e, v_ref[:, sl]) * (1.0 / den)).astype(BF16) for e, den, sl in zip(probs, dens, heads)]
    o_ref[...] = x2 + _dot(jnp.concatenate(outs, axis=1), wxo_ref[...])


def _mix(x, yr, ys, wo, bo, g, wq, kv, wxo, seq, mlen):
    n, d = x.shape
    c = yr.shape[1]
    tm = _pick(seq, (256, 128))
    per_seq = seq // tm
    once = pl.Buffered(1)
    return pl.pallas_call(
        _mix_kernel,
        grid=(n // tm,),
        in_specs=[
            pl.BlockSpec((tm, d), lambda i: (i, 0)),
            pl.BlockSpec((tm, c), lambda i: (i, 0)),
            pl.BlockSpec((tm, d - c), lambda i: (i, 0)),
            pl.BlockSpec((d, d), lambda i: (0, 0), pipeline_mode=once),
            pl.BlockSpec((1, d), lambda i: (0, 0)),
            pl.BlockSpec((1, d), lambda i: (0, 0)),
            pl.BlockSpec((d, d), lambda i: (0, 0), pipeline_mode=once),
            pl.BlockSpec((mlen, d), lambda i: (i // per_seq, 0)),
            pl.BlockSpec((mlen, d), lambda i: (i // per_seq, 1)),
            pl.BlockSpec((d, d), lambda i: (0, 0), pipeline_mode=once),
        ],
        out_specs=pl.BlockSpec((tm, d), lambda i: (i, 0)),
        out_shape=jax.ShapeDtypeStruct((n, d), F32),
        compiler_params=_cparams(1),
        name="mix",
    )(x, yr, ys, wo, bo.reshape(1, d), g.reshape(1, d), wq, kv, kv, wxo)


def _pad_cols(w, width):
    return jnp.pad(w, ((0, 0), (0, width - w.shape[1])))


def _pad_rows(w, height):
    return jnp.pad(w, ((0, height - w.shape[0]), (0, 0)))


def _rope_tables(seq):
    half = HEAD_DIM // 2
    lane = jnp.arange(LANES)
    inv_freq = ROPE_THETA ** (-jnp.arange(0, HEAD_DIM, 2, dtype=F32) / HEAD_DIM)
    ang = jnp.arange(seq, dtype=F32)[:, None] * inv_freq[lane % half][None, :]
    sign = jnp.where((lane % HEAD_DIM) < half, -1.0, 1.0)
    return jnp.cos(ang), jnp.sin(ang) * sign[None, :]


def kernel(x, mem, f1_norm, f1_gate, f1_up, f1_down, mix_norm, w_in, b_in_attn, rw_mu, rw_w0, rw_decay_up, rw_a0, rw_aaa_up, rw_gate_up, rw_k_k, rw_k_a, rw_r_k, rw_lnx_w, rw_lnx_b, attn_sinks, w_out, b_out, xa_norm, mem_norm, w_xq, w_xkv, w_xo, f2_norm, f2_gate, f2_up, f2_down, final_norm):
    batch, seq, d = x.shape
    mlen = mem.shape[1]
    depth = f1_norm.shape[0]
    c = rw_w0.shape[1]
    sw = d - c
    dl, al, gl = rw_decay_up.shape[1], rw_aaa_up.shape[1], rw_gate_up.shape[1]
    dlp, alp, glp = (_round_up(v, LANES) for v in (dl, al, gl))
    kvw = b_in_attn.shape[1] - sw
    q_heads = sw // HEAD_DIM
    group = q_heads // (kvw // (2 * HEAD_DIM))
    n_tiles = c // LANES
    n = batch * seq
    q_off = 3 * c
    lora_off = q_off + sw
    lora_w = dlp + alp + glp
    kv_off = lora_off + lora_w
    ncols = _round_up(kv_off + kvw, 2 * LANES)
    cos_t, sin_t = _rope_tables(seq)

    xf = x.reshape(n, d)
    memf = mem.reshape(batch * mlen, d)
    for l in range(depth):
        xf = _ffn(xf, f1_norm[l], f1_gate[l].astype(BF16), f1_up[l].astype(BF16), f1_down[l].astype(BF16), None)

        wl = w_in[l]
        o1, o2, o3 = 3 * c, 3 * c + dl, 3 * c + dl + al
        shift = o3 + gl
        w_all = jnp.concatenate([
            wl[:, :o1], wl[:, shift:shift + sw],
            _pad_cols(wl[:, o1:o2], dlp), _pad_cols(wl[:, o2:o3], alp), _pad_cols(wl[:, o3:shift], glp),
            wl[:, shift + sw:]], axis=1)
        w_all = _pad_cols(w_all, ncols).astype(BF16)
        mu = rw_mu[l][None, :]
        mu_all = _pad_cols(jnp.concatenate([
            mu[:, :o1], jnp.zeros((1, sw), F32),
            _pad_cols(mu[:, o1:o2], dlp), _pad_cols(mu[:, o2:o3], alp), _pad_cols(mu[:, o3:shift], glp)],
            axis=1), ncols)
        bia = b_in_attn[l][None, :]
        b_all = _pad_cols(jnp.concatenate([
            jnp.zeros((1, q_off), F32), bia[:, :sw], jnp.zeros((1, lora_w), F32), bia[:, sw:]], axis=1), ncols)
        z = _proj(xf, mix_norm[l], w_all, mu_all, b_all, seq)

        tiles = lambda v: v.reshape(n_tiles, 1, LANES)
        vecs = [rw_w0[l].reshape(1, c), rw_a0[l].reshape(1, c), tiles(rw_k_k[l]), tiles(rw_k_a[l]),
                tiles(rw_r_k[l]), tiles(rw_lnx_w[l]), tiles(rw_lnx_b[l])]
        loras = [_pad_rows(rw_decay_up[l], dlp).astype(BF16), _pad_rows(rw_aaa_up[l], alp).astype(BF16),
                 _pad_rows(rw_gate_up[l], glp).astype(BF16)]
        y_rwkv, (f2_down_b, w_out_b, w_xq_b, w_xo_b) = _rwkv(
            z, batch, seq, c, lora_off, lora_w, dlp, alp, vecs, loras,
            [f2_down[l], w_out[l], w_xq[l], w_xo[l]])
        y_swa, (f2_gate_b, f2_up_b) = _swa(z, attn_sinks[l], cos_t, sin_t, batch, seq, q_off, sw, kv_off, kvw,
                                           group, [f2_gate[l], f2_up[l]])

        kv_mem = _memkv(memf, mem_norm[l], w_xkv[l])
        xf = _mix(xf, y_rwkv, y_swa, w_out_b, b_out[l], xa_norm[l], w_xq_b, kv_mem, w_xo_b, seq, mlen)

        last = l == depth - 1
        xf = _ffn(xf, f2_norm[l], f2_gate_b, f2_up_b, f2_down_b, final_norm if last else None)
    return xf.reshape(batch, seq, d)
```

```python
import functools

import jax
import jax.numpy as jnp
from jax import lax
from jax.experimental import pallas as pl
from jax.experimental.pallas import tpu as pltpu

F32, BF16 = jnp.float32, jnp.bfloat16

LANES = 128
SUBLANES = 8
VMEM_LIMIT_BYTES = 56 * 1024 * 1024

HEAD_DIM = 64
HEADS_PER_TILE = LANES // HEAD_DIM
CHUNK = 128
BLOCK = 128
XATTN_HEADS = 4
RMS_EPS = 1e-6
GN_EPS = 64e-5
NEG_INF = -1e30
ROPE_THETA = 10000.0
EXP_M_HALF = 0.6065306597126334
LOG2_E = 1.4426950408889634


def _round_up(n, m):
    return (n + m - 1) // m * m


def _pick(n, prefs):
    for p in prefs:
        if n % p == 0:
            return p
    raise ValueError(f"no tile in {prefs} divides {n}")


def _cparams(n_axes):
    return pltpu.CompilerParams(dimension_semantics=("arbitrary",) * n_axes,
                                vmem_limit_bytes=VMEM_LIMIT_BYTES)


def _dot(a, b):
    return jnp.dot(a, b, preferred_element_type=F32)


def _dot_nt(a, b):
    return lax.dot_general(a, b, (((1,), (1,)), ((), ())), preferred_element_type=F32)


def _cast_specs(weights, nsteps, flat_step):
    in_specs, out_specs, out_shapes = [], [], []
    for w in weights:
        units = w.shape[0] // (2 * SUBLANES)
        nblk = max(d for d in range(1, nsteps + 1) if units % d == 0)
        imap = lambda *g, nblk=nblk: (jnp.minimum(flat_step(*g), nblk - 1), 0)
        spec = pl.BlockSpec((w.shape[0] // nblk, w.shape[1]), imap)
        in_specs.append(spec)
        out_specs.append(spec)
        out_shapes.append(jax.ShapeDtypeStruct(w.shape, BF16))
    return in_specs, out_specs, out_shapes


def _cast_blocks(src_refs, dst_refs):
    for src, dst in zip(src_refs, dst_refs):
        dst[...] = src[...].astype(BF16)


def _rms(x, g):
    ms = jnp.mean(x * x, axis=-1, keepdims=True)
    return x * lax.rsqrt(ms + RMS_EPS) * g


def _ffn_kernel(x_ref, g_ref, wg_ref, wu_ref, wd_ref, fg_ref, o_ref, h_ref, *, final_norm, rows):
    j = pl.program_id(1)
    tm = x_ref.shape[0]

    def hidden_tile(first):
        for r0 in range(0, tm, rows):
            rs = slice(r0, r0 + rows)
            if first:
                base = x_ref[rs, :]
                h = _rms(base, g_ref[...]).astype(BF16)
                h_ref[rs, :] = h
            else:
                base = o_ref[rs, :]
                h = h_ref[rs, :]
            gate = _dot(h, wg_ref[...])
            up = _dot(h, wu_ref[...])
            act = (gate * jax.nn.sigmoid(gate) * up).astype(BF16)
            o_ref[rs, :] = base + 0.5 * _dot(act, wd_ref[...])

    pl.when(j == 0)(lambda: hidden_tile(True))
    pl.when(j > 0)(lambda: hidden_tile(False))

    if final_norm:
        @pl.when(j == pl.num_programs(1) - 1)
        def _():
            for r0 in range(0, tm, rows):
                o_ref[r0:r0 + rows, :] = _rms(o_ref[r0:r0 + rows, :], fg_ref[...])


def _ffn(x, g, wg, wu, wd, fg):
    n, d = x.shape
    f = wg.shape[1]
    tm = _pick(n, (1024, 512, 256, 128))
    tf = _pick(f, (512, 256, 128))
    rows = min(tm, 512)
    final_norm = fg is not None
    fg = g if fg is None else fg
    return pl.pallas_call(
        functools.partial(_ffn_kernel, final_norm=final_norm, rows=rows),
        grid=(n // tm, f // tf),
        in_specs=[
            pl.BlockSpec((tm, d), lambda i, j: (i, 0)),
            pl.BlockSpec((1, d), lambda i, j: (0, 0)),
            pl.BlockSpec((d, tf), lambda i, j: (0, j)),
            pl.BlockSpec((d, tf), lambda i, j: (0, j)),
            pl.BlockSpec((tf, d), lambda i, j: (j, 0)),
            pl.BlockSpec((1, d), lambda i, j: (0, 0)),
        ],
        out_specs=pl.BlockSpec((tm, d), lambda i, j: (i, 0)),
        out_shape=jax.ShapeDtypeStruct((n, d), F32),
        scratch_shapes=[pltpu.VMEM((tm, d), BF16)],
        compiler_params=_cparams(2),
        name="ffn",
    )(x, g.reshape(1, d), wg, wu, wd, fg.reshape(1, d))


def _proj_kernel(x_ref, g_ref, w_ref, mu_ref, b_ref, o_ref, carry_ref, *, tiles_per_seq, tn):
    i = pl.program_id(0)
    tm = x_ref.shape[0]
    ncols = w_ref.shape[1]

    @pl.when(i % tiles_per_seq == 0)
    def _():
        carry_ref[...] = jnp.zeros(carry_ref.shape, F32)

    h = _rms(x_ref[...], g_ref[...]).astype(BF16)
    row = lax.broadcasted_iota(jnp.int32, (SUBLANES, tn), 0)
    for c0 in range(0, ncols, tn):
        cs = slice(c0, c0 + tn)
        z = _dot(h, w_ref[:, cs])
        prev_tail = carry_ref[:, cs]
        carry_ref[:, cs] = z[tm - SUBLANES:, :]
        zs = pltpu.roll(z, 1, 0)
        head = jnp.where(row == 0, pltpu.roll(prev_tail, 1, 0), zs[:SUBLANES])
        mu = mu_ref[:, cs]
        b = b_ref[:, cs]
        z0 = z[:SUBLANES]
        o_ref[:SUBLANES, cs] = z0 + (head - z0) * mu + b
        z1 = z[SUBLANES:]
        o_ref[SUBLANES:, cs] = z1 + (zs[SUBLANES:] - z1) * mu + b


def _proj(x, g, w_all, mu_all, b_all, seq):
    n, d = x.shape
    ncols = w_all.shape[1]
    tm = _pick(seq, (512, 256, 128))
    tn = _pick(ncols, (1024, 512, 256, 128))
    return pl.pallas_call(
        functools.partial(_proj_kernel, tiles_per_seq=seq // tm, tn=tn),
        grid=(n // tm,),
        in_specs=[
            pl.BlockSpec((tm, d), lambda i: (i, 0)),
            pl.BlockSpec((1, d), lambda i: (0, 0)),
            pl.BlockSpec((d, ncols), lambda i: (0, 0), pipeline_mode=pl.Buffered(1)),
            pl.BlockSpec((1, ncols), lambda i: (0, 0)),
            pl.BlockSpec((1, ncols), lambda i: (0, 0)),
        ],
        out_specs=pl.BlockSpec((tm, ncols), lambda i: (i, 0)),
        out_shape=jax.ShapeDtypeStruct((n, ncols), F32),
        scratch_shapes=[pltpu.VMEM((SUBLANES, ncols), F32)],
        compiler_params=_cparams(1),
        name="proj",
    )(x, g.reshape(1, d), w_all, mu_all, b_all)


def _bdot(a, b):
    return lax.dot_general(a, b, (((2,), (1,)), ((0,), (0,))), preferred_element_type=F32)


def _bdot_nt(a, b):
    return lax.dot_general(a, b, (((2,), (2,)), ((0,), (0,))), preferred_element_type=F32)


def _inv_unit_lower(nmat):
    L = nmat.shape[-1]
    r = lax.broadcasted_iota(jnp.int32, (L, L), 0)
    c = lax.broadcasted_iota(jnp.int32, (L, L), 1)
    eye = jnp.where(r == c, 1.0, 0.0)
    t = eye + nmat
    pw = nmat.astype(BF16)
    pw = _bdot(pw, pw).astype(BF16)
    steps = L.bit_length() - 2
    for i in range(steps):
        if i + 1 < steps:
            both = _bdot(jnp.concatenate([t.astype(BF16), pw], axis=1), pw)
            t = t + both[:, :L]
            pw = both[:, L:].astype(BF16)
        else:
            t = t + _bdot(t.astype(BF16), pw)
    return t


RWKV_INPUTS = 14


def _rwkv_kernel(*refs, n_cast, **static):
    cast_in = refs[RWKV_INPUTS:RWKV_INPUTS + n_cast]
    cast_out = refs[RWKV_INPUTS + n_cast + 1:RWKV_INPUTS + 2 * n_cast + 1]
    chunk_refs = refs[:RWKV_INPUTS] + (refs[RWKV_INPUTS + n_cast],) + refs[RWKV_INPUTS + 2 * n_cast + 1:]
    _rwkv_chunk(*chunk_refs, **static)
    _cast_blocks(cast_in, cast_out)


def _rwkv_chunk(r_ref, k_ref, v_ref, lora_ref, w0_ref, a0_ref, kk_ref, ka_ref, rk_ref, gw_ref, gb_ref,
                du_ref, au_ref, gu_ref,
                o_ref,
                s_ref, pr_ref, pk_ref, pv_ref, pa_ref, pg_ref, plp_ref, plw_ref,
                *, n_tiles, sub, dlp, alp):
    L = CHUNK
    nt = sub * n_tiles

    @pl.when(pl.program_id(1) == 0)
    def _():
        s_ref[...] = jnp.zeros(s_ref.shape, F32)

    lora = lora_ref[...]
    wd = jnp.tanh(lora[:, :dlp])
    ad = lora[:, dlp:dlp + alp]
    gd = jax.nn.sigmoid(lora[:, dlp + alp:])
    lw = -EXP_M_HALF * jax.nn.sigmoid(w0_ref[...] + _dot(wd.astype(BF16), du_ref[...]))
    asig = jax.nn.sigmoid(a0_ref[...] + _dot(ad.astype(BF16), au_ref[...]))
    gate = _dot(gd.astype(BF16), gu_ref[...])
    row = lax.broadcasted_iota(jnp.int32, (L, L), 0)
    col = lax.broadcasted_iota(jnp.int32, (L, L), 1)
    incl = row >= col
    strict = row > col
    tril = jnp.where(incl, 1.0, 0.0).astype(BF16)
    h1 = lw.astype(BF16)
    h2 = (lw - h1.astype(F32)).astype(BF16)
    for c in range(sub):
        rows = slice(c * L, (c + 1) * L)
        logp = _dot(tril, h1[rows]) + _dot(tril, h2[rows])
        for p in range(n_tiles):
            sl = slice(p * LANES, (p + 1) * LANES)
            i = c * n_tiles + p
            pr_ref[i] = r_ref[rows, sl]
            pk_ref[i] = k_ref[rows, sl]
            pv_ref[i] = v_ref[rows, sl]
            pa_ref[i] = asig[rows, sl]
            pg_ref[i] = gate[rows, sl]
            plp_ref[i] = logp[:, sl]
            plw_ref[i] = lw[rows, sl]

    lane = lax.broadcasted_iota(jnp.int32, (1, LANES), 1)
    lo = lane < HEAD_DIM
    same_head = jnp.where(row < HEAD_DIM, 0, 1) == jnp.where(col < HEAD_DIM, 0, 1)
    inv_hd = 1.0 / HEAD_DIM

    def first(x):
        return jnp.where(lo, x, 0.0)

    def second(x):
        return jnp.where(lo, 0.0, x)

    def segsum(x):
        return jnp.where(lo, jnp.sum(first(x), axis=-1, keepdims=True),
                         jnp.sum(second(x), axis=-1, keepdims=True))

    def per_tile(ref):
        return jnp.concatenate([ref[...]] * sub, axis=0)

    r = pr_ref[...]
    k = pk_ref[...]
    v = pv_ref[...]
    a_s = pa_ref[...]
    lp = plp_ref[...]
    kk = k * per_tile(kk_ref)
    ss = segsum(kk * kk)
    kk = kk * lax.rsqrt(jnp.maximum(ss, 1e-24))
    a = -kk
    b = kk * a_s
    km = k * (1.0 + (a_s - 1.0) * per_tile(ka_ref))
    cmid = lp[:, L // 2 - 1:L // 2, :]
    clast = lp[:, L - 1:L, :]
    lpe = lp - plw_ref[...]
    e_inv = jnp.exp(cmid - lp)
    e_mid = jnp.exp(cmid)
    at = a * jnp.exp(lpe - cmid)
    a_abs = at * e_mid
    rt = r * jnp.exp(lp - cmid)
    r_abs = rt * e_mid
    bt = b * e_inv
    kt = km * e_inv
    e_l = jnp.exp(clast - cmid)
    e_p = jnp.exp(clast)
    lhs = jnp.concatenate([first(at), second(at), first(rt), second(rt)], axis=1).astype(BF16)
    gmat = _bdot_nt(lhs, jnp.concatenate([bt, kt], axis=1).astype(BF16))
    n_lo = jnp.where(strict, gmat[:, 0:L, 0:L], 0.0)
    ak_lo = jnp.where(strict, gmat[:, 0:L, L:], 0.0)
    n_hi = jnp.where(strict, gmat[:, L:2 * L, 0:L], 0.0)
    ak_hi = jnp.where(strict, gmat[:, L:2 * L, L:], 0.0)
    rb_lo = jnp.where(incl, gmat[:, 2 * L:3 * L, 0:L], 0.0)
    rk_lo = jnp.where(incl, gmat[:, 2 * L:3 * L, L:], 0.0)
    rb_hi = jnp.where(incl, gmat[:, 3 * L:, 0:L], 0.0)
    rk_hi = jnp.where(incl, gmat[:, 3 * L:, L:], 0.0)
    t_all = _inv_unit_lower(jnp.concatenate([n_lo, n_hi], axis=0))
    t_wide = jnp.concatenate([t_all[:nt], t_all[nt:]], axis=2).astype(BF16)
    ak_wide = jnp.concatenate([ak_lo, ak_hi], axis=2).astype(BF16)
    read_wide = jnp.concatenate([rb_lo, rb_hi, rk_lo, rk_hi], axis=2).astype(BF16)
    v_lohi = jnp.concatenate([first(v), second(v)], axis=1).astype(BF16)
    vt = jnp.swapaxes(v, 1, 2)
    bk_last = jnp.concatenate([bt * e_l, kt * e_l], axis=1).astype(BF16)
    a_abs_b = a_abs.astype(BF16)
    r_abs_b = r_abs.astype(BF16)
    ys = []
    for c in range(sub):
        tl = slice(c * n_tiles, (c + 1) * n_tiles)
        s = s_ref[...]
        s_b = s.astype(BF16)
        rhs = _bdot_nt(a_abs_b[tl], s_b) + _bdot(ak_wide[tl], v_lohi[tl])
        u = _bdot(t_wide[tl], jnp.concatenate([first(rhs), second(rhs)], axis=1).astype(BF16))
        u_lohi = jnp.concatenate([first(u), second(u)], axis=1).astype(BF16)
        ys.append(_bdot_nt(r_abs_b[tl], s_b) + _bdot(read_wide[tl], jnp.concatenate([u_lohi, v_lohi[tl]], axis=1)))
        upd = _bdot(jnp.concatenate([jnp.swapaxes(u, 1, 2), vt[tl]], axis=2).astype(BF16), bk_last[tl])
        s_ref[...] = s * e_p[tl] + jnp.where(same_head, upd, 0.0)
    y = jnp.concatenate(ys, axis=0)
    mean = segsum(y) * inv_hd
    dev = y - mean
    var = segsum(dev * dev) * inv_hd
    yn = dev * lax.rsqrt(var + GN_EPS) * per_tile(gw_ref) + per_tile(gb_ref)
    bonus = segsum(r * km * per_tile(rk_ref)) * v
    out = ((yn + bonus) * pg_ref[...]).astype(BF16)
    for c in range(sub):
        for p in range(n_tiles):
            o_ref[c * L:(c + 1) * L, p * LANES:(p + 1) * LANES] = out[c * n_tiles + p]


def _rwkv(z, batch, seq, c, lora_off, lora_w, dlp, alp, vecs, loras, casts):
    n = z.shape[0]
    nc = seq // CHUNK
    n_tiles = c // LANES
    sub = _pick(nc, (2, 1))
    steps = nc // sub
    rows = sub * CHUNK
    assert lora_off % lora_w == 0 and len(vecs) + len(loras) + 4 == RWKV_INPUTS
    lora_blk = lora_off // lora_w
    cast_in, cast_out, cast_shapes = _cast_specs(casts, batch * steps, lambda b, t: b * steps + t)
    vec_spec = pl.BlockSpec((n_tiles, 1, LANES), lambda b, t: (0, 0, 0))
    in_specs = [
        pl.BlockSpec((rows, c), lambda b, t: (b * steps + t, 0)),
        pl.BlockSpec((rows, c), lambda b, t: (b * steps + t, 1)),
        pl.BlockSpec((rows, c), lambda b, t: (b * steps + t, 2)),
        pl.BlockSpec((rows, lora_w), lambda b, t: (b * steps + t, lora_blk)),
        pl.BlockSpec((1, c), lambda b, t: (0, 0)),
        pl.BlockSpec((1, c), lambda b, t: (0, 0)),
    ] + [vec_spec] * 5 + [pl.BlockSpec(w.shape, lambda b, t: (0, 0)) for w in loras]
    tile_f32 = pltpu.VMEM((sub * n_tiles, CHUNK, LANES), F32)
    y, *cast = pl.pallas_call(
        functools.partial(_rwkv_kernel, n_cast=len(casts), n_tiles=n_tiles, sub=sub, dlp=dlp, alp=alp),
        grid=(batch, steps),
        in_specs=in_specs + cast_in,
        out_specs=[pl.BlockSpec((rows, c), lambda b, t: (b * steps + t, 0))] + cast_out,
        out_shape=[jax.ShapeDtypeStruct((n, c), BF16)] + cast_shapes,
        scratch_shapes=[pltpu.VMEM((n_tiles, LANES, LANES), F32)] + [tile_f32] * 7,
        compiler_params=_cparams(2),
        name="rwkv",
    )(z, z, z, z, *vecs, *loras, *casts)
    return y, cast


def _swa_kernel(sink_ref, q_ref, kvc_ref, kvp_ref, cosc_ref, sinc_ref, cosp_ref, sinp_ref, *rest,
                n_tiles, group, qb):
    n_cast = len(rest) // 2
    o_ref = rest[n_cast]
    _cast_blocks(rest[:n_cast], rest[n_cast + 1:])
    nblk = pl.program_id(1)
    lane = lax.broadcasted_iota(jnp.int32, (1, LANES), 1)
    lo = lane < HEAD_DIM
    rot_lo = jnp.bitwise_and(lane, HEAD_DIM - 1) < HEAD_DIM // 2

    def rope(x, cos, sin_signed):
        partner = jnp.where(rot_lo, pltpu.roll(x, LANES - HEAD_DIM // 2, 1), pltpu.roll(x, HEAD_DIM // 2, 1))
        return x * cos + partner * sin_signed

    cosc = cosc_ref[...]
    sinc = sinc_ref[...]
    kvc = kvc_ref[...]
    kvp = kvp_ref[...]
    keys = jnp.concatenate([rope(kvp[:, :LANES], cosp_ref[...], sinp_ref[...]),
                            rope(kvc[:, :LANES], cosc, sinc)], axis=0)
    vals = jnp.concatenate([kvp[:, LANES:], kvc[:, LANES:]], axis=0)
    keys_sw = pltpu.roll(keys, HEAD_DIM, 1)
    vals_sw = pltpu.roll(vals, HEAD_DIM, 1)
    k_first = [jnp.where(lo, keys, 0.0).astype(BF16), jnp.where(lo, keys_sw, 0.0).astype(BF16)]
    k_second = [jnp.where(lo, 0.0, keys_sw).astype(BF16), jnp.where(lo, 0.0, keys).astype(BF16)]
    ones_kv = jnp.ones(vals.shape, BF16)
    with_ones = lambda x: jnp.concatenate([x.astype(BF16), ones_kv], axis=1)
    v_first = [with_ones(jnp.where(lo, vals, 0.0)), with_ones(jnp.where(lo, vals_sw, 0.0))]
    v_second = [with_ones(jnp.where(lo, 0.0, vals_sw)), with_ones(jnp.where(lo, 0.0, vals))]
    qi = lax.broadcasted_iota(jnp.int32, (BLOCK, 2 * BLOCK), 0)
    ki = lax.broadcasted_iota(jnp.int32, (BLOCK, 2 * BLOCK), 1)
    window = (ki > qi) & (ki <= qi + BLOCK)
    valid = [window & ((nblk > 0) | (ki >= BLOCK))] + [window] * (qb - 1)
    scale = HEAD_DIM ** -0.5 * LOG2_E
    heads = [(j, p, half) for j in range(qb) for p in range(n_tiles) for half in range(HEADS_PER_TILE)]
    scores = []
    for j in range(qb):
        qrows = slice(j * BLOCK, (j + 1) * BLOCK)
        krows = slice(j * BLOCK, (j + 2) * BLOCK)
        for p in range(n_tiles):
            g = (p * HEADS_PER_TILE) // group
            qp = (rope(q_ref[qrows, p * LANES:(p + 1) * LANES], cosc[qrows], sinc[qrows]) * scale).astype(BF16)
            scores += [_dot_nt(qp, k_first[g][krows]), _dot_nt(qp, k_second[g][krows])]
    probs, sink_terms = [], []
    for (j, p, half), s in zip(heads, scores):
        s = jnp.where(valid[j], s, NEG_INF)
        sink = sink_ref[p * HEADS_PER_TILE + half] * LOG2_E
        m = jnp.maximum(jnp.max(s, axis=-1, keepdims=True), sink)
        probs.append(jnp.exp2(s - m).astype(BF16))
        sink_terms.append(jnp.exp2(sink - m))
    outs = []
    for (j, p, half), e, st in zip(heads, probs, sink_terms):
        g = (p * HEADS_PER_TILE) // group
        krows = slice(j * BLOCK, (j + 2) * BLOCK)
        pv = _dot(e, (v_second[g] if half else v_first[g])[krows])
        outs.append(pv[:, :LANES] * (1.0 / (pv[:, LANES:] + st)))
    for j in range(qb):
        for p in range(n_tiles):
            i = (j * n_tiles + p) * HEADS_PER_TILE
            o_ref[j * BLOCK:(j + 1) * BLOCK, p * LANES:(p + 1) * LANES] = (outs[i] + outs[i + 1]).astype(BF16)


def _swa(z, sinks, cos_t, sin_t, batch, seq, q_off, qw, kv_off, kvw, group, casts):
    n = z.shape[0]
    nb = seq // BLOCK
    assert q_off % qw == 0 and kv_off % kvw == 0 and kvw == 2 * LANES
    q_blk, kv_blk = q_off // qw, kv_off // kvw
    qb = _pick(nb, (4, 2, 1))
    steps = nb // qb
    prev = lambda t: jnp.maximum(t * qb - 1, 0)
    cast_in, cast_out, cast_shapes = _cast_specs(casts, batch * steps, lambda b, t: b * steps + t)
    y, *cast = pl.pallas_call(
        functools.partial(_swa_kernel, n_tiles=qw // LANES, group=group, qb=qb),
        grid=(batch, steps),
        in_specs=[
            pl.BlockSpec(memory_space=pltpu.SMEM),
            pl.BlockSpec((qb * BLOCK, qw), lambda b, t: (b * steps + t, q_blk)),
            pl.BlockSpec((qb * BLOCK, kvw), lambda b, t: (b * steps + t, kv_blk)),
            pl.BlockSpec((BLOCK, kvw), lambda b, t: (b * nb + prev(t), kv_blk)),
            pl.BlockSpec((qb * BLOCK, LANES), lambda b, t: (t, 0)),
            pl.BlockSpec((qb * BLOCK, LANES), lambda b, t: (t, 0)),
            pl.BlockSpec((BLOCK, LANES), lambda b, t: (prev(t), 0)),
            pl.BlockSpec((BLOCK, LANES), lambda b, t: (prev(t), 0)),
        ] + cast_in,
        out_specs=[pl.BlockSpec((qb * BLOCK, qw), lambda b, t: (b * steps + t, 0))] + cast_out,
        out_shape=[jax.ShapeDtypeStruct((n, qw), BF16)] + cast_shapes,
        compiler_params=_cparams(2),
        name="swa",
    )(sinks, z, z, z, cos_t, sin_t, cos_t, sin_t, *casts)
    return y, cast


def _memkv_kernel(m_ref, g_ref, w_ref, o_ref):
    h = _rms(m_ref[...], g_ref[...]).astype(BF16)
    o_ref[...] = _dot(h, w_ref[...].astype(BF16)).astype(BF16)


def _memkv(mem, g, w):
    n, d = mem.shape
    ncols = w.shape[1]
    tm = _pick(n, (512, 256, 128))
    tn = _pick(ncols, (1024, 512, 256, 128))
    return pl.pallas_call(
        _memkv_kernel,
        grid=(ncols // tn, n // tm),
        in_specs=[
            pl.BlockSpec((tm, d), lambda j, i: (i, 0)),
            pl.BlockSpec((1, d), lambda j, i: (0, 0)),
            pl.BlockSpec((d, tn), lambda j, i: (0, j)),
        ],
        out_specs=pl.BlockSpec((tm, tn), lambda j, i: (i, j)),
        out_shape=jax.ShapeDtypeStruct((n, ncols), BF16),
        compiler_params=_cparams(2),
        name="memkv",
    )(mem, g.reshape(1, d), w)


def _mix_kernel(x_ref, yr_ref, ys_ref, wo_ref, bo_ref, g_ref, wq_ref, k_ref, v_ref, wxo_ref, o_ref):
    c = yr_ref.shape[1]
    d = x_ref.shape[1]
    hd = d // XATTN_HEADS
    x2 = x_ref[...] + _dot(yr_ref[...], wo_ref[:c, :]) + _dot(ys_ref[...], wo_ref[c:, :]) + bo_ref[...]
    q = _dot(_rms(x2, g_ref[...]).astype(BF16), wq_ref[...]).astype(BF16)
    scale = hd ** -0.5
    heads = [slice(h * hd, (h + 1) * hd) for h in range(XATTN_HEADS)]
    scores = [_dot_nt(q[:, sl], k_ref[:, sl]) * scale for sl in heads]
    probs, dens = [], []
    for s in scores:
        e = jnp.exp(s - jnp.max(s, axis=-1, keepdims=True))
        probs.append(e.astype(BF16))
        dens.append(jnp.sum(e, axis=-1, keepdims=True))
    outs = [(_dot(e, v_ref[:, sl]) * (1.0 / den)).astype(BF16) for e, den, sl in zip(probs, dens, heads)]
    o_ref[...] = x2 + _dot(jnp.concatenate(outs, axis=1), wxo_ref[...])


def _mix(x, yr, ys, wo, bo, g, wq, kv, wxo, seq, mlen):
    n, d = x.shape
    c = yr.shape[1]
    tm = _pick(seq, (256, 128))
    per_seq = seq // tm
    once = pl.Buffered(1)
    return pl.pallas_call(
        _mix_kernel,
        grid=(n // tm,),
        in_specs=[
            pl.BlockSpec((tm, d), lambda i: (i, 0)),
            pl.BlockSpec((tm, c), lambda i: (i, 0)),
            pl.BlockSpec((tm, d - c), lambda i: (i, 0)),
            pl.BlockSpec((d, d), lambda i: (0, 0), pipeline_mode=once),
            pl.BlockSpec((1, d), lambda i: (0, 0)),
            pl.BlockSpec((1, d), lambda i: (0, 0)),
            pl.BlockSpec((d, d), lambda i: (0, 0), pipeline_mode=once),
            pl.BlockSpec((mlen, d), lambda i: (i // per_seq, 0)),
            pl.BlockSpec((mlen, d), lambda i: (i // per_seq, 1)),
            pl.BlockSpec((d, d), lambda i: (0, 0), pipeline_mode=once),
        ],
        out_specs=pl.BlockSpec((tm, d), lambda i: (i, 0)),
        out_shape=jax.ShapeDtypeStruct((n, d), F32),
        compiler_params=_cparams(1),
        name="mix",
    )(x, yr, ys, wo, bo.reshape(1, d), g.reshape(1, d), wq, kv, kv, wxo)


def _pad_cols(w, width):
    return jnp.pad(w, ((0, 0), (0, width - w.shape[1])))


def _pad_rows(w, height):
    return jnp.pad(w, ((0, height - w.shape[0]), (0, 0)))


def _rope_tables(seq):
    half = HEAD_DIM // 2
    lane = jnp.arange(LANES)
    inv_freq = ROPE_THETA ** (-jnp.arange(0, HEAD_DIM, 2, dtype=F32) / HEAD_DIM)
    ang = jnp.arange(seq, dtype=F32)[:, None] * inv_freq[lane % half][None, :]
    sign = jnp.where((lane % HEAD_DIM) < half, -1.0, 1.0)
    return jnp.cos(ang), jnp.sin(ang) * sign[None, :]


def kernel(x, mem, f1_norm, f1_gate, f1_up, f1_down, mix_norm, w_in, b_in_attn, rw_mu, rw_w0, rw_decay_up, rw_a0, rw_aaa_up, rw_gate_up, rw_k_k, rw_k_a, rw_r_k, rw_lnx_w, rw_lnx_b, attn_sinks, w_out, b_out, xa_norm, mem_norm, w_xq, w_xkv, w_xo, f2_norm, f2_gate, f2_up, f2_down, final_norm):
    batch, seq, d = x.shape
    mlen = mem.shape[1]
    depth = f1_norm.shape[0]
    c = rw_w0.shape[1]
    sw = d - c
    dl, al, gl = rw_decay_up.shape[1], rw_aaa_up.shape[1], rw_gate_up.shape[1]
    dlp, alp, glp = (_round_up(v, LANES) for v in (dl, al, gl))
    kvw = b_in_attn.shape[1] - sw
    q_heads = sw // HEAD_DIM
    group = q_heads // (kvw // (2 * HEAD_DIM))
    n_tiles = c // LANES
    n = batch * seq
    q_off = 3 * c
    lora_off = q_off + sw
    lora_w = dlp + alp + glp
    kv_off = lora_off + lora_w
    ncols = _round_up(kv_off + kvw, 2 * LANES)
    cos_t, sin_t = _rope_tables(seq)

    xf = x.reshape(n, d)
    memf = mem.reshape(batch * mlen, d)
    for l in range(depth):
        xf = _ffn(xf, f1_norm[l], f1_gate[l].astype(BF16), f1_up[l].astype(BF16), f1_down[l].astype(BF16), None)

        wl = w_in[l]
        o1, o2, o3 = 3 * c, 3 * c + dl, 3 * c + dl + al
        shift = o3 + gl
        w_all = jnp.concatenate([
            wl[:, :o1], wl[:, shift:shift + sw],
            _pad_cols(wl[:, o1:o2], dlp), _pad_cols(wl[:, o2:o3], alp), _pad_cols(wl[:, o3:shift], glp),
            wl[:, shift + sw:]], axis=1)
        w_all = _pad_cols(w_all, ncols).astype(BF16)
        mu = rw_mu[l][None, :]
        mu_all = _pad_cols(jnp.concatenate([
            mu[:, :o1], jnp.zeros((1, sw), F32),
            _pad_cols(mu[:, o1:o2], dlp), _pad_cols(mu[:, o2:o3], alp), _pad_cols(mu[:, o3:shift], glp)],
            axis=1), ncols)
        bia = b_in_attn[l][None, :]
        b_all = _pad_cols(jnp.concatenate([
            jnp.zeros((1, q_off), F32), bia[:, :sw], jnp.zeros((1, lora_w), F32), bia[:, sw:]], axis=1), ncols)
        z = _proj(xf, mix_norm[l], w_all, mu_all, b_all, seq)

        tiles = lambda v: v.reshape(n_tiles, 1, LANES)
        vecs = [rw_w0[l].reshape(1, c), rw_a0[l].reshape(1, c), tiles(rw_k_k[l]), tiles(rw_k_a[l]),
                tiles(rw_r_k[l]), tiles(rw_lnx_w[l]), tiles(rw_lnx_b[l])]
        loras = [_pad_rows(rw_decay_up[l], dlp).astype(BF16), _pad_rows(rw_aaa_up[l], alp).astype(BF16),
                 _pad_rows(rw_gate_up[l], glp).astype(BF16)]
        y_rwkv, (f2_down_b, w_out_b, w_xq_b, w_xo_b) = _rwkv(
            z, batch, seq, c, lora_off, lora_w, dlp, alp, vecs, loras,
            [f2_down[l], w_out[l], w_xq[l], w_xo[l]])
        y_swa, (f2_gate_b, f2_up_b) = _swa(z, attn_sinks[l], cos_t, sin_t, batch, seq, q_off, sw, kv_off, kvw,
                                           group, [f2_gate[l], f2_up[l]])

        kv_mem = _memkv(memf, mem_norm[l], w_xkv[l])
        xf = _mix(xf, y_rwkv, y_swa, w_out_b, b_out[l], xa_norm[l], w_xq_b, kv_mem, w_xo_b, seq, mlen)

        last = l == depth - 1
        xf = _ffn(xf, f2_norm[l], f2_gate_b, f2_up_b, f2_down_b, final_norm if last else None)
    return xf.reshape(batch, seq, d)
```

```python
import functools

import jax
import jax.numpy as jnp
from jax import lax
from jax.experimental import pallas as pl
from jax.experimental.pallas import tpu as pltpu

F32, BF16 = jnp.float32, jnp.bfloat16

LANES = 128
SUBLANES = 8
VMEM_LIMIT_BYTES = 56 * 1024 * 1024

HEAD_DIM = 64
HEADS_PER_TILE = LANES // HEAD_DIM
CHUNK = 128
BLOCK = 128
XATTN_HEADS = 4
RMS_EPS = 1e-6
GN_EPS = 64e-5
NEG_INF = -1e30
ROPE_THETA = 10000.0
EXP_M_HALF = 0.6065306597126334
LOG2_E = 1.4426950408889634


def _round_up(n, m):
    return (n + m - 1) // m * m


def _pick(n, prefs):
    for p in prefs:
        if n % p == 0:
            return p
    raise ValueError(f"no tile in {prefs} divides {n}")


def _cparams(n_axes):
    return pltpu.CompilerParams(dimension_semantics=("arbitrary",) * n_axes,
                                vmem_limit_bytes=VMEM_LIMIT_BYTES)


def _dot(a, b):
    return jnp.dot(a, b, preferred_element_type=F32)


def _dot_nt(a, b):
    return lax.dot_general(a, b, (((1,), (1,)), ((), ())), preferred_element_type=F32)


def _cast_specs(weights, nsteps, flat_step):
    in_specs, out_specs, out_shapes = [], [], []
    for w in weights:
        units = w.shape[0] // (2 * SUBLANES)
        nblk = max(d for d in range(1, nsteps + 1) if units % d == 0)
        imap = lambda *g, nblk=nblk: (jnp.minimum(flat_step(*g), nblk - 1), 0)
        spec = pl.BlockSpec((w.shape[0] // nblk, w.shape[1]), imap)
        in_specs.append(spec)
        out_specs.append(spec)
        out_shapes.append(jax.ShapeDtypeStruct(w.shape, BF16))
    return in_specs, out_specs, out_shapes


def _cast_blocks(src_refs, dst_refs):
    for src, dst in zip(src_refs, dst_refs):
        dst[...] = src[...].astype(BF16)


def _rms(x, g):
    ms = jnp.mean(x * x, axis=-1, keepdims=True)
    return x * lax.rsqrt(ms + RMS_EPS) * g


def _ffn_kernel(x_ref, g_ref, wg_ref, wu_ref, wd_ref, fg_ref, o_ref, h_ref, *, final_norm, rows):
    j = pl.program_id(1)
    tm = x_ref.shape[0]

    def hidden_tile(first, last):
        for r0 in range(0, tm, rows):
            rs = slice(r0, r0 + rows)
            if first:
                base = x_ref[rs, :]
                h = _rms(base, g_ref[...]).astype(BF16)
                h_ref[rs, :] = h
            else:
                base = o_ref[rs, :]
                h = h_ref[rs, :]
            gate = _dot(h, wg_ref[...])
            up = _dot(h, wu_ref[...])
            act = (gate * jax.nn.sigmoid(gate) * up).astype(BF16)
            out = base + 0.5 * _dot(act, wd_ref[...])
            o_ref[rs, :] = _rms(out, fg_ref[...]) if last else out

    pl.when(j == 0)(lambda: hidden_tile(True, False))
    if final_norm:
        n_last = pl.num_programs(1) - 1
        pl.when((j > 0) & (j < n_last))(lambda: hidden_tile(False, False))
        pl.when(j == n_last)(lambda: hidden_tile(False, True))
    else:
        pl.when(j > 0)(lambda: hidden_tile(False, False))


def _ffn(x, g, wg, wu, wd, fg):
    n, d = x.shape
    f = wg.shape[1]
    tm = _pick(n, (1024, 512, 256, 128))
    tf = _pick(f, (512, 256, 128))
    rows = min(tm, 512)
    final_norm = fg is not None
    assert f // tf >= 2
    fg = g if fg is None else fg
    return pl.pallas_call(
        functools.partial(_ffn_kernel, final_norm=final_norm, rows=rows),
        grid=(n // tm, f // tf),
        in_specs=[
            pl.BlockSpec((tm, d), lambda i, j: (i, 0)),
            pl.BlockSpec((1, d), lambda i, j: (0, 0)),
            pl.BlockSpec((d, tf), lambda i, j: (0, j)),
            pl.BlockSpec((d, tf), lambda i, j: (0, j)),
            pl.BlockSpec((tf, d), lambda i, j: (j, 0)),
            pl.BlockSpec((1, d), lambda i, j: (0, 0)),
        ],
        out_specs=pl.BlockSpec((tm, d), lambda i, j: (i, 0)),
        out_shape=jax.ShapeDtypeStruct((n, d), F32),
        scratch_shapes=[pltpu.VMEM((tm, d), BF16)],
        compiler_params=_cparams(2),
        name="ffn",
    )(x, g.reshape(1, d), wg, wu, wd, fg.reshape(1, d))


def _proj_kernel(x_ref, g_ref, w_ref, mu_ref, b_ref, o_ref, carry_ref, *, tiles_per_seq, tn):
    i = pl.program_id(0)
    tm = x_ref.shape[0]
    ncols = w_ref.shape[1]

    @pl.when(i % tiles_per_seq == 0)
    def _():
        carry_ref[...] = jnp.zeros(carry_ref.shape, F32)

    h = _rms(x_ref[...], g_ref[...]).astype(BF16)
    row = lax.broadcasted_iota(jnp.int32, (SUBLANES, tn), 0)
    for c0 in range(0, ncols, tn):
        cs = slice(c0, c0 + tn)
        z = _dot(h, w_ref[:, cs])
        prev_tail = carry_ref[:, cs]
        carry_ref[:, cs] = z[tm - SUBLANES:, :]
        zs = pltpu.roll(z, 1, 0)
        head = jnp.where(row == 0, pltpu.roll(prev_tail, 1, 0), zs[:SUBLANES])
        mu = mu_ref[:, cs]
        b = b_ref[:, cs]
        z0 = z[:SUBLANES]
        o_ref[:SUBLANES, cs] = z0 + (head - z0) * mu + b
        z1 = z[SUBLANES:]
        o_ref[SUBLANES:, cs] = z1 + (zs[SUBLANES:] - z1) * mu + b


def _proj(x, g, w_all, mu_all, b_all, seq):
    n, d = x.shape
    ncols = w_all.shape[1]
    tm = _pick(seq, (512, 256, 128))
    tn = _pick(ncols, (1024, 512, 256, 128))
    return pl.pallas_call(
        functools.partial(_proj_kernel, tiles_per_seq=seq // tm, tn=tn),
        grid=(n // tm,),
        in_specs=[
            pl.BlockSpec((tm, d), lambda i: (i, 0)),
            pl.BlockSpec((1, d), lambda i: (0, 0)),
            pl.BlockSpec((d, ncols), lambda i: (0, 0), pipeline_mode=pl.Buffered(1)),
            pl.BlockSpec((1, ncols), lambda i: (0, 0)),
            pl.BlockSpec((1, ncols), lambda i: (0, 0)),
        ],
        out_specs=pl.BlockSpec((tm, ncols), lambda i: (i, 0)),
        out_shape=jax.ShapeDtypeStruct((n, ncols), F32),
        scratch_shapes=[pltpu.VMEM((SUBLANES, ncols), F32)],
        compiler_params=_cparams(1),
        name="proj",
    )(x, g.reshape(1, d), w_all, mu_all, b_all)


def _bdot(a, b):
    return lax.dot_general(a, b, (((2,), (1,)), ((0,), (0,))), preferred_element_type=F32)


def _bdot_nt(a, b):
    return lax.dot_general(a, b, (((2,), (2,)), ((0,), (0,))), preferred_element_type=F32)


def _inv_unit_lower(nmat):
    L = nmat.shape[-1]
    r = lax.broadcasted_iota(jnp.int32, (L, L), 0)
    c = lax.broadcasted_iota(jnp.int32, (L, L), 1)
    eye = jnp.where(r == c, 1.0, 0.0)
    t = eye + nmat
    pw = nmat.astype(BF16)
    pw = _bdot(pw, pw).astype(BF16)
    steps = L.bit_length() - 2
    for i in range(steps):
        if i + 1 < steps:
            both = _bdot(jnp.concatenate([t.astype(BF16), pw], axis=1), pw)
            t = t + both[:, :L]
            pw = both[:, L:].astype(BF16)
        else:
            t = t + _bdot(t.astype(BF16), pw)
    return t


RWKV_INPUTS = 14


def _rwkv_kernel(*refs, n_cast, **static):
    cast_in = refs[RWKV_INPUTS:RWKV_INPUTS + n_cast]
    cast_out = refs[RWKV_INPUTS + n_cast + 1:RWKV_INPUTS + 2 * n_cast + 1]
    chunk_refs = refs[:RWKV_INPUTS] + (refs[RWKV_INPUTS + n_cast],) + refs[RWKV_INPUTS + 2 * n_cast + 1:]
    _rwkv_chunk(*chunk_refs, **static)
    _cast_blocks(cast_in, cast_out)


def _rwkv_chunk(r_ref, k_ref, v_ref, lora_ref, w0_ref, a0_ref, kk_ref, ka_ref, rk_ref, gw_ref, gb_ref,
                du_ref, au_ref, gu_ref,
                o_ref,
                s_ref, pr_ref, pk_ref, pv_ref, pa_ref, pg_ref, plp_ref, plw_ref,
                *, n_tiles, sub, dlp, alp):
    L = CHUNK
    nt = sub * n_tiles

    @pl.when(pl.program_id(1) == 0)
    def _():
        s_ref[...] = jnp.zeros(s_ref.shape, F32)

    lora = lora_ref[...]
    wd = jnp.tanh(lora[:, :dlp])
    ad = lora[:, dlp:dlp + alp]
    gd = jax.nn.sigmoid(lora[:, dlp + alp:])
    lw = -EXP_M_HALF * jax.nn.sigmoid(w0_ref[...] + _dot(wd.astype(BF16), du_ref[...]))
    asig = jax.nn.sigmoid(a0_ref[...] + _dot(ad.astype(BF16), au_ref[...]))
    gate = _dot(gd.astype(BF16), gu_ref[...])
    row = lax.broadcasted_iota(jnp.int32, (L, L), 0)
    col = lax.broadcasted_iota(jnp.int32, (L, L), 1)
    incl = row >= col
    strict = row > col
    tril = jnp.where(incl, 1.0, 0.0).astype(BF16)
    h1 = lw.astype(BF16)
    h2 = (lw - h1.astype(F32)).astype(BF16)
    for c in range(sub):
        rows = slice(c * L, (c + 1) * L)
        logp = _dot(tril, h1[rows]) + _dot(tril, h2[rows])
        for p in range(n_tiles):
            sl = slice(p * LANES, (p + 1) * LANES)
            i = c * n_tiles + p
            pr_ref[i] = r_ref[rows, sl]
            pk_ref[i] = k_ref[rows, sl]
            pv_ref[i] = v_ref[rows, sl]
            pa_ref[i] = asig[rows, sl]
            pg_ref[i] = gate[rows, sl]
            plp_ref[i] = logp[:, sl]
            plw_ref[i] = lw[rows, sl]

    lane = lax.broadcasted_iota(jnp.int32, (1, LANES), 1)
    lo = lane < HEAD_DIM
    same_head = jnp.where(row < HEAD_DIM, 0, 1) == jnp.where(col < HEAD_DIM, 0, 1)
    inv_hd = 1.0 / HEAD_DIM

    def first(x):
        return jnp.where(lo, x, 0.0)

    def second(x):
        return jnp.where(lo, 0.0, x)

    def segsum(x):
        return jnp.where(lo, jnp.sum(first(x), axis=-1, keepdims=True),
                         jnp.sum(second(x), axis=-1, keepdims=True))

    def per_tile(ref):
        return jnp.concatenate([ref[...]] * sub, axis=0)

    r = pr_ref[...]
    k = pk_ref[...]
    v = pv_ref[...]
    a_s = pa_ref[...]
    lp = plp_ref[...]
    kk = k * per_tile(kk_ref)
    ss = segsum(kk * kk)
    kk = kk * lax.rsqrt(jnp.maximum(ss, 1e-24))
    a = -kk
    b = kk * a_s
    km = k * (1.0 + (a_s - 1.0) * per_tile(ka_ref))
    cmid = lp[:, L // 2 - 1:L // 2, :]
    clast = lp[:, L - 1:L, :]
    lpe = lp - plw_ref[...]
    e_inv = jnp.exp(cmid - lp)
    e_mid = jnp.exp(cmid)
    at = a * jnp.exp(lpe - cmid)
    a_abs = at * e_mid
    rt = r * jnp.exp(lp - cmid)
    r_abs = rt * e_mid
    bt = b * e_inv
    kt = km * e_inv
    e_l = jnp.exp(clast - cmid)
    e_p = jnp.exp(clast)
    lhs = jnp.concatenate([first(at), second(at), first(rt), second(rt)], axis=1).astype(BF16)
    gmat = _bdot_nt(lhs, jnp.concatenate([bt, kt], axis=1).astype(BF16))
    n_lo = jnp.where(strict, gmat[:, 0:L, 0:L], 0.0)
    ak_lo = jnp.where(strict, gmat[:, 0:L, L:], 0.0)
    n_hi = jnp.where(strict, gmat[:, L:2 * L, 0:L], 0.0)
    ak_hi = jnp.where(strict, gmat[:, L:2 * L, L:], 0.0)
    rb_lo = jnp.where(incl, gmat[:, 2 * L:3 * L, 0:L], 0.0)
    rk_lo = jnp.where(incl, gmat[:, 2 * L:3 * L, L:], 0.0)
    rb_hi = jnp.where(incl, gmat[:, 3 * L:, 0:L], 0.0)
    rk_hi = jnp.where(incl, gmat[:, 3 * L:, L:], 0.0)
    t_all = _inv_unit_lower(jnp.concatenate([n_lo, n_hi], axis=0))
    t_wide = jnp.concatenate([t_all[:nt], t_all[nt:]], axis=2).astype(BF16)
    ak_wide = jnp.concatenate([ak_lo, ak_hi], axis=2).astype(BF16)
    read_wide = jnp.concatenate([rb_lo, rb_hi, rk_lo, rk_hi], axis=2).astype(BF16)
    v_lohi = jnp.concatenate([first(v), second(v)], axis=1).astype(BF16)
    vt = jnp.swapaxes(v, 1, 2)
    bk_last = jnp.concatenate([bt * e_l, kt * e_l], axis=1).astype(BF16)
    a_abs_b = a_abs.astype(BF16)
    r_abs_b = r_abs.astype(BF16)
    ys = []
    for c in range(sub):
        tl = slice(c * n_tiles, (c + 1) * n_tiles)
        s = s_ref[...]
        s_b = s.astype(BF16)
        rhs = _bdot_nt(a_abs_b[tl], s_b) + _bdot(ak_wide[tl], v_lohi[tl])
        u = _bdot(t_wide[tl], jnp.concatenate([first(rhs), second(rhs)], axis=1).astype(BF16))
        u_lohi = jnp.concatenate([first(u), second(u)], axis=1).astype(BF16)
        ys.append(_bdot_nt(r_abs_b[tl], s_b) + _bdot(read_wide[tl], jnp.concatenate([u_lohi, v_lohi[tl]], axis=1)))
        upd = _bdot(jnp.concatenate([jnp.swapaxes(u, 1, 2), vt[tl]], axis=2).astype(BF16), bk_last[tl])
        s_ref[...] = s * e_p[tl] + jnp.where(same_head, upd, 0.0)
    y = jnp.concatenate(ys, axis=0)
    mean = segsum(y) * inv_hd
    dev = y - mean
    var = segsum(dev * dev) * inv_hd
    yn = dev * lax.rsqrt(var + GN_EPS) * per_tile(gw_ref) + per_tile(gb_ref)
    bonus = segsum(r * km * per_tile(rk_ref)) * v
    out = ((yn + bonus) * pg_ref[...]).astype(BF16)
    for c in range(sub):
        for p in range(n_tiles):
            o_ref[c * L:(c + 1) * L, p * LANES:(p + 1) * LANES] = out[c * n_tiles + p]


def _rwkv(z, batch, seq, c, lora_off, lora_w, dlp, alp, vecs, loras, casts):
    n = z.shape[0]
    nc = seq // CHUNK
    n_tiles = c // LANES
    sub = _pick(nc, (2, 1))
    steps = nc // sub
    rows = sub * CHUNK
    assert lora_off % lora_w == 0 and len(vecs) + len(loras) + 4 == RWKV_INPUTS
    lora_blk = lora_off // lora_w
    cast_in, cast_out, cast_shapes = _cast_specs(casts, batch * steps, lambda b, t: b * steps + t)
    vec_spec = pl.BlockSpec((n_tiles, 1, LANES), lambda b, t: (0, 0, 0))
    in_specs = [
        pl.BlockSpec((rows, c), lambda b, t: (b * steps + t, 0)),
        pl.BlockSpec((rows, c), lambda b, t: (b * steps + t, 1)),
        pl.BlockSpec((rows, c), lambda b, t: (b * steps + t, 2)),
        pl.BlockSpec((rows, lora_w), lambda b, t: (b * steps + t, lora_blk)),
        pl.BlockSpec((1, c), lambda b, t: (0, 0)),
        pl.BlockSpec((1, c), lambda b, t: (0, 0)),
    ] + [vec_spec] * 5 + [pl.BlockSpec(w.shape, lambda b, t: (0, 0)) for w in loras]
    tile_f32 = pltpu.VMEM((sub * n_tiles, CHUNK, LANES), F32)
    y, *cast = pl.pallas_call(
        functools.partial(_rwkv_kernel, n_cast=len(casts), n_tiles=n_tiles, sub=sub, dlp=dlp, alp=alp),
        grid=(batch, steps),
        in_specs=in_specs + cast_in,
        out_specs=[pl.BlockSpec((rows, c), lambda b, t: (b * steps + t, 0))] + cast_out,
        out_shape=[jax.ShapeDtypeStruct((n, c), BF16)] + cast_shapes,
        scratch_shapes=[pltpu.VMEM((n_tiles, LANES, LANES), F32)] + [tile_f32] * 7,
        compiler_params=_cparams(2),
        name="rwkv",
    )(z, z, z, z, *vecs, *loras, *casts)
    return y, cast


def _swa_kernel(sink_ref, q_ref, kvc_ref, kvp_ref, cosc_ref, sinc_ref, cosp_ref, sinp_ref, *rest,
                n_tiles, group, qb):
    n_cast = len(rest) // 2
    o_ref = rest[n_cast]
    _cast_blocks(rest[:n_cast], rest[n_cast + 1:])
    nblk = pl.program_id(1)
    lane = lax.broadcasted_iota(jnp.int32, (1, LANES), 1)
    lo = lane < HEAD_DIM
    rot_lo = jnp.bitwise_and(lane, HEAD_DIM - 1) < HEAD_DIM // 2

    def rope(x, cos, sin_signed):
        partner = jnp.where(rot_lo, pltpu.roll(x, LANES - HEAD_DIM // 2, 1), pltpu.roll(x, HEAD_DIM // 2, 1))
        return x * cos + partner * sin_signed

    cosc = cosc_ref[...]
    sinc = sinc_ref[...]
    kvc = kvc_ref[...]
    kvp = kvp_ref[...]
    keys = jnp.concatenate([rope(kvp[:, :LANES], cosp_ref[...], sinp_ref[...]),
                            rope(kvc[:, :LANES], cosc, sinc)], axis=0)
    vals = jnp.concatenate([kvp[:, LANES:], kvc[:, LANES:]], axis=0)
    keys_sw = pltpu.roll(keys, HEAD_DIM, 1)
    vals_sw = pltpu.roll(vals, HEAD_DIM, 1)
    k_first = [jnp.where(lo, keys, 0.0).astype(BF16), jnp.where(lo, keys_sw, 0.0).astype(BF16)]
    k_second = [jnp.where(lo, 0.0, keys_sw).astype(BF16), jnp.where(lo, 0.0, keys).astype(BF16)]
    ones_kv = jnp.ones(vals.shape, BF16)
    with_ones = lambda x: jnp.concatenate([x.astype(BF16), ones_kv], axis=1)
    v_first = [with_ones(jnp.where(lo, vals, 0.0)), with_ones(jnp.where(lo, vals_sw, 0.0))]
    v_second = [with_ones(jnp.where(lo, 0.0, vals_sw)), with_ones(jnp.where(lo, 0.0, vals))]
    qi = lax.broadcasted_iota(jnp.int32, (BLOCK, 2 * BLOCK), 0)
    ki = lax.broadcasted_iota(jnp.int32, (BLOCK, 2 * BLOCK), 1)
    window = (ki > qi) & (ki <= qi + BLOCK)
    valid = [window & ((nblk > 0) | (ki >= BLOCK))] + [window] * (qb - 1)
    scale = HEAD_DIM ** -0.5 * LOG2_E
    heads = [(j, p, half) for j in range(qb) for p in range(n_tiles) for half in range(HEADS_PER_TILE)]
    scores = []
    for j in range(qb):
        qrows = slice(j * BLOCK, (j + 1) * BLOCK)
        krows = slice(j * BLOCK, (j + 2) * BLOCK)
        for p in range(n_tiles):
            g = (p * HEADS_PER_TILE) // group
            qp = (rope(q_ref[qrows, p * LANES:(p + 1) * LANES], cosc[qrows], sinc[qrows]) * scale).astype(BF16)
            scores += [_dot_nt(qp, k_first[g][krows]), _dot_nt(qp, k_second[g][krows])]
    probs, sink_terms = [], []
    for (j, p, half), s in zip(heads, scores):
        s = jnp.where(valid[j], s, NEG_INF)
        sink = sink_ref[p * HEADS_PER_TILE + half] * LOG2_E
        m = jnp.maximum(jnp.max(s, axis=-1, keepdims=True), sink)
        probs.append(jnp.exp2(s - m).astype(BF16))
        sink_terms.append(jnp.exp2(sink - m))
    outs = []
    for (j, p, half), e, st in zip(heads, probs, sink_terms):
        g = (p * HEADS_PER_TILE) // group
        krows = slice(j * BLOCK, (j + 2) * BLOCK)
        pv = _dot(e, (v_second[g] if half else v_first[g])[krows])
        outs.append(pv[:, :LANES] * (1.0 / (pv[:, LANES:] + st)))
    for j in range(qb):
        for p in range(n_tiles):
            i = (j * n_tiles + p) * HEADS_PER_TILE
            o_ref[j * BLOCK:(j + 1) * BLOCK, p * LANES:(p + 1) * LANES] = (outs[i] + outs[i + 1]).astype(BF16)


def _swa(z, sinks, cos_t, sin_t, batch, seq, q_off, qw, kv_off, kvw, group, casts):
    n = z.shape[0]
    nb = seq // BLOCK
    assert q_off % qw == 0 and kv_off % kvw == 0 and kvw == 2 * LANES
    q_blk, kv_blk = q_off // qw, kv_off // kvw
    qb = _pick(nb, (8, 4, 2, 1))
    steps = nb // qb
    prev = lambda t: jnp.maximum(t * qb - 1, 0)
    cast_in, cast_out, cast_shapes = _cast_specs(casts, batch * steps, lambda b, t: b * steps + t)
    y, *cast = pl.pallas_call(
        functools.partial(_swa_kernel, n_tiles=qw // LANES, group=group, qb=qb),
        grid=(batch, steps),
        in_specs=[
            pl.BlockSpec(memory_space=pltpu.SMEM),
            pl.BlockSpec((qb * BLOCK, qw), lambda b, t: (b * steps + t, q_blk)),
            pl.BlockSpec((qb * BLOCK, kvw), lambda b, t: (b * steps + t, kv_blk)),
            pl.BlockSpec((BLOCK, kvw), lambda b, t: (b * nb + prev(t), kv_blk)),
            pl.BlockSpec((qb * BLOCK, LANES), lambda b, t: (t, 0)),
            pl.BlockSpec((qb * BLOCK, LANES), lambda b, t: (t, 0)),
            pl.BlockSpec((BLOCK, LANES), lambda b, t: (prev(t), 0)),
            pl.BlockSpec((BLOCK, LANES), lambda b, t: (prev(t), 0)),
        ] + cast_in,
        out_specs=[pl.BlockSpec((qb * BLOCK, qw), lambda b, t: (b * steps + t, 0))] + cast_out,
        out_shape=[jax.ShapeDtypeStruct((n, qw), BF16)] + cast_shapes,
        compiler_params=_cparams(2),
        name="swa",
    )(sinks, z, z, z, cos_t, sin_t, cos_t, sin_t, *casts)
    return y, cast


def _memkv_kernel(m_ref, g_ref, w_ref, o_ref):
    h = _rms(m_ref[...], g_ref[...]).astype(BF16)
    o_ref[...] = _dot(h, w_ref[...].astype(BF16)).astype(BF16)


def _memkv(mem, g, w):
    n, d = mem.shape
    ncols = w.shape[1]
    tm = _pick(n, (512, 256, 128))
    tn = _pick(ncols, (1024, 512, 256, 128))
    return pl.pallas_call(
        _memkv_kernel,
        grid=(ncols // tn, n // tm),
        in_specs=[
            pl.BlockSpec((tm, d), lambda j, i: (i, 0)),
            pl.BlockSpec((1, d), lambda j, i: (0, 0)),
            pl.BlockSpec((d, tn), lambda j, i: (0, j)),
        ],
        out_specs=pl.BlockSpec((tm, tn), lambda j, i: (i, j)),
        out_shape=jax.ShapeDtypeStruct((n, ncols), BF16),
        compiler_params=_cparams(2),
        name="memkv",
    )(mem, g.reshape(1, d), w)


def _mix_kernel(x_ref, yr_ref, ys_ref, wo_ref, bo_ref, g_ref, wq_ref, k_ref, v_ref, wxo_ref, o_ref):
    c = yr_ref.shape[1]
    d = x_ref.shape[1]
    hd = d // XATTN_HEADS
    x2 = x_ref[...] + _dot(yr_ref[...], wo_ref[:c, :]) + _dot(ys_ref[...], wo_ref[c:, :]) + bo_ref[...]
    q = _dot(_rms(x2, g_ref[...]).astype(BF16), wq_ref[...]).astype(BF16)
    scale = hd ** -0.5
    heads = [slice(h * hd, (h + 1) * hd) for h in range(XATTN_HEADS)]
    scores = [_dot_nt(q[:, sl], k_ref[:, sl]) * scale for sl in heads]
    probs, dens = [], []
    for s in scores:
        e = jnp.exp(s - jnp.max(s, axis=-1, keepdims=True))
        probs.append(e.astype(BF16))
        dens.append(jnp.sum(e, axis=-1, keepdims=True))
    outs = [(_dot(e, v_ref[:, sl]) * (1.0 / den)).astype(BF16) for e, den, sl in zip(probs, dens, heads)]
    o_ref[...] = x2 + _dot(jnp.concatenate(outs, axis=1), wxo_ref[...])


def _mix(x, yr, ys, wo, bo, g, wq, kv, wxo, seq, mlen):
    n, d = x.shape
    c = yr.shape[1]
    tm = _pick(seq, (256, 128))
    per_seq = seq // tm
    once = pl.Buffered(1)
    return pl.pallas_call(
        _mix_kernel,
        grid=(n // tm,),
        in_specs=[
            pl.BlockSpec((tm, d), lambda i: (i, 0)),
            pl.BlockSpec((tm, c), lambda i: (i, 0)),
            pl.BlockSpec((tm, d - c), lambda i: (i, 0)),
            pl.BlockSpec((d, d), lambda i: (0, 0), pipeline_mode=once),
            pl.BlockSpec((1, d), lambda i: (0, 0)),
            pl.BlockSpec((1, d), lambda i: (0, 0)),
            pl.BlockSpec((d, d), lambda i: (0, 0), pipeline_mode=once),
            pl.BlockSpec((mlen, d), lambda i: (i // per_seq, 0)),
            pl.BlockSpec((mlen, d), lambda i: (i // per_seq, 1)),
            pl.BlockSpec((d, d), lambda i: (0, 0), pipeline_mode=once),
        ],
        out_specs=pl.BlockSpec((tm, d), lambda i: (i, 0)),
        out_shape=jax.ShapeDtypeStruct((n, d), F32),
        compiler_params=_cparams(1),
        name="mix",
    )(x, yr, ys, wo, bo.reshape(1, d), g.reshape(1, d), wq, kv, kv, wxo)


def _pad_cols(w, width):
    return jnp.pad(w, ((0, 0), (0, width - w.shape[1])))


def _pad_rows(w, height):
    return jnp.pad(w, ((0, height - w.shape[0]), (0, 0)))


def _rope_tables(seq):
    half = HEAD_DIM // 2
    lane = jnp.arange(LANES)
    inv_freq = ROPE_THETA ** (-jnp.arange(0, HEAD_DIM, 2, dtype=F32) / HEAD_DIM)
    ang = jnp.arange(seq, dtype=F32)[:, None] * inv_freq[lane % half][None, :]
    sign = jnp.where((lane % HEAD_DIM) < half, -1.0, 1.0)
    return jnp.cos(ang), jnp.sin(ang) * sign[None, :]


def kernel(x, mem, f1_norm, f1_gate, f1_up, f1_down, mix_norm, w_in, b_in_attn, rw_mu, rw_w0, rw_decay_up, rw_a0, rw_aaa_up, rw_gate_up, rw_k_k, rw_k_a, rw_r_k, rw_lnx_w, rw_lnx_b, attn_sinks, w_out, b_out, xa_norm, mem_norm, w_xq, w_xkv, w_xo, f2_norm, f2_gate, f2_up, f2_down, final_norm):
    batch, seq, d = x.shape
    mlen = mem.shape[1]
    depth = f1_norm.shape[0]
    c = rw_w0.shape[1]
    sw = d - c
    dl, al, gl = rw_decay_up.shape[1], rw_aaa_up.shape[1], rw_gate_up.shape[1]
    dlp, alp, glp = (_round_up(v, LANES) for v in (dl, al, gl))
    kvw = b_in_attn.shape[1] - sw
    q_heads = sw // HEAD_DIM
    group = q_heads // (kvw // (2 * HEAD_DIM))
    n_tiles = c // LANES
    n = batch * seq
    q_off = 3 * c
    lora_off = q_off + sw
    lora_w = dlp + alp + glp
    kv_off = lora_off + lora_w
    ncols = _round_up(kv_off + kvw, 2 * LANES)
    cos_t, sin_t = _rope_tables(seq)

    xf = x.reshape(n, d)
    memf = mem.reshape(batch * mlen, d)
    for l in range(depth):
        xf = _ffn(xf, f1_norm[l], f1_gate[l].astype(BF16), f1_up[l].astype(BF16), f1_down[l].astype(BF16), None)

        wl = w_in[l]
        o1, o2, o3 = 3 * c, 3 * c + dl, 3 * c + dl + al
        shift = o3 + gl
        w_all = jnp.concatenate([
            wl[:, :o1], wl[:, shift:shift + sw],
            _pad_cols(wl[:, o1:o2], dlp), _pad_cols(wl[:, o2:o3], alp), _pad_cols(wl[:, o3:shift], glp),
            wl[:, shift + sw:]], axis=1)
        w_all = _pad_cols(w_all, ncols).astype(BF16)
        mu = rw_mu[l][None, :]
        mu_all = _pad_cols(jnp.concatenate([
            mu[:, :o1], jnp.zeros((1, sw), F32),
            _pad_cols(mu[:, o1:o2], dlp), _pad_cols(mu[:, o2:o3], alp), _pad_cols(mu[:, o3:shift], glp)],
            axis=1), ncols)
        bia = b_in_attn[l][None, :]
        b_all = _pad_cols(jnp.concatenate([
            jnp.zeros((1, q_off), F32), bia[:, :sw], jnp.zeros((1, lora_w), F32), bia[:, sw:]], axis=1), ncols)
        z = _proj(xf, mix_norm[l], w_all, mu_all, b_all, seq)

        tiles = lambda v: v.reshape(n_tiles, 1, LANES)
        vecs = [rw_w0[l].reshape(1, c), rw_a0[l].reshape(1, c), tiles(rw_k_k[l]), tiles(rw_k_a[l]),
                tiles(rw_r_k[l]), tiles(rw_lnx_w[l]), tiles(rw_lnx_b[l])]
        loras = [_pad_rows(rw_decay_up[l], dlp).astype(BF16), _pad_rows(rw_aaa_up[l], alp).astype(BF16),
                 _pad_rows(rw_gate_up[l], glp).astype(BF16)]
        y_rwkv, (f2_down_b, w_out_b, w_xq_b, w_xo_b) = _rwkv(
            z, batch, seq, c, lora_off, lora_w, dlp, alp, vecs, loras,
            [f2_down[l], w_out[l], w_xq[l], w_xo[l]])
        y_swa, (f2_gate_b, f2_up_b) = _swa(z, attn_sinks[l], cos_t, sin_t, batch, seq, q_off, sw, kv_off, kvw,
                                           group, [f2_gate[l], f2_up[l]])

        kv_mem = _memkv(memf, mem_norm[l], w_xkv[l])
        xf = _mix(xf, y_rwkv, y_swa, w_out_b, b_out[l], xa_norm[l], w_xq_b, kv_mem, w_xo_b, seq, mlen)

        last = l == depth - 1
        xf = _ffn(xf, f2_norm[l], f2_gate_b, f2_up_b, f2_down_b, final_norm if last else None)
    return xf.reshape(batch, seq, d)
```

```python
import functools

import jax
import jax.numpy as jnp
from jax import lax
from jax.experimental import pallas as pl
from jax.experimental.pallas import tpu as pltpu

F32, BF16 = jnp.float32, jnp.bfloat16

LANES = 128
SUBLANES = 8
VMEM_LIMIT_BYTES = 56 * 1024 * 1024

HEAD_DIM = 64
HEADS_PER_TILE = LANES // HEAD_DIM
CHUNK = 128
BLOCK = 128
XATTN_HEADS = 4
RMS_EPS = 1e-6
GN_EPS = 64e-5
NEG_INF = -1e30
ROPE_THETA = 10000.0
EXP_M_HALF = 0.6065306597126334
LOG2_E = 1.4426950408889634


def _round_up(n, m):
    return (n + m - 1) // m * m


def _pick(n, prefs):
    for p in prefs:
        if n % p == 0:
            return p
    raise ValueError(f"no tile in {prefs} divides {n}")


def _cparams(n_axes):
    return pltpu.CompilerParams(dimension_semantics=("arbitrary",) * n_axes,
                                vmem_limit_bytes=VMEM_LIMIT_BYTES)


def _dot(a, b):
    return jnp.dot(a, b, preferred_element_type=F32)


def _dot_nt(a, b):
    return lax.dot_general(a, b, (((1,), (1,)), ((), ())), preferred_element_type=F32)


def _cast_specs(weights, nsteps, flat_step):
    in_specs, out_specs, out_shapes = [], [], []
    for w in weights:
        units = w.shape[0] // (2 * SUBLANES)
        nblk = max(d for d in range(1, nsteps + 1) if units % d == 0)
        imap = lambda *g, nblk=nblk: (jnp.minimum(flat_step(*g), nblk - 1), 0)
        spec = pl.BlockSpec((w.shape[0] // nblk, w.shape[1]), imap)
        in_specs.append(spec)
        out_specs.append(spec)
        out_shapes.append(jax.ShapeDtypeStruct(w.shape, BF16))
    return in_specs, out_specs, out_shapes


def _cast_blocks(src_refs, dst_refs):
    for src, dst in zip(src_refs, dst_refs):
        dst[...] = src[...].astype(BF16)


def _rms(x, g):
    ms = jnp.mean(x * x, axis=-1, keepdims=True)
    return x * lax.rsqrt(ms + RMS_EPS) * g


def _ffn_kernel(x_ref, g_ref, wg_ref, wu_ref, wd_ref, fg_ref, o_ref, h_ref, *, final_norm, rows):
    j = pl.program_id(1)
    tm = x_ref.shape[0]

    def hidden_tile(first, last):
        for r0 in range(0, tm, rows):
            rs = slice(r0, r0 + rows)
            if first:
                base = x_ref[rs, :]
                h = _rms(base, g_ref[...]).astype(BF16)
                h_ref[rs, :] = h
            else:
                base = o_ref[rs, :]
                h = h_ref[rs, :]
            gate = _dot(h, wg_ref[...])
            up = _dot(h, wu_ref[...])
            act = (gate * jax.nn.sigmoid(gate) * up).astype(BF16)
            out = base + 0.5 * _dot(act, wd_ref[...])
            o_ref[rs, :] = _rms(out, fg_ref[...]) if last else out

    pl.when(j == 0)(lambda: hidden_tile(True, False))
    if final_norm:
        n_last = pl.num_programs(1) - 1
        pl.when((j > 0) & (j < n_last))(lambda: hidden_tile(False, False))
        pl.when(j == n_last)(lambda: hidden_tile(False, True))
    else:
        pl.when(j > 0)(lambda: hidden_tile(False, False))


def _ffn(x, g, wg, wu, wd, fg):
    n, d = x.shape
    f = wg.shape[1]
    tm = _pick(n, (1024, 512, 256, 128))
    tf = _pick(f, (512, 256, 128))
    rows = min(tm, 512)
    final_norm = fg is not None
    assert f // tf >= 2
    fg = g if fg is None else fg
    return pl.pallas_call(
        functools.partial(_ffn_kernel, final_norm=final_norm, rows=rows),
        grid=(n // tm, f // tf),
        in_specs=[
            pl.BlockSpec((tm, d), lambda i, j: (i, 0)),
            pl.BlockSpec((1, d), lambda i, j: (0, 0)),
            pl.BlockSpec((d, tf), lambda i, j: (0, j)),
            pl.BlockSpec((d, tf), lambda i, j: (0, j)),
            pl.BlockSpec((tf, d), lambda i, j: (j, 0)),
            pl.BlockSpec((1, d), lambda i, j: (0, 0)),
        ],
        out_specs=pl.BlockSpec((tm, d), lambda i, j: (i, 0)),
        out_shape=jax.ShapeDtypeStruct((n, d), F32),
        scratch_shapes=[pltpu.VMEM((tm, d), BF16)],
        compiler_params=_cparams(2),
        name="ffn",
    )(x, g.reshape(1, d), wg, wu, wd, fg.reshape(1, d))


def _proj_kernel(x_ref, g_ref, w_ref, mu_ref, b_ref, o_ref, carry_ref, *, tiles_per_seq, tn):
    i = pl.program_id(0)
    tm = x_ref.shape[0]
    ncols = w_ref.shape[1]

    @pl.when(i % tiles_per_seq == 0)
    def _():
        carry_ref[...] = jnp.zeros(carry_ref.shape, F32)

    h = _rms(x_ref[...], g_ref[...]).astype(BF16)
    row = lax.broadcasted_iota(jnp.int32, (SUBLANES, tn), 0)
    for c0 in range(0, ncols, tn):
        cs = slice(c0, c0 + tn)
        z = _dot(h, w_ref[:, cs])
        prev_tail = carry_ref[:, cs]
        carry_ref[:, cs] = z[tm - SUBLANES:, :]
        zs = pltpu.roll(z, 1, 0)
        head = jnp.where(row == 0, pltpu.roll(prev_tail, 1, 0), zs[:SUBLANES])
        mu = mu_ref[:, cs]
        b = b_ref[:, cs]
        z0 = z[:SUBLANES]
        o_ref[:SUBLANES, cs] = z0 + (head - z0) * mu + b
        z1 = z[SUBLANES:]
        o_ref[SUBLANES:, cs] = z1 + (zs[SUBLANES:] - z1) * mu + b


def _proj(x, g, w_all, mu_all, b_all, seq):
    n, d = x.shape
    ncols = w_all.shape[1]
    tm = _pick(seq, (512, 256, 128))
    tn = _pick(ncols, (1024, 512, 256, 128))
    return pl.pallas_call(
        functools.partial(_proj_kernel, tiles_per_seq=seq // tm, tn=tn),
        grid=(n // tm,),
        in_specs=[
            pl.BlockSpec((tm, d), lambda i: (i, 0)),
            pl.BlockSpec((1, d), lambda i: (0, 0)),
            pl.BlockSpec((d, ncols), lambda i: (0, 0), pipeline_mode=pl.Buffered(1)),
            pl.BlockSpec((1, ncols), lambda i: (0, 0)),
            pl.BlockSpec((1, ncols), lambda i: (0, 0)),
        ],
        out_specs=pl.BlockSpec((tm, ncols), lambda i: (i, 0)),
        out_shape=jax.ShapeDtypeStruct((n, ncols), F32),
        scratch_shapes=[pltpu.VMEM((SUBLANES, ncols), F32)],
        compiler_params=_cparams(1),
        name="proj",
    )(x, g.reshape(1, d), w_all, mu_all, b_all)


def _bdot(a, b):
    return lax.dot_general(a, b, (((2,), (1,)), ((0,), (0,))), preferred_element_type=F32)


def _bdot_nt(a, b):
    return lax.dot_general(a, b, (((2,), (2,)), ((0,), (0,))), preferred_element_type=F32)


def _inv_unit_lower(nmat):
    L = nmat.shape[-1]
    r = lax.broadcasted_iota(jnp.int32, (L, L), 0)
    c = lax.broadcasted_iota(jnp.int32, (L, L), 1)
    eye = jnp.where(r == c, 1.0, 0.0)
    t = eye + nmat
    pw = nmat.astype(BF16)
    pw = _bdot(pw, pw).astype(BF16)
    steps = L.bit_length() - 2
    for i in range(steps):
        if i + 1 < steps:
            both = _bdot(jnp.concatenate([t.astype(BF16), pw], axis=1), pw)
            t = t + both[:, :L]
            pw = both[:, L:].astype(BF16)
        else:
            t = t + _bdot(t.astype(BF16), pw)
    return t


RWKV_INPUTS = 14


def _rwkv_kernel(*refs, n_cast, **static):
    cast_in = refs[RWKV_INPUTS:RWKV_INPUTS + n_cast]
    cast_out = refs[RWKV_INPUTS + n_cast + 1:RWKV_INPUTS + 2 * n_cast + 1]
    chunk_refs = refs[:RWKV_INPUTS] + (refs[RWKV_INPUTS + n_cast],) + refs[RWKV_INPUTS + 2 * n_cast + 1:]
    _rwkv_chunk(*chunk_refs, **static)
    _cast_blocks(cast_in, cast_out)


def _rwkv_chunk(r_ref, k_ref, v_ref, lora_ref, w0_ref, a0_ref, kk_ref, ka_ref, rk_ref, gw_ref, gb_ref,
                du_ref, au_ref, gu_ref,
                o_ref,
                s_ref, pr_ref, pk_ref, pv_ref, pa_ref, pg_ref, plp_ref, plw_ref,
                *, n_tiles, sub, dlp, alp):
    L = CHUNK
    nt = sub * n_tiles

    @pl.when(pl.program_id(1) == 0)
    def _():
        s_ref[...] = jnp.zeros(s_ref.shape, F32)

    lora = lora_ref[...]
    wd = jnp.tanh(lora[:, :dlp])
    ad = lora[:, dlp:dlp + alp]
    gd = jax.nn.sigmoid(lora[:, dlp + alp:])
    lw = -EXP_M_HALF * jax.nn.sigmoid(w0_ref[...] + _dot(wd.astype(BF16), du_ref[...]))
    asig = jax.nn.sigmoid(a0_ref[...] + _dot(ad.astype(BF16), au_ref[...]))
    gate = _dot(gd.astype(BF16), gu_ref[...])
    row = lax.broadcasted_iota(jnp.int32, (L, L), 0)
    col = lax.broadcasted_iota(jnp.int32, (L, L), 1)
    incl = row >= col
    strict = row > col
    tril = jnp.where(incl, 1.0, 0.0).astype(BF16)
    h1 = lw.astype(BF16)
    h2 = (lw - h1.astype(F32)).astype(BF16)
    for c in range(sub):
        rows = slice(c * L, (c + 1) * L)
        logp = _dot(tril, h1[rows]) + _dot(tril, h2[rows])
        for p in range(n_tiles):
            sl = slice(p * LANES, (p + 1) * LANES)
            i = c * n_tiles + p
            pr_ref[i] = r_ref[rows, sl]
            pk_ref[i] = k_ref[rows, sl]
            pv_ref[i] = v_ref[rows, sl]
            pa_ref[i] = asig[rows, sl]
            pg_ref[i] = gate[rows, sl]
            plp_ref[i] = logp[:, sl]
            plw_ref[i] = lw[rows, sl]

    lane = lax.broadcasted_iota(jnp.int32, (1, LANES), 1)
    lo = lane < HEAD_DIM
    same_head = jnp.where(row < HEAD_DIM, 0, 1) == jnp.where(col < HEAD_DIM, 0, 1)
    inv_hd = 1.0 / HEAD_DIM

    def first(x):
        return jnp.where(lo, x, 0.0)

    def second(x):
        return jnp.where(lo, 0.0, x)

    def segsum(x):
        return jnp.where(lo, jnp.sum(first(x), axis=-1, keepdims=True),
                         jnp.sum(second(x), axis=-1, keepdims=True))

    def per_tile(ref):
        return jnp.concatenate([ref[...]] * sub, axis=0)

    r = pr_ref[...]
    k = pk_ref[...]
    v = pv_ref[...]
    a_s = pa_ref[...]
    lp = plp_ref[...]
    kk = k * per_tile(kk_ref)
    ss = segsum(kk * kk)
    kk = kk * lax.rsqrt(jnp.maximum(ss, 1e-24))
    a = -kk
    b = kk * a_s
    km = k * (1.0 + (a_s - 1.0) * per_tile(ka_ref))
    cmid = lp[:, L // 2 - 1:L // 2, :]
    clast = lp[:, L - 1:L, :]
    lpe = lp - plw_ref[...]
    e_inv = jnp.exp(cmid - lp)
    e_mid = jnp.exp(cmid)
    at = a * jnp.exp(lpe - cmid)
    a_abs = at * e_mid
    rt = r * jnp.exp(lp - cmid)
    r_abs = rt * e_mid
    bt = b * e_inv
    kt = km * e_inv
    e_l = jnp.exp(clast - cmid)
    e_p = jnp.exp(clast)
    lhs = jnp.concatenate([first(at), second(at), first(rt), second(rt)], axis=1).astype(BF16)
    gmat = _bdot_nt(lhs, jnp.concatenate([bt, kt], axis=1).astype(BF16))
    n_lo = jnp.where(strict, gmat[:, 0:L, 0:L], 0.0)
    ak_lo = jnp.where(strict, gmat[:, 0:L, L:], 0.0)
    n_hi = jnp.where(strict, gmat[:, L:2 * L, 0:L], 0.0)
    ak_hi = jnp.where(strict, gmat[:, L:2 * L, L:], 0.0)
    rb_lo = jnp.where(incl, gmat[:, 2 * L:3 * L, 0:L], 0.0)
    rk_lo = jnp.where(incl, gmat[:, 2 * L:3 * L, L:], 0.0)
    rb_hi = jnp.where(incl, gmat[:, 3 * L:, 0:L], 0.0)
    rk_hi = jnp.where(incl, gmat[:, 3 * L:, L:], 0.0)
    t_all = _inv_unit_lower(jnp.concatenate([n_lo, n_hi], axis=0))
    t_wide = jnp.concatenate([t_all[:nt], t_all[nt:]], axis=2).astype(BF16)
    ak_wide = jnp.concatenate([ak_lo, ak_hi], axis=2).astype(BF16)
    read_wide = jnp.concatenate([rb_lo, rb_hi, rk_lo, rk_hi], axis=2).astype(BF16)
    v_lohi = jnp.concatenate([first(v), second(v)], axis=1).astype(BF16)
    vt = jnp.swapaxes(v, 1, 2)
    bk_last = jnp.concatenate([bt * e_l, kt * e_l], axis=1).astype(BF16)
    a_abs_b = a_abs.astype(BF16)
    r_abs_b = r_abs.astype(BF16)
    ys = []
    for c in range(sub):
        tl = slice(c * n_tiles, (c + 1) * n_tiles)
        s = s_ref[...]
        s_b = s.astype(BF16)
        rhs = _bdot_nt(a_abs_b[tl], s_b) + _bdot(ak_wide[tl], v_lohi[tl])
        u = _bdot(t_wide[tl], jnp.concatenate([first(rhs), second(rhs)], axis=1).astype(BF16))
        u_lohi = jnp.concatenate([first(u), second(u)], axis=1).astype(BF16)
        ys.append(_bdot_nt(r_abs_b[tl], s_b) + _bdot(read_wide[tl], jnp.concatenate([u_lohi, v_lohi[tl]], axis=1)))
        upd = _bdot(jnp.concatenate([jnp.swapaxes(u, 1, 2), vt[tl]], axis=2).astype(BF16), bk_last[tl])
        s_ref[...] = s * e_p[tl] + jnp.where(same_head, upd, 0.0)
    y = jnp.concatenate(ys, axis=0)
    mean = segsum(y) * inv_hd
    dev = y - mean
    var = segsum(dev * dev) * inv_hd
    yn = dev * lax.rsqrt(var + GN_EPS) * per_tile(gw_ref) + per_tile(gb_ref)
    bonus = segsum(r * km * per_tile(rk_ref)) * v
    out = ((yn + bonus) * pg_ref[...]).astype(BF16)
    for c in range(sub):
        for p in range(n_tiles):
            o_ref[c * L:(c + 1) * L, p * LANES:(p + 1) * LANES] = out[c * n_tiles + p]


def _rwkv(z, batch, seq, c, lora_off, lora_w, dlp, alp, vecs, loras, casts):
    n = z.shape[0]
    nc = seq // CHUNK
    n_tiles = c // LANES
    sub = _pick(nc, (2, 1))
    steps = nc // sub
    rows = sub * CHUNK
    assert lora_off % lora_w == 0 and len(vecs) + len(loras) + 4 == RWKV_INPUTS
    lora_blk = lora_off // lora_w
    cast_in, cast_out, cast_shapes = _cast_specs(casts, batch * steps, lambda b, t: b * steps + t)
    vec_spec = pl.BlockSpec((n_tiles, 1, LANES), lambda b, t: (0, 0, 0))
    in_specs = [
        pl.BlockSpec((rows, c), lambda b, t: (b * steps + t, 0)),
        pl.BlockSpec((rows, c), lambda b, t: (b * steps + t, 1)),
        pl.BlockSpec((rows, c), lambda b, t: (b * steps + t, 2)),
        pl.BlockSpec((rows, lora_w), lambda b, t: (b * steps + t, lora_blk)),
        pl.BlockSpec((1, c), lambda b, t: (0, 0)),
        pl.BlockSpec((1, c), lambda b, t: (0, 0)),
    ] + [vec_spec] * 5 + [pl.BlockSpec(w.shape, lambda b, t: (0, 0)) for w in loras]
    tile_f32 = pltpu.VMEM((sub * n_tiles, CHUNK, LANES), F32)
    y, *cast = pl.pallas_call(
        functools.partial(_rwkv_kernel, n_cast=len(casts), n_tiles=n_tiles, sub=sub, dlp=dlp, alp=alp),
        grid=(batch, steps),
        in_specs=in_specs + cast_in,
        out_specs=[pl.BlockSpec((rows, c), lambda b, t: (b * steps + t, 0))] + cast_out,
        out_shape=[jax.ShapeDtypeStruct((n, c), BF16)] + cast_shapes,
        scratch_shapes=[pltpu.VMEM((n_tiles, LANES, LANES), F32)] + [tile_f32] * 7,
        compiler_params=_cparams(2),
        name="rwkv",
    )(z, z, z, z, *vecs, *loras, *casts)
    return y, cast


def _swa_kernel(sink_ref, q_ref, kvc_ref, kvp_ref, cosc_ref, sinc_ref, cosp_ref, sinp_ref, *rest,
                n_tiles, group, qb):
    n_cast = len(rest) // 2
    o_ref = rest[n_cast]
    _cast_blocks(rest[:n_cast], rest[n_cast + 1:])
    nblk = pl.program_id(1)
    lane = lax.broadcasted_iota(jnp.int32, (1, LANES), 1)
    lo = lane < HEAD_DIM
    rot_lo = jnp.bitwise_and(lane, HEAD_DIM - 1) < HEAD_DIM // 2

    def rope(x, cos, sin_signed):
        partner = jnp.where(rot_lo, pltpu.roll(x, LANES - HEAD_DIM // 2, 1), pltpu.roll(x, HEAD_DIM // 2, 1))
        return x * cos + partner * sin_signed

    cosc = cosc_ref[...]
    sinc = sinc_ref[...]
    kvc = kvc_ref[...]
    kvp = kvp_ref[...]
    keys = jnp.concatenate([rope(kvp[:, :LANES], cosp_ref[...], sinp_ref[...]),
                            rope(kvc[:, :LANES], cosc, sinc)], axis=0)
    vals = jnp.concatenate([kvp[:, LANES:], kvc[:, LANES:]], axis=0)
    keys_sw = pltpu.roll(keys, HEAD_DIM, 1)
    vals_sw = pltpu.roll(vals, HEAD_DIM, 1)
    k_first = [jnp.where(lo, keys, 0.0).astype(BF16), jnp.where(lo, keys_sw, 0.0).astype(BF16)]
    k_second = [jnp.where(lo, 0.0, keys_sw).astype(BF16), jnp.where(lo, 0.0, keys).astype(BF16)]
    ones_kv = jnp.ones(vals.shape, BF16)
    with_ones = lambda x: jnp.concatenate([x.astype(BF16), ones_kv], axis=1)
    v_first = [with_ones(jnp.where(lo, vals, 0.0)), with_ones(jnp.where(lo, vals_sw, 0.0))]
    v_second = [with_ones(jnp.where(lo, 0.0, vals_sw)), with_ones(jnp.where(lo, 0.0, vals))]
    qi = lax.broadcasted_iota(jnp.int32, (BLOCK, 2 * BLOCK), 0)
    ki = lax.broadcasted_iota(jnp.int32, (BLOCK, 2 * BLOCK), 1)
    window = (ki > qi) & (ki <= qi + BLOCK)
    valid = [window & ((nblk > 0) | (ki >= BLOCK))] + [window] * (qb - 1)
    scale = HEAD_DIM ** -0.5 * LOG2_E
    heads = [(j, p, half) for j in range(qb) for p in range(n_tiles) for half in range(HEADS_PER_TILE)]
    scores = []
    for j in range(qb):
        qrows = slice(j * BLOCK, (j + 1) * BLOCK)
        krows = slice(j * BLOCK, (j + 2) * BLOCK)
        for p in range(n_tiles):
            g = (p * HEADS_PER_TILE) // group
            qp = (rope(q_ref[qrows, p * LANES:(p + 1) * LANES], cosc[qrows], sinc[qrows]) * scale).astype(BF16)
            scores += [_dot_nt(qp, k_first[g][krows]), _dot_nt(qp, k_second[g][krows])]
    probs, sink_terms = [], []
    for (j, p, half), s in zip(heads, scores):
        s = jnp.where(valid[j], s, NEG_INF)
        sink = sink_ref[p * HEADS_PER_TILE + half] * LOG2_E
        m = jnp.maximum(jnp.max(s, axis=-1, keepdims=True), sink)
        probs.append(jnp.exp2(s - m).astype(BF16))
        sink_terms.append(jnp.exp2(sink - m))
    outs = []
    for (j, p, half), e, st in zip(heads, probs, sink_terms):
        g = (p * HEADS_PER_TILE) // group
        krows = slice(j * BLOCK, (j + 2) * BLOCK)
        pv = _dot(e, (v_second[g] if half else v_first[g])[krows])
        outs.append(pv[:, :LANES] * (1.0 / (pv[:, LANES:] + st)))
    for j in range(qb):
        for p in range(n_tiles):
            i = (j * n_tiles + p) * HEADS_PER_TILE
            o_ref[j * BLOCK:(j + 1) * BLOCK, p * LANES:(p + 1) * LANES] = (outs[i] + outs[i + 1]).astype(BF16)


def _swa(z, sinks, cos_t, sin_t, batch, seq, q_off, qw, kv_off, kvw, group, casts):
    n = z.shape[0]
    nb = seq // BLOCK
    assert q_off % qw == 0 and kv_off % kvw == 0 and kvw == 2 * LANES
    q_blk, kv_blk = q_off // qw, kv_off // kvw
    qb = _pick(nb, (8, 4, 2, 1))
    steps = nb // qb
    prev = lambda t: jnp.maximum(t * qb - 1, 0)
    cast_in, cast_out, cast_shapes = _cast_specs(casts, batch * steps, lambda b, t: b * steps + t)
    y, *cast = pl.pallas_call(
        functools.partial(_swa_kernel, n_tiles=qw // LANES, group=group, qb=qb),
        grid=(batch, steps),
        in_specs=[
            pl.BlockSpec(memory_space=pltpu.SMEM),
            pl.BlockSpec((qb * BLOCK, qw), lambda b, t: (b * steps + t, q_blk)),
            pl.BlockSpec((qb * BLOCK, kvw), lambda b, t: (b * steps + t, kv_blk)),
            pl.BlockSpec((BLOCK, kvw), lambda b, t: (b * nb + prev(t), kv_blk)),
            pl.BlockSpec((qb * BLOCK, LANES), lambda b, t: (t, 0)),
            pl.BlockSpec((qb * BLOCK, LANES), lambda b, t: (t, 0)),
            pl.BlockSpec((BLOCK, LANES), lambda b, t: (prev(t), 0)),
            pl.BlockSpec((BLOCK, LANES), lambda b, t: (prev(t), 0)),
        ] + cast_in,
        out_specs=[pl.BlockSpec((qb * BLOCK, qw), lambda b, t: (b * steps + t, 0))] + cast_out,
        out_shape=[jax.ShapeDtypeStruct((n, qw), BF16)] + cast_shapes,
        compiler_params=_cparams(2),
        name="swa",
    )(sinks, z, z, z, cos_t, sin_t, cos_t, sin_t, *casts)
    return y, cast


def _memkv_kernel(m_ref, g_ref, w_ref, o_ref):
    h = _rms(m_ref[...], g_ref[...]).astype(BF16)
    o_ref[...] = _dot(h, w_ref[...].astype(BF16)).astype(BF16)


def _memkv(mem, g, w):
    n, d = mem.shape
    ncols = w.shape[1]
    tm = _pick(n, (512, 256, 128))
    tn = _pick(ncols, (1024, 512, 256, 128))
    return pl.pallas_call(
        _memkv_kernel,
        grid=(ncols // tn, n // tm),
        in_specs=[
            pl.BlockSpec((tm, d), lambda j, i: (i, 0)),
            pl.BlockSpec((1, d), lambda j, i: (0, 0)),
            pl.BlockSpec((d, tn), lambda j, i: (0, j)),
        ],
        out_specs=pl.BlockSpec((tm, tn), lambda j, i: (i, j)),
        out_shape=jax.ShapeDtypeStruct((n, ncols), BF16),
        compiler_params=_cparams(2),
        name="memkv",
    )(mem, g.reshape(1, d), w)


def _mix_kernel(x_ref, yr_ref, ys_ref, wo_ref, bo_ref, g_ref, wq_ref, k_ref, v_ref, wxo_ref, o_ref):
    c = yr_ref.shape[1]
    d = x_ref.shape[1]
    hd = d // XATTN_HEADS
    x2 = x_ref[...] + _dot(yr_ref[...], wo_ref[:c, :]) + _dot(ys_ref[...], wo_ref[c:, :]) + bo_ref[...]
    q = _dot(_rms(x2, g_ref[...]).astype(BF16), wq_ref[...]).astype(BF16)
    scale = hd ** -0.5
    heads = [slice(h * hd, (h + 1) * hd) for h in range(XATTN_HEADS)]
    scores = [_dot_nt(q[:, sl], k_ref[:, sl]) * scale for sl in heads]
    probs, dens = [], []
    for s in scores:
        e = jnp.exp(s - jnp.max(s, axis=-1, keepdims=True))
        probs.append(e.astype(BF16))
        dens.append(jnp.sum(e, axis=-1, keepdims=True))
    outs = [(_dot(e, v_ref[:, sl]) * (1.0 / den)).astype(BF16) for e, den, sl in zip(probs, dens, heads)]
    o_ref[...] = x2 + _dot(jnp.concatenate(outs, axis=1), wxo_ref[...])


def _mix(x, yr, ys, wo, bo, g, wq, kv, wxo, seq, mlen):
    n, d = x.shape
    c = yr.shape[1]
    tm = _pick(seq, (256, 128))
    per_seq = seq // tm
    once = pl.Buffered(1)
    return pl.pallas_call(
        _mix_kernel,
        grid=(n // tm,),
        in_specs=[
            pl.BlockSpec((tm, d), lambda i: (i, 0)),
            pl.BlockSpec((tm, c), lambda i: (i, 0)),
            pl.BlockSpec((tm, d - c), lambda i: (i, 0)),
            pl.BlockSpec((d, d), lambda i: (0, 0), pipeline_mode=once),
            pl.BlockSpec((1, d), lambda i: (0, 0)),
            pl.BlockSpec((1, d), lambda i: (0, 0)),
            pl.BlockSpec((d, d), lambda i: (0, 0), pipeline_mode=once),
            pl.BlockSpec((mlen, d), lambda i: (i // per_seq, 0)),
            pl.BlockSpec((mlen, d), lambda i: (i // per_seq, 1)),
            pl.BlockSpec((d, d), lambda i: (0, 0), pipeline_mode=once),
        ],
        out_specs=pl.BlockSpec((tm, d), lambda i: (i, 0)),
        out_shape=jax.ShapeDtypeStruct((n, d), F32),
        compiler_params=_cparams(1),
        name="mix",
    )(x, yr, ys, wo, bo.reshape(1, d), g.reshape(1, d), wq, kv, kv, wxo)


def _pad_cols(w, width):
    return jnp.pad(w, ((0, 0), (0, width - w.shape[1])))


def _pad_rows(w, height):
    return jnp.pad(w, ((0, height - w.shape[0]), (0, 0)))


def _rope_tables(seq):
    half = HEAD_DIM // 2
    lane = jnp.arange(LANES)
    inv_freq = ROPE_THETA ** (-jnp.arange(0, HEAD_DIM, 2, dtype=F32) / HEAD_DIM)
    ang = jnp.arange(seq, dtype=F32)[:, None] * inv_freq[lane % half][None, :]
    sign = jnp.where((lane % HEAD_DIM) < half, -1.0, 1.0)
    return jnp.cos(ang), jnp.sin(ang) * sign[None, :]


def kernel(x, mem, f1_norm, f1_gate, f1_up, f1_down, mix_norm, w_in, b_in_attn, rw_mu, rw_w0, rw_decay_up, rw_a0, rw_aaa_up, rw_gate_up, rw_k_k, rw_k_a, rw_r_k, rw_lnx_w, rw_lnx_b, attn_sinks, w_out, b_out, xa_norm, mem_norm, w_xq, w_xkv, w_xo, f2_norm, f2_gate, f2_up, f2_down, final_norm):
    batch, seq, d = x.shape
    mlen = mem.shape[1]
    depth = f1_norm.shape[0]
    c = rw_w0.shape[1]
    sw = d - c
    dl, al, gl = rw_decay_up.shape[1], rw_aaa_up.shape[1], rw_gate_up.shape[1]
    dlp, alp, glp = (_round_up(v, LANES) for v in (dl, al, gl))
    kvw = b_in_attn.shape[1] - sw
    q_heads = sw // HEAD_DIM
    group = q_heads // (kvw // (2 * HEAD_DIM))
    n_tiles = c // LANES
    n = batch * seq
    q_off = 3 * c
    lora_off = q_off + sw
    lora_w = dlp + alp + glp
    kv_off = lora_off + lora_w
    ncols = _round_up(kv_off + kvw, 2 * LANES)
    cos_t, sin_t = _rope_tables(seq)

    xf = x.reshape(n, d)
    memf = mem.reshape(batch * mlen, d)
    for l in range(depth):
        xf = _ffn(xf, f1_norm[l], f1_gate[l].astype(BF16), f1_up[l].astype(BF16), f1_down[l].astype(BF16), None)

        wl = w_in[l].astype(BF16)
        o1, o2, o3 = 3 * c, 3 * c + dl, 3 * c + dl + al
        shift = o3 + gl
        zcols = lambda w: jnp.zeros((d, w), BF16)
        cols = [wl[:, :o1], wl[:, shift:shift + sw],
                wl[:, o1:o2], zcols(dlp - dl), wl[:, o2:o3], zcols(alp - al), wl[:, o3:shift], zcols(glp - gl),
                wl[:, shift + sw:], zcols(ncols - kv_off - kvw)]
        w_all = jnp.concatenate([piece for piece in cols if piece.shape[1]], axis=1)
        mu = rw_mu[l][None, :]
        mu_all = _pad_cols(jnp.concatenate([
            mu[:, :o1], jnp.zeros((1, sw), F32),
            _pad_cols(mu[:, o1:o2], dlp), _pad_cols(mu[:, o2:o3], alp), _pad_cols(mu[:, o3:shift], glp)],
            axis=1), ncols)
        bia = b_in_attn[l][None, :]
        b_all = _pad_cols(jnp.concatenate([
            jnp.zeros((1, q_off), F32), bia[:, :sw], jnp.zeros((1, lora_w), F32), bia[:, sw:]], axis=1), ncols)
        z = _proj(xf, mix_norm[l], w_all, mu_all, b_all, seq)

        tiles = lambda v: v.reshape(n_tiles, 1, LANES)
        vecs = [rw_w0[l].reshape(1, c), rw_a0[l].reshape(1, c), tiles(rw_k_k[l]), tiles(rw_k_a[l]),
                tiles(rw_r_k[l]), tiles(rw_lnx_w[l]), tiles(rw_lnx_b[l])]
        loras = [_pad_rows(rw_decay_up[l], dlp).astype(BF16), _pad_rows(rw_aaa_up[l], alp).astype(BF16),
                 _pad_rows(rw_gate_up[l], glp).astype(BF16)]
        y_rwkv, (f2_down_b, w_out_b, w_xq_b, w_xo_b) = _rwkv(
            z, batch, seq, c, lora_off, lora_w, dlp, alp, vecs, loras,
            [f2_down[l], w_out[l], w_xq[l], w_xo[l]])
        y_swa, (f2_gate_b, f2_up_b) = _swa(z, attn_sinks[l], cos_t, sin_t, batch, seq, q_off, sw, kv_off, kvw,
                                           group, [f2_gate[l], f2_up[l]])

        kv_mem = _memkv(memf, mem_norm[l], w_xkv[l])
        xf = _mix(xf, y_rwkv, y_swa, w_out_b, b_out[l], xa_norm[l], w_xq_b, kv_mem, w_xo_b, seq, mlen)

        last = l == depth - 1
        xf = _ffn(xf, f2_norm[l], f2_gate_b, f2_up_b, f2_down_b, final_norm if last else None)
    return xf.reshape(batch, seq, d)
```

```python
import functools

import jax
import jax.numpy as jnp
from jax import lax
from jax.experimental import pallas as pl
from jax.experimental.pallas import tpu as pltpu

F32, BF16 = jnp.float32, jnp.bfloat16

LANES = 128
SUBLANES = 8
VMEM_LIMIT_BYTES = 56 * 1024 * 1024

HEAD_DIM = 64
HEADS_PER_TILE = LANES // HEAD_DIM
CHUNK = 128
BLOCK = 128
XATTN_HEADS = 4
RMS_EPS = 1e-6
GN_EPS = 64e-5
NEG_INF = -1e30
ROPE_THETA = 10000.0
EXP_M_HALF = 0.6065306597126334
LOG2_E = 1.4426950408889634


def _round_up(n, m):
    return (n + m - 1) // m * m


def _pick(n, prefs):
    for p in prefs:
        if n % p == 0:
            return p
    raise ValueError(f"no tile in {prefs} divides {n}")


def _cparams(n_axes, parallel=()):
    semantics = tuple("parallel" if a in parallel else "arbitrary" for a in range(n_axes))
    return pltpu.CompilerParams(dimension_semantics=semantics, vmem_limit_bytes=VMEM_LIMIT_BYTES)


def _dot(a, b):
    return jnp.dot(a, b, preferred_element_type=F32)


def _dot_nt(a, b):
    return lax.dot_general(a, b, (((1,), (1,)), ((), ())), preferred_element_type=F32)


def _cast_specs(weights, nsteps, flat_step):
    in_specs, out_specs, out_shapes = [], [], []
    for w in weights:
        units = w.shape[0] // (2 * SUBLANES)
        nblk = max(d for d in range(1, nsteps + 1) if units % d == 0)
        imap = lambda *g, nblk=nblk: (jnp.minimum(flat_step(*g), nblk - 1), 0)
        spec = pl.BlockSpec((w.shape[0] // nblk, w.shape[1]), imap)
        in_specs.append(spec)
        out_specs.append(spec)
        out_shapes.append(jax.ShapeDtypeStruct(w.shape, BF16))
    return in_specs, out_specs, out_shapes


def _cast_blocks(src_refs, dst_refs):
    for src, dst in zip(src_refs, dst_refs):
        dst[...] = src[...].astype(BF16)


def _rms(x, g):
    ms = jnp.mean(x * x, axis=-1, keepdims=True)
    return x * lax.rsqrt(ms + RMS_EPS) * g


def _ffn_kernel(x_ref, g_ref, wg_ref, wu_ref, wd_ref, fg_ref, o_ref, h_ref, *, final_norm, rows):
    j = pl.program_id(1)
    tm = x_ref.shape[0]

    def hidden_tile(first, last):
        for r0 in range(0, tm, rows):
            rs = slice(r0, r0 + rows)
            if first:
                base = x_ref[rs, :]
                h = _rms(base, g_ref[...]).astype(BF16)
                h_ref[rs, :] = h
            else:
                base = o_ref[rs, :]
                h = h_ref[rs, :]
            gate = _dot(h, wg_ref[...])
            up = _dot(h, wu_ref[...])
            act = (gate * jax.nn.sigmoid(gate) * up).astype(BF16)
            out = base + 0.5 * _dot(act, wd_ref[...])
            o_ref[rs, :] = _rms(out, fg_ref[...]) if last else out

    pl.when(j == 0)(lambda: hidden_tile(True, False))
    if final_norm:
        n_last = pl.num_programs(1) - 1
        pl.when((j > 0) & (j < n_last))(lambda: hidden_tile(False, False))
        pl.when(j == n_last)(lambda: hidden_tile(False, True))
    else:
        pl.when(j > 0)(lambda: hidden_tile(False, False))


def _ffn(x, g, wg, wu, wd, fg):
    n, d = x.shape
    f = wg.shape[1]
    tm = _pick(n, (1024, 512, 256, 128))
    tf = _pick(f, (512, 256, 128))
    rows = min(tm, 512)
    final_norm = fg is not None
    assert f // tf >= 2
    fg = g if fg is None else fg
    return pl.pallas_call(
        functools.partial(_ffn_kernel, final_norm=final_norm, rows=rows),
        grid=(n // tm, f // tf),
        in_specs=[
            pl.BlockSpec((tm, d), lambda i, j: (i, 0)),
            pl.BlockSpec((1, d), lambda i, j: (0, 0)),
            pl.BlockSpec((d, tf), lambda i, j: (0, j)),
            pl.BlockSpec((d, tf), lambda i, j: (0, j)),
            pl.BlockSpec((tf, d), lambda i, j: (j, 0)),
            pl.BlockSpec((1, d), lambda i, j: (0, 0)),
        ],
        out_specs=pl.BlockSpec((tm, d), lambda i, j: (i, 0)),
        out_shape=jax.ShapeDtypeStruct((n, d), F32),
        scratch_shapes=[pltpu.VMEM((tm, d), BF16)],
        compiler_params=_cparams(2, parallel=(0,)),
        name="ffn",
    )(x, g.reshape(1, d), wg, wu, wd, fg.reshape(1, d))


def _proj_kernel(x_ref, g_ref, w_ref, mu_ref, b_ref, o_ref, carry_ref, *, tiles_per_seq, tn):
    i = pl.program_id(0)
    tm = x_ref.shape[0]
    ncols = w_ref.shape[1]

    @pl.when(i % tiles_per_seq == 0)
    def _():
        carry_ref[...] = jnp.zeros(carry_ref.shape, F32)

    h = _rms(x_ref[...], g_ref[...]).astype(BF16)
    row = lax.broadcasted_iota(jnp.int32, (SUBLANES, tn), 0)
    for c0 in range(0, ncols, tn):
        cs = slice(c0, c0 + tn)
        z = _dot(h, w_ref[:, cs])
        prev_tail = carry_ref[:, cs]
        carry_ref[:, cs] = z[tm - SUBLANES:, :]
        zs = pltpu.roll(z, 1, 0)
        head = jnp.where(row == 0, pltpu.roll(prev_tail, 1, 0), zs[:SUBLANES])
        mu = mu_ref[:, cs]
        b = b_ref[:, cs]
        z0 = z[:SUBLANES]
        o_ref[:SUBLANES, cs] = z0 + (head - z0) * mu + b
        z1 = z[SUBLANES:]
        o_ref[SUBLANES:, cs] = z1 + (zs[SUBLANES:] - z1) * mu + b


def _proj(x, g, w_all, mu_all, b_all, seq):
    n, d = x.shape
    ncols = w_all.shape[1]
    tm = _pick(seq, (512, 256, 128))
    tn = _pick(ncols, (1024, 512, 256, 128))
    return pl.pallas_call(
        functools.partial(_proj_kernel, tiles_per_seq=seq // tm, tn=tn),
        grid=(n // tm,),
        in_specs=[
            pl.BlockSpec((tm, d), lambda i: (i, 0)),
            pl.BlockSpec((1, d), lambda i: (0, 0)),
            pl.BlockSpec((d, ncols), lambda i: (0, 0), pipeline_mode=pl.Buffered(1)),
            pl.BlockSpec((1, ncols), lambda i: (0, 0)),
            pl.BlockSpec((1, ncols), lambda i: (0, 0)),
        ],
        out_specs=pl.BlockSpec((tm, ncols), lambda i: (i, 0)),
        out_shape=jax.ShapeDtypeStruct((n, ncols), F32),
        scratch_shapes=[pltpu.VMEM((SUBLANES, ncols), F32)],
        compiler_params=_cparams(1),
        name="proj",
    )(x, g.reshape(1, d), w_all, mu_all, b_all)


def _bdot(a, b):
    return lax.dot_general(a, b, (((2,), (1,)), ((0,), (0,))), preferred_element_type=F32)


def _bdot_nt(a, b):
    return lax.dot_general(a, b, (((2,), (2,)), ((0,), (0,))), preferred_element_type=F32)


def _inv_unit_lower(nmat):
    L = nmat.shape[-1]
    r = lax.broadcasted_iota(jnp.int32, (L, L), 0)
    c = lax.broadcasted_iota(jnp.int32, (L, L), 1)
    eye = jnp.where(r == c, 1.0, 0.0)
    t = eye + nmat
    pw = nmat.astype(BF16)
    pw = _bdot(pw, pw).astype(BF16)
    steps = L.bit_length() - 2
    for i in range(steps):
        if i + 1 < steps:
            both = _bdot(jnp.concatenate([t.astype(BF16), pw], axis=1), pw)
            t = t + both[:, :L]
            pw = both[:, L:].astype(BF16)
        else:
            t = t + _bdot(t.astype(BF16), pw)
    return t


RWKV_INPUTS = 14


def _rwkv_kernel(*refs, n_cast, **static):
    cast_in = refs[RWKV_INPUTS:RWKV_INPUTS + n_cast]
    cast_out = refs[RWKV_INPUTS + n_cast + 1:RWKV_INPUTS + 2 * n_cast + 1]
    chunk_refs = refs[:RWKV_INPUTS] + (refs[RWKV_INPUTS + n_cast],) + refs[RWKV_INPUTS + 2 * n_cast + 1:]
    _rwkv_chunk(*chunk_refs, **static)
    _cast_blocks(cast_in, cast_out)


def _rwkv_chunk(r_ref, k_ref, v_ref, lora_ref, w0_ref, a0_ref, kk_ref, ka_ref, rk_ref, gw_ref, gb_ref,
                du_ref, au_ref, gu_ref,
                o_ref,
                s_ref, pr_ref, pk_ref, pv_ref, pa_ref, pg_ref, plp_ref, plw_ref,
                *, n_tiles, sub, dlp, alp):
    L = CHUNK
    nt = sub * n_tiles

    @pl.when(pl.program_id(1) == 0)
    def _():
        s_ref[...] = jnp.zeros(s_ref.shape, F32)

    lora = lora_ref[...]
    wd = jnp.tanh(lora[:, :dlp])
    ad = lora[:, dlp:dlp + alp]
    gd = jax.nn.sigmoid(lora[:, dlp + alp:])
    lw = -EXP_M_HALF * jax.nn.sigmoid(w0_ref[...] + _dot(wd.astype(BF16), du_ref[...]))
    asig = jax.nn.sigmoid(a0_ref[...] + _dot(ad.astype(BF16), au_ref[...]))
    gate = _dot(gd.astype(BF16), gu_ref[...])
    row = lax.broadcasted_iota(jnp.int32, (L, L), 0)
    col = lax.broadcasted_iota(jnp.int32, (L, L), 1)
    incl = row >= col
    strict = row > col
    tril = jnp.where(incl, 1.0, 0.0).astype(BF16)
    h1 = lw.astype(BF16)
    h2 = (lw - h1.astype(F32)).astype(BF16)
    for c in range(sub):
        rows = slice(c * L, (c + 1) * L)
        logp = _dot(tril, h1[rows]) + _dot(tril, h2[rows])
        for p in range(n_tiles):
            sl = slice(p * LANES, (p + 1) * LANES)
            i = c * n_tiles + p
            pr_ref[i] = r_ref[rows, sl]
            pk_ref[i] = k_ref[rows, sl]
            pv_ref[i] = v_ref[rows, sl]
            pa_ref[i] = asig[rows, sl]
            pg_ref[i] = gate[rows, sl]
            plp_ref[i] = logp[:, sl]
            plw_ref[i] = lw[rows, sl]

    lane = lax.broadcasted_iota(jnp.int32, (1, LANES), 1)
    lo = lane < HEAD_DIM
    same_head = jnp.where(row < HEAD_DIM, 0, 1) == jnp.where(col < HEAD_DIM, 0, 1)
    inv_hd = 1.0 / HEAD_DIM

    def first(x):
        return jnp.where(lo, x, 0.0)

    def second(x):
        return jnp.where(lo, 0.0, x)

    def segsum(x):
        return jnp.where(lo, jnp.sum(first(x), axis=-1, keepdims=True),
                         jnp.sum(second(x), axis=-1, keepdims=True))

    def per_tile(ref):
        return jnp.concatenate([ref[...]] * sub, axis=0)

    r = pr_ref[...]
    k = pk_ref[...]
    v = pv_ref[...]
    a_s = pa_ref[...]
    lp = plp_ref[...]
    kk = k * per_tile(kk_ref)
    ss = segsum(kk * kk)
    kk = kk * lax.rsqrt(jnp.maximum(ss, 1e-24))
    a = -kk
    b = kk * a_s
    km = k * (1.0 + (a_s - 1.0) * per_tile(ka_ref))
    cmid = lp[:, L // 2 - 1:L // 2, :]
    clast = lp[:, L - 1:L, :]
    lpe = lp - plw_ref[...]
    e_inv = jnp.exp(cmid - lp)
    e_mid = jnp.exp(cmid)
    at = a * jnp.exp(lpe - cmid)
    a_abs = at * e_mid
    rt = r * jnp.exp(lp - cmid)
    r_abs = rt * e_mid
    bt = b * e_inv
    kt = km * e_inv
    e_l = jnp.exp(clast - cmid)
    e_p = jnp.exp(clast)
    lhs = jnp.concatenate([first(at), second(at), first(rt), second(rt)], axis=1).astype(BF16)
    gmat = _bdot_nt(lhs, jnp.concatenate([bt, kt], axis=1).astype(BF16))
    n_lo = jnp.where(strict, gmat[:, 0:L, 0:L], 0.0)
    ak_lo = jnp.where(strict, gmat[:, 0:L, L:], 0.0)
    n_hi = jnp.where(strict, gmat[:, L:2 * L, 0:L], 0.0)
    ak_hi = jnp.where(strict, gmat[:, L:2 * L, L:], 0.0)
    rb_lo = jnp.where(incl, gmat[:, 2 * L:3 * L, 0:L], 0.0)
    rk_lo = jnp.where(incl, gmat[:, 2 * L:3 * L, L:], 0.0)
    rb_hi = jnp.where(incl, gmat[:, 3 * L:, 0:L], 0.0)
    rk_hi = jnp.where(incl, gmat[:, 3 * L:, L:], 0.0)
    t_all = _inv_unit_lower(jnp.concatenate([n_lo, n_hi], axis=0))
    t_wide = jnp.concatenate([t_all[:nt], t_all[nt:]], axis=2).astype(BF16)
    ak_wide = jnp.concatenate([ak_lo, ak_hi], axis=2).astype(BF16)
    read_wide = jnp.concatenate([rb_lo, rb_hi, rk_lo, rk_hi], axis=2).astype(BF16)
    v_lohi = jnp.concatenate([first(v), second(v)], axis=1).astype(BF16)
    vt = jnp.swapaxes(v, 1, 2)
    bk_last = jnp.concatenate([bt * e_l, kt * e_l], axis=1).astype(BF16)
    a_abs_b = a_abs.astype(BF16)
    r_abs_b = r_abs.astype(BF16)
    ys = []
    for c in range(sub):
        tl = slice(c * n_tiles, (c + 1) * n_tiles)
        s = s_ref[...]
        s_b = s.astype(BF16)
        rhs = _bdot_nt(a_abs_b[tl], s_b) + _bdot(ak_wide[tl], v_lohi[tl])
        u = _bdot(t_wide[tl], jnp.concatenate([first(rhs), second(rhs)], axis=1).astype(BF16))
        u_lohi = jnp.concatenate([first(u), second(u)], axis=1).astype(BF16)
        ys.append(_bdot_nt(r_abs_b[tl], s_b) + _bdot(read_wide[tl], jnp.concatenate([u_lohi, v_lohi[tl]], axis=1)))
        upd = _bdot(jnp.concatenate([jnp.swapaxes(u, 1, 2), vt[tl]], axis=2).astype(BF16), bk_last[tl])
        s_ref[...] = s * e_p[tl] + jnp.where(same_head, upd, 0.0)
    y = jnp.concatenate(ys, axis=0)
    mean = segsum(y) * inv_hd
    dev = y - mean
    var = segsum(dev * dev) * inv_hd
    yn = dev * lax.rsqrt(var + GN_EPS) * per_tile(gw_ref) + per_tile(gb_ref)
    bonus = segsum(r * km * per_tile(rk_ref)) * v
    out = ((yn + bonus) * pg_ref[...]).astype(BF16)
    for c in range(sub):
        for p in range(n_tiles):
            o_ref[c * L:(c + 1) * L, p * LANES:(p + 1) * LANES] = out[c * n_tiles + p]


def _rwkv(z, batch, seq, c, lora_off, lora_w, dlp, alp, vecs, loras, casts):
    n = z.shape[0]
    nc = seq // CHUNK
    n_tiles = c // LANES
    sub = _pick(nc, (2, 1))
    steps = nc // sub
    rows = sub * CHUNK
    assert lora_off % lora_w == 0 and len(vecs) + len(loras) + 4 == RWKV_INPUTS
    lora_blk = lora_off // lora_w
    cast_in, cast_out, cast_shapes = _cast_specs(casts, batch * steps, lambda b, t: b * steps + t)
    vec_spec = pl.BlockSpec((n_tiles, 1, LANES), lambda b, t: (0, 0, 0))
    in_specs = [
        pl.BlockSpec((rows, c), lambda b, t: (b * steps + t, 0)),
        pl.BlockSpec((rows, c), lambda b, t: (b * steps + t, 1)),
        pl.BlockSpec((rows, c), lambda b, t: (b * steps + t, 2)),
        pl.BlockSpec((rows, lora_w), lambda b, t: (b * steps + t, lora_blk)),
        pl.BlockSpec((1, c), lambda b, t: (0, 0)),
        pl.BlockSpec((1, c), lambda b, t: (0, 0)),
    ] + [vec_spec] * 5 + [pl.BlockSpec(w.shape, lambda b, t: (0, 0)) for w in loras]
    tile_f32 = pltpu.VMEM((sub * n_tiles, CHUNK, LANES), F32)
    y, *cast = pl.pallas_call(
        functools.partial(_rwkv_kernel, n_cast=len(casts), n_tiles=n_tiles, sub=sub, dlp=dlp, alp=alp),
        grid=(batch, steps),
        in_specs=in_specs + cast_in,
        out_specs=[pl.BlockSpec((rows, c), lambda b, t: (b * steps + t, 0))] + cast_out,
        out_shape=[jax.ShapeDtypeStruct((n, c), BF16)] + cast_shapes,
        scratch_shapes=[pltpu.VMEM((n_tiles, LANES, LANES), F32)] + [tile_f32] * 7,
        compiler_params=_cparams(2),
        name="rwkv",
    )(z, z, z, z, *vecs, *loras, *casts)
    return y, cast


def _swa_kernel(sink_ref, q_ref, kvc_ref, kvp_ref, cosc_ref, sinc_ref, cosp_ref, sinp_ref, *rest,
                n_tiles, group, qb):
    n_cast = len(rest) // 2
    o_ref = rest[n_cast]
    _cast_blocks(rest[:n_cast], rest[n_cast + 1:])
    nblk = pl.program_id(1)
    lane = lax.broadcasted_iota(jnp.int32, (1, LANES), 1)
    lo = lane < HEAD_DIM
    rot_lo = jnp.bitwise_and(lane, HEAD_DIM - 1) < HEAD_DIM // 2

    def rope(x, cos, sin_signed):
        partner = jnp.where(rot_lo, pltpu.roll(x, LANES - HEAD_DIM // 2, 1), pltpu.roll(x, HEAD_DIM // 2, 1))
        return x * cos + partner * sin_signed

    cosc = cosc_ref[...]
    sinc = sinc_ref[...]
    kvc = kvc_ref[...]
    kvp = kvp_ref[...]
    keys = jnp.concatenate([rope(kvp[:, :LANES], cosp_ref[...], sinp_ref[...]),
                            rope(kvc[:, :LANES], cosc, sinc)], axis=0)
    vals = jnp.concatenate([kvp[:, LANES:], kvc[:, LANES:]], axis=0)
    keys_sw = pltpu.roll(keys, HEAD_DIM, 1)
    vals_sw = pltpu.roll(vals, HEAD_DIM, 1)
    k_first = [jnp.where(lo, keys, 0.0).astype(BF16), jnp.where(lo, keys_sw, 0.0).astype(BF16)]
    k_second = [jnp.where(lo, 0.0, keys_sw).astype(BF16), jnp.where(lo, 0.0, keys).astype(BF16)]
    ones_kv = jnp.ones(vals.shape, BF16)
    with_ones = lambda x: jnp.concatenate([x.astype(BF16), ones_kv], axis=1)
    v_first = [with_ones(jnp.where(lo, vals, 0.0)), with_ones(jnp.where(lo, vals_sw, 0.0))]
    v_second = [with_ones(jnp.where(lo, 0.0, vals_sw)), with_ones(jnp.where(lo, 0.0, vals))]
    qi = lax.broadcasted_iota(jnp.int32, (BLOCK, 2 * BLOCK), 0)
    ki = lax.broadcasted_iota(jnp.int32, (BLOCK, 2 * BLOCK), 1)
    window = (ki > qi) & (ki <= qi + BLOCK)
    valid = [window & ((nblk > 0) | (ki >= BLOCK))] + [window] * (qb - 1)
    scale = HEAD_DIM ** -0.5 * LOG2_E
    heads = [(j, p, half) for j in range(qb) for p in range(n_tiles) for half in range(HEADS_PER_TILE)]
    scores = []
    for j in range(qb):
        qrows = slice(j * BLOCK, (j + 1) * BLOCK)
        krows = slice(j * BLOCK, (j + 2) * BLOCK)
        for p in range(n_tiles):
            g = (p * HEADS_PER_TILE) // group
            qp = (rope(q_ref[qrows, p * LANES:(p + 1) * LANES], cosc[qrows], sinc[qrows]) * scale).astype(BF16)
            scores += [_dot_nt(qp, k_first[g][krows]), _dot_nt(qp, k_second[g][krows])]
    probs, sink_terms = [], []
    for (j, p, half), s in zip(heads, scores):
        s = jnp.where(valid[j], s, NEG_INF)
        sink = sink_ref[p * HEADS_PER_TILE + half] * LOG2_E
        m = jnp.maximum(jnp.max(s, axis=-1, keepdims=True), sink)
        probs.append(jnp.exp2(s - m).astype(BF16))
        sink_terms.append(jnp.exp2(sink - m))
    outs = []
    for (j, p, half), e, st in zip(heads, probs, sink_terms):
        g = (p * HEADS_PER_TILE) // group
        krows = slice(j * BLOCK, (j + 2) * BLOCK)
        pv = _dot(e, (v_second[g] if half else v_first[g])[krows])
        outs.append(pv[:, :LANES] * (1.0 / (pv[:, LANES:] + st)))
    for j in range(qb):
        for p in range(n_tiles):
            i = (j * n_tiles + p) * HEADS_PER_TILE
            o_ref[j * BLOCK:(j + 1) * BLOCK, p * LANES:(p + 1) * LANES] = (outs[i] + outs[i + 1]).astype(BF16)


def _swa(z, sinks, cos_t, sin_t, batch, seq, q_off, qw, kv_off, kvw, group, casts):
    n = z.shape[0]
    nb = seq // BLOCK
    assert q_off % qw == 0 and kv_off % kvw == 0 and kvw == 2 * LANES
    q_blk, kv_blk = q_off // qw, kv_off // kvw
    qb = _pick(nb, (8, 4, 2, 1))
    steps = nb // qb
    prev = lambda t: jnp.maximum(t * qb - 1, 0)
    cast_in, cast_out, cast_shapes = _cast_specs(casts, batch * steps, lambda b, t: b * steps + t)
    y, *cast = pl.pallas_call(
        functools.partial(_swa_kernel, n_tiles=qw // LANES, group=group, qb=qb),
        grid=(batch, steps),
        in_specs=[
            pl.BlockSpec(memory_space=pltpu.SMEM),
            pl.BlockSpec((qb * BLOCK, qw), lambda b, t: (b * steps + t, q_blk)),
            pl.BlockSpec((qb * BLOCK, kvw), lambda b, t: (b * steps + t, kv_blk)),
            pl.BlockSpec((BLOCK, kvw), lambda b, t: (b * nb + prev(t), kv_blk)),
            pl.BlockSpec((qb * BLOCK, LANES), lambda b, t: (t, 0)),
            pl.BlockSpec((qb * BLOCK, LANES), lambda b, t: (t, 0)),
            pl.BlockSpec((BLOCK, LANES), lambda b, t: (prev(t), 0)),
            pl.BlockSpec((BLOCK, LANES), lambda b, t: (prev(t), 0)),
        ] + cast_in,
        out_specs=[pl.BlockSpec((qb * BLOCK, qw), lambda b, t: (b * steps + t, 0))] + cast_out,
        out_shape=[jax.ShapeDtypeStruct((n, qw), BF16)] + cast_shapes,
        compiler_params=_cparams(2),
        name="swa",
    )(sinks, z, z, z, cos_t, sin_t, cos_t, sin_t, *casts)
    return y, cast


def _memkv_kernel(m_ref, g_ref, w_ref, o_ref):
    h = _rms(m_ref[...], g_ref[...]).astype(BF16)
    o_ref[...] = _dot(h, w_ref[...].astype(BF16)).astype(BF16)


def _memkv(mem, g, w):
    n, d = mem.shape
    ncols = w.shape[1]
    tm = _pick(n, (512, 256, 128))
    tn = _pick(ncols, (1024, 512, 256, 128))
    return pl.pallas_call(
        _memkv_kernel,
        grid=(ncols // tn, n // tm),
        in_specs=[
            pl.BlockSpec((tm, d), lambda j, i: (i, 0)),
            pl.BlockSpec((1, d), lambda j, i: (0, 0)),
            pl.BlockSpec((d, tn), lambda j, i: (0, j)),
        ],
        out_specs=pl.BlockSpec((tm, tn), lambda j, i: (i, j)),
        out_shape=jax.ShapeDtypeStruct((n, ncols), BF16),
        compiler_params=_cparams(2, parallel=(0, 1)),
        name="memkv",
    )(mem, g.reshape(1, d), w)


def _mix_kernel(x_ref, yr_ref, ys_ref, wo_ref, bo_ref, g_ref, wq_ref, k_ref, v_ref, wxo_ref, o_ref):
    c = yr_ref.shape[1]
    d = x_ref.shape[1]
    hd = d // XATTN_HEADS
    x2 = x_ref[...] + _dot(yr_ref[...], wo_ref[:c, :]) + _dot(ys_ref[...], wo_ref[c:, :]) + bo_ref[...]
    q = _dot(_rms(x2, g_ref[...]).astype(BF16), wq_ref[...]).astype(BF16)
    scale = hd ** -0.5
    heads = [slice(h * hd, (h + 1) * hd) for h in range(XATTN_HEADS)]
    scores = [_dot_nt(q[:, sl], k_ref[:, sl]) * scale for sl in heads]
    probs, dens = [], []
    for s in scores:
        e = jnp.exp(s - jnp.max(s, axis=-1, keepdims=True))
        probs.append(e.astype(BF16))
        dens.append(jnp.sum(e, axis=-1, keepdims=True))
    outs = [(_dot(e, v_ref[:, sl]) * (1.0 / den)).astype(BF16) for e, den, sl in zip(probs, dens, heads)]
    o_ref[...] = x2 + _dot(jnp.concatenate(outs, axis=1), wxo_ref[...])


def _mix(x, yr, ys, wo, bo, g, wq, kv, wxo, seq, mlen):
    n, d = x.shape
    c = yr.shape[1]
    tm = _pick(seq, (256, 128))
    per_seq = seq // tm
    once = pl.Buffered(1)
    return pl.pallas_call(
        _mix_kernel,
        grid=(n // tm,),
        in_specs=[
            pl.BlockSpec((tm, d), lambda i: (i, 0)),
            pl.BlockSpec((tm, c), lambda i: (i, 0)),
            pl.BlockSpec((tm, d - c), lambda i: (i, 0)),
            pl.BlockSpec((d, d), lambda i: (0, 0), pipeline_mode=once),
            pl.BlockSpec((1, d), lambda i: (0, 0)),
            pl.BlockSpec((1, d), lambda i: (0, 0)),
            pl.BlockSpec((d, d), lambda i: (0, 0), pipeline_mode=once),
            pl.BlockSpec((mlen, d), lambda i: (i // per_seq, 0)),
            pl.BlockSpec((mlen, d), lambda i: (i // per_seq, 1)),
            pl.BlockSpec((d, d), lambda i: (0, 0), pipeline_mode=once),
        ],
        out_specs=pl.BlockSpec((tm, d), lambda i: (i, 0)),
        out_shape=jax.ShapeDtypeStruct((n, d), F32),
        compiler_params=_cparams(1, parallel=(0,)),
        name="mix",
    )(x, yr, ys, wo, bo.reshape(1, d), g.reshape(1, d), wq, kv, kv, wxo)


def _pad_cols(w, width):
    return jnp.pad(w, ((0, 0), (0, width - w.shape[1])))


def _pad_rows(w, height):
    return jnp.pad(w, ((0, height - w.shape[0]), (0, 0)))


def _rope_tables(seq):
    half = HEAD_DIM // 2
    lane = jnp.arange(LANES)
    inv_freq = ROPE_THETA ** (-jnp.arange(0, HEAD_DIM, 2, dtype=F32) / HEAD_DIM)
    ang = jnp.arange(seq, dtype=F32)[:, None] * inv_freq[lane % half][None, :]
    sign = jnp.where((lane % HEAD_DIM) < half, -1.0, 1.0)
    return jnp.cos(ang), jnp.sin(ang) * sign[None, :]


def kernel(x, mem, f1_norm, f1_gate, f1_up, f1_down, mix_norm, w_in, b_in_attn, rw_mu, rw_w0, rw_decay_up, rw_a0, rw_aaa_up, rw_gate_up, rw_k_k, rw_k_a, rw_r_k, rw_lnx_w, rw_lnx_b, attn_sinks, w_out, b_out, xa_norm, mem_norm, w_xq, w_xkv, w_xo, f2_norm, f2_gate, f2_up, f2_down, final_norm):
    batch, seq, d = x.shape
    mlen = mem.shape[1]
    depth = f1_norm.shape[0]
    c = rw_w0.shape[1]
    sw = d - c
    dl, al, gl = rw_decay_up.shape[1], rw_aaa_up.shape[1], rw_gate_up.shape[1]
    dlp, alp, glp = (_round_up(v, LANES) for v in (dl, al, gl))
    kvw = b_in_attn.shape[1] - sw
    q_heads = sw // HEAD_DIM
    group = q_heads // (kvw // (2 * HEAD_DIM))
    n_tiles = c // LANES
    n = batch * seq
    q_off = 3 * c
    lora_off = q_off + sw
    lora_w = dlp + alp + glp
    kv_off = lora_off + lora_w
    ncols = _round_up(kv_off + kvw, 2 * LANES)
    cos_t, sin_t = _rope_tables(seq)

    xf = x.reshape(n, d)
    memf = mem.reshape(batch * mlen, d)
    for l in range(depth):
        xf = _ffn(xf, f1_norm[l], f1_gate[l].astype(BF16), f1_up[l].astype(BF16), f1_down[l].astype(BF16), None)

        wl = w_in[l]
        o1, o2, o3 = 3 * c, 3 * c + dl, 3 * c + dl + al
        shift = o3 + gl
        w_all = jnp.concatenate([
            wl[:, :o1], wl[:, shift:shift + sw],
            _pad_cols(wl[:, o1:o2], dlp), _pad_cols(wl[:, o2:o3], alp), _pad_cols(wl[:, o3:shift], glp),
            wl[:, shift + sw:]], axis=1)
        w_all = _pad_cols(w_all, ncols).astype(BF16)
        mu = rw_mu[l][None, :]
        mu_all = _pad_cols(jnp.concatenate([
            mu[:, :o1], jnp.zeros((1, sw), F32),
            _pad_cols(mu[:, o1:o2], dlp), _pad_cols(mu[:, o2:o3], alp), _pad_cols(mu[:, o3:shift], glp)],
            axis=1), ncols)
        bia = b_in_attn[l][None, :]
        b_all = _pad_cols(jnp.concatenate([
            jnp.zeros((1, q_off), F32), bia[:, :sw], jnp.zeros((1, lora_w), F32), bia[:, sw:]], axis=1), ncols)
        z = _proj(xf, mix_norm[l], w_all, mu_all, b_all, seq)

        tiles = lambda v: v.reshape(n_tiles, 1, LANES)
        vecs = [rw_w0[l].reshape(1, c), rw_a0[l].reshape(1, c), tiles(rw_k_k[l]), tiles(rw_k_a[l]),
                tiles(rw_r_k[l]), tiles(rw_lnx_w[l]), tiles(rw_lnx_b[l])]
        loras = [_pad_rows(rw_decay_up[l], dlp).astype(BF16), _pad_rows(rw_aaa_up[l], alp).astype(BF16),
                 _pad_rows(rw_gate_up[l], glp).astype(BF16)]
        y_rwkv, (f2_down_b, w_out_b, w_xq_b, w_xo_b) = _rwkv(
            z, batch, seq, c, lora_off, lora_w, dlp, alp, vecs, loras,
            [f2_down[l], w_out[l], w_xq[l], w_xo[l]])
        y_swa, (f2_gate_b, f2_up_b) = _swa(z, attn_sinks[l], cos_t, sin_t, batch, seq, q_off, sw, kv_off, kvw,
                                           group, [f2_gate[l], f2_up[l]])

        kv_mem = _memkv(memf, mem_norm[l], w_xkv[l])
        xf = _mix(xf, y_rwkv, y_swa, w_out_b, b_out[l], xa_norm[l], w_xq_b, kv_mem, w_xo_b, seq, mlen)

        last = l == depth - 1
        xf = _ffn(xf, f2_norm[l], f2_gate_b, f2_up_b, f2_down_b, final_norm if last else None)
    return xf.reshape(batch, seq, d)
```
